```python
import jax, jax.numpy as jnp
from jax import lax
import numpy as np

D_MODEL = 1024
BATCH = 8
SEQ = 4096
DEPTH = 1

EPS = 1e-6
GRID_W = 64
D_MIX = D_MODEL

GLA_HEADS = 4
GLA_DK = 64
GLA_DV = 128
GLA_KEY_WIDTH = GLA_HEADS * GLA_DK
GLA_WIDTH = GLA_HEADS * GLA_DV
GLA_GATE_RANK = 16
GLA_GATE_NORM = 16.0
GLA_CHUNK = 64

ATT_HEADS = 8
ATT_KV_HEADS = 2
ATT_HEAD_DIM = 64
ATT_WIDTH = ATT_HEADS * ATT_HEAD_DIM
ATT_KV_WIDTH = ATT_KV_HEADS * ATT_HEAD_DIM
Q_BLOCK = 128
ROPE_THETA = 10000.0

N_GROUPS = 8
EXPERTS_PER_GROUP = 8
N_EXPERTS = N_GROUPS * EXPERTS_PER_GROUP
TOP_K = 2
D_EXPERT = 512
MOE_BLOCK = 128

IN_SIZES = (GLA_KEY_WIDTH, GLA_KEY_WIDTH, GLA_WIDTH, GLA_WIDTH,
            GLA_GATE_RANK, GLA_GATE_RANK,
            ATT_WIDTH, ATT_KV_WIDTH, ATT_KV_WIDTH)
D_IN = 2336

kernel_name = "hymba_gla_axialgqa_hmoe_encoder"


def rms_norm(x, gain):
    x32 = x.astype(jnp.float32)
    y = x32 * lax.rsqrt(jnp.mean(x32 * x32, axis=-1, keepdims=True) + EPS)
    return y.astype(x.dtype) * gain


def gla_chunked(q, k, v, log_a):
    B, H, T, dk = q.shape
    dv = v.shape[-1]
    C = GLA_CHUNK
    N = T // C
    f32 = jnp.float32
    qf = q.astype(f32).reshape(B, H, N, C, dk)
    kf = k.astype(f32).reshape(B, H, N, C, dk)
    vf = v.astype(f32).reshape(B, H, N, C, dv)
    b = jnp.cumsum(log_a.astype(f32).reshape(B, H, N, C, dk), axis=3)
    b_ref = b[:, :, :, C // 2 - 1:C // 2, :]
    b_last = b[:, :, :, -1:, :]
    scores = jnp.einsum('bhncd,bhnsd->bhncs', qf * jnp.exp(b - b_ref), kf * jnp.exp(b_ref - b))
    mask = jnp.tril(jnp.ones((C, C), dtype=bool))
    scores = jnp.where(mask, scores, 0.0)
    o_intra = jnp.einsum('bhncs,bhnse->bhnce', scores, vf)
    chunk_kv = jnp.einsum('bhncd,bhnce->bhnde', kf * jnp.exp(b_last - b), vf)
    chunk_decay = jnp.exp(b_last[:, :, :, 0, :])

    def step(state, inp):
        decay, kv = inp
        return decay[..., None] * state + kv, state

    init = jnp.zeros((B, H, dk, dv), f32)
    _, states = lax.scan(step, init, (jnp.moveaxis(chunk_decay, 2, 0), jnp.moveaxis(chunk_kv, 2, 0)))
    states = jnp.moveaxis(states, 0, 2)
    o_inter = jnp.einsum('bhncd,bhnde->bhnce', qf * jnp.exp(b), states)
    return (o_intra + o_inter).reshape(B, H, T, dv).astype(v.dtype)


def gla_group(q, k, v, g, z_fwd, z_bwd, up_f, up_f_bias, up_b, up_b_bias, out_gain):
    B, T, _ = q.shape

    def heads(t, d):
        return t.reshape(B, T, GLA_HEADS, d).transpose(0, 2, 1, 3)

    qh = heads(q, GLA_DK) * (GLA_DK ** -0.5)
    kh = heads(k, GLA_DK)
    vh = heads(v, GLA_DV)
    la_f = heads(jax.nn.log_sigmoid((z_fwd @ up_f + up_f_bias).astype(jnp.float32)) / GLA_GATE_NORM, GLA_DK)
    la_b = heads(jax.nn.log_sigmoid((z_bwd @ up_b + up_b_bias).astype(jnp.float32)) / GLA_GATE_NORM, GLA_DK)
    o_f = gla_chunked(qh, kh, vh, la_f)
    flip = lambda t: jnp.flip(t, axis=2)
    o_b = flip(gla_chunked(flip(qh), flip(kh), flip(vh), flip(la_b)))
    o = rms_norm(o_f + o_b, out_gain)
    o = o.transpose(0, 2, 1, 3).reshape(B, T, GLA_WIDTH)
    return o * jax.nn.silu(g)


def axial_rope_tables(T):
    rows = T // GRID_W
    row = jnp.repeat(jnp.arange(rows), GRID_W).astype(jnp.float32)
    col = jnp.tile(jnp.arange(GRID_W), rows).astype(jnp.float32)
    axis_dim = ATT_HEAD_DIM // 2
    inv_freq = ROPE_THETA ** (-jnp.arange(0, axis_dim, 2, dtype=jnp.float32) / axis_dim)
    ang = jnp.concatenate([row[:, None] * inv_freq, col[:, None] * inv_freq], axis=-1)
    return jnp.cos(ang), jnp.sin(ang)


def apply_axial_rope(x, cos, sin):
    xr = x.astype(jnp.float32).reshape(*x.shape[:-1], -1, 2)
    x0, x1 = xr[..., 0], xr[..., 1]
    out = jnp.stack([x0 * cos - x1 * sin, x0 * sin + x1 * cos], axis=-1)
    return out.reshape(x.shape).astype(x.dtype)


def gqa_group(q, k, v, q_gain, k_gain, out_gain):
    B, T, _ = q.shape
    dh = ATT_HEAD_DIM
    G = ATT_HEADS // ATT_KV_HEADS
    qh = rms_norm(q.reshape(B, T, ATT_HEADS, dh).transpose(0, 2, 1, 3), q_gain)
    kh = rms_norm(k.reshape(B, T, ATT_KV_HEADS, dh).transpose(0, 2, 1, 3), k_gain)
    vh = v.reshape(B, T, ATT_KV_HEADS, dh).transpose(0, 2, 1, 3)
    cos, sin = axial_rope_tables(T)
    qh = apply_axial_rope(qh, cos, sin) * (dh ** -0.5)
    kh = apply_axial_rope(kh, cos, sin)
    nb = T // Q_BLOCK
    q_blocks = qh.reshape(B, ATT_KV_HEADS, G, nb, Q_BLOCK, dh).transpose(3, 0, 1, 2, 4, 5)

    def attend(qb):
        s = jnp.einsum('bkgqd,bksd->bkgqs', qb, kh).astype(jnp.float32)
        p = jax.nn.softmax(s, axis=-1).astype(vh.dtype)
        return jnp.einsum('bkgqs,bksd->bkgqd', p, vh)

    o = lax.map(attend, q_blocks)
    o = o.transpose(1, 0, 4, 2, 3, 5).reshape(B, T, ATT_WIDTH)
    return rms_norm(o, out_gain)


def hier_moe(x, w_group, b_group, w_expert, b_expert, w_gate, w_up, w_down):
    B, T, D = x.shape
    xt = x.reshape(-1, D)
    N = xt.shape[0]
    g_logits = (xt @ w_group + b_group).astype(jnp.float32)
    g_prob = jax.nn.softmax(g_logits, axis=-1)
    g_idx = jnp.argmax(g_logits, axis=-1)
    g_w = jnp.take_along_axis(g_prob, g_idx[:, None], axis=-1)
    e_logits = (xt @ w_expert + b_expert).astype(jnp.float32).reshape(N, N_GROUPS, EXPERTS_PER_GROUP)
    e_logits = jnp.take_along_axis(e_logits, g_idx[:, None, None], axis=1)[:, 0]
    top_val, top_idx = lax.top_k(e_logits, TOP_K)
    gate = g_w * jax.nn.softmax(top_val, axis=-1)
    expert_id = g_idx[:, None] * EXPERTS_PER_GROUP + top_idx

    M = N * TOP_K
    flat_e = expert_id.reshape(M).astype(jnp.int32)
    flat_tok = jnp.repeat(jnp.arange(N, dtype=jnp.int32), TOP_K)
    flat_gate = gate.reshape(M).astype(x.dtype)
    order = jnp.argsort(flat_e)
    se = flat_e[order]
    counts = jnp.bincount(flat_e, length=N_EXPERTS)
    starts = jnp.cumsum(counts) - counts
    padded = (counts + MOE_BLOCK - 1) // MOE_BLOCK * MOE_BLOCK
    pad_ends = jnp.cumsum(padded)
    pad_starts = pad_ends - padded
    dest = pad_starts[se] + (jnp.arange(M, dtype=jnp.int32) - starts[se])
    P = M + N_EXPERTS * MOE_BLOCK
    slot_tok = jnp.zeros((P,), jnp.int32).at[dest].set(flat_tok[order])
    slot_gate = jnp.zeros((P,), x.dtype).at[dest].set(flat_gate[order])
    nblk = P // MOE_BLOCK
    blk_start = jnp.arange(nblk, dtype=jnp.int32) * MOE_BLOCK
    blk_expert = jnp.minimum(jnp.searchsorted(pad_ends, blk_start, side='right'), N_EXPERTS - 1)
    xs = xt[slot_tok].reshape(nblk, MOE_BLOCK, D)

    def run_block(args):
        xb, e = args
        h = jax.nn.silu(xb @ w_gate[e]) * (xb @ w_up[e])
        return h @ w_down[e]

    ys = lax.map(run_block, (xs, blk_expert)).reshape(P, D)
    y = jnp.zeros((N, D), x.dtype).at[slot_tok].add(ys * slot_gate[:, None])
    return y.reshape(B, T, D)


def setup_inputs(seed: int = 0) -> dict:
    key = jax.random.key(seed)
    ks = jax.random.split(key, 24)
    f32 = jnp.float32
    L = DEPTH

    def nrm(k, shape, scale):
        return jax.random.normal(k, shape, f32) * scale

    def gain(k, shape):
        return 1.0 + 0.02 * jax.random.normal(k, shape, f32)

    return {
        "x": nrm(ks[0], (BATCH, SEQ, D_MODEL), 1.0),
        "norm1_gain": gain(ks[1], (L, D_MODEL)),
        "w_in": nrm(ks[2], (L, D_MODEL, D_IN), D_MODEL ** -0.5),
        "gla_up_fwd": nrm(ks[3], (L, GLA_GATE_RANK, GLA_KEY_WIDTH), GLA_GATE_RANK ** -0.5),
        "gla_up_fwd_bias": nrm(ks[4], (L, GLA_KEY_WIDTH), 0.1),
        "gla_up_bwd": nrm(ks[5], (L, GLA_GATE_RANK, GLA_KEY_WIDTH), GLA_GATE_RANK ** -0.5),
        "gla_up_bwd_bias": nrm(ks[6], (L, GLA_KEY_WIDTH), 0.1),
        "gla_out_gain": gain(ks[7], (L, GLA_DV)),
        "q_norm_gain": gain(ks[8], (L, ATT_HEAD_DIM)),
        "k_norm_gain": gain(ks[9], (L, ATT_HEAD_DIM)),
        "att_out_gain": gain(ks[10], (L, ATT_WIDTH)),
        "w_out": nrm(ks[11], (L, D_MIX, D_MODEL), D_MIX ** -0.5),
        "norm2_gain": gain(ks[12], (L, D_MODEL)),
        "w_group": nrm(ks[13], (L, D_MODEL, N_GROUPS), D_MODEL ** -0.5),
        "b_group": nrm(ks[14], (L, N_GROUPS), 0.01),
        "w_expert": nrm(ks[15], (L, D_MODEL, N_EXPERTS), D_MODEL ** -0.5),
        "b_expert": nrm(ks[16], (L, N_EXPERTS), 0.01),
        "w_gate": nrm(ks[17], (L, N_EXPERTS, D_MODEL, D_EXPERT), D_MODEL ** -0.5),
        "w_up": nrm(ks[18], (L, N_EXPERTS, D_MODEL, D_EXPERT), D_MODEL ** -0.5),
        "w_down": nrm(ks[19], (L, N_EXPERTS, D_EXPERT, D_MODEL), D_EXPERT ** -0.5),
        "final_gain": gain(ks[20], (D_MODEL,)),
    }


def reference(x, norm1_gain, w_in, gla_up_fwd, gla_up_fwd_bias, gla_up_bwd, gla_up_bwd_bias,
              gla_out_gain, q_norm_gain, k_norm_gain, att_out_gain, w_out, norm2_gain,
              w_group, b_group, w_expert, b_expert, w_gate, w_up, w_down, final_gain):
    split_points = np.cumsum(IN_SIZES)[:-1].tolist()
    h = x
    for l in range(DEPTH):
        u = rms_norm(h, norm1_gain[l])
        proj = u @ w_in[l]
        gq, gk, gv, gg, zf, zb, aq, ak, av = jnp.split(proj, split_points, axis=-1)
        o_gla = gla_group(gq, gk, gv, gg, zf, zb, gla_up_fwd[l], gla_up_fwd_bias[l],
                          gla_up_bwd[l], gla_up_bwd_bias[l], gla_out_gain[l])
        o_att = gqa_group(aq, ak, av, q_norm_gain[l], k_norm_gain[l], att_out_gain[l])
        mixed = jnp.concatenate([o_gla, o_att], axis=-1)
        h = h + mixed @ w_out[l]
        h = h + hier_moe(rms_norm(h, norm2_gain[l]), w_group[l], b_group[l], w_expert[l],
                         b_expert[l], w_gate[l], w_up[l], w_down[l])
    return rms_norm(h, final_gain)
```

```python
import functools

import numpy as np
import jax
import jax.numpy as jnp
from jax import lax
from jax.experimental import pallas as pl
from jax.experimental.pallas import tpu as pltpu

F32 = jnp.float32
BF16 = jnp.bfloat16

D_MODEL = 1024
EPS = 1e-6
GRID_W = 64

GLA_HEADS = 4
GLA_DK = 64
GLA_DV = 128
GLA_KEY_WIDTH = GLA_HEADS * GLA_DK
GLA_WIDTH = GLA_HEADS * GLA_DV
GLA_GATE_RANK = 16
GLA_GATE_NORM = 16.0
GLA_CHUNK = 64

ATT_HEADS = 8
ATT_KV_HEADS = 2
ATT_HEAD_DIM = 64
ATT_GROUP = ATT_HEADS // ATT_KV_HEADS
ATT_WIDTH = ATT_HEADS * ATT_HEAD_DIM
ATT_KV_WIDTH = ATT_KV_HEADS * ATT_HEAD_DIM
ROPE_THETA = 10000.0

N_GROUPS = 8
EXPERTS_PER_GROUP = 8
N_EXPERTS = N_GROUPS * EXPERTS_PER_GROUP
TOP_K = 2
D_EXPERT = 512
MOE_BLOCK = 128

LANES = 128
Z_PAD = LANES

_OFF_GQ = 0
_OFF_GK = _OFF_GQ + GLA_KEY_WIDTH
_OFF_GV = _OFF_GK + GLA_KEY_WIDTH
_OFF_GG = _OFF_GV + GLA_WIDTH
_OFF_AQ = _OFF_GG + GLA_WIDTH
_OFF_AK = _OFF_AQ + ATT_WIDTH
_OFF_AV = _OFF_AK + ATT_KV_WIDTH
_OFF_Z = _OFF_AV + ATT_KV_WIDTH
D_IN_PAD = _OFF_Z + Z_PAD

VMEM_LIMIT = 56 * 1024 * 1024


def _cparams(semantics):
    return pltpu.CompilerParams(dimension_semantics=semantics, vmem_limit_bytes=VMEM_LIMIT)


def _rms(x, gain):
    return x * lax.rsqrt(jnp.mean(x * x, axis=-1, keepdims=True) + EPS) * gain


def _sigmoid(x):
    return 1.0 / (1.0 + jnp.exp(-x))


def _head_norm_rope(x, head_mean, gain, cos, sin_even, sin_odd):
    w = x.shape[1]
    ms = jnp.dot((x * x).astype(BF16), head_mean, preferred_element_type=F32)
    xn = x * lax.rsqrt(ms + EPS) * gain
    reps = w // LANES
    tile = lambda t: t if reps == 1 else jnp.concatenate([t] * reps, axis=1)
    nxt = pltpu.roll(xn, w - 1, 1)
    prv = pltpu.roll(xn, 1, 1)
    return xn * tile(cos) + nxt * tile(sin_even) + prv * tile(sin_odd)


def _in_proj_kernel(x_ref, g1_ref, w_ref, up_ref, upb_ref, qg_ref, kg_ref, hm_ref,
                    cos_ref, se_ref, so_ref,
                    gq_ref, gk_ref, gv_ref, gg_ref, laf_ref, lab_ref, aq_ref, akt_ref, av_ref):
    u = _rms(x_ref[...], g1_ref[...])
    res = jnp.dot(u.astype(BF16), w_ref[...], preferred_element_type=F32)
    gq_ref[...] = res[:, _OFF_GQ:_OFF_GK].astype(BF16)
    gk_ref[...] = res[:, _OFF_GK:_OFF_GV].astype(BF16)
    gv_ref[...] = res[:, _OFF_GV:_OFF_GG].astype(BF16)
    gg_ref[...] = res[:, _OFF_GG:_OFF_AQ].astype(BF16)
    av_ref[...] = res[:, _OFF_AV:_OFF_Z].astype(BF16)

    z = res[:, _OFF_Z:D_IN_PAD].astype(BF16)
    zl = jnp.dot(z, up_ref[...], preferred_element_type=F32) + upb_ref[...]
    la = (jnp.minimum(zl, 0.0) - jnp.log(1.0 + jnp.exp(-jnp.abs(zl)))) * (1.0 / GLA_GATE_NORM)
    laf_ref[...] = la[:, :GLA_KEY_WIDTH]
    lab_ref[...] = la[:, GLA_KEY_WIDTH:]

    cos, se, so = cos_ref[...], se_ref[...], so_ref[...]
    hm = hm_ref[...]
    q = _head_norm_rope(res[:, _OFF_AQ:_OFF_AK], hm, qg_ref[...], cos, se, so)
    aq_ref[...] = (q * (ATT_HEAD_DIM ** -0.5)).astype(BF16)
    k = _head_norm_rope(res[:, _OFF_AK:_OFF_AV], hm[:ATT_KV_WIDTH, :ATT_KV_WIDTH], kg_ref[...], cos, se, so)
    akt_ref[0] = k.T.astype(BF16)


def _rope_tables(T):
    t = np.arange(T)
    row = (t // GRID_W).astype(np.float32)
    col = (t % GRID_W).astype(np.float32)
    axis_dim = ATT_HEAD_DIM // 2
    inv_freq = (ROPE_THETA ** (-np.arange(0, axis_dim, 2, dtype=np.float32) / axis_dim)).astype(np.float32)
    ang = np.concatenate([row[:, None] * inv_freq, col[:, None] * inv_freq], axis=-1)
    ang = np.repeat(ang, 2, axis=1)
    ang = np.tile(ang, (1, LANES // ATT_HEAD_DIM))
    even = (np.arange(LANES) % 2 == 0)[None, :]
    cos = np.cos(ang)
    sin = np.sin(ang)
    return (jnp.asarray(cos, F32), jnp.asarray(np.where(even, -sin, 0.0), F32),
            jnp.asarray(np.where(even, 0.0, sin), F32))


def _in_proj(xt, B, T, tm, g1, w, up, upb, qg, kg, hm, cos, se, so):
    N = xt.shape[0]
    tpb = T // tm
    const = lambda shape: pl.BlockSpec(shape, lambda i: (0,) * len(shape))
    rows = lambda width: pl.BlockSpec((tm, width), lambda i: (i, 0))
    pos = pl.BlockSpec((tm, LANES), lambda i: (i % tpb, 0))
    out_shape = (
        jax.ShapeDtypeStruct((N, GLA_KEY_WIDTH), BF16),
        jax.ShapeDtypeStruct((N, GLA_KEY_WIDTH), BF16),
        jax.ShapeDtypeStruct((N, GLA_WIDTH), BF16),
        jax.ShapeDtypeStruct((N, GLA_WIDTH), BF16),
        jax.ShapeDtypeStruct((N, GLA_KEY_WIDTH), F32),
        jax.ShapeDtypeStruct((N, GLA_KEY_WIDTH), F32),
        jax.ShapeDtypeStruct((N, ATT_WIDTH), BF16),
        jax.ShapeDtypeStruct((B, ATT_KV_WIDTH, T), BF16),
        jax.ShapeDtypeStruct((N, ATT_KV_WIDTH), BF16),
    )
    out_specs = (
        rows(GLA_KEY_WIDTH), rows(GLA_KEY_WIDTH), rows(GLA_WIDTH), rows(GLA_WIDTH),
        rows(GLA_KEY_WIDTH), rows(GLA_KEY_WIDTH), rows(ATT_WIDTH),
        pl.BlockSpec((1, ATT_KV_WIDTH, tm), lambda i: (i // tpb, 0, i % tpb)),
        rows(ATT_KV_WIDTH),
    )
    return pl.pallas_call(
        _in_proj_kernel,
        out_shape=out_shape,
        grid=(N // tm,),
        in_specs=[rows(D_MODEL), const((1, D_MODEL)), const((D_MODEL, D_IN_PAD)),
                  const((Z_PAD, 2 * GLA_KEY_WIDTH)), const((1, 2 * GLA_KEY_WIDTH)),
                  const((1, ATT_WIDTH)), const((1, ATT_KV_WIDTH)), const((ATT_WIDTH, ATT_WIDTH)),
                  pos, pos, pos],
        out_specs=out_specs,
        compiler_params=_cparams(("arbitrary",)),
        name="in_proj",
    )(xt, g1, w, up, upb, qg, kg, hm, cos, se, so)


def _gla_kernel(q_ref, k_ref, v_ref, g_ref, laf_ref, lab_ref, gain_ref, o_ref, of_scr):
    C = GLA_CHUNK
    T = q_ref.shape[0]
    n_chunks = T // C
    row = lax.broadcasted_iota(jnp.int32, (C, C), 0)
    col = lax.broadcasted_iota(jnp.int32, (C, C), 1)
    tril = row >= col
    triu = row <= col
    gain = gain_ref[...]

    def chunk(r0, state, la_ref, mask, i_ref, i_last):
        q2 = q_ref[pl.ds(r0, C), :].astype(F32) * (GLA_DK ** -0.5)
        k2 = k_ref[pl.ds(r0, C), :].astype(F32)
        v2 = v_ref[pl.ds(r0, C), :]
        la = la_ref[pl.ds(r0, C), :]
        tri = mask.astype(BF16)
        la_hi = la.astype(BF16)
        la_lo = (la - la_hi.astype(F32)).astype(BF16)
        b = (jnp.dot(tri, la_hi, preferred_element_type=F32)
             + jnp.dot(tri, la_lo, preferred_element_type=F32))
        b_ref = b[i_ref:i_ref + 1, :]
        b_last = b[i_last:i_last + 1, :]
        qf = (q2 * jnp.exp(b - b_ref)).astype(BF16)
        kf = (k2 * jnp.exp(b_ref - b)).astype(BF16)
        qi = (q2 * jnp.exp(b)).astype(BF16)
        kl = (k2 * jnp.exp(b_last - b)).astype(BF16)
        decay = jnp.exp(b_last)
        outs, new_state = [], []
        for h in range(2):
            ks = slice(h * GLA_DK, (h + 1) * GLA_DK)
            vh = v2[:, h * GLA_DV:(h + 1) * GLA_DV]
            sc = lax.dot_general(qf[:, ks], kf[:, ks], (((1,), (1,)), ((), ())),
                                 preferred_element_type=F32)
            sc = jnp.where(mask, sc, 0.0).astype(BF16)
            o = jnp.dot(sc, vh, preferred_element_type=F32)
            o = o + lax.dot_general(qi[:, ks], state[h].astype(BF16), (((1,), (1,)), ((), ())),
                                    preferred_element_type=F32)
            kv_t = lax.dot_general(vh, kl[:, ks], (((0,), (0,)), ((), ())),
                                   preferred_element_type=F32)
            new_state.append(decay[:, ks] * state[h] + kv_t)
            outs.append(o)
        return jnp.concatenate(outs, axis=1), tuple(new_state)

    zero_state = tuple(jnp.zeros((GLA_DV, GLA_DK), F32) for _ in range(2))

    def fwd_body(n, state):
        r0 = pl.multiple_of(n * C, C)
        o, state = chunk(r0, state, laf_ref, tril, C // 2 - 1, C - 1)
        of_scr[pl.ds(r0, C), :] = o
        return state

    lax.fori_loop(0, n_chunks, fwd_body, zero_state)

    def bwd_body(i, state):
        r0 = pl.multiple_of((n_chunks - 1 - i) * C, C)
        o_b, state = chunk(r0, state, lab_ref, triu, C // 2, 0)
        o = of_scr[pl.ds(r0, C), :] + o_b
        g = g_ref[pl.ds(r0, C), :].astype(F32)
        normed = jnp.concatenate(
            [_rms(o[:, h * GLA_DV:(h + 1) * GLA_DV], gain) for h in range(2)], axis=1)
        o_ref[pl.ds(r0, C), :] = (normed * (g * _sigmoid(g))).astype(BF16)
        return state

    lax.fori_loop(0, n_chunks, bwd_body, zero_state)


def _gla(gq, gk, gv, gg, laf, lab, gain, B, T):
    N = gq.shape[0]
    pairs = GLA_HEADS // 2
    kspec = pl.BlockSpec((T, 2 * GLA_DK), lambda b, p: (b, p))
    vspec = pl.BlockSpec((T, 2 * GLA_DV), lambda b, p: (b, p))
    return pl.pallas_call(
        _gla_kernel,
        out_shape=jax.ShapeDtypeStruct((N, GLA_WIDTH), BF16),
        grid=(B, pairs),
        in_specs=[kspec, kspec, vspec, vspec, kspec, kspec,
                  pl.BlockSpec((1, GLA_DV), lambda b, p: (0, 0))],
        out_specs=vspec,
        scratch_shapes=[pltpu.VMEM((T, 2 * GLA_DV), F32)],
        compiler_params=_cparams(("arbitrary", "arbitrary")),
        name="gla",
    )(gq, gk, gv, gg, laf, lab, gain)


def _attn_kernel(q_ref, kt_ref, v_ref, o_ref, vext_scr, s_scr, p_scr, *, key_block):
    j = pl.program_id(1)
    tq = q_ref.shape[0]
    T = v_ref.shape[0]
    dh = ATT_HEAD_DIM

    @pl.when(pl.program_id(2) == 0)
    def _():
        lane = lax.broadcasted_iota(jnp.int32, (T, ATT_KV_WIDTH), 1)
        own = (lane >= j * dh) & (lane < (j + 1) * dh)
        vext_scr[...] = jnp.where(own, v_ref[...], jnp.ones((), BF16))

    q = q_ref[...]
    qs = jnp.concatenate([q[:, h * dh:(h + 1) * dh] for h in range(ATT_GROUP)], axis=0)
    m = jnp.full((ATT_GROUP * tq, LANES), -jnp.inf, F32)
    for kb in range(T // key_block):
        ks = slice(kb * key_block, (kb + 1) * key_block)
        s = jnp.dot(qs, kt_ref[0, :, ks], preferred_element_type=F32)
        s_scr[:, ks] = s
        for c in range(key_block // LANES):
            m = jnp.maximum(m, s[:, c * LANES:(c + 1) * LANES])
    mx = jnp.max(m, axis=-1, keepdims=True)
    for kb in range(T // key_block):
        ks = slice(kb * key_block, (kb + 1) * key_block)
        p_scr[:, ks] = jnp.exp(s_scr[:, ks] - mx).astype(BF16)
    acc = jnp.dot(p_scr[...], vext_scr[...], preferred_element_type=F32)
    first = j == 0
    num = jnp.where(first, acc[:, :dh], acc[:, dh:])
    den = jnp.where(first, acc[:, dh:dh + 1], acc[:, 0:1])
    o = num / den
    o_ref[...] = jnp.concatenate([o[h * tq:(h + 1) * tq] for h in range(ATT_GROUP)], axis=1).astype(BF16)


def _attn(aq, akt, av, B, T, tq, key_block):
    N = aq.shape[0]
    nq = T // tq
    gw = ATT_GROUP * ATT_HEAD_DIM
    return pl.pallas_call(
        functools.partial(_attn_kernel, key_block=key_block),
        out_shape=jax.ShapeDtypeStruct((N, ATT_WIDTH), BF16),
        grid=(B, ATT_KV_HEADS, nq),
        in_specs=[pl.BlockSpec((tq, gw), lambda b, j, i: (b * nq + i, j)),
                  pl.BlockSpec((1, ATT_HEAD_DIM, T), lambda b, j, i: (b, j, 0)),
                  pl.BlockSpec((T, ATT_KV_WIDTH), lambda b, j, i: (b, 0))],
        out_specs=pl.BlockSpec((tq, gw), lambda b, j, i: (b * nq + i, j)),
        scratch_shapes=[pltpu.VMEM((T, ATT_KV_WIDTH), BF16),
                        pltpu.VMEM((ATT_GROUP * tq, T), F32),
                        pltpu.VMEM((ATT_GROUP * tq, T), BF16)],
        compiler_params=_cparams(("arbitrary", "arbitrary", "arbitrary")),
        name="attn",
    )(aq, akt, av)


def _out_proj_kernel(x_ref, og_ref, oa_ref, wo_ref, ag_ref, g2_ref, wr_ref, br_ref, tri_ref,
                     h_ref, xn_ref, eid_ref, gate_ref, rank_ref, cnt_ref, run_scr):
    tm = x_ref.shape[0]

    @pl.when(pl.program_id(0) == 0)
    def _():
        run_scr[...] = jnp.zeros_like(run_scr)

    oan = _rms(oa_ref[...].astype(F32), ag_ref[...]).astype(BF16)
    y = (jnp.dot(og_ref[...], wo_ref[:GLA_WIDTH, :], preferred_element_type=F32)
         + jnp.dot(oan, wo_ref[GLA_WIDTH:, :], preferred_element_type=F32))
    h = x_ref[...] + y
    h_ref[...] = h
    xn = _rms(h, g2_ref[...])
    xn_ref[...] = xn

    lt = lax.dot_general(wr_ref[...], xn.astype(BF16), (((1,), (1,)), ((), ())),
                         preferred_element_type=F32) + br_ref[...]
    iota8 = lax.broadcasted_iota(jnp.int32, (N_GROUPS, tm), 0)

    def first_argmax(v):
        top = jnp.max(v, axis=0, keepdims=True)
        idx = jnp.min(jnp.where(v == top, iota8, N_GROUPS), axis=0, keepdims=True)
        return top, idx

    gl = lt[0:N_GROUPS]
    gmax, gidx = first_argmax(gl)
    gw = 1.0 / jnp.sum(jnp.exp(gl - gmax), axis=0, keepdims=True)
    esel = jnp.zeros((EXPERTS_PER_GROUP, tm), F32)
    for g in range(N_GROUPS):
        lo = N_GROUPS + g * EXPERTS_PER_GROUP
        esel = jnp.where(gidx == g, lt[lo:lo + EXPERTS_PER_GROUP], esel)
    v1, i1 = first_argmax(esel)
    rest = jnp.where(iota8 == i1, -jnp.inf, esel)
    v2, i2 = first_argmax(rest)
    t = jnp.exp(v2 - v1)
    den = 1.0 + t
    e1 = gidx * EXPERTS_PER_GROUP + i1
    e2 = gidx * EXPERTS_PER_GROUP + i2
    eid_ref[0:1, :] = e1
    eid_ref[1:2, :] = e2
    gate_ref[0:1, :] = gw * (1.0 / den)
    gate_ref[1:2, :] = gw * (t / den)

    iota_e = lax.broadcasted_iota(jnp.int32, (N_EXPERTS, tm), 0)
    oh1 = iota_e == e1
    oh2 = iota_e == e2
    both = jnp.where(oh1, 1.0, jnp.where(oh2, 1.0, 0.0)).astype(BF16)
    prefix = jnp.dot(both, tri_ref[...], preferred_element_type=F32)
    run = run_scr[...]
    base = prefix - 1.0 + run[:, 0:1]
    rank_ref[0:1, :] = jnp.sum(jnp.where(oh1, base, 0.0), axis=0, keepdims=True).astype(jnp.int32)
    rank_ref[1:2, :] = jnp.sum(jnp.where(oh2, base, 0.0), axis=0, keepdims=True).astype(jnp.int32)
    run = run + prefix[:, tm - 1:tm]
    run_scr[...] = run
    cnt_ref[...] = run


def _out_proj(xt, o_gla, o_att, wo, ag, g2, wr, br, tm):
    N = xt.shape[0]
    const = lambda shape: pl.BlockSpec(shape, lambda i: (0,) * len(shape))
    rows = lambda width: pl.BlockSpec((tm, width), lambda i: (i, 0))
    cols = pl.BlockSpec((TOP_K, tm), lambda i: (0, i))
    tri = jnp.asarray(np.triu(np.ones((tm, tm), np.float32)), BF16)
    return pl.pallas_call(
        _out_proj_kernel,
        out_shape=(jax.ShapeDtypeStruct((N, D_MODEL), F32),
                   jax.ShapeDtypeStruct((N, D_MODEL), F32),
                   jax.ShapeDtypeStruct((TOP_K, N), jnp.int32),
                   jax.ShapeDtypeStruct((TOP_K, N), F32),
                   jax.ShapeDtypeStruct((TOP_K, N), jnp.int32),
                   jax.ShapeDtypeStruct((N_EXPERTS, LANES), F32)),
        grid=(N // tm,),
        in_specs=[rows(D_MODEL), rows(GLA_WIDTH), rows(ATT_WIDTH), const((D_MODEL, D_MODEL)),
                  const((1, ATT_WIDTH)), const((1, D_MODEL)), const((LANES, D_MODEL)),
                  const((LANES, 1)), const((tm, tm))],
        out_specs=(rows(D_MODEL), rows(D_MODEL), cols, cols, cols, const((N_EXPERTS, LANES))),
        scratch_shapes=[pltpu.VMEM((N_EXPERTS, LANES), F32)],
        compiler_params=_cparams(("arbitrary",)),
        name="out_proj",
    )(xt, o_gla, o_att, wo, ag, g2, wr, br, tri)


def _moe_ffn_kernel(be_ref, nu_ref, xs_ref, wg_ref, wu_ref, wd_ref, ys_ref, wg_s, wu_s, wd_s):
    i = pl.program_id(0)
    used = i < nu_ref[0]

    @pl.when(used)
    def _():
        new_expert = jnp.logical_or(i == 0, be_ref[i] != be_ref[jnp.maximum(i - 1, 0)])

        @pl.when(new_expert)
        def _():
            wg_s[...] = wg_ref[0].astype(BF16)
            wu_s[...] = wu_ref[0].astype(BF16)
            wd_s[...] = wd_ref[0].astype(BF16)

        xb = xs_ref[...].astype(BF16)
        hg = jnp.dot(xb, wg_s[...], preferred_element_type=F32)
        hu = jnp.dot(xb, wu_s[...], preferred_element_type=F32)
        hm = (hg * _sigmoid(hg) * hu).astype(BF16)
        ys_ref[...] = jnp.dot(hm, wd_s[...], preferred_element_type=F32)

    @pl.when(jnp.logical_not(used))
    def _():
        ys_ref[...] = jnp.zeros_like(ys_ref)


def _moe_ffn(blk_expert, n_used, xs, w_gate, w_up, w_down):
    P = xs.shape[0]
    nblk = P // MOE_BLOCK
    grid_spec = pltpu.PrefetchScalarGridSpec(
        num_scalar_prefetch=2,
        grid=(nblk,),
        in_specs=[pl.BlockSpec((MOE_BLOCK, D_MODEL), lambda i, be, nu: (i, 0)),
                  pl.BlockSpec((1, D_MODEL, D_EXPERT), lambda i, be, nu: (be[i], 0, 0)),
                  pl.BlockSpec((1, D_MODEL, D_EXPERT), lambda i, be, nu: (be[i], 0, 0)),
                  pl.BlockSpec((1, D_EXPERT, D_MODEL), lambda i, be, nu: (be[i], 0, 0))],
        out_specs=pl.BlockSpec((MOE_BLOCK, D_MODEL), lambda i, be, nu: (i, 0)),
        scratch_shapes=[pltpu.VMEM((D_MODEL, D_EXPERT), BF16),
                        pltpu.VMEM((D_MODEL, D_EXPERT), BF16),
                        pltpu.VMEM((D_EXPERT, D_MODEL), BF16)],
    )
    return pl.pallas_call(
        _moe_ffn_kernel,
        out_shape=jax.ShapeDtypeStruct((P, D_MODEL), F32),
        grid_spec=grid_spec,
        compiler_params=_cparams(("arbitrary",)),
        name="moe_ffn",
    )(blk_expert, n_used, xs, w_gate, w_up, w_down)


def _combine_kernel(h_ref, y0_ref, y1_ref, gate_ref, fg_ref, o_ref, *, final):
    g = gate_ref[...]
    hh = h_ref[...] + y0_ref[...] * g[:, 0:1] + y1_ref[...] * g[:, 1:2]
    o_ref[...] = _rms(hh, fg_ref[...]) if final else hh


def _combine(h, y0, y1, gate_t, fg, tm, final):
    N = h.shape[0]
    rows = pl.BlockSpec((tm, D_MODEL), lambda i: (i, 0))
    return pl.pallas_call(
        functools.partial(_combine_kernel, final=final),
        out_shape=jax.ShapeDtypeStruct((N, D_MODEL), F32),
        grid=(N // tm,),
        in_specs=[rows, rows, rows, pl.BlockSpec((tm, TOP_K), lambda i: (i, 0)),
                  pl.BlockSpec((1, D_MODEL), lambda i: (0, 0))],
        out_specs=rows,
        compiler_params=_cparams(("arbitrary",)),
        name="combine",
    )(h, y0, y1, gate_t, fg)


def _reorder_w_in(w_in):
    n_gla = 2 * GLA_KEY_WIDTH + 2 * GLA_WIDTH
    z0 = n_gla
    a0 = z0 + 2 * GLA_GATE_RANK
    pad = jnp.zeros((D_MODEL, Z_PAD - 2 * GLA_GATE_RANK), w_in.dtype)
    return jnp.concatenate([w_in[:, :n_gla], w_in[:, a0:], w_in[:, z0:a0], pad], axis=1)


def kernel(x, norm1_gain, w_in, gla_up_fwd, gla_up_fwd_bias, gla_up_bwd, gla_up_bwd_bias, gla_out_gain, q_norm_gain, k_norm_gain, att_out_gain, w_out, norm2_gain, w_group, b_group, w_expert, b_expert, w_gate, w_up, w_down, final_gain):
    B, T, D = x.shape
    N = B * T
    depth = norm1_gain.shape[0]
    tm = min(512, T)
    tq = 128
    key_block = min(512, T)
    h = x.reshape(N, D)
    cos, se, so = _rope_tables(T)
    head_mean = jnp.asarray(
        np.kron(np.eye(ATT_HEADS, dtype=np.float32),
                np.full((ATT_HEAD_DIM, ATT_HEAD_DIM), 1.0 / ATT_HEAD_DIM, np.float32)), BF16)
    for l in range(depth):
        w = _reorder_w_in(w_in[l]).astype(BF16)
        r = GLA_GATE_RANK
        up = jnp.zeros((Z_PAD, 2 * GLA_KEY_WIDTH), F32)
        up = up.at[:r, :GLA_KEY_WIDTH].set(gla_up_fwd[l]).at[r:2 * r, GLA_KEY_WIDTH:].set(gla_up_bwd[l])
        upb = jnp.concatenate([gla_up_fwd_bias[l], gla_up_bwd_bias[l]])[None, :]
        qg = jnp.tile(q_norm_gain[l], ATT_HEADS)[None, :]
        kg = jnp.tile(k_norm_gain[l], ATT_KV_HEADS)[None, :]
        gq, gk, gv, gg, laf, lab, aq, akt, av = _in_proj(
            h, B, T, tm, norm1_gain[l][None, :], w, up.astype(BF16), upb, qg, kg, head_mean, cos, se, so)
        o_gla = _gla(gq, gk, gv, gg, laf, lab, gla_out_gain[l][None, :], B, T)
        o_att = _attn(aq, akt, av, B, T, tq, key_block)

        wr = jnp.zeros((LANES, D), F32)
        wr = wr.at[:N_GROUPS].set(w_group[l].T).at[N_GROUPS:N_GROUPS + N_EXPERTS].set(w_expert[l].T)
        br = jnp.zeros((LANES, 1), F32)
        br = br.at[:N_GROUPS, 0].set(b_group[l]).at[N_GROUPS:N_GROUPS + N_EXPERTS, 0].set(b_expert[l])
        h, xn, eid, gate, rank, cnt = _out_proj(
            h, o_gla, o_att, w_out[l].astype(BF16), att_out_gain[l][None, :], norm2_gain[l][None, :],
            wr.astype(BF16), br, tm)

        counts = cnt[:, 0].astype(jnp.int32)
        padded = (counts + MOE_BLOCK - 1) // MOE_BLOCK * MOE_BLOCK
        pad_ends = jnp.cumsum(padded)
        pad_starts = pad_ends - padded
        dest = pad_starts[eid] + rank
        P = N * TOP_K + N_EXPERTS * MOE_BLOCK
        nblk = P // MOE_BLOCK
        blk_start = jnp.arange(nblk, dtype=jnp.int32) * MOE_BLOCK
        blk_expert = jnp.minimum(jnp.searchsorted(pad_ends, blk_start, side='right'),
                                 N_EXPERTS - 1).astype(jnp.int32)
        n_used = (pad_ends[-1:] // MOE_BLOCK).astype(jnp.int32)

        tok = jnp.broadcast_to(jnp.arange(N, dtype=jnp.int32)[None, :], (TOP_K, N))
        slot_tok = jnp.zeros((P,), jnp.int32).at[dest.reshape(-1)].set(tok.reshape(-1))
        xs = xn[slot_tok]
        ys = _moe_ffn(blk_expert, n_used, xs, w_gate[l], w_up[l], w_down[l])
        h = _combine(h, ys[dest[0]], ys[dest[1]], gate.T, final_gain[None, :], tm, l == depth - 1)
    return h.reshape(B, T, D)
```

```python
import functools

import numpy as np
import jax
import jax.numpy as jnp
from jax import lax
from jax.experimental import pallas as pl
from jax.experimental.pallas import tpu as pltpu

F32 = jnp.float32
BF16 = jnp.bfloat16

D_MODEL = 1024
EPS = 1e-6
GRID_W = 64

GLA_HEADS = 4
GLA_DK = 64
GLA_DV = 128
GLA_KEY_WIDTH = GLA_HEADS * GLA_DK
GLA_WIDTH = GLA_HEADS * GLA_DV
GLA_GATE_RANK = 16
GLA_GATE_NORM = 16.0
GLA_CHUNK = 64

ATT_HEADS = 8
ATT_KV_HEADS = 2
ATT_HEAD_DIM = 64
ATT_GROUP = ATT_HEADS // ATT_KV_HEADS
ATT_WIDTH = ATT_HEADS * ATT_HEAD_DIM
ATT_KV_WIDTH = ATT_KV_HEADS * ATT_HEAD_DIM
ROPE_THETA = 10000.0
LOG2_E = 1.4426950408889634

N_GROUPS = 8
EXPERTS_PER_GROUP = 8
N_EXPERTS = N_GROUPS * EXPERTS_PER_GROUP
TOP_K = 2
D_EXPERT = 512
MOE_BLOCK = 128

LANES = 128
Z_PAD = LANES

_OFF_GQ = 0
_OFF_GK = _OFF_GQ + GLA_KEY_WIDTH
_OFF_GV = _OFF_GK + GLA_KEY_WIDTH
_OFF_GG = _OFF_GV + GLA_WIDTH
_OFF_AQ = _OFF_GG + GLA_WIDTH
_OFF_AK = _OFF_AQ + ATT_WIDTH
_OFF_AV = _OFF_AK + ATT_KV_WIDTH
_OFF_Z = _OFF_AV + ATT_KV_WIDTH
D_IN_PAD = _OFF_Z + Z_PAD

VMEM_LIMIT = 56 * 1024 * 1024


def _cparams(semantics):
    return pltpu.CompilerParams(dimension_semantics=semantics, vmem_limit_bytes=VMEM_LIMIT)


def _rms(x, gain):
    return x * lax.rsqrt(jnp.mean(x * x, axis=-1, keepdims=True) + EPS) * gain


def _sigmoid(x):
    return 1.0 / (1.0 + jnp.exp(-x))


def _head_norm_rope(x, head_mean, gain, cos, sin_even, sin_odd):
    w = x.shape[1]
    ms = jnp.dot((x * x).astype(BF16), head_mean, preferred_element_type=F32)
    xn = x * lax.rsqrt(ms + EPS) * gain
    reps = w // LANES
    tile = lambda t: t if reps == 1 else jnp.concatenate([t] * reps, axis=1)
    nxt = pltpu.roll(xn, w - 1, 1)
    prv = pltpu.roll(xn, 1, 1)
    return xn * tile(cos) + nxt * tile(sin_even) + prv * tile(sin_odd)


def _in_proj_kernel(x_ref, g1_ref, w_ref, up_ref, upb_ref, qg_ref, kg_ref, hm_ref,
                    cos_ref, se_ref, so_ref,
                    gq_ref, gk_ref, gv_ref, gg_ref, laf_ref, lab_ref, aq_ref, akt_ref, av_ref):
    u = _rms(x_ref[...], g1_ref[...])
    res = jnp.dot(u.astype(BF16), w_ref[...], preferred_element_type=F32)
    gq_ref[...] = res[:, _OFF_GQ:_OFF_GK].astype(BF16)
    gk_ref[...] = res[:, _OFF_GK:_OFF_GV].astype(BF16)
    gv_ref[...] = res[:, _OFF_GV:_OFF_GG].astype(BF16)
    gg_ref[...] = res[:, _OFF_GG:_OFF_AQ].astype(BF16)
    av_ref[...] = res[:, _OFF_AV:_OFF_Z].astype(BF16)

    z = res[:, _OFF_Z:D_IN_PAD].astype(BF16)
    zl = jnp.dot(z, up_ref[...], preferred_element_type=F32) + upb_ref[...]
    la = (jnp.minimum(zl, 0.0) - jnp.log(1.0 + jnp.exp(-jnp.abs(zl)))) * (1.0 / GLA_GATE_NORM)
    laf_ref[...] = la[:, :GLA_KEY_WIDTH]
    lab_ref[...] = la[:, GLA_KEY_WIDTH:]

    cos, se, so = cos_ref[...], se_ref[...], so_ref[...]
    hm = hm_ref[...]
    q = _head_norm_rope(res[:, _OFF_AQ:_OFF_AK], hm, qg_ref[...], cos, se, so)
    aq_ref[...] = (q * (ATT_HEAD_DIM ** -0.5 * LOG2_E)).astype(BF16)
    k = _head_norm_rope(res[:, _OFF_AK:_OFF_AV], hm[:ATT_KV_WIDTH, :ATT_KV_WIDTH], kg_ref[...], cos, se, so)
    akt_ref[0] = k.T.astype(BF16)


def _rope_tables(T):
    t = np.arange(T)
    row = (t // GRID_W).astype(np.float32)
    col = (t % GRID_W).astype(np.float32)
    axis_dim = ATT_HEAD_DIM // 2
    inv_freq = (ROPE_THETA ** (-np.arange(0, axis_dim, 2, dtype=np.float32) / axis_dim)).astype(np.float32)
    ang = np.concatenate([row[:, None] * inv_freq, col[:, None] * inv_freq], axis=-1)
    ang = np.repeat(ang, 2, axis=1)
    ang = np.tile(ang, (1, LANES // ATT_HEAD_DIM))
    even = (np.arange(LANES) % 2 == 0)[None, :]
    cos = np.cos(ang)
    sin = np.sin(ang)
    return (jnp.asarray(cos, F32), jnp.asarray(np.where(even, -sin, 0.0), F32),
            jnp.asarray(np.where(even, 0.0, sin), F32))


def _in_proj(xt, B, T, tm, g1, w, up, upb, qg, kg, hm, cos, se, so):
    N = xt.shape[0]
    tpb = T // tm
    const = lambda shape: pl.BlockSpec(shape, lambda i: (0,) * len(shape))
    rows = lambda width: pl.BlockSpec((tm, width), lambda i: (i, 0))
    pos = pl.BlockSpec((tm, LANES), lambda i: (i % tpb, 0))
    out_shape = (
        jax.ShapeDtypeStruct((N, GLA_KEY_WIDTH), BF16),
        jax.ShapeDtypeStruct((N, GLA_KEY_WIDTH), BF16),
        jax.ShapeDtypeStruct((N, GLA_WIDTH), BF16),
        jax.ShapeDtypeStruct((N, GLA_WIDTH), BF16),
        jax.ShapeDtypeStruct((N, GLA_KEY_WIDTH), F32),
        jax.ShapeDtypeStruct((N, GLA_KEY_WIDTH), F32),
        jax.ShapeDtypeStruct((N, ATT_WIDTH), BF16),
        jax.ShapeDtypeStruct((B, ATT_KV_WIDTH, T), BF16),
        jax.ShapeDtypeStruct((N, ATT_KV_WIDTH), BF16),
    )
    out_specs = (
        rows(GLA_KEY_WIDTH), rows(GLA_KEY_WIDTH), rows(GLA_WIDTH), rows(GLA_WIDTH),
        rows(GLA_KEY_WIDTH), rows(GLA_KEY_WIDTH), rows(ATT_WIDTH),
        pl.BlockSpec((1, ATT_KV_WIDTH, tm), lambda i: (i // tpb, 0, i % tpb)),
        rows(ATT_KV_WIDTH),
    )
    return pl.pallas_call(
        _in_proj_kernel,
        out_shape=out_shape,
        grid=(N // tm,),
        in_specs=[rows(D_MODEL), const((1, D_MODEL)), const((D_MODEL, D_IN_PAD)),
                  const((Z_PAD, 2 * GLA_KEY_WIDTH)), const((1, 2 * GLA_KEY_WIDTH)),
                  const((1, ATT_WIDTH)), const((1, ATT_KV_WIDTH)), const((ATT_WIDTH, ATT_WIDTH)),
                  pos, pos, pos],
        out_specs=out_specs,
        compiler_params=_cparams(("arbitrary",)),
        name="in_proj",
    )(xt, g1, w, up, upb, qg, kg, hm, cos, se, so)


def _gla_kernel(q_ref, k_ref, v_ref, g_ref, laf_ref, lab_ref, gain_ref, o_ref, part_scr):
    C = GLA_CHUNK
    T = q_ref.shape[0]
    n_chunks = T // C
    row = lax.broadcasted_iota(jnp.int32, (C, C), 0)
    col = lax.broadcasted_iota(jnp.int32, (C, C), 1)
    tril = row >= col
    triu = row <= col
    gain = gain_ref[...]

    def chunk(r0, state, la_ref, mask, i_ref, i_last):
        q2 = q_ref[pl.ds(r0, C), :].astype(F32) * (GLA_DK ** -0.5)
        k2 = k_ref[pl.ds(r0, C), :].astype(F32)
        v2 = v_ref[pl.ds(r0, C), :]
        la = la_ref[pl.ds(r0, C), :]
        tri = mask.astype(BF16)
        la_hi = la.astype(BF16)
        la_lo = (la - la_hi.astype(F32)).astype(BF16)
        b = (jnp.dot(tri, la_hi, preferred_element_type=F32)
             + jnp.dot(tri, la_lo, preferred_element_type=F32))
        b_ref = b[i_ref:i_ref + 1, :]
        b_last = b[i_last:i_last + 1, :]
        qf = (q2 * jnp.exp(b - b_ref)).astype(BF16)
        kf = (k2 * jnp.exp(b_ref - b)).astype(BF16)
        qi = (q2 * jnp.exp(b)).astype(BF16)
        kl = (k2 * jnp.exp(b_last - b)).astype(BF16)
        decay = jnp.exp(b_last)
        outs, new_state = [], []
        for h in range(2):
            ks = slice(h * GLA_DK, (h + 1) * GLA_DK)
            vh = v2[:, h * GLA_DV:(h + 1) * GLA_DV]
            sc = lax.dot_general(qf[:, ks], kf[:, ks], (((1,), (1,)), ((), ())),
                                 preferred_element_type=F32)
            sc = jnp.where(mask, sc, 0.0).astype(BF16)
            o = jnp.dot(sc, vh, preferred_element_type=F32)
            o = o + lax.dot_general(qi[:, ks], state[h].astype(BF16), (((1,), (1,)), ((), ())),
                                    preferred_element_type=F32)
            kv_t = lax.dot_general(vh, kl[:, ks], (((0,), (0,)), ((), ())),
                                   preferred_element_type=F32)
            new_state.append(decay[:, ks] * state[h] + kv_t)
            outs.append(o)
        return jnp.concatenate(outs, axis=1), tuple(new_state)

    zero_state = tuple(jnp.zeros((GLA_DV, GLA_DK), F32) for _ in range(2))

    def both_directions(n, carry):
        rf = pl.multiple_of(n * C, C)
        rb = pl.multiple_of((n_chunks - 1 - n) * C, C)
        o_f, sf = chunk(rf, carry[0], laf_ref, tril, C // 2 - 1, C - 1)
        o_b, sb = chunk(rb, carry[1], lab_ref, triu, C // 2, 0)
        return (rf, o_f), (rb, o_b), (sf, sb)

    def finalize(r0, o):
        g = g_ref[pl.ds(r0, C), :].astype(F32)
        normed = jnp.concatenate(
            [_rms(o[:, h * GLA_DV:(h + 1) * GLA_DV], gain) for h in range(2)], axis=1)
        o_ref[pl.ds(r0, C), :] = (normed * (g * _sigmoid(g))).astype(BF16)

    def first_visits(n, carry):
        (rf, o_f), (rb, o_b), carry = both_directions(n, carry)
        part_scr[pl.ds(rf, C), :] = o_f
        part_scr[pl.ds(rb, C), :] = o_b
        return carry

    def second_visits(n, carry):
        (rf, o_f), (rb, o_b), carry = both_directions(n, carry)
        finalize(rf, part_scr[pl.ds(rf, C), :] + o_f)
        finalize(rb, part_scr[pl.ds(rb, C), :] + o_b)
        return carry

    carry = lax.fori_loop(0, n_chunks // 2, first_visits, (zero_state, zero_state), unroll=2)
    lax.fori_loop(n_chunks // 2, n_chunks, second_visits, carry, unroll=2)


def _gla(gq, gk, gv, gg, laf, lab, gain, B, T):
    N = gq.shape[0]
    pairs = GLA_HEADS // 2
    kspec = pl.BlockSpec((T, 2 * GLA_DK), lambda b, p: (b, p))
    vspec = pl.BlockSpec((T, 2 * GLA_DV), lambda b, p: (b, p))
    return pl.pallas_call(
        _gla_kernel,
        out_shape=jax.ShapeDtypeStruct((N, GLA_WIDTH), BF16),
        grid=(B, pairs),
        in_specs=[kspec, kspec, vspec, vspec, kspec, kspec,
                  pl.BlockSpec((1, GLA_DV), lambda b, p: (0, 0))],
        out_specs=vspec,
        scratch_shapes=[pltpu.VMEM((T, 2 * GLA_DV), F32)],
        compiler_params=_cparams(("arbitrary", "arbitrary")),
        name="gla",
    )(gq, gk, gv, gg, laf, lab, gain)


def _attn_kernel(q_ref, kt_ref, v_ref, o_ref, vext_scr, s0_scr, s1_scr, p0_scr, p1_scr, m0_scr, m1_scr,
                 *, tq, key_block):
    j = pl.program_id(1)
    T = v_ref.shape[0]
    nq = T // tq
    dh = ATT_HEAD_DIM

    lane = lax.broadcasted_iota(jnp.int32, (T, ATT_KV_WIDTH), 1)
    own = (lane >= j * dh) & (lane < (j + 1) * dh)
    vext_scr[...] = jnp.where(own, v_ref[...], jnp.ones((), BF16))

    s_scr, p_scr, m_scr = (s0_scr, s1_scr), (p0_scr, p1_scr), (m0_scr, m1_scr)

    def scores(blk, slot):
        q = q_ref[pl.ds(pl.multiple_of(blk * tq, tq), tq), :]
        qs = jnp.concatenate([q[:, h * dh:(h + 1) * dh] for h in range(ATT_GROUP)], axis=0)
        m = jnp.full((ATT_GROUP * tq, LANES), -jnp.inf, F32)
        for kb in range(T // key_block):
            ks = slice(kb * key_block, (kb + 1) * key_block)
            s = jnp.dot(qs, kt_ref[0, :, ks], preferred_element_type=F32)
            s_scr[slot][:, ks] = s
            for c in range(key_block // LANES):
                m = jnp.maximum(m, s[:, c * LANES:(c + 1) * LANES])
        m_scr[slot][...] = jnp.max(m, axis=-1, keepdims=True)

    def probs(slot):
        mx = m_scr[slot][...]
        for kb in range(T // key_block):
            ks = slice(kb * key_block, (kb + 1) * key_block)
            p_scr[slot][:, ks] = jnp.exp2(s_scr[slot][:, ks] - mx).astype(BF16)

    def out(blk, slot):
        acc = jnp.dot(p_scr[slot][...], vext_scr[...], preferred_element_type=F32)
        first = j == 0
        num = jnp.where(first, acc[:, :dh], acc[:, dh:])
        den = jnp.where(first, acc[:, dh:dh + 1], acc[:, 0:1])
        o = num / den
        o_ref[pl.ds(pl.multiple_of(blk * tq, tq), tq), :] = jnp.concatenate(
            [o[h * tq:(h + 1) * tq] for h in range(ATT_GROUP)], axis=1).astype(BF16)

    scores(0, 0)
    scores(1, 1)
    probs(0)

    def steady(it, carry):
        blk = 2 * it
        out(blk - 2, 0)
        scores(blk, 0)
        probs(1)
        out(blk - 1, 1)
        scores(blk + 1, 1)
        probs(0)
        return carry

    lax.fori_loop(1, nq // 2, steady, 0)
    probs(1)
    out(nq - 2, 0)
    out(nq - 1, 1)


def _attn(aq, akt, av, B, T, tq, key_block):
    N = aq.shape[0]
    assert T % (2 * tq) == 0
    gw = ATT_GROUP * ATT_HEAD_DIM
    rows = ATT_GROUP * tq
    return pl.pallas_call(
        functools.partial(_attn_kernel, tq=tq, key_block=key_block),
        out_shape=jax.ShapeDtypeStruct((N, ATT_WIDTH), BF16),
        grid=(B, ATT_KV_HEADS),
        in_specs=[pl.BlockSpec((T, gw), lambda b, j: (b, j)),
                  pl.BlockSpec((1, ATT_HEAD_DIM, T), lambda b, j: (b, j, 0)),
                  pl.BlockSpec((T, ATT_KV_WIDTH), lambda b, j: (b, 0))],
        out_specs=pl.BlockSpec((T, gw), lambda b, j: (b, j)),
        scratch_shapes=[pltpu.VMEM((T, ATT_KV_WIDTH), BF16),
                        pltpu.VMEM((rows, T), F32), pltpu.VMEM((rows, T), F32),
                        pltpu.VMEM((rows, T), BF16), pltpu.VMEM((rows, T), BF16),
                        pltpu.VMEM((rows, 1), F32), pltpu.VMEM((rows, 1), F32)],
        compiler_params=_cparams(("arbitrary", "arbitrary")),
        name="attn",
    )(aq, akt, av)


def _out_proj_kernel(x_ref, og_ref, oa_ref, wo_ref, ag_ref, g2_ref, wr_ref, br_ref, tri_ref,
                     h_ref, xn_ref, eid_ref, gate_ref, rank_ref, cnt_ref, run_scr):
    tm = x_ref.shape[0]

    @pl.when(pl.program_id(0) == 0)
    def _():
        run_scr[...] = jnp.zeros_like(run_scr)

    oan = _rms(oa_ref[...].astype(F32), ag_ref[...]).astype(BF16)
    y = (jnp.dot(og_ref[...], wo_ref[:GLA_WIDTH, :], preferred_element_type=F32)
         + jnp.dot(oan, wo_ref[GLA_WIDTH:, :], preferred_element_type=F32))
    h = x_ref[...] + y
    h_ref[...] = h
    xn = _rms(h, g2_ref[...])
    xn_ref[...] = xn

    lt = lax.dot_general(wr_ref[...], xn.astype(BF16), (((1,), (1,)), ((), ())),
                         preferred_element_type=F32) + br_ref[...]
    iota8 = lax.broadcasted_iota(jnp.int32, (N_GROUPS, tm), 0)

    def first_argmax(v):
        top = jnp.max(v, axis=0, keepdims=True)
        idx = jnp.min(jnp.where(v == top, iota8, N_GROUPS), axis=0, keepdims=True)
        return top, idx

    gl = lt[0:N_GROUPS]
    gmax, gidx = first_argmax(gl)
    gw = 1.0 / jnp.sum(jnp.exp(gl - gmax), axis=0, keepdims=True)
    esel = jnp.zeros((EXPERTS_PER_GROUP, tm), F32)
    for g in range(N_GROUPS):
        lo = N_GROUPS + g * EXPERTS_PER_GROUP
        esel = jnp.where(gidx == g, lt[lo:lo + EXPERTS_PER_GROUP], esel)
    v1, i1 = first_argmax(esel)
    rest = jnp.where(iota8 == i1, -jnp.inf, esel)
    v2, i2 = first_argmax(rest)
    t = jnp.exp(v2 - v1)
    den = 1.0 + t
    e1 = gidx * EXPERTS_PER_GROUP + i1
    e2 = gidx * EXPERTS_PER_GROUP + i2
    eid_ref[0:1, :] = e1
    eid_ref[1:2, :] = e2
    gate_ref[0:1, :] = gw * (1.0 / den)
    gate_ref[1:2, :] = gw * (t / den)

    iota_e = lax.broadcasted_iota(jnp.int32, (N_EXPERTS, tm), 0)
    oh1 = iota_e == e1
    oh2 = iota_e == e2
    both = jnp.where(oh1, 1.0, jnp.where(oh2, 1.0, 0.0)).astype(BF16)
    prefix = jnp.dot(both, tri_ref[...], preferred_element_type=F32)
    run = run_scr[...]
    base = prefix - 1.0 + run[:, 0:1]
    rank_ref[0:1, :] = jnp.sum(jnp.where(oh1, base, 0.0), axis=0, keepdims=True).astype(jnp.int32)
    rank_ref[1:2, :] = jnp.sum(jnp.where(oh2, base, 0.0), axis=0, keepdims=True).astype(jnp.int32)
    run = run + prefix[:, tm - 1:tm]
    run_scr[...] = run
    cnt_ref[...] = run


def _out_proj(xt, o_gla, o_att, wo, ag, g2, wr, br, tm):
    N = xt.shape[0]
    const = lambda shape: pl.BlockSpec(shape, lambda i: (0,) * len(shape))
    rows = lambda width: pl.BlockSpec((tm, width), lambda i: (i, 0))
    cols = pl.BlockSpec((TOP_K, tm), lambda i: (0, i))
    tri = jnp.asarray(np.triu(np.ones((tm, tm), np.float32)), BF16)
    return pl.pallas_call(
        _out_proj_kernel,
        out_shape=(jax.ShapeDtypeStruct((N, D_MODEL), F32),
                   jax.ShapeDtypeStruct((N, D_MODEL), F32),
                   jax.ShapeDtypeStruct((TOP_K, N), jnp.int32),
                   jax.ShapeDtypeStruct((TOP_K, N), F32),
                   jax.ShapeDtypeStruct((TOP_K, N), jnp.int32),
                   jax.ShapeDtypeStruct((N_EXPERTS, LANES), F32)),
        grid=(N // tm,),
        in_specs=[rows(D_MODEL), rows(GLA_WIDTH), rows(ATT_WIDTH), const((D_MODEL, D_MODEL)),
                  const((1, ATT_WIDTH)), const((1, D_MODEL)), const((LANES, D_MODEL)),
                  const((LANES, 1)), const((tm, tm))],
        out_specs=(rows(D_MODEL), rows(D_MODEL), cols, cols, cols, const((N_EXPERTS, LANES))),
        scratch_shapes=[pltpu.VMEM((N_EXPERTS, LANES), F32)],
        compiler_params=_cparams(("arbitrary",)),
        name="out_proj",
    )(xt, o_gla, o_att, wo, ag, g2, wr, br, tri)


def _moe_ffn_kernel(be_ref, nu_ref, xs_ref, wg_ref, wu_ref, wd_ref, ys_ref, wg_s, wu_s, wd_s):
    i = pl.program_id(0)
    used = i < nu_ref[0]

    @pl.when(used)
    def _():
        new_expert = jnp.logical_or(i == 0, be_ref[i] != be_ref[jnp.maximum(i - 1, 0)])

        @pl.when(new_expert)
        def _():
            wg_s[...] = wg_ref[0].astype(BF16)
            wu_s[...] = wu_ref[0].astype(BF16)
            wd_s[...] = wd_ref[0].astype(BF16)

        xb = xs_ref[...].astype(BF16)
        hg = jnp.dot(xb, wg_s[...], preferred_element_type=F32)
        hu = jnp.dot(xb, wu_s[...], preferred_element_type=F32)
        hm = (hg * _sigmoid(hg) * hu).astype(BF16)
        ys_ref[...] = jnp.dot(hm, wd_s[...], preferred_element_type=F32)

    @pl.when(jnp.logical_not(used))
    def _():
        ys_ref[...] = jnp.zeros_like(ys_ref)


def _moe_ffn(blk_expert, n_used, xs, w_gate, w_up, w_down):
    P = xs.shape[0]
    nblk = P // MOE_BLOCK
    grid_spec = pltpu.PrefetchScalarGridSpec(
        num_scalar_prefetch=2,
        grid=(nblk,),
        in_specs=[pl.BlockSpec((MOE_BLOCK, D_MODEL), lambda i, be, nu: (i, 0)),
                  pl.BlockSpec((1, D_MODEL, D_EXPERT), lambda i, be, nu: (be[i], 0, 0)),
                  pl.BlockSpec((1, D_MODEL, D_EXPERT), lambda i, be, nu: (be[i], 0, 0)),
                  pl.BlockSpec((1, D_EXPERT, D_MODEL), lambda i, be, nu: (be[i], 0, 0))],
        out_specs=pl.BlockSpec((MOE_BLOCK, D_MODEL), lambda i, be, nu: (i, 0)),
        scratch_shapes=[pltpu.VMEM((D_MODEL, D_EXPERT), BF16),
                        pltpu.VMEM((D_MODEL, D_EXPERT), BF16),
                        pltpu.VMEM((D_EXPERT, D_MODEL), BF16)],
    )
    return pl.pallas_call(
        _moe_ffn_kernel,
        out_shape=jax.ShapeDtypeStruct((P, D_MODEL), F32),
        grid_spec=grid_spec,
        compiler_params=_cparams(("arbitrary",)),
        name="moe_ffn",
    )(blk_expert, n_used, xs, w_gate, w_up, w_down)


def _combine_kernel(h_ref, y0_ref, y1_ref, gate_ref, fg_ref, o_ref, *, final):
    g = gate_ref[...]
    hh = h_ref[...] + y0_ref[...] * g[:, 0:1] + y1_ref[...] * g[:, 1:2]
    o_ref[...] = _rms(hh, fg_ref[...]) if final else hh


def _combine(h, y0, y1, gate_t, fg, tm, final):
    N = h.shape[0]
    rows = pl.BlockSpec((tm, D_MODEL), lambda i: (i, 0))
    return pl.pallas_call(
        functools.partial(_combine_kernel, final=final),
        out_shape=jax.ShapeDtypeStruct((N, D_MODEL), F32),
        grid=(N // tm,),
        in_specs=[rows, rows, rows, pl.BlockSpec((tm, TOP_K), lambda i: (i, 0)),
                  pl.BlockSpec((1, D_MODEL), lambda i: (0, 0))],
        out_specs=rows,
        compiler_params=_cparams(("arbitrary",)),
        name="combine",
    )(h, y0, y1, gate_t, fg)


def _reorder_w_in(w_in):
    n_gla = 2 * GLA_KEY_WIDTH + 2 * GLA_WIDTH
    z0 = n_gla
    a0 = z0 + 2 * GLA_GATE_RANK
    pad = jnp.zeros((D_MODEL, Z_PAD - 2 * GLA_GATE_RANK), w_in.dtype)
    return jnp.concatenate([w_in[:, :n_gla], w_in[:, a0:], w_in[:, z0:a0], pad], axis=1)


def kernel(x, norm1_gain, w_in, gla_up_fwd, gla_up_fwd_bias, gla_up_bwd, gla_up_bwd_bias, gla_out_gain, q_norm_gain, k_norm_gain, att_out_gain, w_out, norm2_gain, w_group, b_group, w_expert, b_expert, w_gate, w_up, w_down, final_gain):
    B, T, D = x.shape
    N = B * T
    depth = norm1_gain.shape[0]
    tm = min(512, T)
    tq = 128
    key_block = min(512, T)
    h = x.reshape(N, D)
    cos, se, so = _rope_tables(T)
    head_mean = jnp.asarray(
        np.kron(np.eye(ATT_HEADS, dtype=np.float32),
                np.full((ATT_HEAD_DIM, ATT_HEAD_DIM), 1.0 / ATT_HEAD_DIM, np.float32)), BF16)
    for l in range(depth):
        w = _reorder_w_in(w_in[l]).astype(BF16)
        r = GLA_GATE_RANK
        up = jnp.zeros((Z_PAD, 2 * GLA_KEY_WIDTH), F32)
        up = up.at[:r, :GLA_KEY_WIDTH].set(gla_up_fwd[l]).at[r:2 * r, GLA_KEY_WIDTH:].set(gla_up_bwd[l])
        upb = jnp.concatenate([gla_up_fwd_bias[l], gla_up_bwd_bias[l]])[None, :]
        qg = jnp.tile(q_norm_gain[l], ATT_HEADS)[None, :]
        kg = jnp.tile(k_norm_gain[l], ATT_KV_HEADS)[None, :]
        gq, gk, gv, gg, laf, lab, aq, akt, av = _in_proj(
            h, B, T, tm, norm1_gain[l][None, :], w, up.astype(BF16), upb, qg, kg, head_mean, cos, se, so)
        o_gla = _gla(gq, gk, gv, gg, laf, lab, gla_out_gain[l][None, :], B, T)
        o_att = _attn(aq, akt, av, B, T, tq, key_block)

        wr = jnp.zeros((LANES, D), F32)
        wr = wr.at[:N_GROUPS].set(w_group[l].T).at[N_GROUPS:N_GROUPS + N_EXPERTS].set(w_expert[l].T)
        br = jnp.zeros((LANES, 1), F32)
        br = br.at[:N_GROUPS, 0].set(b_group[l]).at[N_GROUPS:N_GROUPS + N_EXPERTS, 0].set(b_expert[l])
        h, xn, eid, gate, rank, cnt = _out_proj(
            h, o_gla, o_att, w_out[l].astype(BF16), att_out_gain[l][None, :], norm2_gain[l][None, :],
            wr.astype(BF16), br, tm)

        counts = cnt[:, 0].astype(jnp.int32)
        padded = (counts + MOE_BLOCK - 1) // MOE_BLOCK * MOE_BLOCK
        pad_ends = jnp.cumsum(padded)
        pad_starts = pad_ends - padded
        dest = pad_starts[eid] + rank
        P = N * TOP_K + N_EXPERTS * MOE_BLOCK
        nblk = P // MOE_BLOCK
        blk_start = jnp.arange(nblk, dtype=jnp.int32) * MOE_BLOCK
        blk_expert = jnp.minimum(
            jnp.sum((pad_ends[None, :] <= blk_start[:, None]).astype(jnp.int32), axis=1), N_EXPERTS - 1)
        n_used = (pad_ends[-1:] // MOE_BLOCK).astype(jnp.int32)

        tok = jnp.broadcast_to(jnp.arange(N, dtype=jnp.int32)[None, :], (TOP_K, N))
        slot_tok = jnp.zeros((P,), jnp.int32).at[dest.reshape(-1)].set(tok.reshape(-1))
        xs = xn[slot_tok]
        ys = _moe_ffn(blk_expert, n_used, xs, w_gate[l], w_up[l], w_down[l])
        h = _combine(h, ys[dest[0]], ys[dest[1]], gate.T, final_gain[None, :], tm, l == depth - 1)
    return h.reshape(B, T, D)
```

```python
import functools

import numpy as np
import jax
import jax.numpy as jnp
from jax import lax
from jax.experimental import pallas as pl
from jax.experimental.pallas import tpu as pltpu

F32 = jnp.float32
BF16 = jnp.bfloat16

D_MODEL = 1024
EPS = 1e-6
GRID_W = 64

GLA_HEADS = 4
GLA_DK = 64
GLA_DV = 128
GLA_KEY_WIDTH = GLA_HEADS * GLA_DK
GLA_WIDTH = GLA_HEADS * GLA_DV
GLA_GATE_RANK = 16
GLA_GATE_NORM = 16.0
GLA_CHUNK = 64

ATT_HEADS = 8
ATT_KV_HEADS = 2
ATT_HEAD_DIM = 64
ATT_GROUP = ATT_HEADS // ATT_KV_HEADS
ATT_WIDTH = ATT_HEADS * ATT_HEAD_DIM
ATT_KV_WIDTH = ATT_KV_HEADS * ATT_HEAD_DIM
ROPE_THETA = 10000.0
LOG2_E = 1.4426950408889634

N_GROUPS = 8
EXPERTS_PER_GROUP = 8
N_EXPERTS = N_GROUPS * EXPERTS_PER_GROUP
TOP_K = 2
D_EXPERT = 512
MOE_BLOCK = 128

LANES = 128
Z_PAD = LANES

_OFF_GQ = 0
_OFF_GK = _OFF_GQ + GLA_KEY_WIDTH
_OFF_GV = _OFF_GK + GLA_KEY_WIDTH
_OFF_GG = _OFF_GV + GLA_WIDTH
_OFF_AQ = _OFF_GG + GLA_WIDTH
_OFF_AK = _OFF_AQ + ATT_WIDTH
_OFF_AV = _OFF_AK + ATT_KV_WIDTH
_OFF_Z = _OFF_AV + ATT_KV_WIDTH
D_IN_PAD = _OFF_Z + Z_PAD

VMEM_LIMIT = 56 * 1024 * 1024


def _cparams(semantics):
    return pltpu.CompilerParams(dimension_semantics=semantics, vmem_limit_bytes=VMEM_LIMIT)


def _rms(x, gain):
    return x * lax.rsqrt(jnp.mean(x * x, axis=-1, keepdims=True) + EPS) * gain


def _sigmoid(x):
    return 1.0 / (1.0 + jnp.exp(-x))


def _head_norm_rope(x, head_mean, gain, cos, sin_even, sin_odd):
    w = x.shape[1]
    ms = jnp.dot((x * x).astype(BF16), head_mean, preferred_element_type=F32)
    xn = x * lax.rsqrt(ms + EPS) * gain
    reps = w // LANES
    tile = lambda t: t if reps == 1 else jnp.concatenate([t] * reps, axis=1)
    nxt = pltpu.roll(xn, w - 1, 1)
    prv = pltpu.roll(xn, 1, 1)
    return xn * tile(cos) + nxt * tile(sin_even) + prv * tile(sin_odd)


def _in_proj_kernel(x_ref, g1_ref, w_ref, up_ref, upb_ref, qg_ref, kg_ref, hm_ref,
                    cos_ref, se_ref, so_ref,
                    gq_ref, gk_ref, gv_ref, gg_ref, laf_ref, lab_ref, aq_ref, akt_ref, av_ref):
    u = _rms(x_ref[...], g1_ref[...])
    res = jnp.dot(u.astype(BF16), w_ref[...], preferred_element_type=F32)
    gq_ref[...] = res[:, _OFF_GQ:_OFF_GK].astype(BF16)
    gk_ref[...] = res[:, _OFF_GK:_OFF_GV].astype(BF16)
    gv_ref[...] = res[:, _OFF_GV:_OFF_GG].astype(BF16)
    gg_ref[...] = res[:, _OFF_GG:_OFF_AQ].astype(BF16)
    av_ref[...] = res[:, _OFF_AV:_OFF_Z].astype(BF16)

    z = res[:, _OFF_Z:D_IN_PAD].astype(BF16)
    zl = jnp.dot(z, up_ref[...], preferred_element_type=F32) + upb_ref[...]
    la = (jnp.minimum(zl, 0.0) - jnp.log(1.0 + jnp.exp(-jnp.abs(zl)))) * (1.0 / GLA_GATE_NORM)
    laf_ref[...] = la[:, :GLA_KEY_WIDTH]
    lab_ref[...] = la[:, GLA_KEY_WIDTH:]

    cos, se, so = cos_ref[...], se_ref[...], so_ref[...]
    hm = hm_ref[...]
    q = _head_norm_rope(res[:, _OFF_AQ:_OFF_AK], hm, qg_ref[...], cos, se, so)
    aq_ref[...] = (q * (ATT_HEAD_DIM ** -0.5 * LOG2_E)).astype(BF16)
    k = _head_norm_rope(res[:, _OFF_AK:_OFF_AV], hm[:ATT_KV_WIDTH, :ATT_KV_WIDTH], kg_ref[...], cos, se, so)
    akt_ref[0] = k.T.astype(BF16)


def _rope_tables(T):
    t = np.arange(T)
    row = (t // GRID_W).astype(np.float32)
    col = (t % GRID_W).astype(np.float32)
    axis_dim = ATT_HEAD_DIM // 2
    inv_freq = (ROPE_THETA ** (-np.arange(0, axis_dim, 2, dtype=np.float32) / axis_dim)).astype(np.float32)
    ang = np.concatenate([row[:, None] * inv_freq, col[:, None] * inv_freq], axis=-1)
    ang = np.repeat(ang, 2, axis=1)
    ang = np.tile(ang, (1, LANES // ATT_HEAD_DIM))
    even = (np.arange(LANES) % 2 == 0)[None, :]
    cos = np.cos(ang)
    sin = np.sin(ang)
    return (jnp.asarray(cos, F32), jnp.asarray(np.where(even, -sin, 0.0), F32),
            jnp.asarray(np.where(even, 0.0, sin), F32))


def _in_proj(xt, B, T, tm, g1, w, up, upb, qg, kg, hm, cos, se, so):
    N = xt.shape[0]
    tpb = T // tm
    const = lambda shape: pl.BlockSpec(shape, lambda i: (0,) * len(shape))
    rows = lambda width: pl.BlockSpec((tm, width), lambda i: (i, 0))
    pos = pl.BlockSpec((tm, LANES), lambda i: (i % tpb, 0))
    out_shape = (
        jax.ShapeDtypeStruct((N, GLA_KEY_WIDTH), BF16),
        jax.ShapeDtypeStruct((N, GLA_KEY_WIDTH), BF16),
        jax.ShapeDtypeStruct((N, GLA_WIDTH), BF16),
        jax.ShapeDtypeStruct((N, GLA_WIDTH), BF16),
        jax.ShapeDtypeStruct((N, GLA_KEY_WIDTH), F32),
        jax.ShapeDtypeStruct((N, GLA_KEY_WIDTH), F32),
        jax.ShapeDtypeStruct((N, ATT_WIDTH), BF16),
        jax.ShapeDtypeStruct((B, ATT_KV_WIDTH, T), BF16),
        jax.ShapeDtypeStruct((N, ATT_KV_WIDTH), BF16),
    )
    out_specs = (
        rows(GLA_KEY_WIDTH), rows(GLA_KEY_WIDTH), rows(GLA_WIDTH), rows(GLA_WIDTH),
        rows(GLA_KEY_WIDTH), rows(GLA_KEY_WIDTH), rows(ATT_WIDTH),
        pl.BlockSpec((1, ATT_KV_WIDTH, tm), lambda i: (i // tpb, 0, i % tpb)),
        rows(ATT_KV_WIDTH),
    )
    return pl.pallas_call(
        _in_proj_kernel,
        out_shape=out_shape,
        grid=(N // tm,),
        in_specs=[rows(D_MODEL), const((1, D_MODEL)), const((D_MODEL, D_IN_PAD)),
                  const((Z_PAD, 2 * GLA_KEY_WIDTH)), const((1, 2 * GLA_KEY_WIDTH)),
                  const((1, ATT_WIDTH)), const((1, ATT_KV_WIDTH)), const((ATT_WIDTH, ATT_WIDTH)),
                  pos, pos, pos],
        out_specs=out_specs,
        compiler_params=_cparams(("arbitrary",)),
        name="in_proj",
    )(xt, g1, w, up, upb, qg, kg, hm, cos, se, so)


def _gla_kernel(q_ref, k_ref, v_ref, g_ref, laf_ref, lab_ref, gain_ref, o_ref, part_scr):
    C = GLA_CHUNK
    T = q_ref.shape[0]
    n_chunks = T // C
    row = lax.broadcasted_iota(jnp.int32, (C, C), 0)
    col = lax.broadcasted_iota(jnp.int32, (C, C), 1)
    tril = row >= col
    triu = row <= col
    gain = gain_ref[...]

    def chunk(r0, state, la_ref, mask, i_ref, i_last):
        q2 = q_ref[pl.ds(r0, C), :].astype(F32) * (GLA_DK ** -0.5)
        k2 = k_ref[pl.ds(r0, C), :].astype(F32)
        v2 = v_ref[pl.ds(r0, C), :]
        la = la_ref[pl.ds(r0, C), :]
        tri = mask.astype(BF16)
        la_hi = la.astype(BF16)
        la_lo = (la - la_hi.astype(F32)).astype(BF16)
        b = (jnp.dot(tri, la_hi, preferred_element_type=F32)
             + jnp.dot(tri, la_lo, preferred_element_type=F32))
        b_ref = b[i_ref:i_ref + 1, :]
        b_last = b[i_last:i_last + 1, :]
        qf = (q2 * jnp.exp(b - b_ref)).astype(BF16)
        kf = (k2 * jnp.exp(b_ref - b)).astype(BF16)
        qi = (q2 * jnp.exp(b)).astype(BF16)
        kl = (k2 * jnp.exp(b_last - b)).astype(BF16)
        decay = jnp.exp(b_last)
        outs, new_state = [], []
        for h in range(2):
            ks = slice(h * GLA_DK, (h + 1) * GLA_DK)
            vh = v2[:, h * GLA_DV:(h + 1) * GLA_DV]
            sc = lax.dot_general(qf[:, ks], kf[:, ks], (((1,), (1,)), ((), ())),
                                 preferred_element_type=F32)
            sc = jnp.where(mask, sc, 0.0).astype(BF16)
            o = jnp.dot(sc, vh, preferred_element_type=F32)
            o = o + lax.dot_general(qi[:, ks], state[h].astype(BF16), (((1,), (1,)), ((), ())),
                                    preferred_element_type=F32)
            kv_t = lax.dot_general(vh, kl[:, ks], (((0,), (0,)), ((), ())),
                                   preferred_element_type=F32)
            new_state.append(decay[:, ks] * state[h] + kv_t)
            outs.append(o)
        return jnp.concatenate(outs, axis=1), tuple(new_state)

    zero_state = tuple(jnp.zeros((GLA_DV, GLA_DK), F32) for _ in range(2))

    def both_directions(n, carry):
        rf = pl.multiple_of(n * C, C)
        rb = pl.multiple_of((n_chunks - 1 - n) * C, C)
        o_f, sf = chunk(rf, carry[0], laf_ref, tril, C // 2 - 1, C - 1)
        o_b, sb = chunk(rb, carry[1], lab_ref, triu, C // 2, 0)
        return (rf, o_f), (rb, o_b), (sf, sb)

    def finalize(r0, o):
        g = g_ref[pl.ds(r0, C), :].astype(F32)
        normed = jnp.concatenate(
            [_rms(o[:, h * GLA_DV:(h + 1) * GLA_DV], gain) for h in range(2)], axis=1)
        o_ref[pl.ds(r0, C), :] = (normed * (g * _sigmoid(g))).astype(BF16)

    def first_visits(n, carry):
        (rf, o_f), (rb, o_b), carry = both_directions(n, carry)
        part_scr[pl.ds(rf, C), :] = o_f
        part_scr[pl.ds(rb, C), :] = o_b
        return carry

    def second_visits(n, carry):
        (rf, o_f), (rb, o_b), carry = both_directions(n, carry)
        finalize(rf, part_scr[pl.ds(rf, C), :] + o_f)
        finalize(rb, part_scr[pl.ds(rb, C), :] + o_b)
        return carry

    carry = lax.fori_loop(0, n_chunks // 2, first_visits, (zero_state, zero_state), unroll=2)
    lax.fori_loop(n_chunks // 2, n_chunks, second_visits, carry, unroll=2)


def _gla(gq, gk, gv, gg, laf, lab, gain, B, T):
    N = gq.shape[0]
    pairs = GLA_HEADS // 2
    kspec = pl.BlockSpec((T, 2 * GLA_DK), lambda b, p: (b, p))
    vspec = pl.BlockSpec((T, 2 * GLA_DV), lambda b, p: (b, p))
    return pl.pallas_call(
        _gla_kernel,
        out_shape=jax.ShapeDtypeStruct((N, GLA_WIDTH), BF16),
        grid=(B, pairs),
        in_specs=[kspec, kspec, vspec, vspec, kspec, kspec,
                  pl.BlockSpec((1, GLA_DV), lambda b, p: (0, 0))],
        out_specs=vspec,
        scratch_shapes=[pltpu.VMEM((T, 2 * GLA_DV), F32)],
        compiler_params=_cparams(("arbitrary", "arbitrary")),
        name="gla",
    )(gq, gk, gv, gg, laf, lab, gain)


def _attn_kernel(q_ref, kt_ref, v_ref, o_ref, vext_scr, s0_scr, s1_scr, p0_scr, p1_scr, m0_scr, m1_scr,
                 *, tq, key_block):
    j = pl.program_id(1)
    T = v_ref.shape[0]
    nq = T // tq
    dh = ATT_HEAD_DIM

    lane = lax.broadcasted_iota(jnp.int32, (T, ATT_KV_WIDTH), 1)
    own = (lane >= j * dh) & (lane < (j + 1) * dh)
    vext_scr[...] = jnp.where(own, v_ref[...], jnp.ones((), BF16))

    s_scr, p_scr, m_scr = (s0_scr, s1_scr), (p0_scr, p1_scr), (m0_scr, m1_scr)

    def scores(blk, slot):
        q = q_ref[pl.ds(pl.multiple_of(blk * tq, tq), tq), :]
        qs = jnp.concatenate([q[:, h * dh:(h + 1) * dh] for h in range(ATT_GROUP)], axis=0)
        m = jnp.full((ATT_GROUP * tq, LANES), -jnp.inf, F32)
        for kb in range(T // key_block):
            ks = slice(kb * key_block, (kb + 1) * key_block)
            s = jnp.dot(qs, kt_ref[0, :, ks], preferred_element_type=F32)
            s_scr[slot][:, ks] = s
            for c in range(key_block // LANES):
                m = jnp.maximum(m, s[:, c * LANES:(c + 1) * LANES])
        m_scr[slot][...] = jnp.max(m, axis=-1, keepdims=True)

    def probs(slot):
        mx = m_scr[slot][...]
        for kb in range(T // key_block):
            ks = slice(kb * key_block, (kb + 1) * key_block)
            p_scr[slot][:, ks] = jnp.exp2(s_scr[slot][:, ks] - mx).astype(BF16)

    def out(blk, slot):
        acc = jnp.dot(p_scr[slot][...], vext_scr[...], preferred_element_type=F32)
        first = j == 0
        num = jnp.where(first, acc[:, :dh], acc[:, dh:])
        den = jnp.where(first, acc[:, dh:dh + 1], acc[:, 0:1])
        o = num / den
        o_ref[pl.ds(pl.multiple_of(blk * tq, tq), tq), :] = jnp.concatenate(
            [o[h * tq:(h + 1) * tq] for h in range(ATT_GROUP)], axis=1).astype(BF16)

    scores(0, 0)
    scores(1, 1)
    probs(0)

    def steady(it, carry):
        blk = 2 * it
        out(blk - 2, 0)
        scores(blk, 0)
        probs(1)
        out(blk - 1, 1)
        scores(blk + 1, 1)
        probs(0)
        return carry

    lax.fori_loop(1, nq // 2, steady, 0)
    probs(1)
    out(nq - 2, 0)
    out(nq - 1, 1)


def _attn(aq, akt, av, B, T, tq, key_block):
    N = aq.shape[0]
    assert T % (2 * tq) == 0
    gw = ATT_GROUP * ATT_HEAD_DIM
    rows = ATT_GROUP * tq
    return pl.pallas_call(
        functools.partial(_attn_kernel, tq=tq, key_block=key_block),
        out_shape=jax.ShapeDtypeStruct((N, ATT_WIDTH), BF16),
        grid=(B, ATT_KV_HEADS),
        in_specs=[pl.BlockSpec((T, gw), lambda b, j: (b, j)),
                  pl.BlockSpec((1, ATT_HEAD_DIM, T), lambda b, j: (b, j, 0)),
                  pl.BlockSpec((T, ATT_KV_WIDTH), lambda b, j: (b, 0))],
        out_specs=pl.BlockSpec((T, gw), lambda b, j: (b, j)),
        scratch_shapes=[pltpu.VMEM((T, ATT_KV_WIDTH), BF16),
                        pltpu.VMEM((rows, T), F32), pltpu.VMEM((rows, T), F32),
                        pltpu.VMEM((rows, T), BF16), pltpu.VMEM((rows, T), BF16),
                        pltpu.VMEM((rows, 1), F32), pltpu.VMEM((rows, 1), F32)],
        compiler_params=_cparams(("arbitrary", "arbitrary")),
        name="attn",
    )(aq, akt, av)


def _out_proj_kernel(x_ref, og_ref, oa_ref, wo_ref, ag_ref, g2_ref, wr_ref, br_ref, tri_ref,
                     h_ref, xn_ref, eid_ref, gate_ref, rank_ref, cnt_ref, run_scr):
    tm = x_ref.shape[0]

    @pl.when(pl.program_id(0) == 0)
    def _():
        run_scr[...] = jnp.zeros_like(run_scr)

    oan = _rms(oa_ref[...].astype(F32), ag_ref[...]).astype(BF16)
    y = (jnp.dot(og_ref[...], wo_ref[:GLA_WIDTH, :], preferred_element_type=F32)
         + jnp.dot(oan, wo_ref[GLA_WIDTH:, :], preferred_element_type=F32))
    h = x_ref[...] + y
    h_ref[...] = h
    xn = _rms(h, g2_ref[...])
    xn_ref[...] = xn

    lt = lax.dot_general(wr_ref[...], xn.astype(BF16), (((1,), (1,)), ((), ())),
                         preferred_element_type=F32) + br_ref[...]
    iota8 = lax.broadcasted_iota(jnp.int32, (N_GROUPS, tm), 0)

    def first_argmax(v):
        top = jnp.max(v, axis=0, keepdims=True)
        idx = jnp.min(jnp.where(v == top, iota8, N_GROUPS), axis=0, keepdims=True)
        return top, idx

    gl = lt[0:N_GROUPS]
    gmax, gidx = first_argmax(gl)
    gw = 1.0 / jnp.sum(jnp.exp(gl - gmax), axis=0, keepdims=True)
    esel = jnp.zeros((EXPERTS_PER_GROUP, tm), F32)
    for g in range(N_GROUPS):
        lo = N_GROUPS + g * EXPERTS_PER_GROUP
        esel = jnp.where(gidx == g, lt[lo:lo + EXPERTS_PER_GROUP], esel)
    v1, i1 = first_argmax(esel)
    rest = jnp.where(iota8 == i1, -jnp.inf, esel)
    v2, i2 = first_argmax(rest)
    t = jnp.exp(v2 - v1)
    den = 1.0 + t
    e1 = gidx * EXPERTS_PER_GROUP + i1
    e2 = gidx * EXPERTS_PER_GROUP + i2
    eid_ref[0:1, :] = e1
    eid_ref[1:2, :] = e2
    gate_ref[0:1, :] = gw * (1.0 / den)
    gate_ref[1:2, :] = gw * (t / den)

    iota_e = lax.broadcasted_iota(jnp.int32, (N_EXPERTS, tm), 0)
    oh1 = iota_e == e1
    oh2 = iota_e == e2
    both = jnp.where(oh1, 1.0, jnp.where(oh2, 1.0, 0.0)).astype(BF16)
    prefix = jnp.dot(both, tri_ref[...], preferred_element_type=F32)
    run = run_scr[...]
    base = prefix - 1.0 + run[:, 0:1]
    rank_ref[0:1, :] = jnp.sum(jnp.where(oh1, base, 0.0), axis=0, keepdims=True).astype(jnp.int32)
    rank_ref[1:2, :] = jnp.sum(jnp.where(oh2, base, 0.0), axis=0, keepdims=True).astype(jnp.int32)
    run = run + prefix[:, tm - 1:tm]
    run_scr[...] = run
    cnt_ref[...] = run


def _out_proj(xt, o_gla, o_att, wo, ag, g2, wr, br, tm):
    N = xt.shape[0]
    const = lambda shape: pl.BlockSpec(shape, lambda i: (0,) * len(shape))
    rows = lambda width: pl.BlockSpec((tm, width), lambda i: (i, 0))
    cols = pl.BlockSpec((TOP_K, tm), lambda i: (0, i))
    tri = jnp.asarray(np.triu(np.ones((tm, tm), np.float32)), BF16)
    return pl.pallas_call(
        _out_proj_kernel,
        out_shape=(jax.ShapeDtypeStruct((N, D_MODEL), F32),
                   jax.ShapeDtypeStruct((N, D_MODEL), F32),
                   jax.ShapeDtypeStruct((TOP_K, N), jnp.int32),
                   jax.ShapeDtypeStruct((TOP_K, N), F32),
                   jax.ShapeDtypeStruct((TOP_K, N), jnp.int32),
                   jax.ShapeDtypeStruct((N_EXPERTS, LANES), F32)),
        grid=(N // tm,),
        in_specs=[rows(D_MODEL), rows(GLA_WIDTH), rows(ATT_WIDTH), const((D_MODEL, D_MODEL)),
                  const((1, ATT_WIDTH)), const((1, D_MODEL)), const((LANES, D_MODEL)),
                  const((LANES, 1)), const((tm, tm))],
        out_specs=(rows(D_MODEL), rows(D_MODEL), cols, cols, cols, const((N_EXPERTS, LANES))),
        scratch_shapes=[pltpu.VMEM((N_EXPERTS, LANES), F32)],
        compiler_params=_cparams(("arbitrary",)),
        name="out_proj",
    )(xt, o_gla, o_att, wo, ag, g2, wr, br, tri)


def _dispatch_kernel(zs_ref, nu_ref, dest_ref, xn_hbm, xs_hbm, zero_buf, zero_sem, row_sem):
    tn = dest_ref.shape[1]
    i = pl.program_id(0)

    def fill(row0):
        return pltpu.make_async_copy(
            zero_buf, xs_hbm.at[pl.ds(pl.multiple_of(row0, MOE_BLOCK), MOE_BLOCK)], zero_sem)

    @pl.when(i == 0)
    def _():
        zero_buf[...] = jnp.zeros_like(zero_buf)
        fills = [fill(zs_ref[e]) for e in range(N_EXPERTS)]
        for f in fills:
            f.start()
        for f in fills:
            f.wait()

        def fill_tail(blk, carry):
            f = fill(blk * MOE_BLOCK)
            f.start()
            f.wait()
            return carry

        lax.fori_loop(nu_ref[0], xs_hbm.shape[0] // MOE_BLOCK, fill_tail, 0)

    base = i * tn

    def issue(t, carry):
        for k in range(TOP_K):
            pltpu.make_async_copy(xn_hbm.at[pl.ds(base + t, 1)], xs_hbm.at[pl.ds(dest_ref[k, t], 1)],
                                  row_sem).start()
        return carry

    lax.fori_loop(0, tn, issue, 0, unroll=8)
    pltpu.make_async_copy(xn_hbm.at[pl.ds(0, TOP_K * tn)], xs_hbm.at[pl.ds(0, TOP_K * tn)], row_sem).wait()


def _dispatch(zero_starts, n_used, dest, xn, P, tn):
    N = xn.shape[0]
    grid_spec = pltpu.PrefetchScalarGridSpec(
        num_scalar_prefetch=2,
        grid=(N // tn,),
        in_specs=[pl.BlockSpec((TOP_K, tn), lambda i, zs, nu: (0, i), memory_space=pltpu.SMEM),
                  pl.BlockSpec(memory_space=pl.ANY)],
        out_specs=pl.BlockSpec(memory_space=pl.ANY),
        scratch_shapes=[pltpu.VMEM((MOE_BLOCK, D_MODEL), F32),
                        pltpu.SemaphoreType.DMA, pltpu.SemaphoreType.DMA],
    )
    return pl.pallas_call(
        _dispatch_kernel,
        out_shape=jax.ShapeDtypeStruct((P, D_MODEL), F32),
        grid_spec=grid_spec,
        compiler_params=_cparams(("arbitrary",)),
        name="dispatch",
    )(zero_starts, n_used, dest, xn)


def _moe_ffn_kernel(be_ref, nu_ref, xs_ref, wg_ref, wu_ref, wd_ref, ys_ref, wg_s, wu_s, wd_s):
    i = pl.program_id(0)
    used = i < nu_ref[0]

    @pl.when(used)
    def _():
        new_expert = jnp.logical_or(i == 0, be_ref[i] != be_ref[jnp.maximum(i - 1, 0)])

        @pl.when(new_expert)
        def _():
            wg_s[...] = wg_ref[0].astype(BF16)
            wu_s[...] = wu_ref[0].astype(BF16)
            wd_s[...] = wd_ref[0].astype(BF16)

        xb = xs_ref[...].astype(BF16)
        hg = jnp.dot(xb, wg_s[...], preferred_element_type=F32)
        hu = jnp.dot(xb, wu_s[...], preferred_element_type=F32)
        hm = (hg * _sigmoid(hg) * hu).astype(BF16)
        ys_ref[...] = jnp.dot(hm, wd_s[...], preferred_element_type=F32)

    @pl.when(jnp.logical_not(used))
    def _():
        ys_ref[...] = jnp.zeros_like(ys_ref)


def _moe_ffn(blk_expert, n_used, xs, w_gate, w_up, w_down):
    P = xs.shape[0]
    nblk = P // MOE_BLOCK
    grid_spec = pltpu.PrefetchScalarGridSpec(
        num_scalar_prefetch=2,
        grid=(nblk,),
        in_specs=[pl.BlockSpec((MOE_BLOCK, D_MODEL), lambda i, be, nu: (i, 0)),
                  pl.BlockSpec((1, D_MODEL, D_EXPERT), lambda i, be, nu: (be[i], 0, 0)),
                  pl.BlockSpec((1, D_MODEL, D_EXPERT), lambda i, be, nu: (be[i], 0, 0)),
                  pl.BlockSpec((1, D_EXPERT, D_MODEL), lambda i, be, nu: (be[i], 0, 0))],
        out_specs=pl.BlockSpec((MOE_BLOCK, D_MODEL), lambda i, be, nu: (i, 0)),
        scratch_shapes=[pltpu.VMEM((D_MODEL, D_EXPERT), BF16),
                        pltpu.VMEM((D_MODEL, D_EXPERT), BF16),
                        pltpu.VMEM((D_EXPERT, D_MODEL), BF16)],
    )
    return pl.pallas_call(
        _moe_ffn_kernel,
        out_shape=jax.ShapeDtypeStruct((P, D_MODEL), F32),
        grid_spec=grid_spec,
        compiler_params=_cparams(("arbitrary",)),
        name="moe_ffn",
    )(blk_expert, n_used, xs, w_gate, w_up, w_down)


def _combine_kernel(dcur_ref, dnext_ref, h_ref, gate_ref, fg_ref, ys_hbm, o_ref, ybuf, sems, *, final):
    tm = h_ref.shape[0]
    i = pl.program_id(0)
    n = pl.num_programs(0)

    def gather(dref, slot):
        def issue(t, carry):
            for k in range(TOP_K):
                pltpu.make_async_copy(ys_hbm.at[pl.ds(dref[k, t], 1)],
                                      ybuf.at[slot, pl.ds(k * tm + t, 1)], sems.at[slot]).start()
            return carry

        lax.fori_loop(0, tm, issue, 0, unroll=8)

    @pl.when(i == 0)
    def _():
        gather(dcur_ref, 0)

    @pl.when(i + 1 < n)
    def _():
        gather(dnext_ref, (i + 1) % 2)

    slot = i % 2
    pltpu.make_async_copy(ys_hbm.at[pl.ds(0, TOP_K * tm)], ybuf.at[slot], sems.at[slot]).wait()
    g = gate_ref[...]
    hh = h_ref[...]
    for k in range(TOP_K):
        hh = hh + ybuf[slot, pl.ds(k * tm, tm), :] * g[:, k:k + 1]
    o_ref[...] = _rms(hh, fg_ref[...]) if final else hh


def _combine(h, ys, dest, gate_t, fg, tm, final):
    N = h.shape[0]
    n = N // tm
    rows = pl.BlockSpec((tm, D_MODEL), lambda i: (i, 0))
    return pl.pallas_call(
        functools.partial(_combine_kernel, final=final),
        out_shape=jax.ShapeDtypeStruct((N, D_MODEL), F32),
        grid=(n,),
        in_specs=[pl.BlockSpec((TOP_K, tm), lambda i: (0, i), memory_space=pltpu.SMEM),
                  pl.BlockSpec((TOP_K, tm), lambda i: (0, jnp.minimum(i + 1, n - 1)), memory_space=pltpu.SMEM),
                  rows, pl.BlockSpec((tm, TOP_K), lambda i: (i, 0)),
                  pl.BlockSpec((1, D_MODEL), lambda i: (0, 0)),
                  pl.BlockSpec(memory_space=pl.ANY)],
        out_specs=rows,
        scratch_shapes=[pltpu.VMEM((2, TOP_K * tm, D_MODEL), F32), pltpu.SemaphoreType.DMA((2,))],
        compiler_params=_cparams(("arbitrary",)),
        name="combine",
    )(dest, dest, h, gate_t, fg, ys)


def _reorder_w_in(w_in):
    n_gla = 2 * GLA_KEY_WIDTH + 2 * GLA_WIDTH
    z0 = n_gla
    a0 = z0 + 2 * GLA_GATE_RANK
    pad = jnp.zeros((D_MODEL, Z_PAD - 2 * GLA_GATE_RANK), w_in.dtype)
    return jnp.concatenate([w_in[:, :n_gla], w_in[:, a0:], w_in[:, z0:a0], pad], axis=1)


def kernel(x, norm1_gain, w_in, gla_up_fwd, gla_up_fwd_bias, gla_up_bwd, gla_up_bwd_bias, gla_out_gain, q_norm_gain, k_norm_gain, att_out_gain, w_out, norm2_gain, w_group, b_group, w_expert, b_expert, w_gate, w_up, w_down, final_gain):
    B, T, D = x.shape
    N = B * T
    depth = norm1_gain.shape[0]
    tm = min(512, T)
    tq = 128
    key_block = min(512, T)
    h = x.reshape(N, D)
    cos, se, so = _rope_tables(T)
    head_mean = jnp.asarray(
        np.kron(np.eye(ATT_HEADS, dtype=np.float32),
                np.full((ATT_HEAD_DIM, ATT_HEAD_DIM), 1.0 / ATT_HEAD_DIM, np.float32)), BF16)
    for l in range(depth):
        w = _reorder_w_in(w_in[l]).astype(BF16)
        r = GLA_GATE_RANK
        up = jnp.zeros((Z_PAD, 2 * GLA_KEY_WIDTH), F32)
        up = up.at[:r, :GLA_KEY_WIDTH].set(gla_up_fwd[l]).at[r:2 * r, GLA_KEY_WIDTH:].set(gla_up_bwd[l])
        upb = jnp.concatenate([gla_up_fwd_bias[l], gla_up_bwd_bias[l]])[None, :]
        qg = jnp.tile(q_norm_gain[l], ATT_HEADS)[None, :]
        kg = jnp.tile(k_norm_gain[l], ATT_KV_HEADS)[None, :]
        gq, gk, gv, gg, laf, lab, aq, akt, av = _in_proj(
            h, B, T, tm, norm1_gain[l][None, :], w, up.astype(BF16), upb, qg, kg, head_mean, cos, se, so)
        o_gla = _gla(gq, gk, gv, gg, laf, lab, gla_out_gain[l][None, :], B, T)
        o_att = _attn(aq, akt, av, B, T, tq, key_block)

        wr = jnp.zeros((LANES, D), F32)
        wr = wr.at[:N_GROUPS].set(w_group[l].T).at[N_GROUPS:N_GROUPS + N_EXPERTS].set(w_expert[l].T)
        br = jnp.zeros((LANES, 1), F32)
        br = br.at[:N_GROUPS, 0].set(b_group[l]).at[N_GROUPS:N_GROUPS + N_EXPERTS, 0].set(b_expert[l])
        h, xn, eid, gate, rank, cnt = _out_proj(
            h, o_gla, o_att, w_out[l].astype(BF16), att_out_gain[l][None, :], norm2_gain[l][None, :],
            wr.astype(BF16), br, tm)

        counts = cnt[:, 0].astype(jnp.int32)
        padded = (counts + MOE_BLOCK - 1) // MOE_BLOCK * MOE_BLOCK
        pad_ends = jnp.cumsum(padded)
        pad_starts = pad_ends - padded
        dest = pad_starts[eid] + rank
        P = N * TOP_K + N_EXPERTS * MOE_BLOCK
        nblk = P // MOE_BLOCK
        blk_start = jnp.arange(nblk, dtype=jnp.int32) * MOE_BLOCK
        blk_expert = jnp.minimum(
            jnp.sum((pad_ends[None, :] <= blk_start[:, None]).astype(jnp.int32), axis=1), N_EXPERTS - 1)
        n_used = (pad_ends[-1:] // MOE_BLOCK).astype(jnp.int32)

        zero_starts = jnp.maximum(pad_ends - MOE_BLOCK, 0).astype(jnp.int32)
        xs = _dispatch(zero_starts, n_used, dest, xn, P, tm)
        ys = _moe_ffn(blk_expert, n_used, xs, w_gate[l], w_up[l], w_down[l])
        h = _combine(h, ys, dest, gate.T, final_gain[None, :], tm, l == depth - 1)
    return h.reshape(B, T, D)
```

```python
import functools

import numpy as np
import jax
import jax.numpy as jnp
from jax import lax
from jax.experimental import pallas as pl
from jax.experimental.pallas import tpu as pltpu

F32 = jnp.float32
BF16 = jnp.bfloat16

D_MODEL = 1024
EPS = 1e-6
GRID_W = 64

GLA_HEADS = 4
GLA_DK = 64
GLA_DV = 128
GLA_KEY_WIDTH = GLA_HEADS * GLA_DK
GLA_WIDTH = GLA_HEADS * GLA_DV
GLA_GATE_RANK = 16
GLA_GATE_NORM = 16.0
GLA_CHUNK = 64

ATT_HEADS = 8
ATT_KV_HEADS = 2
ATT_HEAD_DIM = 64
ATT_GROUP = ATT_HEADS // ATT_KV_HEADS
ATT_WIDTH = ATT_HEADS * ATT_HEAD_DIM
ATT_KV_WIDTH = ATT_KV_HEADS * ATT_HEAD_DIM
ROPE_THETA = 10000.0
LOG2_E = 1.4426950408889634

N_GROUPS = 8
EXPERTS_PER_GROUP = 8
N_EXPERTS = N_GROUPS * EXPERTS_PER_GROUP
TOP_K = 2
D_EXPERT = 512
MOE_BLOCK = 128

LANES = 128
Z_PAD = LANES

_OFF_GQ = 0
_OFF_GK = _OFF_GQ + GLA_KEY_WIDTH
_OFF_GV = _OFF_GK + GLA_KEY_WIDTH
_OFF_GG = _OFF_GV + GLA_WIDTH
_OFF_AQ = _OFF_GG + GLA_WIDTH
_OFF_AK = _OFF_AQ + ATT_WIDTH
_OFF_AV = _OFF_AK + ATT_KV_WIDTH
_OFF_Z = _OFF_AV + ATT_KV_WIDTH
D_IN_PAD = _OFF_Z + Z_PAD

VMEM_LIMIT = 56 * 1024 * 1024


def _cparams(semantics):
    return pltpu.CompilerParams(dimension_semantics=semantics, vmem_limit_bytes=VMEM_LIMIT)


def _rms(x, gain):
    return x * lax.rsqrt(jnp.mean(x * x, axis=-1, keepdims=True) + EPS) * gain


def _sigmoid(x):
    return 1.0 / (1.0 + jnp.exp(-x))


def _head_norm_rope(x, head_mean, gain, cos, sin_even, sin_odd):
    w = x.shape[1]
    ms = jnp.dot((x * x).astype(BF16), head_mean, preferred_element_type=F32)
    xn = x * lax.rsqrt(ms + EPS) * gain
    reps = w // LANES
    tile = lambda t: t if reps == 1 else jnp.concatenate([t] * reps, axis=1)
    nxt = pltpu.roll(xn, w - 1, 1)
    prv = pltpu.roll(xn, 1, 1)
    return xn * tile(cos) + nxt * tile(sin_even) + prv * tile(sin_odd)


def _in_proj_kernel(x_ref, g1_ref, w_ref, up_ref, upb_ref, qg_ref, kg_ref, hm_ref,
                    cos_ref, se_ref, so_ref,
                    gq_ref, gk_ref, gv_ref, gg_ref, laf_ref, lab_ref, aq_ref, akt_ref, av_ref):
    u = _rms(x_ref[...], g1_ref[...])
    res = jnp.dot(u.astype(BF16), w_ref[...], preferred_element_type=F32)
    gq_ref[...] = res[:, _OFF_GQ:_OFF_GK].astype(BF16)
    gk_ref[...] = res[:, _OFF_GK:_OFF_GV].astype(BF16)
    gv_ref[...] = res[:, _OFF_GV:_OFF_GG].astype(BF16)
    gg_ref[...] = res[:, _OFF_GG:_OFF_AQ].astype(BF16)
    av_ref[...] = res[:, _OFF_AV:_OFF_Z].astype(BF16)

    z = res[:, _OFF_Z:D_IN_PAD].astype(BF16)
    zl = jnp.dot(z, up_ref[...], preferred_element_type=F32) + upb_ref[...]
    la = (jnp.minimum(zl, 0.0) - jnp.log(1.0 + jnp.exp(-jnp.abs(zl)))) * (1.0 / GLA_GATE_NORM)
    laf_ref[...] = la[:, :GLA_KEY_WIDTH]
    lab_ref[...] = la[:, GLA_KEY_WIDTH:]

    cos, se, so = cos_ref[...], se_ref[...], so_ref[...]
    hm = hm_ref[...]
    q = _head_norm_rope(res[:, _OFF_AQ:_OFF_AK], hm, qg_ref[...], cos, se, so)
    aq_ref[...] = (q * (ATT_HEAD_DIM ** -0.5 * LOG2_E)).astype(BF16)
    k = _head_norm_rope(res[:, _OFF_AK:_OFF_AV], hm[:ATT_KV_WIDTH, :ATT_KV_WIDTH], kg_ref[...], cos, se, so)
    akt_ref[0] = k.T.astype(BF16)


def _rope_tables(T):
    t = np.arange(T)
    row = (t // GRID_W).astype(np.float32)
    col = (t % GRID_W).astype(np.float32)
    axis_dim = ATT_HEAD_DIM // 2
    inv_freq = (ROPE_THETA ** (-np.arange(0, axis_dim, 2, dtype=np.float32) / axis_dim)).astype(np.float32)
    ang = np.concatenate([row[:, None] * inv_freq, col[:, None] * inv_freq], axis=-1)
    ang = np.repeat(ang, 2, axis=1)
    ang = np.tile(ang, (1, LANES // ATT_HEAD_DIM))
    even = (np.arange(LANES) % 2 == 0)[None, :]
    cos = np.cos(ang)
    sin = np.sin(ang)
    return (jnp.asarray(cos, F32), jnp.asarray(np.where(even, -sin, 0.0), F32),
            jnp.asarray(np.where(even, 0.0, sin), F32))


def _in_proj(xt, B, T, tm, g1, w, up, upb, qg, kg, hm, cos, se, so):
    N = xt.shape[0]
    tpb = T // tm
    const = lambda shape: pl.BlockSpec(shape, lambda i: (0,) * len(shape))
    rows = lambda width: pl.BlockSpec((tm, width), lambda i: (i, 0))
    pos = pl.BlockSpec((tm, LANES), lambda i: (i % tpb, 0))
    out_shape = (
        jax.ShapeDtypeStruct((N, GLA_KEY_WIDTH), BF16),
        jax.ShapeDtypeStruct((N, GLA_KEY_WIDTH), BF16),
        jax.ShapeDtypeStruct((N, GLA_WIDTH), BF16),
        jax.ShapeDtypeStruct((N, GLA_WIDTH), BF16),
        jax.ShapeDtypeStruct((N, GLA_KEY_WIDTH), F32),
        jax.ShapeDtypeStruct((N, GLA_KEY_WIDTH), F32),
        jax.ShapeDtypeStruct((N, ATT_WIDTH), BF16),
        jax.ShapeDtypeStruct((B, ATT_KV_WIDTH, T), BF16),
        jax.ShapeDtypeStruct((N, ATT_KV_WIDTH), BF16),
    )
    out_specs = (
        rows(GLA_KEY_WIDTH), rows(GLA_KEY_WIDTH), rows(GLA_WIDTH), rows(GLA_WIDTH),
        rows(GLA_KEY_WIDTH), rows(GLA_KEY_WIDTH), rows(ATT_WIDTH),
        pl.BlockSpec((1, ATT_KV_WIDTH, tm), lambda i: (i // tpb, 0, i % tpb)),
        rows(ATT_KV_WIDTH),
    )
    return pl.pallas_call(
        _in_proj_kernel,
        out_shape=out_shape,
        grid=(N // tm,),
        in_specs=[rows(D_MODEL), const((1, D_MODEL)), const((D_MODEL, D_IN_PAD)),
                  const((Z_PAD, 2 * GLA_KEY_WIDTH)), const((1, 2 * GLA_KEY_WIDTH)),
                  const((1, ATT_WIDTH)), const((1, ATT_KV_WIDTH)), const((ATT_WIDTH, ATT_WIDTH)),
                  pos, pos, pos],
        out_specs=out_specs,
        compiler_params=_cparams(("arbitrary",)),
        name="in_proj",
    )(xt, g1, w, up, upb, qg, kg, hm, cos, se, so)


def _gla_kernel(q_ref, k_ref, v_ref, g_ref, laf_ref, lab_ref, gain_ref, o_ref, part_scr):
    C = GLA_CHUNK
    T = q_ref.shape[0]
    n_chunks = T // C
    row = lax.broadcasted_iota(jnp.int32, (C, C), 0)
    col = lax.broadcasted_iota(jnp.int32, (C, C), 1)
    tril = row >= col
    triu = row <= col
    gain = gain_ref[...]

    def chunk(r0, state, la_ref, mask, i_ref, i_last):
        q2 = q_ref[pl.ds(r0, C), :].astype(F32) * (GLA_DK ** -0.5)
        k2 = k_ref[pl.ds(r0, C), :].astype(F32)
        v2 = v_ref[pl.ds(r0, C), :]
        la = la_ref[pl.ds(r0, C), :]
        tri = mask.astype(BF16)
        la_hi = la.astype(BF16)
        la_lo = (la - la_hi.astype(F32)).astype(BF16)
        b = (jnp.dot(tri, la_hi, preferred_element_type=F32)
             + jnp.dot(tri, la_lo, preferred_element_type=F32))
        b_ref = b[i_ref:i_ref + 1, :]
        b_last = b[i_last:i_last + 1, :]
        qf = (q2 * jnp.exp(b - b_ref)).astype(BF16)
        kf = (k2 * jnp.exp(b_ref - b)).astype(BF16)
        qi = (q2 * jnp.exp(b)).astype(BF16)
        kl = (k2 * jnp.exp(b_last - b)).astype(BF16)
        decay = jnp.exp(b_last)
        outs, new_state = [], []
        for h in range(2):
            ks = slice(h * GLA_DK, (h + 1) * GLA_DK)
            vh = v2[:, h * GLA_DV:(h + 1) * GLA_DV]
            sc = lax.dot_general(qf[:, ks], kf[:, ks], (((1,), (1,)), ((), ())),
                                 preferred_element_type=F32)
            sc = jnp.where(mask, sc, 0.0).astype(BF16)
            o = jnp.dot(sc, vh, preferred_element_type=F32)
            o = o + lax.dot_general(qi[:, ks], state[h].astype(BF16), (((1,), (1,)), ((), ())),
                                    preferred_element_type=F32)
            kv_t = lax.dot_general(vh, kl[:, ks], (((0,), (0,)), ((), ())),
                                   preferred_element_type=F32)
            new_state.append(decay[:, ks] * state[h] + kv_t)
            outs.append(o)
        return jnp.concatenate(outs, axis=1), tuple(new_state)

    zero_state = tuple(jnp.zeros((GLA_DV, GLA_DK), F32) for _ in range(2))

    def both_directions(n, carry):
        rf = pl.multiple_of(n * C, C)
        rb = pl.multiple_of((n_chunks - 1 - n) * C, C)
        o_f, sf = chunk(rf, carry[0], laf_ref, tril, C // 2 - 1, C - 1)
        o_b, sb = chunk(rb, carry[1], lab_ref, triu, C // 2, 0)
        return (rf, o_f), (rb, o_b), (sf, sb)

    def finalize(r0, o):
        g = g_ref[pl.ds(r0, C), :].astype(F32)
        normed = jnp.concatenate(
            [_rms(o[:, h * GLA_DV:(h + 1) * GLA_DV], gain) for h in range(2)], axis=1)
        o_ref[pl.ds(r0, C), :] = (normed * (g * _sigmoid(g))).astype(BF16)

    def first_visits(n, carry):
        (rf, o_f), (rb, o_b), carry = both_directions(n, carry)
        part_scr[pl.ds(rf, C), :] = o_f
        part_scr[pl.ds(rb, C), :] = o_b
        return carry

    def second_visits(n, carry):
        (rf, o_f), (rb, o_b), carry = both_directions(n, carry)
        finalize(rf, part_scr[pl.ds(rf, C), :] + o_f)
        finalize(rb, part_scr[pl.ds(rb, C), :] + o_b)
        return carry

    carry = lax.fori_loop(0, n_chunks // 2, first_visits, (zero_state, zero_state), unroll=2)
    lax.fori_loop(n_chunks // 2, n_chunks, second_visits, carry, unroll=2)


def _gla(gq, gk, gv, gg, laf, lab, gain, B, T):
    N = gq.shape[0]
    pairs = GLA_HEADS // 2
    kspec = pl.BlockSpec((T, 2 * GLA_DK), lambda b, p: (b, p))
    vspec = pl.BlockSpec((T, 2 * GLA_DV), lambda b, p: (b, p))
    return pl.pallas_call(
        _gla_kernel,
        out_shape=jax.ShapeDtypeStruct((N, GLA_WIDTH), BF16),
        grid=(B, pairs),
        in_specs=[kspec, kspec, vspec, vspec, kspec, kspec,
                  pl.BlockSpec((1, GLA_DV), lambda b, p: (0, 0))],
        out_specs=vspec,
        scratch_shapes=[pltpu.VMEM((T, 2 * GLA_DV), F32)],
        compiler_params=_cparams(("arbitrary", "arbitrary")),
        name="gla",
    )(gq, gk, gv, gg, laf, lab, gain)


def _attn_kernel(q_ref, kt_ref, v_ref, o_ref, vext_scr, s0_scr, s1_scr, p0_scr, p1_scr, m0_scr, m1_scr,
                 *, tq, key_block):
    j = pl.program_id(1)
    T = v_ref.shape[0]
    nq = T // tq
    dh = ATT_HEAD_DIM

    lane = lax.broadcasted_iota(jnp.int32, (T, ATT_KV_WIDTH), 1)
    own = (lane >= j * dh) & (lane < (j + 1) * dh)
    vext_scr[...] = jnp.where(own, v_ref[...], jnp.ones((), BF16))

    s_scr, p_scr, m_scr = (s0_scr, s1_scr), (p0_scr, p1_scr), (m0_scr, m1_scr)

    def scores(blk, slot):
        q = q_ref[pl.ds(pl.multiple_of(blk * tq, tq), tq), :]
        qs = jnp.concatenate([q[:, h * dh:(h + 1) * dh] for h in range(ATT_GROUP)], axis=0)
        m = jnp.full((ATT_GROUP * tq, LANES), -jnp.inf, F32)
        for kb in range(T // key_block):
            ks = slice(kb * key_block, (kb + 1) * key_block)
            s = jnp.dot(qs, kt_ref[0, :, ks], preferred_element_type=F32)
            s_scr[slot][:, ks] = s
            for c in range(key_block // LANES):
                m = jnp.maximum(m, s[:, c * LANES:(c + 1) * LANES])
        m_scr[slot][...] = jnp.max(m, axis=-1, keepdims=True)

    def probs(slot):
        mx = m_scr[slot][...]
        for kb in range(T // key_block):
            ks = slice(kb * key_block, (kb + 1) * key_block)
            p_scr[slot][:, ks] = jnp.exp2(s_scr[slot][:, ks] - mx).astype(BF16)

    def out(blk, slot):
        acc = jnp.dot(p_scr[slot][...], vext_scr[...], preferred_element_type=F32)
        first = j == 0
        num = jnp.where(first, acc[:, :dh], acc[:, dh:])
        den = jnp.where(first, acc[:, dh:dh + 1], acc[:, 0:1])
        o = num / den
        o_ref[pl.ds(pl.multiple_of(blk * tq, tq), tq), :] = jnp.concatenate(
            [o[h * tq:(h + 1) * tq] for h in range(ATT_GROUP)], axis=1).astype(BF16)

    scores(0, 0)
    scores(1, 1)
    probs(0)

    def steady(it, carry):
        blk = 2 * it
        out(blk - 2, 0)
        scores(blk, 0)
        probs(1)
        out(blk - 1, 1)
        scores(blk + 1, 1)
        probs(0)
        return carry

    lax.fori_loop(1, nq // 2, steady, 0)
    probs(1)
    out(nq - 2, 0)
    out(nq - 1, 1)


def _attn(aq, akt, av, B, T, tq, key_block):
    N = aq.shape[0]
    assert T % (2 * tq) == 0
    gw = ATT_GROUP * ATT_HEAD_DIM
    rows = ATT_GROUP * tq
    return pl.pallas_call(
        functools.partial(_attn_kernel, tq=tq, key_block=key_block),
        out_shape=jax.ShapeDtypeStruct((N, ATT_WIDTH), BF16),
        grid=(B, ATT_KV_HEADS),
        in_specs=[pl.BlockSpec((T, gw), lambda b, j: (b, j)),
                  pl.BlockSpec((1, ATT_HEAD_DIM, T), lambda b, j: (b, j, 0)),
                  pl.BlockSpec((T, ATT_KV_WIDTH), lambda b, j: (b, 0))],
        out_specs=pl.BlockSpec((T, gw), lambda b, j: (b, j)),
        scratch_shapes=[pltpu.VMEM((T, ATT_KV_WIDTH), BF16),
                        pltpu.VMEM((rows, T), F32), pltpu.VMEM((rows, T), F32),
                        pltpu.VMEM((rows, T), BF16), pltpu.VMEM((rows, T), BF16),
                        pltpu.VMEM((rows, 1), F32), pltpu.VMEM((rows, 1), F32)],
        compiler_params=_cparams(("arbitrary", "arbitrary")),
        name="attn",
    )(aq, akt, av)


def _out_proj_kernel(x_ref, og_ref, oa_ref, wo_ref, ag_ref, g2_ref, wr_ref, br_ref, tri_ref,
                     h_ref, xn_ref, eid_ref, gate_ref, rank_ref, cnt_ref, run_scr):
    tm = x_ref.shape[0]

    @pl.when(pl.program_id(0) == 0)
    def _():
        run_scr[...] = jnp.zeros_like(run_scr)

    oan = _rms(oa_ref[...].astype(F32), ag_ref[...]).astype(BF16)
    y = (jnp.dot(og_ref[...], wo_ref[:GLA_WIDTH, :], preferred_element_type=F32)
         + jnp.dot(oan, wo_ref[GLA_WIDTH:, :], preferred_element_type=F32))
    h = x_ref[...] + y
    h_ref[...] = h
    xn = _rms(h, g2_ref[...])
    xn_ref[...] = xn

    lt = lax.dot_general(wr_ref[...], xn.astype(BF16), (((1,), (1,)), ((), ())),
                         preferred_element_type=F32) + br_ref[...]
    iota8 = lax.broadcasted_iota(jnp.int32, (N_GROUPS, tm), 0)

    def first_argmax(v):
        top = jnp.max(v, axis=0, keepdims=True)
        idx = jnp.min(jnp.where(v == top, iota8, N_GROUPS), axis=0, keepdims=True)
        return top, idx

    gl = lt[0:N_GROUPS]
    gmax, gidx = first_argmax(gl)
    gw = 1.0 / jnp.sum(jnp.exp(gl - gmax), axis=0, keepdims=True)
    esel = jnp.zeros((EXPERTS_PER_GROUP, tm), F32)
    for g in range(N_GROUPS):
        lo = N_GROUPS + g * EXPERTS_PER_GROUP
        esel = jnp.where(gidx == g, lt[lo:lo + EXPERTS_PER_GROUP], esel)
    v1, i1 = first_argmax(esel)
    rest = jnp.where(iota8 == i1, -jnp.inf, esel)
    v2, i2 = first_argmax(rest)
    t = jnp.exp(v2 - v1)
    den = 1.0 + t
    e1 = gidx * EXPERTS_PER_GROUP + i1
    e2 = gidx * EXPERTS_PER_GROUP + i2
    eid_ref[0:1, :] = e1
    eid_ref[1:2, :] = e2
    gate_ref[0:1, :] = gw * (1.0 / den)
    gate_ref[1:2, :] = gw * (t / den)

    iota_e = lax.broadcasted_iota(jnp.int32, (N_EXPERTS, tm), 0)
    oh1 = iota_e == e1
    oh2 = iota_e == e2
    both = jnp.where(oh1, 1.0, jnp.where(oh2, 1.0, 0.0)).astype(BF16)
    prefix = jnp.dot(both, tri_ref[...], preferred_element_type=F32)
    run = run_scr[...]
    base = prefix - 1.0 + run[:, 0:1]
    rank_ref[0:1, :] = jnp.sum(jnp.where(oh1, base, 0.0), axis=0, keepdims=True).astype(jnp.int32)
    rank_ref[1:2, :] = jnp.sum(jnp.where(oh2, base, 0.0), axis=0, keepdims=True).astype(jnp.int32)
    run = run + prefix[:, tm - 1:tm]
    run_scr[...] = run
    cnt_ref[...] = run


def _out_proj(xt, o_gla, o_att, wo, ag, g2, wr, br, tm):
    N = xt.shape[0]
    const = lambda shape: pl.BlockSpec(shape, lambda i: (0,) * len(shape))
    rows = lambda width: pl.BlockSpec((tm, width), lambda i: (i, 0))
    cols = pl.BlockSpec((TOP_K, tm), lambda i: (0, i))
    tri = jnp.asarray(np.triu(np.ones((tm, tm), np.float32)), BF16)
    return pl.pallas_call(
        _out_proj_kernel,
        out_shape=(jax.ShapeDtypeStruct((N, D_MODEL), F32),
                   jax.ShapeDtypeStruct((N, D_MODEL), F32),
                   jax.ShapeDtypeStruct((TOP_K, N), jnp.int32),
                   jax.ShapeDtypeStruct((TOP_K, N), F32),
                   jax.ShapeDtypeStruct((TOP_K, N), jnp.int32),
                   jax.ShapeDtypeStruct((N_EXPERTS, LANES), F32)),
        grid=(N // tm,),
        in_specs=[rows(D_MODEL), rows(GLA_WIDTH), rows(ATT_WIDTH), const((D_MODEL, D_MODEL)),
                  const((1, ATT_WIDTH)), const((1, D_MODEL)), const((LANES, D_MODEL)),
                  const((LANES, 1)), const((tm, tm))],
        out_specs=(rows(D_MODEL), rows(D_MODEL), cols, cols, cols, const((N_EXPERTS, LANES))),
        scratch_shapes=[pltpu.VMEM((N_EXPERTS, LANES), F32)],
        compiler_params=_cparams(("arbitrary",)),
        name="out_proj",
    )(xt, o_gla, o_att, wo, ag, g2, wr, br, tri)


def _dispatch_kernel(zs_ref, nu_ref, dest_ref, xn_ref, xs_hbm, zero_buf, zero_sem, row_sem):
    tn = dest_ref.shape[1]
    i = pl.program_id(0)

    def fill(row0):
        return pltpu.make_async_copy(
            zero_buf, xs_hbm.at[pl.ds(pl.multiple_of(row0, MOE_BLOCK), MOE_BLOCK)], zero_sem)

    @pl.when(i == 0)
    def _():
        zero_buf[...] = jnp.zeros_like(zero_buf)
        fills = [fill(zs_ref[e]) for e in range(N_EXPERTS)]
        for f in fills:
            f.start()
        for f in fills:
            f.wait()

        def fill_tail(blk, carry):
            f = fill(blk * MOE_BLOCK)
            f.start()
            f.wait()
            return carry

        lax.fori_loop(nu_ref[0], xs_hbm.shape[0] // MOE_BLOCK, fill_tail, 0)

    def issue(t, carry):
        for k in range(TOP_K):
            pltpu.make_async_copy(xn_ref.at[pl.ds(t, 1)], xs_hbm.at[pl.ds(dest_ref[k, t], 1)],
                                  row_sem).start()
        return carry

    lax.fori_loop(0, tn, issue, 0, unroll=8)
    for k in range(TOP_K):
        pltpu.make_async_copy(xn_ref, xs_hbm.at[pl.ds(0, tn)], row_sem).wait()


def _dispatch(zero_starts, n_used, dest, xn, P, tn):
    N = xn.shape[0]
    grid_spec = pltpu.PrefetchScalarGridSpec(
        num_scalar_prefetch=2,
        grid=(N // tn,),
        in_specs=[pl.BlockSpec((TOP_K, tn), lambda i, zs, nu: (0, i), memory_space=pltpu.SMEM),
                  pl.BlockSpec((tn, D_MODEL), lambda i, zs, nu: (i, 0))],
        out_specs=pl.BlockSpec(memory_space=pl.ANY),
        scratch_shapes=[pltpu.VMEM((MOE_BLOCK, D_MODEL), F32),
                        pltpu.SemaphoreType.DMA, pltpu.SemaphoreType.DMA],
    )
    return pl.pallas_call(
        _dispatch_kernel,
        out_shape=jax.ShapeDtypeStruct((P, D_MODEL), F32),
        grid_spec=grid_spec,
        compiler_params=_cparams(("arbitrary",)),
        name="dispatch",
    )(zero_starts, n_used, dest, xn)


def _moe_ffn_kernel(be_ref, nu_ref, xs_ref, wg_ref, wu_ref, wd_ref, ys_ref, wg_s, wu_s, wd_s):
    i = pl.program_id(0)
    used = i < nu_ref[0]

    @pl.when(used)
    def _():
        new_expert = jnp.logical_or(i == 0, be_ref[i] != be_ref[jnp.maximum(i - 1, 0)])

        @pl.when(new_expert)
        def _():
            wg_s[...] = wg_ref[0].astype(BF16)
            wu_s[...] = wu_ref[0].astype(BF16)
            wd_s[...] = wd_ref[0].astype(BF16)

        xb = xs_ref[...].astype(BF16)
        hg = jnp.dot(xb, wg_s[...], preferred_element_type=F32)
        hu = jnp.dot(xb, wu_s[...], preferred_element_type=F32)
        hm = (hg * _sigmoid(hg) * hu).astype(BF16)
        ys_ref[...] = jnp.dot(hm, wd_s[...], preferred_element_type=F32)

    @pl.when(jnp.logical_not(used))
    def _():
        ys_ref[...] = jnp.zeros_like(ys_ref)


def _moe_ffn(blk_expert, n_used, xs, w_gate, w_up, w_down):
    P = xs.shape[0]
    nblk = P // MOE_BLOCK
    grid_spec = pltpu.PrefetchScalarGridSpec(
        num_scalar_prefetch=2,
        grid=(nblk,),
        in_specs=[pl.BlockSpec((MOE_BLOCK, D_MODEL), lambda i, be, nu: (i, 0)),
                  pl.BlockSpec((1, D_MODEL, D_EXPERT), lambda i, be, nu: (be[i], 0, 0)),
                  pl.BlockSpec((1, D_MODEL, D_EXPERT), lambda i, be, nu: (be[i], 0, 0)),
                  pl.BlockSpec((1, D_EXPERT, D_MODEL), lambda i, be, nu: (be[i], 0, 0))],
        out_specs=pl.BlockSpec((MOE_BLOCK, D_MODEL), lambda i, be, nu: (i, 0)),
        scratch_shapes=[pltpu.VMEM((D_MODEL, D_EXPERT), BF16),
                        pltpu.VMEM((D_MODEL, D_EXPERT), BF16),
                        pltpu.VMEM((D_EXPERT, D_MODEL), BF16)],
    )
    return pl.pallas_call(
        _moe_ffn_kernel,
        out_shape=jax.ShapeDtypeStruct((P, D_MODEL), F32),
        grid_spec=grid_spec,
        compiler_params=_cparams(("arbitrary",)),
        name="moe_ffn",
    )(blk_expert, n_used, xs, w_gate, w_up, w_down)


def _combine_kernel(dcur_ref, dnext_ref, h_ref, gate_ref, fg_ref, ys_hbm, o_ref, ybuf, sems, *, final):
    tm = h_ref.shape[0]
    i = pl.program_id(0)
    n = pl.num_programs(0)

    def gather(dref, slot):
        def issue(t, carry):
            for k in range(TOP_K):
                pltpu.make_async_copy(ys_hbm.at[pl.ds(dref[k, t], 1)],
                                      ybuf.at[slot, pl.ds(k * tm + t, 1)], sems.at[slot]).start()
            return carry

        lax.fori_loop(0, tm, issue, 0, unroll=8)

    @pl.when(i == 0)
    def _():
        gather(dcur_ref, 0)

    @pl.when(i + 1 < n)
    def _():
        gather(dnext_ref, (i + 1) % 2)

    slot = i % 2
    pltpu.make_async_copy(ys_hbm.at[pl.ds(0, TOP_K * tm)], ybuf.at[slot], sems.at[slot]).wait()
    g = gate_ref[...]
    hh = h_ref[...]
    for k in range(TOP_K):
        hh = hh + ybuf[slot, pl.ds(k * tm, tm), :] * g[:, k:k + 1]
    o_ref[...] = _rms(hh, fg_ref[...]) if final else hh


def _combine(h, ys, dest, gate_t, fg, tm, final):
    N = h.shape[0]
    n = N // tm
    rows = pl.BlockSpec((tm, D_MODEL), lambda i: (i, 0))
    return pl.pallas_call(
        functools.partial(_combine_kernel, final=final),
        out_shape=jax.ShapeDtypeStruct((N, D_MODEL), F32),
        grid=(n,),
        in_specs=[pl.BlockSpec((TOP_K, tm), lambda i: (0, i), memory_space=pltpu.SMEM),
                  pl.BlockSpec((TOP_K, tm), lambda i: (0, jnp.minimum(i + 1, n - 1)), memory_space=pltpu.SMEM),
                  rows, pl.BlockSpec((tm, TOP_K), lambda i: (i, 0)),
                  pl.BlockSpec((1, D_MODEL), lambda i: (0, 0)),
                  pl.BlockSpec(memory_space=pl.ANY)],
        out_specs=rows,
        scratch_shapes=[pltpu.VMEM((2, TOP_K * tm, D_MODEL), F32), pltpu.SemaphoreType.DMA((2,))],
        compiler_params=_cparams(("arbitrary",)),
        name="combine",
    )(dest, dest, h, gate_t, fg, ys)


def _reorder_w_in(w_in):
    n_gla = 2 * GLA_KEY_WIDTH + 2 * GLA_WIDTH
    z0 = n_gla
    a0 = z0 + 2 * GLA_GATE_RANK
    pad = jnp.zeros((D_MODEL, Z_PAD - 2 * GLA_GATE_RANK), w_in.dtype)
    return jnp.concatenate([w_in[:, :n_gla], w_in[:, a0:], w_in[:, z0:a0], pad], axis=1)


def kernel(x, norm1_gain, w_in, gla_up_fwd, gla_up_fwd_bias, gla_up_bwd, gla_up_bwd_bias, gla_out_gain, q_norm_gain, k_norm_gain, att_out_gain, w_out, norm2_gain, w_group, b_group, w_expert, b_expert, w_gate, w_up, w_down, final_gain):
    B, T, D = x.shape
    N = B * T
    depth = norm1_gain.shape[0]
    tm = min(512, T)
    tq = 128
    key_block = min(512, T)
    h = x.reshape(N, D)
    cos, se, so = _rope_tables(T)
    head_mean = jnp.asarray(
        np.kron(np.eye(ATT_HEADS, dtype=np.float32),
                np.full((ATT_HEAD_DIM, ATT_HEAD_DIM), 1.0 / ATT_HEAD_DIM, np.float32)), BF16)
    for l in range(depth):
        w = _reorder_w_in(w_in[l]).astype(BF16)
        r = GLA_GATE_RANK
        up = jnp.zeros((Z_PAD, 2 * GLA_KEY_WIDTH), F32)
        up = up.at[:r, :GLA_KEY_WIDTH].set(gla_up_fwd[l]).at[r:2 * r, GLA_KEY_WIDTH:].set(gla_up_bwd[l])
        upb = jnp.concatenate([gla_up_fwd_bias[l], gla_up_bwd_bias[l]])[None, :]
        qg = jnp.tile(q_norm_gain[l], ATT_HEADS)[None, :]
        kg = jnp.tile(k_norm_gain[l], ATT_KV_HEADS)[None, :]
        gq, gk, gv, gg, laf, lab, aq, akt, av = _in_proj(
            h, B, T, tm, norm1_gain[l][None, :], w, up.astype(BF16), upb, qg, kg, head_mean, cos, se, so)
        o_gla = _gla(gq, gk, gv, gg, laf, lab, gla_out_gain[l][None, :], B, T)
        o_att = _attn(aq, akt, av, B, T, tq, key_block)

        wr = jnp.zeros((LANES, D), F32)
        wr = wr.at[:N_GROUPS].set(w_group[l].T).at[N_GROUPS:N_GROUPS + N_EXPERTS].set(w_expert[l].T)
        br = jnp.zeros((LANES, 1), F32)
        br = br.at[:N_GROUPS, 0].set(b_group[l]).at[N_GROUPS:N_GROUPS + N_EXPERTS, 0].set(b_expert[l])
        h, xn, eid, gate, rank, cnt = _out_proj(
            h, o_gla, o_att, w_out[l].astype(BF16), att_out_gain[l][None, :], norm2_gain[l][None, :],
            wr.astype(BF16), br, tm)

        counts = cnt[:, 0].astype(jnp.int32)
        padded = (counts + MOE_BLOCK - 1) // MOE_BLOCK * MOE_BLOCK
        pad_ends = jnp.cumsum(padded)
        pad_starts = pad_ends - padded
        seg_start = jnp.sum(jnp.where(eid[:, :, None] == jnp.arange(N_EXPERTS, dtype=jnp.int32),
                                      pad_starts.astype(jnp.int32), 0), axis=-1)
        dest = seg_start + rank
        P = N * TOP_K + N_EXPERTS * MOE_BLOCK
        nblk = P // MOE_BLOCK
        blk_start = jnp.arange(nblk, dtype=jnp.int32) * MOE_BLOCK
        blk_expert = jnp.minimum(
            jnp.sum((pad_ends[None, :] <= blk_start[:, None]).astype(jnp.int32), axis=1), N_EXPERTS - 1)
        n_used = (pad_ends[-1:] // MOE_BLOCK).astype(jnp.int32)

        zero_starts = jnp.maximum(pad_ends - MOE_BLOCK, 0).astype(jnp.int32)
        xs = _dispatch(zero_starts, n_used, dest, xn, P, tm)
        ys = _moe_ffn(blk_expert, n_used, xs, w_gate[l], w_up[l], w_down[l])
        h = _combine(h, ys, dest, gate.T, final_gain[None, :], tm, l == depth - 1)
    return h.reshape(B, T, D)
```

```python
import functools

import numpy as np
import jax
import jax.numpy as jnp
from jax import lax
from jax.experimental import pallas as pl
from jax.experimental.pallas import tpu as pltpu

F32 = jnp.float32
BF16 = jnp.bfloat16

D_MODEL = 1024
EPS = 1e-6
GRID_W = 64

GLA_HEADS = 4
GLA_DK = 64
GLA_DV = 128
GLA_KEY_WIDTH = GLA_HEADS * GLA_DK
GLA_WIDTH = GLA_HEADS * GLA_DV
GLA_GATE_RANK = 16
GLA_GATE_NORM = 16.0
GLA_CHUNK = 64

ATT_HEADS = 8
ATT_KV_HEADS = 2
ATT_HEAD_DIM = 64
ATT_GROUP = ATT_HEADS // ATT_KV_HEADS
ATT_WIDTH = ATT_HEADS * ATT_HEAD_DIM
ATT_KV_WIDTH = ATT_KV_HEADS * ATT_HEAD_DIM
ROPE_THETA = 10000.0
LOG2_E = 1.4426950408889634

N_GROUPS = 8
EXPERTS_PER_GROUP = 8
N_EXPERTS = N_GROUPS * EXPERTS_PER_GROUP
TOP_K = 2
D_EXPERT = 512
MOE_BLOCK = 128

LANES = 128
Z_PAD = LANES

_OFF_GQ = 0
_OFF_GK = _OFF_GQ + GLA_KEY_WIDTH
_OFF_GV = _OFF_GK + GLA_KEY_WIDTH
_OFF_GG = _OFF_GV + GLA_WIDTH
_OFF_AQ = _OFF_GG + GLA_WIDTH
_OFF_AK = _OFF_AQ + ATT_WIDTH
_OFF_AV = _OFF_AK + ATT_KV_WIDTH
_OFF_Z = _OFF_AV + ATT_KV_WIDTH
D_IN_PAD = _OFF_Z + Z_PAD

VMEM_LIMIT = 56 * 1024 * 1024


def _cparams(semantics):
    return pltpu.CompilerParams(dimension_semantics=semantics, vmem_limit_bytes=VMEM_LIMIT)


def _rms(x, gain):
    return x * lax.rsqrt(jnp.mean(x * x, axis=-1, keepdims=True) + EPS) * gain


def _sigmoid(x):
    return 1.0 / (1.0 + jnp.exp(-x))


def _head_norm_rope(x, head_mean, gain, cos, sin_even, sin_odd):
    w = x.shape[1]
    ms = jnp.dot((x * x).astype(BF16), head_mean, preferred_element_type=F32)
    xn = x * lax.rsqrt(ms + EPS) * gain
    reps = w // LANES
    tile = lambda t: t if reps == 1 else jnp.concatenate([t] * reps, axis=1)
    nxt = pltpu.roll(xn, w - 1, 1)
    prv = pltpu.roll(xn, 1, 1)
    return xn * tile(cos) + nxt * tile(sin_even) + prv * tile(sin_odd)


def _in_proj_kernel(x_ref, g1_ref, w_ref, up_ref, upb_ref, qg_ref, kg_ref, hm_ref,
                    cos_ref, se_ref, so_ref,
                    gq_ref, gk_ref, gv_ref, gg_ref, laf_ref, lab_ref, aqt_ref, ak_ref, avt_ref):
    u = _rms(x_ref[...], g1_ref[...])
    res = jnp.dot(u.astype(BF16), w_ref[...], preferred_element_type=F32)
    gq_ref[...] = res[:, _OFF_GQ:_OFF_GK].astype(BF16)
    gk_ref[...] = res[:, _OFF_GK:_OFF_GV].astype(BF16)
    gv_ref[...] = res[:, _OFF_GV:_OFF_GG].astype(BF16)
    gg_ref[...] = res[:, _OFF_GG:_OFF_AQ].astype(BF16)
    avt_ref[0] = res[:, _OFF_AV:_OFF_Z].T.astype(BF16)

    z = res[:, _OFF_Z:D_IN_PAD].astype(BF16)
    zl = jnp.dot(z, up_ref[...], preferred_element_type=F32) + upb_ref[...]
    la = (jnp.minimum(zl, 0.0) - jnp.log(1.0 + jnp.exp(-jnp.abs(zl)))) * (1.0 / GLA_GATE_NORM)
    laf_ref[...] = la[:, :GLA_KEY_WIDTH]
    lab_ref[...] = la[:, GLA_KEY_WIDTH:]

    cos, se, so = cos_ref[...], se_ref[...], so_ref[...]
    hm = hm_ref[...]
    q = _head_norm_rope(res[:, _OFF_AQ:_OFF_AK], hm, qg_ref[...], cos, se, so)
    aqt_ref[0] = (q * (ATT_HEAD_DIM ** -0.5 * LOG2_E)).T.astype(BF16)
    k = _head_norm_rope(res[:, _OFF_AK:_OFF_AV], hm[:ATT_KV_WIDTH, :ATT_KV_WIDTH], kg_ref[...], cos, se, so)
    ak_ref[...] = k.astype(BF16)


def _rope_tables(T):
    t = np.arange(T)
    row = (t // GRID_W).astype(np.float32)
    col = (t % GRID_W).astype(np.float32)
    axis_dim = ATT_HEAD_DIM // 2
    inv_freq = (ROPE_THETA ** (-np.arange(0, axis_dim, 2, dtype=np.float32) / axis_dim)).astype(np.float32)
    ang = np.concatenate([row[:, None] * inv_freq, col[:, None] * inv_freq], axis=-1)
    ang = np.repeat(ang, 2, axis=1)
    ang = np.tile(ang, (1, LANES // ATT_HEAD_DIM))
    even = (np.arange(LANES) % 2 == 0)[None, :]
    cos = np.cos(ang)
    sin = np.sin(ang)
    return (jnp.asarray(cos, F32), jnp.asarray(np.where(even, -sin, 0.0), F32),
            jnp.asarray(np.where(even, 0.0, sin), F32))


def _in_proj(xt, B, T, tm, g1, w, up, upb, qg, kg, hm, cos, se, so):
    N = xt.shape[0]
    tpb = T // tm
    const = lambda shape: pl.BlockSpec(shape, lambda i: (0,) * len(shape))
    rows = lambda width: pl.BlockSpec((tm, width), lambda i: (i, 0))
    pos = pl.BlockSpec((tm, LANES), lambda i: (i % tpb, 0))
    out_shape = (
        jax.ShapeDtypeStruct((N, GLA_KEY_WIDTH), BF16),
        jax.ShapeDtypeStruct((N, GLA_KEY_WIDTH), BF16),
        jax.ShapeDtypeStruct((N, GLA_WIDTH), BF16),
        jax.ShapeDtypeStruct((N, GLA_WIDTH), BF16),
        jax.ShapeDtypeStruct((N, GLA_KEY_WIDTH), F32),
        jax.ShapeDtypeStruct((N, GLA_KEY_WIDTH), F32),
        jax.ShapeDtypeStruct((B, ATT_WIDTH, T), BF16),
        jax.ShapeDtypeStruct((N, ATT_KV_WIDTH), BF16),
        jax.ShapeDtypeStruct((B, ATT_KV_WIDTH, T), BF16),
    )
    cols = lambda width: pl.BlockSpec((1, width, tm), lambda i: (i // tpb, 0, i % tpb))
    out_specs = (
        rows(GLA_KEY_WIDTH), rows(GLA_KEY_WIDTH), rows(GLA_WIDTH), rows(GLA_WIDTH),
        rows(GLA_KEY_WIDTH), rows(GLA_KEY_WIDTH),
        cols(ATT_WIDTH), rows(ATT_KV_WIDTH), cols(ATT_KV_WIDTH),
    )
    return pl.pallas_call(
        _in_proj_kernel,
        out_shape=out_shape,
        grid=(N // tm,),
        in_specs=[rows(D_MODEL), const((1, D_MODEL)), const((D_MODEL, D_IN_PAD)),
                  const((Z_PAD, 2 * GLA_KEY_WIDTH)), const((1, 2 * GLA_KEY_WIDTH)),
                  const((1, ATT_WIDTH)), const((1, ATT_KV_WIDTH)), const((ATT_WIDTH, ATT_WIDTH)),
                  pos, pos, pos],
        out_specs=out_specs,
        compiler_params=_cparams(("arbitrary",)),
        name="in_proj",
    )(xt, g1, w, up, upb, qg, kg, hm, cos, se, so)


def _gla_kernel(q_ref, k_ref, v_ref, g_ref, laf_ref, lab_ref, gain_ref, o_ref, part_scr):
    C = GLA_CHUNK
    T = q_ref.shape[0]
    n_chunks = T // C
    row = lax.broadcasted_iota(jnp.int32, (C, C), 0)
    col = lax.broadcasted_iota(jnp.int32, (C, C), 1)
    tril = row >= col
    triu = row <= col
    gain = gain_ref[...]

    def chunk(r0, state, la_ref, mask, i_ref, i_last):
        q2 = q_ref[pl.ds(r0, C), :].astype(F32) * (GLA_DK ** -0.5)
        k2 = k_ref[pl.ds(r0, C), :].astype(F32)
        v2 = v_ref[pl.ds(r0, C), :]
        la = la_ref[pl.ds(r0, C), :]
        tri = mask.astype(BF16)
        la_hi = la.astype(BF16)
        la_lo = (la - la_hi.astype(F32)).astype(BF16)
        b = (jnp.dot(tri, la_hi, preferred_element_type=F32)
             + jnp.dot(tri, la_lo, preferred_element_type=F32))
        b_ref = b[i_ref:i_ref + 1, :]
        b_last = b[i_last:i_last + 1, :]
        qf = (q2 * jnp.exp(b - b_ref)).astype(BF16)
        kf = (k2 * jnp.exp(b_ref - b)).astype(BF16)
        qi = (q2 * jnp.exp(b)).astype(BF16)
        kl = (k2 * jnp.exp(b_last - b)).astype(BF16)
        decay = jnp.exp(b_last)
        outs, new_state = [], []
        for h in range(2):
            ks = slice(h * GLA_DK, (h + 1) * GLA_DK)
            vh = v2[:, h * GLA_DV:(h + 1) * GLA_DV]
            sc = lax.dot_general(qf[:, ks], kf[:, ks], (((1,), (1,)), ((), ())),
                                 preferred_element_type=F32)
            sc = jnp.where(mask, sc, 0.0).astype(BF16)
            o = jnp.dot(sc, vh, preferred_element_type=F32)
            o = o + lax.dot_general(qi[:, ks], state[h].astype(BF16), (((1,), (1,)), ((), ())),
                                    preferred_element_type=F32)
            kv_t = lax.dot_general(vh, kl[:, ks], (((0,), (0,)), ((), ())),
                                   preferred_element_type=F32)
            new_state.append(decay[:, ks] * state[h] + kv_t)
            outs.append(o)
        return jnp.concatenate(outs, axis=1), tuple(new_state)

    zero_state = tuple(jnp.zeros((GLA_DV, GLA_DK), F32) for _ in range(2))

    def both_directions(n, carry):
        rf = pl.multiple_of(n * C, C)
        rb = pl.multiple_of((n_chunks - 1 - n) * C, C)
        o_f, sf = chunk(rf, carry[0], laf_ref, tril, C // 2 - 1, C - 1)
        o_b, sb = chunk(rb, carry[1], lab_ref, triu, C // 2, 0)
        return (rf, o_f), (rb, o_b), (sf, sb)

    def finalize(r0, o):
        g = g_ref[pl.ds(r0, C), :].astype(F32)
        normed = jnp.concatenate(
            [_rms(o[:, h * GLA_DV:(h + 1) * GLA_DV], gain) for h in range(2)], axis=1)
        o_ref[pl.ds(r0, C), :] = (normed * (g * _sigmoid(g))).astype(BF16)

    def first_visits(n, carry):
        (rf, o_f), (rb, o_b), carry = both_directions(n, carry)
        part_scr[pl.ds(rf, C), :] = o_f
        part_scr[pl.ds(rb, C), :] = o_b
        return carry

    def second_visits(n, carry):
        (rf, o_f), (rb, o_b), carry = both_directions(n, carry)
        finalize(rf, part_scr[pl.ds(rf, C), :] + o_f)
        finalize(rb, part_scr[pl.ds(rb, C), :] + o_b)
        return carry

    carry = lax.fori_loop(0, n_chunks // 2, first_visits, (zero_state, zero_state), unroll=2)
    lax.fori_loop(n_chunks // 2, n_chunks, second_visits, carry, unroll=2)


def _gla(gq, gk, gv, gg, laf, lab, gain, B, T):
    N = gq.shape[0]
    pairs = GLA_HEADS // 2
    kspec = pl.BlockSpec((T, 2 * GLA_DK), lambda b, p: (b, p))
    vspec = pl.BlockSpec((T, 2 * GLA_DV), lambda b, p: (b, p))
    return pl.pallas_call(
        _gla_kernel,
        out_shape=jax.ShapeDtypeStruct((N, GLA_WIDTH), BF16),
        grid=(B, pairs),
        in_specs=[kspec, kspec, vspec, vspec, kspec, kspec,
                  pl.BlockSpec((1, GLA_DV), lambda b, p: (0, 0))],
        out_specs=vspec,
        scratch_shapes=[pltpu.VMEM((T, 2 * GLA_DV), F32)],
        compiler_params=_cparams(("arbitrary", "arbitrary")),
        name="gla",
    )(gq, gk, gv, gg, laf, lab, gain)


def _attn_kernel(qt_ref, k_ref, vt_ref, ot_ref, vext_scr, s0_scr, s1_scr, p0_scr, p1_scr, m0_scr, m1_scr,
                 *, tq, key_block, steps_per_iter):
    j = pl.program_id(1)
    T = k_ref.shape[0]
    nq = T // tq
    dh = ATT_HEAD_DIM
    cols = ATT_GROUP * tq
    first = j == 0

    row = lax.broadcasted_iota(jnp.int32, (ATT_KV_WIDTH, T), 0)
    own = (row >= j * dh) & (row < (j + 1) * dh)
    vext_scr[...] = jnp.where(own, vt_ref[0], jnp.ones((), BF16))

    s_scr, p_scr, m_scr = (s0_scr, s1_scr), (p0_scr, p1_scr), (m0_scr, m1_scr)

    def step(blk, sa, do_scores=True, do_probs=True, do_out=True):
        sb = 1 - sa
        if do_scores:
            c0 = pl.multiple_of(blk * tq, tq)
            q4 = jnp.concatenate([qt_ref[0, h * dh:(h + 1) * dh, pl.ds(c0, tq)] for h in range(ATT_GROUP)],
                                 axis=1)
            zeros = jnp.zeros_like(q4)
            qe = jnp.where(first, jnp.concatenate([q4, zeros], axis=0), jnp.concatenate([zeros, q4], axis=0))
            m = jnp.full((8, cols), -jnp.inf, F32)
        if do_probs:
            mx = m_scr[sb][...]
        if do_out:
            acc = jnp.zeros((ATT_KV_WIDTH, cols), F32)
        for kb in range(T // key_block):
            ks = slice(kb * key_block, (kb + 1) * key_block)
            if do_scores:
                s = jnp.dot(k_ref[ks, :], qe, preferred_element_type=F32)
                s_scr[sa][ks, :] = s
                m = jnp.maximum(m, jnp.max(s.reshape(key_block // 8, 8, cols), axis=0))
            if do_probs:
                p_scr[sb][ks, :] = jnp.exp2(s_scr[sb][ks, :] - mx).astype(BF16)
            if do_out:
                acc = acc + jnp.dot(vext_scr[:, ks], p_scr[sa][ks, :], preferred_element_type=F32)
        if do_scores:
            m_scr[sa][...] = jnp.max(m, axis=0, keepdims=True)
        if do_out:
            num = jnp.where(first, acc[:dh], acc[dh:])
            den = jnp.where(first, acc[dh:dh + 1], acc[0:1])
            o = (num / den).astype(BF16)
            c0 = pl.multiple_of((blk - 2) * tq, tq)
            for h in range(ATT_GROUP):
                ot_ref[0, h * dh:(h + 1) * dh, pl.ds(c0, tq)] = o[:, h * tq:(h + 1) * tq]

    step(0, 0, do_probs=False, do_out=False)
    step(1, 1, do_out=False)

    def steady(it, carry):
        for u in range(steps_per_iter):
            pl.when(it >= 0)(functools.partial(step, 2 + steps_per_iter * it + u, u % 2))
        return carry

    lax.fori_loop(0, (nq - 2) // steps_per_iter, steady, 0)
    step(nq, 0, do_scores=False)
    step(nq + 1, 1, do_scores=False, do_probs=False)


def _attn(aqt, ak, avt, B, T, tq, key_block):
    assert T % (2 * tq) == 0 and tq % LANES == 0
    gw = ATT_GROUP * ATT_HEAD_DIM
    cols = ATT_GROUP * tq
    steps_per_iter = 2
    return pl.pallas_call(
        functools.partial(_attn_kernel, tq=tq, key_block=key_block, steps_per_iter=steps_per_iter),
        out_shape=jax.ShapeDtypeStruct((B, ATT_WIDTH, T), BF16),
        grid=(B, ATT_KV_HEADS),
        in_specs=[pl.BlockSpec((1, gw, T), lambda b, j: (b, j, 0)),
                  pl.BlockSpec((T, ATT_KV_WIDTH), lambda b, j: (b, 0)),
                  pl.BlockSpec((1, ATT_KV_WIDTH, T), lambda b, j: (b, 0, 0))],
        out_specs=pl.BlockSpec((1, gw, T), lambda b, j: (b, j, 0)),
        scratch_shapes=[pltpu.VMEM((ATT_KV_WIDTH, T), BF16),
                        pltpu.VMEM((T, cols), F32), pltpu.VMEM((T, cols), F32),
                        pltpu.VMEM((T, cols), BF16), pltpu.VMEM((T, cols), BF16),
                        pltpu.VMEM((1, cols), F32), pltpu.VMEM((1, cols), F32)],
        compiler_params=_cparams(("arbitrary", "arbitrary")),
        name="attn",
    )(aqt, ak, avt)


def _out_proj_kernel(x_ref, og_ref, oa_ref, wo_ref, ag_ref, g2_ref, wr_ref, br_ref, tri_ref,
                     h_ref, xn_ref, eid_ref, gate_ref, rank_ref, cnt_ref, run_scr):
    tm = x_ref.shape[0]

    @pl.when(pl.program_id(0) == 0)
    def _():
        run_scr[...] = jnp.zeros_like(run_scr)

    oat = oa_ref[0].astype(F32)
    oan = (oat * lax.rsqrt(jnp.mean(oat * oat, axis=0, keepdims=True) + EPS) * ag_ref[...]).astype(BF16)
    y = (jnp.dot(og_ref[...], wo_ref[:GLA_WIDTH, :], preferred_element_type=F32)
         + lax.dot_general(oan, wo_ref[GLA_WIDTH:, :], (((0,), (0,)), ((), ())),
                           preferred_element_type=F32))
    h = x_ref[...] + y
    h_ref[...] = h
    xn = _rms(h, g2_ref[...])
    xn_ref[...] = xn

    lt = lax.dot_general(wr_ref[...], xn.astype(BF16), (((1,), (1,)), ((), ())),
                         preferred_element_type=F32) + br_ref[...]
    iota8 = lax.broadcasted_iota(jnp.int32, (N_GROUPS, tm), 0)

    def first_argmax(v):
        top = jnp.max(v, axis=0, keepdims=True)
        idx = jnp.min(jnp.where(v == top, iota8, N_GROUPS), axis=0, keepdims=True)
        return top, idx

    gl = lt[0:N_GROUPS]
    gmax, gidx = first_argmax(gl)
    gw = 1.0 / jnp.sum(jnp.exp(gl - gmax), axis=0, keepdims=True)
    esel = jnp.zeros((EXPERTS_PER_GROUP, tm), F32)
    for g in range(N_GROUPS):
        lo = N_GROUPS + g * EXPERTS_PER_GROUP
        esel = jnp.where(gidx == g, lt[lo:lo + EXPERTS_PER_GROUP], esel)
    v1, i1 = first_argmax(esel)
    rest = jnp.where(iota8 == i1, -jnp.inf, esel)
    v2, i2 = first_argmax(rest)
    t = jnp.exp(v2 - v1)
    den = 1.0 + t
    e1 = gidx * EXPERTS_PER_GROUP + i1
    e2 = gidx * EXPERTS_PER_GROUP + i2
    eid_ref[0:1, :] = e1
    eid_ref[1:2, :] = e2
    gate_ref[0:1, :] = gw * (1.0 / den)
    gate_ref[1:2, :] = gw * (t / den)

    iota_e = lax.broadcasted_iota(jnp.int32, (N_EXPERTS, tm), 0)
    oh1 = iota_e == e1
    oh2 = iota_e == e2
    both = jnp.where(oh1, 1.0, jnp.where(oh2, 1.0, 0.0)).astype(BF16)
    prefix = jnp.dot(both, tri_ref[...], preferred_element_type=F32)
    run = run_scr[...]
    base = prefix - 1.0 + run[:, 0:1]
    rank_ref[0:1, :] = jnp.sum(jnp.where(oh1, base, 0.0), axis=0, keepdims=True).astype(jnp.int32)
    rank_ref[1:2, :] = jnp.sum(jnp.where(oh2, base, 0.0), axis=0, keepdims=True).astype(jnp.int32)
    run = run + prefix[:, tm - 1:tm]
    run_scr[...] = run
    cnt_ref[...] = run


def _out_proj(xt, o_gla, o_att_t, wo, ag, g2, wr, br, tm):
    N = xt.shape[0]
    tpb = o_att_t.shape[2] // tm
    const = lambda shape: pl.BlockSpec(shape, lambda i: (0,) * len(shape))
    rows = lambda width: pl.BlockSpec((tm, width), lambda i: (i, 0))
    cols = pl.BlockSpec((TOP_K, tm), lambda i: (0, i))
    tri = jnp.asarray(np.triu(np.ones((tm, tm), np.float32)), BF16)
    return pl.pallas_call(
        _out_proj_kernel,
        out_shape=(jax.ShapeDtypeStruct((N, D_MODEL), F32),
                   jax.ShapeDtypeStruct((N, D_MODEL), F32),
                   jax.ShapeDtypeStruct((TOP_K, N), jnp.int32),
                   jax.ShapeDtypeStruct((TOP_K, N), F32),
                   jax.ShapeDtypeStruct((TOP_K, N), jnp.int32),
                   jax.ShapeDtypeStruct((N_EXPERTS, LANES), F32)),
        grid=(N // tm,),
        in_specs=[rows(D_MODEL), rows(GLA_WIDTH),
                  pl.BlockSpec((1, ATT_WIDTH, tm), lambda i: (i // tpb, 0, i % tpb)),
                  const((D_MODEL, D_MODEL)),
                  const((ATT_WIDTH, 1)), const((1, D_MODEL)), const((LANES, D_MODEL)),
                  const((LANES, 1)), const((tm, tm))],
        out_specs=(rows(D_MODEL), rows(D_MODEL), cols, cols, cols, const((N_EXPERTS, LANES))),
        scratch_shapes=[pltpu.VMEM((N_EXPERTS, LANES), F32)],
        compiler_params=_cparams(("arbitrary",)),
        name="out_proj",
    )(xt, o_gla, o_att_t, wo, ag, g2, wr, br, tri)


def _dispatch_kernel(zs_ref, nu_ref, dest_ref, xn_ref, xs_hbm, zero_buf, zero_sem, row_sem):
    tn = dest_ref.shape[1]
    i = pl.program_id(0)

    def fill(row0):
        return pltpu.make_async_copy(
            zero_buf, xs_hbm.at[pl.ds(pl.multiple_of(row0, MOE_BLOCK), MOE_BLOCK)], zero_sem)

    @pl.when(i == 0)
    def _():
        zero_buf[...] = jnp.zeros_like(zero_buf)
        fills = [fill(zs_ref[e]) for e in range(N_EXPERTS)]
        for f in fills:
            f.start()
        for f in fills:
            f.wait()

        def fill_tail(blk, carry):
            f = fill(blk * MOE_BLOCK)
            f.start()
            f.wait()
            return carry

        lax.fori_loop(nu_ref[0], xs_hbm.shape[0] // MOE_BLOCK, fill_tail, 0)

    def issue(t, carry):
        for k in range(TOP_K):
            pltpu.make_async_copy(xn_ref.at[pl.ds(t, 1)], xs_hbm.at[pl.ds(dest_ref[k, t], 1)],
                                  row_sem).start()
        return carry

    lax.fori_loop(0, tn, issue, 0, unroll=8)
    for k in range(TOP_K):
        pltpu.make_async_copy(xn_ref, xs_hbm.at[pl.ds(0, tn)], row_sem).wait()


def _dispatch(zero_starts, n_used, dest, xn, P, tn):
    N = xn.shape[0]
    grid_spec = pltpu.PrefetchScalarGridSpec(
        num_scalar_prefetch=2,
        grid=(N // tn,),
        in_specs=[pl.BlockSpec((TOP_K, tn), lambda i, zs, nu: (0, i), memory_space=pltpu.SMEM),
                  pl.BlockSpec((tn, D_MODEL), lambda i, zs, nu: (i, 0))],
        out_specs=pl.BlockSpec(memory_space=pl.ANY),
        scratch_shapes=[pltpu.VMEM((MOE_BLOCK, D_MODEL), F32),
                        pltpu.SemaphoreType.DMA, pltpu.SemaphoreType.DMA],
    )
    return pl.pallas_call(
        _dispatch_kernel,
        out_shape=jax.ShapeDtypeStruct((P, D_MODEL), F32),
        grid_spec=grid_spec,
        compiler_params=_cparams(("arbitrary",)),
        name="dispatch",
    )(zero_starts, n_used, dest, xn)


def _moe_ffn_kernel(be_ref, nu_ref, xs_ref, wg_ref, wu_ref, wd_ref, ys_ref, wg_s, wu_s, wd_s):
    i = pl.program_id(0)
    used = i < nu_ref[0]

    @pl.when(used)
    def _():
        new_expert = jnp.logical_or(i == 0, be_ref[i] != be_ref[jnp.maximum(i - 1, 0)])

        @pl.when(new_expert)
        def _():
            wg_s[...] = wg_ref[0].astype(BF16)
            wu_s[...] = wu_ref[0].astype(BF16)
            wd_s[...] = wd_ref[0].astype(BF16)

        xb = xs_ref[...].astype(BF16)
        hg = jnp.dot(xb, wg_s[...], preferred_element_type=F32)
        hu = jnp.dot(xb, wu_s[...], preferred_element_type=F32)
        hm = (hg * _sigmoid(hg) * hu).astype(BF16)
        ys_ref[...] = jnp.dot(hm, wd_s[...], preferred_element_type=F32)

    @pl.when(jnp.logical_not(used))
    def _():
        ys_ref[...] = jnp.zeros_like(ys_ref)


def _moe_ffn(blk_expert, n_used, xs, w_gate, w_up, w_down):
    P = xs.shape[0]
    nblk = P // MOE_BLOCK
    grid_spec = pltpu.PrefetchScalarGridSpec(
        num_scalar_prefetch=2,
        grid=(nblk,),
        in_specs=[pl.BlockSpec((MOE_BLOCK, D_MODEL), lambda i, be, nu: (i, 0)),
                  pl.BlockSpec((1, D_MODEL, D_EXPERT), lambda i, be, nu: (be[i], 0, 0)),
                  pl.BlockSpec((1, D_MODEL, D_EXPERT), lambda i, be, nu: (be[i], 0, 0)),
                  pl.BlockSpec((1, D_EXPERT, D_MODEL), lambda i, be, nu: (be[i], 0, 0))],
        out_specs=pl.BlockSpec((MOE_BLOCK, D_MODEL), lambda i, be, nu: (i, 0)),
        scratch_shapes=[pltpu.VMEM((D_MODEL, D_EXPERT), BF16),
                        pltpu.VMEM((D_MODEL, D_EXPERT), BF16),
                        pltpu.VMEM((D_EXPERT, D_MODEL), BF16)],
    )
    return pl.pallas_call(
        _moe_ffn_kernel,
        out_shape=jax.ShapeDtypeStruct((P, D_MODEL), F32),
        grid_spec=grid_spec,
        compiler_params=_cparams(("arbitrary",)),
        name="moe_ffn",
    )(blk_expert, n_used, xs, w_gate, w_up, w_down)


def _combine_kernel(dcur_ref, dnext_ref, h_ref, gate_ref, fg_ref, ys_hbm, o_ref, ybuf, sems, *, final):
    tm = h_ref.shape[0]
    i = pl.program_id(0)
    n = pl.num_programs(0)

    def gather(dref, slot):
        def issue(t, carry):
            for k in range(TOP_K):
                pltpu.make_async_copy(ys_hbm.at[pl.ds(dref[k, t], 1)],
                                      ybuf.at[slot, pl.ds(k * tm + t, 1)], sems.at[slot]).start()
            return carry

        lax.fori_loop(0, tm, issue, 0, unroll=8)

    @pl.when(i == 0)
    def _():
        gather(dcur_ref, 0)

    @pl.when(i + 1 < n)
    def _():
        gather(dnext_ref, (i + 1) % 2)

    slot = i % 2
    pltpu.make_async_copy(ys_hbm.at[pl.ds(0, TOP_K * tm)], ybuf.at[slot], sems.at[slot]).wait()
    g = gate_ref[...]
    hh = h_ref[...]
    for k in range(TOP_K):
        hh = hh + ybuf[slot, pl.ds(k * tm, tm), :] * g[:, k:k + 1]
    o_ref[...] = _rms(hh, fg_ref[...]) if final else hh


def _combine(h, ys, dest, gate_t, fg, tm, final):
    N = h.shape[0]
    n = N // tm
    rows = pl.BlockSpec((tm, D_MODEL), lambda i: (i, 0))
    return pl.pallas_call(
        functools.partial(_combine_kernel, final=final),
        out_shape=jax.ShapeDtypeStruct((N, D_MODEL), F32),
        grid=(n,),
        in_specs=[pl.BlockSpec((TOP_K, tm), lambda i: (0, i), memory_space=pltpu.SMEM),
                  pl.BlockSpec((TOP_K, tm), lambda i: (0, jnp.minimum(i + 1, n - 1)), memory_space=pltpu.SMEM),
                  rows, pl.BlockSpec((tm, TOP_K), lambda i: (i, 0)),
                  pl.BlockSpec((1, D_MODEL), lambda i: (0, 0)),
                  pl.BlockSpec(memory_space=pl.ANY)],
        out_specs=rows,
        scratch_shapes=[pltpu.VMEM((2, TOP_K * tm, D_MODEL), F32), pltpu.SemaphoreType.DMA((2,))],
        compiler_params=_cparams(("arbitrary",)),
        name="combine",
    )(dest, dest, h, gate_t, fg, ys)


def _reorder_w_in(w_in):
    n_gla = 2 * GLA_KEY_WIDTH + 2 * GLA_WIDTH
    z0 = n_gla
    a0 = z0 + 2 * GLA_GATE_RANK
    pad = jnp.zeros((D_MODEL, Z_PAD - 2 * GLA_GATE_RANK), w_in.dtype)
    return jnp.concatenate([w_in[:, :n_gla], w_in[:, a0:], w_in[:, z0:a0], pad], axis=1)


def kernel(x, norm1_gain, w_in, gla_up_fwd, gla_up_fwd_bias, gla_up_bwd, gla_up_bwd_bias, gla_out_gain, q_norm_gain, k_norm_gain, att_out_gain, w_out, norm2_gain, w_group, b_group, w_expert, b_expert, w_gate, w_up, w_down, final_gain):
    B, T, D = x.shape
    N = B * T
    depth = norm1_gain.shape[0]
    tm = min(512, T)
    tq = 128
    key_block = min(512, T)
    h = x.reshape(N, D)
    cos, se, so = _rope_tables(T)
    head_mean = jnp.asarray(
        np.kron(np.eye(ATT_HEADS, dtype=np.float32),
                np.full((ATT_HEAD_DIM, ATT_HEAD_DIM), 1.0 / ATT_HEAD_DIM, np.float32)), BF16)
    for l in range(depth):
        w = _reorder_w_in(w_in[l]).astype(BF16)
        r = GLA_GATE_RANK
        up = jnp.zeros((Z_PAD, 2 * GLA_KEY_WIDTH), F32)
        up = up.at[:r, :GLA_KEY_WIDTH].set(gla_up_fwd[l]).at[r:2 * r, GLA_KEY_WIDTH:].set(gla_up_bwd[l])
        upb = jnp.concatenate([gla_up_fwd_bias[l], gla_up_bwd_bias[l]])[None, :]
        qg = jnp.tile(q_norm_gain[l], ATT_HEADS)[None, :]
        kg = jnp.tile(k_norm_gain[l], ATT_KV_HEADS)[None, :]
        gq, gk, gv, gg, laf, lab, aqt, ak, avt = _in_proj(
            h, B, T, tm, norm1_gain[l][None, :], w, up.astype(BF16), upb, qg, kg, head_mean, cos, se, so)
        o_gla = _gla(gq, gk, gv, gg, laf, lab, gla_out_gain[l][None, :], B, T)
        o_att_t = _attn(aqt, ak, avt, B, T, tq, key_block)

        wr = jnp.zeros((LANES, D), F32)
        wr = wr.at[:N_GROUPS].set(w_group[l].T).at[N_GROUPS:N_GROUPS + N_EXPERTS].set(w_expert[l].T)
        br = jnp.zeros((LANES, 1), F32)
        br = br.at[:N_GROUPS, 0].set(b_group[l]).at[N_GROUPS:N_GROUPS + N_EXPERTS, 0].set(b_expert[l])
        h, xn, eid, gate, rank, cnt = _out_proj(
            h, o_gla, o_att_t, w_out[l].astype(BF16), att_out_gain[l][:, None], norm2_gain[l][None, :],
            wr.astype(BF16), br, tm)

        counts = cnt[:, 0].astype(jnp.int32)
        padded = (counts + MOE_BLOCK - 1) // MOE_BLOCK * MOE_BLOCK
        pad_ends = jnp.cumsum(padded)
        pad_starts = pad_ends - padded
        seg_start = jnp.sum(jnp.where(eid[:, :, None] == jnp.arange(N_EXPERTS, dtype=jnp.int32),
                                      pad_starts.astype(jnp.int32), 0), axis=-1)
        dest = seg_start + rank
        P = N * TOP_K + N_EXPERTS * MOE_BLOCK
        nblk = P // MOE_BLOCK
        blk_start = jnp.arange(nblk, dtype=jnp.int32) * MOE_BLOCK
        blk_expert = jnp.minimum(
            jnp.sum((pad_ends[None, :] <= blk_start[:, None]).astype(jnp.int32), axis=1), N_EXPERTS - 1)
        n_used = (pad_ends[-1:] // MOE_BLOCK).astype(jnp.int32)

        zero_starts = jnp.maximum(pad_ends - MOE_BLOCK, 0).astype(jnp.int32)
        xs = _dispatch(zero_starts, n_used, dest, xn, P, tm)
        ys = _moe_ffn(blk_expert, n_used, xs, w_gate[l], w_up[l], w_down[l])
        h = _combine(h, ys, dest, gate.T, final_gain[None, :], tm, l == depth - 1)
    return h.reshape(B, T, D)
```

```python
import functools

import numpy as np
import jax
import jax.numpy as jnp
from jax import lax
from jax.experimental import pallas as pl
from jax.experimental.pallas import tpu as pltpu

F32 = jnp.float32
BF16 = jnp.bfloat16

D_MODEL = 1024
EPS = 1e-6
GRID_W = 64

GLA_HEADS = 4
GLA_DK = 64
GLA_DV = 128
GLA_KEY_WIDTH = GLA_HEADS * GLA_DK
GLA_WIDTH = GLA_HEADS * GLA_DV
GLA_GATE_RANK = 16
GLA_GATE_NORM = 16.0
GLA_CHUNK = 64

ATT_HEADS = 8
ATT_KV_HEADS = 2
ATT_HEAD_DIM = 64
ATT_GROUP = ATT_HEADS // ATT_KV_HEADS
ATT_WIDTH = ATT_HEADS * ATT_HEAD_DIM
ATT_KV_WIDTH = ATT_KV_HEADS * ATT_HEAD_DIM
ROPE_THETA = 10000.0
LOG2_E = 1.4426950408889634

N_GROUPS = 8
EXPERTS_PER_GROUP = 8
N_EXPERTS = N_GROUPS * EXPERTS_PER_GROUP
TOP_K = 2
D_EXPERT = 512
MOE_BLOCK = 256

LANES = 128
Z_PAD = LANES

_OFF_GQ = 0
_OFF_GK = _OFF_GQ + GLA_KEY_WIDTH
_OFF_GV = _OFF_GK + GLA_KEY_WIDTH
_OFF_GG = _OFF_GV + GLA_WIDTH
_OFF_AQ = _OFF_GG + GLA_WIDTH
_OFF_AK = _OFF_AQ + ATT_WIDTH
_OFF_AV = _OFF_AK + ATT_KV_WIDTH
_OFF_Z = _OFF_AV + ATT_KV_WIDTH
D_IN_PAD = _OFF_Z + Z_PAD

VMEM_LIMIT = 56 * 1024 * 1024


def _cparams(semantics):
    return pltpu.CompilerParams(dimension_semantics=semantics, vmem_limit_bytes=VMEM_LIMIT)


def _rms(x, gain):
    return x * lax.rsqrt(jnp.mean(x * x, axis=-1, keepdims=True) + EPS) * gain


def _sigmoid(x):
    return 1.0 / (1.0 + jnp.exp(-x))


def _head_norm_rope(x, head_mean, gain, cos, sin_even, sin_odd):
    w = x.shape[1]
    ms = jnp.dot((x * x).astype(BF16), head_mean, preferred_element_type=F32)
    xn = x * lax.rsqrt(ms + EPS) * gain
    reps = w // LANES
    tile = lambda t: t if reps == 1 else jnp.concatenate([t] * reps, axis=1)
    nxt = pltpu.roll(xn, w - 1, 1)
    prv = pltpu.roll(xn, 1, 1)
    return xn * tile(cos) + nxt * tile(sin_even) + prv * tile(sin_odd)


def _in_proj_kernel(x_ref, g1_ref, w_ref, up_ref, upb_ref, qg_ref, kg_ref, hm_ref,
                    cos_ref, se_ref, so_ref,
                    gq_ref, gk_ref, gv_ref, gg_ref, laf_ref, lab_ref, aqt_ref, ak_ref, avt_ref):
    u = _rms(x_ref[...], g1_ref[...])
    res = jnp.dot(u.astype(BF16), w_ref[...], preferred_element_type=F32)
    gq_ref[...] = res[:, _OFF_GQ:_OFF_GK].astype(BF16)
    gk_ref[...] = res[:, _OFF_GK:_OFF_GV].astype(BF16)
    gv_ref[...] = res[:, _OFF_GV:_OFF_GG].astype(BF16)
    gg_ref[...] = res[:, _OFF_GG:_OFF_AQ].astype(BF16)
    avt_ref[0] = res[:, _OFF_AV:_OFF_Z].T.astype(BF16)

    z = res[:, _OFF_Z:D_IN_PAD].astype(BF16)
    zl = jnp.dot(z, up_ref[...], preferred_element_type=F32) + upb_ref[...]
    la = (jnp.minimum(zl, 0.0) - jnp.log(1.0 + jnp.exp(-jnp.abs(zl)))) * (1.0 / GLA_GATE_NORM)
    laf_ref[...] = la[:, :GLA_KEY_WIDTH]
    lab_ref[...] = la[:, GLA_KEY_WIDTH:]

    cos, se, so = cos_ref[...], se_ref[...], so_ref[...]
    hm = hm_ref[...]
    q = _head_norm_rope(res[:, _OFF_AQ:_OFF_AK], hm, qg_ref[...], cos, se, so)
    aqt_ref[0] = (q * (ATT_HEAD_DIM ** -0.5 * LOG2_E)).T.astype(BF16)
    k = _head_norm_rope(res[:, _OFF_AK:_OFF_AV], hm[:ATT_KV_WIDTH, :ATT_KV_WIDTH], kg_ref[...], cos, se, so)
    ak_ref[...] = k.astype(BF16)


def _rope_tables(T):
    t = np.arange(T)
    row = (t // GRID_W).astype(np.float32)
    col = (t % GRID_W).astype(np.float32)
    axis_dim = ATT_HEAD_DIM // 2
    inv_freq = (ROPE_THETA ** (-np.arange(0, axis_dim, 2, dtype=np.float32) / axis_dim)).astype(np.float32)
    ang = np.concatenate([row[:, None] * inv_freq, col[:, None] * inv_freq], axis=-1)
    ang = np.repeat(ang, 2, axis=1)
    ang = np.tile(ang, (1, LANES // ATT_HEAD_DIM))
    even = (np.arange(LANES) % 2 == 0)[None, :]
    cos = np.cos(ang)
    sin = np.sin(ang)
    return (jnp.asarray(cos, F32), jnp.asarray(np.where(even, -sin, 0.0), F32),
            jnp.asarray(np.where(even, 0.0, sin), F32))


def _in_proj(xt, B, T, tm, g1, w, up, upb, qg, kg, hm, cos, se, so):
    N = xt.shape[0]
    tpb = T // tm
    const = lambda shape: pl.BlockSpec(shape, lambda i: (0,) * len(shape))
    rows = lambda width: pl.BlockSpec((tm, width), lambda i: (i, 0))
    pos = pl.BlockSpec((tm, LANES), lambda i: (i % tpb, 0))
    out_shape = (
        jax.ShapeDtypeStruct((N, GLA_KEY_WIDTH), BF16),
        jax.ShapeDtypeStruct((N, GLA_KEY_WIDTH), BF16),
        jax.ShapeDtypeStruct((N, GLA_WIDTH), BF16),
        jax.ShapeDtypeStruct((N, GLA_WIDTH), BF16),
        jax.ShapeDtypeStruct((N, GLA_KEY_WIDTH), F32),
        jax.ShapeDtypeStruct((N, GLA_KEY_WIDTH), F32),
        jax.ShapeDtypeStruct((B, ATT_WIDTH, T), BF16),
        jax.ShapeDtypeStruct((N, ATT_KV_WIDTH), BF16),
        jax.ShapeDtypeStruct((B, ATT_KV_WIDTH, T), BF16),
    )
    cols = lambda width: pl.BlockSpec((1, width, tm), lambda i: (i // tpb, 0, i % tpb))
    out_specs = (
        rows(GLA_KEY_WIDTH), rows(GLA_KEY_WIDTH), rows(GLA_WIDTH), rows(GLA_WIDTH),
        rows(GLA_KEY_WIDTH), rows(GLA_KEY_WIDTH),
        cols(ATT_WIDTH), rows(ATT_KV_WIDTH), cols(ATT_KV_WIDTH),
    )
    return pl.pallas_call(
        _in_proj_kernel,
        out_shape=out_shape,
        grid=(N // tm,),
        in_specs=[rows(D_MODEL), const((1, D_MODEL)), const((D_MODEL, D_IN_PAD)),
                  const((Z_PAD, 2 * GLA_KEY_WIDTH)), const((1, 2 * GLA_KEY_WIDTH)),
                  const((1, ATT_WIDTH)), const((1, ATT_KV_WIDTH)), const((ATT_WIDTH, ATT_WIDTH)),
                  pos, pos, pos],
        out_specs=out_specs,
        compiler_params=_cparams(("arbitrary",)),
        name="in_proj",
    )(xt, g1, w, up, upb, qg, kg, hm, cos, se, so)


GLA_GROUP = 4


def _gla_kernel(q_ref, k_ref, v_ref, g_ref, laf_ref, lab_ref, gain_ref, o_ref,
                part_scr, qif_scr, qib_scr, kvf_scr, kvb_scr, decf_scr, decb_scr, lf_scr, lb_scr):
    C, G = GLA_CHUNK, GLA_GROUP
    R = C * G
    T = q_ref.shape[0]
    n_chunks = T // C
    gain = gain_ref[...]
    row = lax.broadcasted_iota(jnp.int32, (C, C), 0)
    col = lax.broadcasted_iota(jnp.int32, (C, C), 1)
    tril = row >= col
    triu = row <= col
    rr = lax.broadcasted_iota(jnp.int32, (R, R), 0)
    cc = lax.broadcasted_iota(jnp.int32, (R, R), 1)
    same_chunk = (rr // C) == (cc // C)
    lf_scr[...] = jnp.where(same_chunk & (rr >= cc), 1.0, 0.0).astype(BF16)
    lb_scr[...] = jnp.where(same_chunk & (rr <= cc), 1.0, 0.0).astype(BF16)

    directions = ((laf_ref, lf_scr, tril, C // 2 - 1, C - 1, qif_scr, kvf_scr, decf_scr),
                  (lab_ref, lb_scr, triu, C // 2, 0, qib_scr, kvb_scr, decb_scr))
    heads = tuple((slice(h * GLA_DK, (h + 1) * GLA_DK), slice(h * GLA_DV, (h + 1) * GLA_DV)) for h in range(2))
    nt = (((1,), (1,)), ((), ()))
    tn = (((0,), (0,)), ((), ()))

    def intra(sb, carry):
        r0 = pl.multiple_of(sb * R, R)
        q2 = q_ref[pl.ds(r0, R), :].astype(F32) * (GLA_DK ** -0.5)
        k2 = k_ref[pl.ds(r0, R), :].astype(F32)
        v2 = v_ref[pl.ds(r0, R), :]
        cums = []
        for la_ref, tri_scr, *_ in directions:
            la = la_ref[pl.ds(r0, R), :]
            la_hi = la.astype(BF16)
            la_lo = (la - la_hi.astype(F32)).astype(BF16)
            tri = tri_scr[...]
            cums.append(jnp.dot(tri, la_hi, preferred_element_type=F32)
                        + jnp.dot(tri, la_lo, preferred_element_type=F32))
        scaled = []
        for b, (_, _, mask, i_ref, i_last, qi_scr, kv_scr, dec_scr) in zip(cums, directions):
            for c in range(G):
                rows = slice(c * C, (c + 1) * C)
                bc, qc, kc = b[rows], q2[rows], k2[rows]
                b_ref = bc[i_ref:i_ref + 1, :]
                b_last = bc[i_last:i_last + 1, :]
                qf = (qc * jnp.exp(bc - b_ref)).astype(BF16)
                kf = (kc * jnp.exp(b_ref - bc)).astype(BF16)
                qi_scr[pl.ds(r0 + c * C, C), :] = (qc * jnp.exp(bc)).astype(BF16)
                kl = (kc * jnp.exp(b_last - bc)).astype(BF16)
                decay = jnp.exp(b_last)
                for h, (ks, _) in enumerate(heads):
                    dec_scr[sb * G + c, h] = decay[:, ks]
                scaled.append((c, mask, kv_scr, qf, kf, kl))
        raw = [[lax.dot_general(qf[:, ks], kf[:, ks], nt, preferred_element_type=F32) for ks, _ in heads]
               for _, _, _, qf, kf, _ in scaled]
        o_intra = [[None, None] for _ in range(G)]
        for (c, mask, kv_scr, _, _, kl), sc2 in zip(scaled, raw):
            rows = slice(c * C, (c + 1) * C)
            for h, (ks, vs) in enumerate(heads):
                vh = v2[rows, vs]
                sc = jnp.where(mask, sc2[h], 0.0).astype(BF16)
                oi = jnp.dot(sc, vh, preferred_element_type=F32)
                o_intra[c][h] = oi if o_intra[c][h] is None else o_intra[c][h] + oi
                kv_scr[sb * G + c, h] = lax.dot_general(vh, kl[:, ks], tn, preferred_element_type=F32)
        for c in range(G):
            part_scr[pl.ds(r0 + c * C, C), :] = jnp.concatenate(o_intra[c], axis=1)
        return carry

    lax.fori_loop(0, n_chunks // G, intra, 0)

    def scan(n, carry):
        sf, sb = carry
        nb = n_chunks - 1 - n
        new_f, new_b = [], []
        for h in range(2):
            kv = kvf_scr[n, h]
            kvf_scr[n, h] = sf[h]
            new_f.append(decf_scr[n, h] * sf[h] + kv)
            kv = kvb_scr[nb, h]
            kvb_scr[nb, h] = sb[h]
            new_b.append(decb_scr[nb, h] * sb[h] + kv)
        return tuple(new_f), tuple(new_b)

    zero_state = tuple(jnp.zeros((GLA_DV, GLA_DK), F32) for _ in range(2))
    lax.fori_loop(0, n_chunks, scan, (zero_state, zero_state), unroll=4)

    def finish(sb, carry):
        r0 = pl.multiple_of(sb * R, R)
        g = g_ref[pl.ds(r0, R), :].astype(F32)
        part = part_scr[pl.ds(r0, R), :]
        qf = qif_scr[pl.ds(r0, R), :]
        qb = qib_scr[pl.ds(r0, R), :]
        inter = []
        for c in range(G):
            rows = slice(c * C, (c + 1) * C)
            inter.append([
                lax.dot_general(qf[rows, ks], kvf_scr[sb * G + c, h].astype(BF16), nt, preferred_element_type=F32)
                + lax.dot_general(qb[rows, ks], kvb_scr[sb * G + c, h].astype(BF16), nt, preferred_element_type=F32)
                for h, (ks, _) in enumerate(heads)])
        gate = g * _sigmoid(g)
        for c in range(G):
            rows = slice(c * C, (c + 1) * C)
            normed = jnp.concatenate([_rms(part[rows, vs] + inter[c][h], gain) for h, (_, vs) in enumerate(heads)],
                                     axis=1)
            o_ref[pl.ds(r0 + c * C, C), :] = (normed * gate[rows]).astype(BF16)
        return carry

    lax.fori_loop(0, n_chunks // G, finish, 0)


def _gla(gq, gk, gv, gg, laf, lab, gain, B, T):
    N = gq.shape[0]
    pairs = GLA_HEADS // 2
    n_chunks = T // GLA_CHUNK
    assert n_chunks % GLA_GROUP == 0
    group_rows = GLA_CHUNK * GLA_GROUP
    kspec = pl.BlockSpec((T, 2 * GLA_DK), lambda b, p: (b, p))
    vspec = pl.BlockSpec((T, 2 * GLA_DV), lambda b, p: (b, p))
    state = pltpu.VMEM((n_chunks, 2, GLA_DV, GLA_DK), F32)
    decay = pltpu.VMEM((n_chunks, 2, 1, GLA_DK), F32)
    return pl.pallas_call(
        _gla_kernel,
        out_shape=jax.ShapeDtypeStruct((N, GLA_WIDTH), BF16),
        grid=(B, pairs),
        in_specs=[kspec, kspec, vspec, vspec, kspec, kspec,
                  pl.BlockSpec((1, GLA_DV), lambda b, p: (0, 0))],
        out_specs=vspec,
        scratch_shapes=[pltpu.VMEM((T, 2 * GLA_DV), F32),
                        pltpu.VMEM((T, 2 * GLA_DK), BF16), pltpu.VMEM((T, 2 * GLA_DK), BF16),
                        state, state, decay, decay,
                        pltpu.VMEM((group_rows, group_rows), BF16), pltpu.VMEM((group_rows, group_rows), BF16)],
        compiler_params=_cparams(("arbitrary", "arbitrary")),
        name="gla",
    )(gq, gk, gv, gg, laf, lab, gain)


def _attn_kernel(qt_ref, k_ref, vt_ref, ot_ref, vext_scr, s0_scr, s1_scr, p0_scr, p1_scr, m0_scr, m1_scr,
                 *, tq, key_block, steps_per_iter):
    j = pl.program_id(1)
    T = k_ref.shape[0]
    nq = T // tq
    dh = ATT_HEAD_DIM
    cols = ATT_GROUP * tq
    first = j == 0

    row = lax.broadcasted_iota(jnp.int32, (ATT_KV_WIDTH, T), 0)
    own = (row >= j * dh) & (row < (j + 1) * dh)
    vext_scr[...] = jnp.where(own, vt_ref[0], jnp.ones((), BF16))

    s_scr, p_scr, m_scr = (s0_scr, s1_scr), (p0_scr, p1_scr), (m0_scr, m1_scr)

    def step(blk, sa, do_scores=True, do_probs=True, do_out=True):
        sb = 1 - sa
        if do_scores:
            c0 = pl.multiple_of(blk * tq, tq)
            q4 = jnp.concatenate([qt_ref[0, h * dh:(h + 1) * dh, pl.ds(c0, tq)] for h in range(ATT_GROUP)],
                                 axis=1)
            zeros = jnp.zeros_like(q4)
            qe = jnp.where(first, jnp.concatenate([q4, zeros], axis=0), jnp.concatenate([zeros, q4], axis=0))
            m = jnp.full((8, cols), -jnp.inf, F32)
        if do_probs:
            mx = m_scr[sb][...]
        if do_out:
            acc = jnp.zeros((ATT_KV_WIDTH, cols), F32)
        for kb in range(T // key_block):
            ks = slice(kb * key_block, (kb + 1) * key_block)
            if do_scores:
                s = jnp.dot(k_ref[ks, :], qe, preferred_element_type=F32)
                s_scr[sa][ks, :] = s
                m = jnp.maximum(m, jnp.max(s.reshape(key_block // 8, 8, cols), axis=0))
            if do_probs:
                p_scr[sb][ks, :] = jnp.exp2(s_scr[sb][ks, :] - mx).astype(BF16)
            if do_out:
                acc = acc + jnp.dot(vext_scr[:, ks], p_scr[sa][ks, :], preferred_element_type=F32)
        if do_scores:
            m_scr[sa][...] = jnp.max(m, axis=0, keepdims=True)
        if do_out:
            num = jnp.where(first, acc[:dh], acc[dh:])
            den = jnp.where(first, acc[dh:dh + 1], acc[0:1])
            o = (num / den).astype(BF16)
            c0 = pl.multiple_of((blk - 2) * tq, tq)
            for h in range(ATT_GROUP):
                ot_ref[0, h * dh:(h + 1) * dh, pl.ds(c0, tq)] = o[:, h * tq:(h + 1) * tq]

    step(0, 0, do_probs=False, do_out=False)
    step(1, 1, do_out=False)

    def steady(it, carry):
        for u in range(steps_per_iter):
            pl.when(it >= 0)(functools.partial(step, 2 + steps_per_iter * it + u, u % 2))
        return carry

    lax.fori_loop(0, (nq - 2) // steps_per_iter, steady, 0)
    step(nq, 0, do_scores=False)
    step(nq + 1, 1, do_scores=False, do_probs=False)


def _attn(aqt, ak, avt, B, T, tq, key_block):
    assert T % (2 * tq) == 0 and tq % LANES == 0
    gw = ATT_GROUP * ATT_HEAD_DIM
    cols = ATT_GROUP * tq
    steps_per_iter = 2
    return pl.pallas_call(
        functools.partial(_attn_kernel, tq=tq, key_block=key_block, steps_per_iter=steps_per_iter),
        out_shape=jax.ShapeDtypeStruct((B, ATT_WIDTH, T), BF16),
        grid=(B, ATT_KV_HEADS),
        in_specs=[pl.BlockSpec((1, gw, T), lambda b, j: (b, j, 0)),
                  pl.BlockSpec((T, ATT_KV_WIDTH), lambda b, j: (b, 0)),
                  pl.BlockSpec((1, ATT_KV_WIDTH, T), lambda b, j: (b, 0, 0))],
        out_specs=pl.BlockSpec((1, gw, T), lambda b, j: (b, j, 0)),
        scratch_shapes=[pltpu.VMEM((ATT_KV_WIDTH, T), BF16),
                        pltpu.VMEM((T, cols), F32), pltpu.VMEM((T, cols), F32),
                        pltpu.VMEM((T, cols), BF16), pltpu.VMEM((T, cols), BF16),
                        pltpu.VMEM((1, cols), F32), pltpu.VMEM((1, cols), F32)],
        compiler_params=_cparams(("arbitrary", "arbitrary")),
        name="attn",
    )(aqt, ak, avt)


def _out_proj_kernel(x_ref, og_ref, oa_ref, wo_ref, ag_ref, g2_ref, wr_ref, br_ref, tri_ref,
                     h_ref, xn_ref, eid_ref, gate_ref, rank_ref, cnt_ref, run_scr):
    tm = x_ref.shape[0]

    @pl.when(pl.program_id(0) == 0)
    def _():
        run_scr[...] = jnp.zeros_like(run_scr)

    oat = oa_ref[0].astype(F32)
    oan = (oat * lax.rsqrt(jnp.mean(oat * oat, axis=0, keepdims=True) + EPS) * ag_ref[...]).astype(BF16)
    y = (jnp.dot(og_ref[...], wo_ref[:GLA_WIDTH, :], preferred_element_type=F32)
         + lax.dot_general(oan, wo_ref[GLA_WIDTH:, :], (((0,), (0,)), ((), ())),
                           preferred_element_type=F32))
    h = x_ref[...] + y
    h_ref[...] = h
    xn = _rms(h, g2_ref[...])
    xn_ref[...] = xn

    lt = lax.dot_general(wr_ref[...], xn.astype(BF16), (((1,), (1,)), ((), ())),
                         preferred_element_type=F32) + br_ref[...]
    iota8 = lax.broadcasted_iota(jnp.int32, (N_GROUPS, tm), 0)

    def first_argmax(v):
        top = jnp.max(v, axis=0, keepdims=True)
        idx = jnp.min(jnp.where(v == top, iota8, N_GROUPS), axis=0, keepdims=True)
        return top, idx

    gl = lt[0:N_GROUPS]
    gmax, gidx = first_argmax(gl)
    gw = 1.0 / jnp.sum(jnp.exp(gl - gmax), axis=0, keepdims=True)
    esel = jnp.zeros((EXPERTS_PER_GROUP, tm), F32)
    for g in range(N_GROUPS):
        lo = N_GROUPS + g * EXPERTS_PER_GROUP
        esel = jnp.where(gidx == g, lt[lo:lo + EXPERTS_PER_GROUP], esel)
    v1, i1 = first_argmax(esel)
    rest = jnp.where(iota8 == i1, -jnp.inf, esel)
    v2, i2 = first_argmax(rest)
    t = jnp.exp(v2 - v1)
    den = 1.0 + t
    e1 = gidx * EXPERTS_PER_GROUP + i1
    e2 = gidx * EXPERTS_PER_GROUP + i2
    eid_ref[0:1, :] = e1
    eid_ref[1:2, :] = e2
    gate_ref[0:1, :] = gw * (1.0 / den)
    gate_ref[1:2, :] = gw * (t / den)

    iota_e = lax.broadcasted_iota(jnp.int32, (N_EXPERTS, tm), 0)
    oh1 = iota_e == e1
    oh2 = iota_e == e2
    both = jnp.where(oh1, 1.0, jnp.where(oh2, 1.0, 0.0)).astype(BF16)
    prefix = jnp.dot(both, tri_ref[...], preferred_element_type=F32)
    run = run_scr[...]
    base = prefix - 1.0 + run[:, 0:1]
    rank_ref[0:1, :] = jnp.sum(jnp.where(oh1, base, 0.0), axis=0, keepdims=True).astype(jnp.int32)
    rank_ref[1:2, :] = jnp.sum(jnp.where(oh2, base, 0.0), axis=0, keepdims=True).astype(jnp.int32)
    run = run + prefix[:, tm - 1:tm]
    run_scr[...] = run
    cnt_ref[...] = run


def _out_proj(xt, o_gla, o_att_t, wo, ag, g2, wr, br, tm):
    N = xt.shape[0]
    tpb = o_att_t.shape[2] // tm
    const = lambda shape: pl.BlockSpec(shape, lambda i: (0,) * len(shape))
    rows = lambda width: pl.BlockSpec((tm, width), lambda i: (i, 0))
    cols = pl.BlockSpec((TOP_K, tm), lambda i: (0, i))
    tri = jnp.asarray(np.triu(np.ones((tm, tm), np.float32)), BF16)
    return pl.pallas_call(
        _out_proj_kernel,
        out_shape=(jax.ShapeDtypeStruct((N, D_MODEL), F32),
                   jax.ShapeDtypeStruct((N, D_MODEL), F32),
                   jax.ShapeDtypeStruct((TOP_K, N), jnp.int32),
                   jax.ShapeDtypeStruct((TOP_K, N), F32),
                   jax.ShapeDtypeStruct((TOP_K, N), jnp.int32),
                   jax.ShapeDtypeStruct((N_EXPERTS, LANES), F32)),
        grid=(N // tm,),
        in_specs=[rows(D_MODEL), rows(GLA_WIDTH),
                  pl.BlockSpec((1, ATT_WIDTH, tm), lambda i: (i // tpb, 0, i % tpb)),
                  const((D_MODEL, D_MODEL)),
                  const((ATT_WIDTH, 1)), const((1, D_MODEL)), const((LANES, D_MODEL)),
                  const((LANES, 1)), const((tm, tm))],
        out_specs=(rows(D_MODEL), rows(D_MODEL), cols, cols, cols, const((N_EXPERTS, LANES))),
        scratch_shapes=[pltpu.VMEM((N_EXPERTS, LANES), F32)],
        compiler_params=_cparams(("arbitrary",)),
        name="out_proj",
    )(xt, o_gla, o_att_t, wo, ag, g2, wr, br, tri)


def _dispatch_kernel(zs_ref, nu_ref, dest_ref, xn_ref, xs_hbm, zero_buf, zero_sem, row_sem):
    tn = dest_ref.shape[1]
    i = pl.program_id(0)

    def fill(row0):
        return pltpu.make_async_copy(
            zero_buf, xs_hbm.at[pl.ds(pl.multiple_of(row0, MOE_BLOCK), MOE_BLOCK)], zero_sem)

    @pl.when(i == 0)
    def _():
        zero_buf[...] = jnp.zeros_like(zero_buf)
        fills = [fill(zs_ref[e]) for e in range(N_EXPERTS)]
        for f in fills:
            f.start()
        for f in fills:
            f.wait()

        def fill_tail(blk, carry):
            f = fill(blk * MOE_BLOCK)
            f.start()
            f.wait()
            return carry

        lax.fori_loop(nu_ref[0], xs_hbm.shape[0] // MOE_BLOCK, fill_tail, 0)

    def issue(t, carry):
        for k in range(TOP_K):
            pltpu.make_async_copy(xn_ref.at[pl.ds(t, 1)], xs_hbm.at[pl.ds(dest_ref[k, t], 1)],
                                  row_sem).start(priority=k % 2)
        return carry

    lax.fori_loop(0, tn, issue, 0, unroll=8)
    for k in range(TOP_K):
        pltpu.make_async_copy(xn_ref, xs_hbm.at[pl.ds(0, tn)], row_sem).wait()


def _dispatch(zero_starts, n_used, dest, xn, P, tn):
    N = xn.shape[0]
    grid_spec = pltpu.PrefetchScalarGridSpec(
        num_scalar_prefetch=2,
        grid=(N // tn,),
        in_specs=[pl.BlockSpec((TOP_K, tn), lambda i, zs, nu: (0, i), memory_space=pltpu.SMEM),
                  pl.BlockSpec((tn, D_MODEL), lambda i, zs, nu: (i, 0))],
        out_specs=pl.BlockSpec(memory_space=pl.ANY),
        scratch_shapes=[pltpu.VMEM((MOE_BLOCK, D_MODEL), F32),
                        pltpu.SemaphoreType.DMA, pltpu.SemaphoreType.DMA],
    )
    return pl.pallas_call(
        _dispatch_kernel,
        out_shape=jax.ShapeDtypeStruct((P, D_MODEL), F32),
        grid_spec=grid_spec,
        compiler_params=_cparams(("arbitrary",)),
        name="dispatch",
    )(zero_starts, n_used, dest, xn)


def _moe_ffn_kernel(be_ref, nu_ref, xs_ref, wg_ref, wu_ref, wd_ref, ys_ref, wg_s, wu_s, wd_s):
    i = pl.program_id(0)
    used = i < nu_ref[0]

    @pl.when(used)
    def _():
        new_expert = jnp.logical_or(i == 0, be_ref[i] != be_ref[jnp.maximum(i - 1, 0)])

        @pl.when(new_expert)
        def _():
            wg_s[...] = wg_ref[0].astype(BF16)
            wu_s[...] = wu_ref[0].astype(BF16)
            wd_s[...] = wd_ref[0].astype(BF16)

        xb = xs_ref[...].astype(BF16)
        hg = jnp.dot(xb, wg_s[...], preferred_element_type=F32)
        hu = jnp.dot(xb, wu_s[...], preferred_element_type=F32)
        hm = (hg * _sigmoid(hg) * hu).astype(BF16)
        ys_ref[...] = jnp.dot(hm, wd_s[...], preferred_element_type=F32)

    @pl.when(jnp.logical_not(used))
    def _():
        ys_ref[...] = jnp.zeros_like(ys_ref)


def _moe_ffn(blk_expert, n_used, xs, w_gate, w_up, w_down):
    P = xs.shape[0]
    nblk = P // MOE_BLOCK
    grid_spec = pltpu.PrefetchScalarGridSpec(
        num_scalar_prefetch=2,
        grid=(nblk,),
        in_specs=[pl.BlockSpec((MOE_BLOCK, D_MODEL), lambda i, be, nu: (i, 0)),
                  pl.BlockSpec((1, D_MODEL, D_EXPERT), lambda i, be, nu: (be[i], 0, 0)),
                  pl.BlockSpec((1, D_MODEL, D_EXPERT), lambda i, be, nu: (be[i], 0, 0)),
                  pl.BlockSpec((1, D_EXPERT, D_MODEL), lambda i, be, nu: (be[i], 0, 0))],
        out_specs=pl.BlockSpec((MOE_BLOCK, D_MODEL), lambda i, be, nu: (i, 0)),
        scratch_shapes=[pltpu.VMEM((D_MODEL, D_EXPERT), BF16),
                        pltpu.VMEM((D_MODEL, D_EXPERT), BF16),
                        pltpu.VMEM((D_EXPERT, D_MODEL), BF16)],
    )
    return pl.pallas_call(
        _moe_ffn_kernel,
        out_shape=jax.ShapeDtypeStruct((P, D_MODEL), F32),
        grid_spec=grid_spec,
        compiler_params=_cparams(("arbitrary",)),
        name="moe_ffn",
    )(blk_expert, n_used, xs, w_gate, w_up, w_down)


def _combine_kernel(dcur_ref, dnext_ref, h_ref, gate_ref, fg_ref, ys_hbm, o_ref, ybuf, sems, *, final):
    tm = h_ref.shape[0]
    i = pl.program_id(0)
    n = pl.num_programs(0)

    def gather(dref, slot):
        def issue(t, carry):
            for k in range(TOP_K):
                pltpu.make_async_copy(ys_hbm.at[pl.ds(dref[k, t], 1)],
                                      ybuf.at[slot, pl.ds(k * tm + t, 1)], sems.at[slot]).start(priority=k % 2)
            return carry

        lax.fori_loop(0, tm, issue, 0, unroll=8)

    @pl.when(i == 0)
    def _():
        gather(dcur_ref, 0)

    @pl.when(i + 1 < n)
    def _():
        gather(dnext_ref, (i + 1) % 2)

    slot = i % 2
    pltpu.make_async_copy(ys_hbm.at[pl.ds(0, TOP_K * tm)], ybuf.at[slot], sems.at[slot]).wait()
    g = gate_ref[...]
    hh = h_ref[...]
    for k in range(TOP_K):
        hh = hh + ybuf[slot, pl.ds(k * tm, tm), :] * g[:, k:k + 1]
    o_ref[...] = _rms(hh, fg_ref[...]) if final else hh


def _combine(h, ys, dest, gate_t, fg, tm, final):
    N = h.shape[0]
    n = N // tm
    rows = pl.BlockSpec((tm, D_MODEL), lambda i: (i, 0))
    return pl.pallas_call(
        functools.partial(_combine_kernel, final=final),
        out_shape=jax.ShapeDtypeStruct((N, D_MODEL), F32),
        grid=(n,),
        in_specs=[pl.BlockSpec((TOP_K, tm), lambda i: (0, i), memory_space=pltpu.SMEM),
                  pl.BlockSpec((TOP_K, tm), lambda i: (0, jnp.minimum(i + 1, n - 1)), memory_space=pltpu.SMEM),
                  rows, pl.BlockSpec((tm, TOP_K), lambda i: (i, 0)),
                  pl.BlockSpec((1, D_MODEL), lambda i: (0, 0)),
                  pl.BlockSpec(memory_space=pl.ANY)],
        out_specs=rows,
        scratch_shapes=[pltpu.VMEM((2, TOP_K * tm, D_MODEL), F32), pltpu.SemaphoreType.DMA((2,))],
        compiler_params=_cparams(("arbitrary",)),
        name="combine",
    )(dest, dest, h, gate_t, fg, ys)


def _reorder_w_in(w_in):
    n_gla = 2 * GLA_KEY_WIDTH + 2 * GLA_WIDTH
    z0 = n_gla
    a0 = z0 + 2 * GLA_GATE_RANK
    pad = jnp.zeros((D_MODEL, Z_PAD - 2 * GLA_GATE_RANK), w_in.dtype)
    return jnp.concatenate([w_in[:, :n_gla], w_in[:, a0:], w_in[:, z0:a0], pad], axis=1)


def kernel(x, norm1_gain, w_in, gla_up_fwd, gla_up_fwd_bias, gla_up_bwd, gla_up_bwd_bias, gla_out_gain, q_norm_gain, k_norm_gain, att_out_gain, w_out, norm2_gain, w_group, b_group, w_expert, b_expert, w_gate, w_up, w_down, final_gain):
    B, T, D = x.shape
    N = B * T
    depth = norm1_gain.shape[0]
    tm = min(512, T)
    tq = 128
    key_block = min(512, T)
    h = x.reshape(N, D)
    cos, se, so = _rope_tables(T)
    head_mean = jnp.asarray(
        np.kron(np.eye(ATT_HEADS, dtype=np.float32),
                np.full((ATT_HEAD_DIM, ATT_HEAD_DIM), 1.0 / ATT_HEAD_DIM, np.float32)), BF16)
    for l in range(depth):
        w = _reorder_w_in(w_in[l]).astype(BF16)
        r = GLA_GATE_RANK
        up = jnp.zeros((Z_PAD, 2 * GLA_KEY_WIDTH), F32)
        up = up.at[:r, :GLA_KEY_WIDTH].set(gla_up_fwd[l]).at[r:2 * r, GLA_KEY_WIDTH:].set(gla_up_bwd[l])
        upb = jnp.concatenate([gla_up_fwd_bias[l], gla_up_bwd_bias[l]])[None, :]
        qg = jnp.tile(q_norm_gain[l], ATT_HEADS)[None, :]
        kg = jnp.tile(k_norm_gain[l], ATT_KV_HEADS)[None, :]
        gq, gk, gv, gg, laf, lab, aqt, ak, avt = _in_proj(
            h, B, T, tm, norm1_gain[l][None, :], w, up.astype(BF16), upb, qg, kg, head_mean, cos, se, so)
        o_gla = _gla(gq, gk, gv, gg, laf, lab, gla_out_gain[l][None, :], B, T)
        o_att_t = _attn(aqt, ak, avt, B, T, tq, key_block)

        wr = jnp.zeros((LANES, D), F32)
        wr = wr.at[:N_GROUPS].set(w_group[l].T).at[N_GROUPS:N_GROUPS + N_EXPERTS].set(w_expert[l].T)
        br = jnp.zeros((LANES, 1), F32)
        br = br.at[:N_GROUPS, 0].set(b_group[l]).at[N_GROUPS:N_GROUPS + N_EXPERTS, 0].set(b_expert[l])
        h, xn, eid, gate, rank, cnt = _out_proj(
            h, o_gla, o_att_t, w_out[l].astype(BF16), att_out_gain[l][:, None], norm2_gain[l][None, :],
            wr.astype(BF16), br, tm)

        counts = cnt[:, 0].astype(jnp.int32)
        padded = (counts + MOE_BLOCK - 1) // MOE_BLOCK * MOE_BLOCK
        pad_ends = jnp.cumsum(padded)
        pad_starts = pad_ends - padded
        seg_start = jnp.sum(jnp.where(eid[:, :, None] == jnp.arange(N_EXPERTS, dtype=jnp.int32),
                                      pad_starts.astype(jnp.int32), 0), axis=-1)
        dest = seg_start + rank
        P = N * TOP_K + N_EXPERTS * MOE_BLOCK
        nblk = P // MOE_BLOCK
        blk_start = jnp.arange(nblk, dtype=jnp.int32) * MOE_BLOCK
        blk_expert = jnp.minimum(
            jnp.sum((pad_ends[None, :] <= blk_start[:, None]).astype(jnp.int32), axis=1), N_EXPERTS - 1)
        n_used = (pad_ends[-1:] // MOE_BLOCK).astype(jnp.int32)

        zero_starts = jnp.maximum(pad_ends - MOE_BLOCK, 0).astype(jnp.int32)
        xs = _dispatch(zero_starts, n_used, dest, xn, P, tm)
        ys = _moe_ffn(blk_expert, n_used, xs, w_gate[l], w_up[l], w_down[l])
        h = _combine(h, ys, dest, gate.T, final_gain[None, :], tm, l == depth - 1)
    return h.reshape(B, T, D)
```

```python
import functools

import numpy as np
import jax
import jax.numpy as jnp
from jax import lax
from jax.experimental import pallas as pl
from jax.experimental.pallas import tpu as pltpu

F32 = jnp.float32
BF16 = jnp.bfloat16

D_MODEL = 1024
EPS = 1e-6
GRID_W = 64

GLA_HEADS = 4
GLA_DK = 64
GLA_DV = 128
GLA_KEY_WIDTH = GLA_HEADS * GLA_DK
GLA_WIDTH = GLA_HEADS * GLA_DV
GLA_GATE_RANK = 16
GLA_GATE_NORM = 16.0
GLA_CHUNK = 64

ATT_HEADS = 8
ATT_KV_HEADS = 2
ATT_HEAD_DIM = 64
ATT_GROUP = ATT_HEADS // ATT_KV_HEADS
ATT_WIDTH = ATT_HEADS * ATT_HEAD_DIM
ATT_KV_WIDTH = ATT_KV_HEADS * ATT_HEAD_DIM
ROPE_THETA = 10000.0
LOG2_E = 1.4426950408889634

N_GROUPS = 8
EXPERTS_PER_GROUP = 8
N_EXPERTS = N_GROUPS * EXPERTS_PER_GROUP
TOP_K = 2
D_EXPERT = 512
MOE_BLOCK = 256

LANES = 128
Z_PAD = LANES

_OFF_GQ = 0
_OFF_GK = _OFF_GQ + GLA_KEY_WIDTH
_OFF_GV = _OFF_GK + GLA_KEY_WIDTH
_OFF_GG = _OFF_GV + GLA_WIDTH
_OFF_AQ = _OFF_GG + GLA_WIDTH
_OFF_AK = _OFF_AQ + ATT_WIDTH
_OFF_AV = _OFF_AK + ATT_KV_WIDTH
_OFF_Z = _OFF_AV + ATT_KV_WIDTH
D_IN_PAD = _OFF_Z + Z_PAD

VMEM_LIMIT = 56 * 1024 * 1024


def _cparams(semantics):
    return pltpu.CompilerParams(dimension_semantics=semantics, vmem_limit_bytes=VMEM_LIMIT)


def _rms(x, gain):
    return x * lax.rsqrt(jnp.mean(x * x, axis=-1, keepdims=True) + EPS) * gain


def _sigmoid(x):
    return 1.0 / (1.0 + jnp.exp(-x))


def _head_norm_rope(x, head_mean, gain, cos, sin_even, sin_odd):
    w = x.shape[1]
    ms = jnp.dot((x * x).astype(BF16), head_mean, preferred_element_type=F32)
    xn = x * lax.rsqrt(ms + EPS) * gain
    reps = w // LANES
    tile = lambda t: t if reps == 1 else jnp.concatenate([t] * reps, axis=1)
    nxt = pltpu.roll(xn, w - 1, 1)
    prv = pltpu.roll(xn, 1, 1)
    return xn * tile(cos) + nxt * tile(sin_even) + prv * tile(sin_odd)


def _in_proj_kernel(x_ref, g1_ref, w_ref, up_ref, upb_ref, qg_ref, kg_ref, hm_ref,
                    cos_ref, se_ref, so_ref,
                    gq_ref, gk_ref, gv_ref, gg_ref, laf_ref, lab_ref, aqt_ref, ak_ref, avt_ref):
    u = _rms(x_ref[...], g1_ref[...])
    res = jnp.dot(u.astype(BF16), w_ref[...], preferred_element_type=F32)
    gq_ref[...] = res[:, _OFF_GQ:_OFF_GK].astype(BF16)
    gk_ref[...] = res[:, _OFF_GK:_OFF_GV].astype(BF16)
    gv_ref[...] = res[:, _OFF_GV:_OFF_GG].astype(BF16)
    gg_ref[...] = res[:, _OFF_GG:_OFF_AQ].astype(BF16)
    avt_ref[0] = res[:, _OFF_AV:_OFF_Z].T.astype(BF16)

    z = res[:, _OFF_Z:D_IN_PAD].astype(BF16)
    zl = jnp.dot(z, up_ref[...], preferred_element_type=F32) + upb_ref[...]
    la = (jnp.minimum(zl, 0.0) - jnp.log(1.0 + jnp.exp(-jnp.abs(zl)))) * (1.0 / GLA_GATE_NORM)
    laf_ref[...] = la[:, :GLA_KEY_WIDTH]
    lab_ref[...] = la[:, GLA_KEY_WIDTH:]

    cos, se, so = cos_ref[...], se_ref[...], so_ref[...]
    hm = hm_ref[...]
    q = _head_norm_rope(res[:, _OFF_AQ:_OFF_AK], hm, qg_ref[...], cos, se, so)
    aqt_ref[0] = (q * (ATT_HEAD_DIM ** -0.5 * LOG2_E)).T.astype(BF16)
    k = _head_norm_rope(res[:, _OFF_AK:_OFF_AV], hm[:ATT_KV_WIDTH, :ATT_KV_WIDTH], kg_ref[...], cos, se, so)
    ak_ref[...] = k.astype(BF16)


def _rope_tables(T):
    t = np.arange(T)
    row = (t // GRID_W).astype(np.float32)
    col = (t % GRID_W).astype(np.float32)
    axis_dim = ATT_HEAD_DIM // 2
    inv_freq = (ROPE_THETA ** (-np.arange(0, axis_dim, 2, dtype=np.float32) / axis_dim)).astype(np.float32)
    ang = np.concatenate([row[:, None] * inv_freq, col[:, None] * inv_freq], axis=-1)
    ang = np.repeat(ang, 2, axis=1)
    ang = np.tile(ang, (1, LANES // ATT_HEAD_DIM))
    even = (np.arange(LANES) % 2 == 0)[None, :]
    cos = np.cos(ang)
    sin = np.sin(ang)
    return (jnp.asarray(cos, F32), jnp.asarray(np.where(even, -sin, 0.0), F32),
            jnp.asarray(np.where(even, 0.0, sin), F32))


def _in_proj(xt, B, T, tm, g1, w, up, upb, qg, kg, hm, cos, se, so):
    N = xt.shape[0]
    tpb = T // tm
    const = lambda shape: pl.BlockSpec(shape, lambda i: (0,) * len(shape))
    rows = lambda width: pl.BlockSpec((tm, width), lambda i: (i, 0))
    pos = pl.BlockSpec((tm, LANES), lambda i: (i % tpb, 0))
    out_shape = (
        jax.ShapeDtypeStruct((N, GLA_KEY_WIDTH), BF16),
        jax.ShapeDtypeStruct((N, GLA_KEY_WIDTH), BF16),
        jax.ShapeDtypeStruct((N, GLA_WIDTH), BF16),
        jax.ShapeDtypeStruct((N, GLA_WIDTH), BF16),
        jax.ShapeDtypeStruct((N, GLA_KEY_WIDTH), F32),
        jax.ShapeDtypeStruct((N, GLA_KEY_WIDTH), F32),
        jax.ShapeDtypeStruct((B, ATT_WIDTH, T), BF16),
        jax.ShapeDtypeStruct((N, ATT_KV_WIDTH), BF16),
        jax.ShapeDtypeStruct((B, ATT_KV_WIDTH, T), BF16),
    )
    cols = lambda width: pl.BlockSpec((1, width, tm), lambda i: (i // tpb, 0, i % tpb))
    out_specs = (
        rows(GLA_KEY_WIDTH), rows(GLA_KEY_WIDTH), rows(GLA_WIDTH), rows(GLA_WIDTH),
        rows(GLA_KEY_WIDTH), rows(GLA_KEY_WIDTH),
        cols(ATT_WIDTH), rows(ATT_KV_WIDTH), cols(ATT_KV_WIDTH),
    )
    return pl.pallas_call(
        _in_proj_kernel,
        out_shape=out_shape,
        grid=(N // tm,),
        in_specs=[rows(D_MODEL), const((1, D_MODEL)), const((D_MODEL, D_IN_PAD)),
                  const((Z_PAD, 2 * GLA_KEY_WIDTH)), const((1, 2 * GLA_KEY_WIDTH)),
                  const((1, ATT_WIDTH)), const((1, ATT_KV_WIDTH)), const((ATT_WIDTH, ATT_WIDTH)),
                  pos, pos, pos],
        out_specs=out_specs,
        compiler_params=_cparams(("arbitrary",)),
        name="in_proj",
    )(xt, g1, w, up, upb, qg, kg, hm, cos, se, so)


GLA_GROUP = 4


def _gla_kernel(q_ref, k_ref, v_ref, g_ref, laf_ref, lab_ref, gain_ref, o_ref,
                part_scr, qif_scr, qib_scr, kvf_scr, kvb_scr, decf_scr, decb_scr, lf_scr, lb_scr):
    C, G = GLA_CHUNK, GLA_GROUP
    R = C * G
    T = q_ref.shape[0]
    n_chunks = T // C
    gain = gain_ref[...]
    row = lax.broadcasted_iota(jnp.int32, (C, C), 0)
    col = lax.broadcasted_iota(jnp.int32, (C, C), 1)
    tril = row >= col
    triu = row <= col
    rr = lax.broadcasted_iota(jnp.int32, (R, R), 0)
    cc = lax.broadcasted_iota(jnp.int32, (R, R), 1)
    same_chunk = (rr // C) == (cc // C)
    lf_scr[...] = jnp.where(same_chunk & (rr >= cc), 1.0, 0.0).astype(BF16)
    lb_scr[...] = jnp.where(same_chunk & (rr <= cc), 1.0, 0.0).astype(BF16)

    directions = ((laf_ref, lf_scr, tril, C // 2 - 1, C - 1, qif_scr, kvf_scr, decf_scr),
                  (lab_ref, lb_scr, triu, C // 2, 0, qib_scr, kvb_scr, decb_scr))
    heads = tuple((slice(h * GLA_DK, (h + 1) * GLA_DK), slice(h * GLA_DV, (h + 1) * GLA_DV)) for h in range(2))
    nt = (((1,), (1,)), ((), ()))
    tn = (((0,), (0,)), ((), ()))

    def intra(sb, carry):
        r0 = pl.multiple_of(sb * R, R)
        q2 = q_ref[pl.ds(r0, R), :].astype(F32) * (GLA_DK ** -0.5)
        k2 = k_ref[pl.ds(r0, R), :].astype(F32)
        v2 = v_ref[pl.ds(r0, R), :]
        cums = []
        for la_ref, tri_scr, *_ in directions:
            la = la_ref[pl.ds(r0, R), :]
            la_hi = la.astype(BF16)
            la_lo = (la - la_hi.astype(F32)).astype(BF16)
            tri = tri_scr[...]
            cums.append(jnp.dot(tri, la_hi, preferred_element_type=F32)
                        + jnp.dot(tri, la_lo, preferred_element_type=F32))
        scaled = []
        for b, (_, _, mask, i_ref, i_last, qi_scr, kv_scr, dec_scr) in zip(cums, directions):
            for c in range(G):
                rows = slice(c * C, (c + 1) * C)
                bc, qc, kc = b[rows], q2[rows], k2[rows]
                b_ref = bc[i_ref:i_ref + 1, :]
                b_last = bc[i_last:i_last + 1, :]
                qf = (qc * jnp.exp(bc - b_ref)).astype(BF16)
                kf = (kc * jnp.exp(b_ref - bc)).astype(BF16)
                qi_scr[pl.ds(r0 + c * C, C), :] = (qc * jnp.exp(bc)).astype(BF16)
                kl = (kc * jnp.exp(b_last - bc)).astype(BF16)
                decay = jnp.exp(b_last)
                for h, (ks, _) in enumerate(heads):
                    dec_scr[sb * G + c, h] = decay[:, ks]
                scaled.append((c, mask, kv_scr, qf, kf, kl))
        raw = [[lax.dot_general(qf[:, ks], kf[:, ks], nt, preferred_element_type=F32) for ks, _ in heads]
               for _, _, _, qf, kf, _ in scaled]
        o_intra = [[None, None] for _ in range(G)]
        for (c, mask, kv_scr, _, _, kl), sc2 in zip(scaled, raw):
            rows = slice(c * C, (c + 1) * C)
            for h, (ks, vs) in enumerate(heads):
                vh = v2[rows, vs]
                sc = jnp.where(mask, sc2[h], 0.0).astype(BF16)
                oi = jnp.dot(sc, vh, preferred_element_type=F32)
                o_intra[c][h] = oi if o_intra[c][h] is None else o_intra[c][h] + oi
                kv_scr[sb * G + c, h] = lax.dot_general(vh, kl[:, ks], tn, preferred_element_type=F32)
        for c in range(G):
            part_scr[pl.ds(r0 + c * C, C), :] = jnp.concatenate(o_intra[c], axis=1)
        return carry

    lax.fori_loop(0, n_chunks // G, intra, 0)

    def scan(n, carry):
        sf, sb = carry
        nb = n_chunks - 1 - n
        new_f, new_b = [], []
        for h in range(2):
            kv = kvf_scr[n, h]
            kvf_scr[n, h] = sf[h]
            new_f.append(decf_scr[n, h] * sf[h] + kv)
            kv = kvb_scr[nb, h]
            kvb_scr[nb, h] = sb[h]
            new_b.append(decb_scr[nb, h] * sb[h] + kv)
        return tuple(new_f), tuple(new_b)

    zero_state = tuple(jnp.zeros((GLA_DV, GLA_DK), F32) for _ in range(2))
    lax.fori_loop(0, n_chunks, scan, (zero_state, zero_state), unroll=4)

    def finish(sb, carry):
        r0 = pl.multiple_of(sb * R, R)
        g = g_ref[pl.ds(r0, R), :].astype(F32)
        part = part_scr[pl.ds(r0, R), :]
        qf = qif_scr[pl.ds(r0, R), :]
        qb = qib_scr[pl.ds(r0, R), :]
        inter = []
        for c in range(G):
            rows = slice(c * C, (c + 1) * C)
            inter.append([
                lax.dot_general(qf[rows, ks], kvf_scr[sb * G + c, h].astype(BF16), nt, preferred_element_type=F32)
                + lax.dot_general(qb[rows, ks], kvb_scr[sb * G + c, h].astype(BF16), nt, preferred_element_type=F32)
                for h, (ks, _) in enumerate(heads)])
        gate = g * _sigmoid(g)
        for c in range(G):
            rows = slice(c * C, (c + 1) * C)
            normed = jnp.concatenate([_rms(part[rows, vs] + inter[c][h], gain) for h, (_, vs) in enumerate(heads)],
                                     axis=1)
            o_ref[pl.ds(r0 + c * C, C), :] = (normed * gate[rows]).astype(BF16)
        return carry

    lax.fori_loop(0, n_chunks // G, finish, 0)


def _gla(gq, gk, gv, gg, laf, lab, gain, B, T):
    N = gq.shape[0]
    pairs = GLA_HEADS // 2
    n_chunks = T // GLA_CHUNK
    assert n_chunks % GLA_GROUP == 0
    group_rows = GLA_CHUNK * GLA_GROUP
    kspec = pl.BlockSpec((T, 2 * GLA_DK), lambda b, p: (b, p))
    vspec = pl.BlockSpec((T, 2 * GLA_DV), lambda b, p: (b, p))
    state = pltpu.VMEM((n_chunks, 2, GLA_DV, GLA_DK), F32)
    decay = pltpu.VMEM((n_chunks, 2, 1, GLA_DK), F32)
    return pl.pallas_call(
        _gla_kernel,
        out_shape=jax.ShapeDtypeStruct((N, GLA_WIDTH), BF16),
        grid=(B, pairs),
        in_specs=[kspec, kspec, vspec, vspec, kspec, kspec,
                  pl.BlockSpec((1, GLA_DV), lambda b, p: (0, 0))],
        out_specs=vspec,
        scratch_shapes=[pltpu.VMEM((T, 2 * GLA_DV), F32),
                        pltpu.VMEM((T, 2 * GLA_DK), BF16), pltpu.VMEM((T, 2 * GLA_DK), BF16),
                        state, state, decay, decay,
                        pltpu.VMEM((group_rows, group_rows), BF16), pltpu.VMEM((group_rows, group_rows), BF16)],
        compiler_params=_cparams(("arbitrary", "arbitrary")),
        name="gla",
    )(gq, gk, gv, gg, laf, lab, gain)


def _attn_kernel(qt_ref, k_ref, vt_ref, ot_ref, vext_scr, s0_scr, s1_scr, p0_scr, p1_scr, m0_scr, m1_scr,
                 *, tq, key_block, steps_per_iter):
    j = pl.program_id(1)
    T = k_ref.shape[0]
    nq = T // tq
    dh = ATT_HEAD_DIM
    cols = ATT_GROUP * tq
    first = j == 0

    row = lax.broadcasted_iota(jnp.int32, (ATT_KV_WIDTH, T), 0)
    own = (row >= j * dh) & (row < (j + 1) * dh)
    vext_scr[...] = jnp.where(own, vt_ref[0], jnp.ones((), BF16))

    s_scr, p_scr, m_scr = (s0_scr, s1_scr), (p0_scr, p1_scr), (m0_scr, m1_scr)

    def step(blk, sa, do_scores=True, do_probs=True, do_out=True):
        sb = 1 - sa
        if do_scores:
            c0 = pl.multiple_of(blk * tq, tq)
            q4 = jnp.concatenate([qt_ref[0, h * dh:(h + 1) * dh, pl.ds(c0, tq)] for h in range(ATT_GROUP)],
                                 axis=1)
            zeros = jnp.zeros_like(q4)
            qe = jnp.where(first, jnp.concatenate([q4, zeros], axis=0), jnp.concatenate([zeros, q4], axis=0))
            m = jnp.full((8, cols), -jnp.inf, F32)
        if do_probs:
            mx = m_scr[sb][...]
        if do_out:
            acc = jnp.zeros((ATT_KV_WIDTH, cols), F32)
        for kb in range(T // key_block):
            ks = slice(kb * key_block, (kb + 1) * key_block)
            if do_scores:
                s = jnp.dot(k_ref[ks, :], qe, preferred_element_type=F32)
                s_scr[sa][ks, :] = s
                m = jnp.maximum(m, jnp.max(s.reshape(key_block // 8, 8, cols), axis=0))
            if do_probs:
                p_scr[sb][ks, :] = jnp.exp2(s_scr[sb][ks, :] - mx).astype(BF16)
            if do_out:
                acc = acc + jnp.dot(vext_scr[:, ks], p_scr[sa][ks, :], preferred_element_type=F32)
        if do_scores:
            m_scr[sa][...] = jnp.max(m, axis=0, keepdims=True)
        if do_out:
            num = jnp.where(first, acc[:dh], acc[dh:])
            den = jnp.where(first, acc[dh:dh + 1], acc[0:1])
            o = (num / den).astype(BF16)
            c0 = pl.multiple_of((blk - 2) * tq, tq)
            for h in range(ATT_GROUP):
                ot_ref[0, h * dh:(h + 1) * dh, pl.ds(c0, tq)] = o[:, h * tq:(h + 1) * tq]

    step(0, 0, do_probs=False, do_out=False)
    step(1, 1, do_out=False)

    def steady(it, carry):
        for u in range(steps_per_iter):
            pl.when(it >= 0)(functools.partial(step, 2 + steps_per_iter * it + u, u % 2))
        return carry

    lax.fori_loop(0, (nq - 2) // steps_per_iter, steady, 0)
    step(nq, 0, do_scores=False)
    step(nq + 1, 1, do_scores=False, do_probs=False)


def _attn(aqt, ak, avt, B, T, tq, key_block):
    assert T % (2 * tq) == 0 and tq % LANES == 0
    gw = ATT_GROUP * ATT_HEAD_DIM
    cols = ATT_GROUP * tq
    steps_per_iter = 2
    return pl.pallas_call(
        functools.partial(_attn_kernel, tq=tq, key_block=key_block, steps_per_iter=steps_per_iter),
        out_shape=jax.ShapeDtypeStruct((B, ATT_WIDTH, T), BF16),
        grid=(B, ATT_KV_HEADS),
        in_specs=[pl.BlockSpec((1, gw, T), lambda b, j: (b, j, 0)),
                  pl.BlockSpec((T, ATT_KV_WIDTH), lambda b, j: (b, 0)),
                  pl.BlockSpec((1, ATT_KV_WIDTH, T), lambda b, j: (b, 0, 0))],
        out_specs=pl.BlockSpec((1, gw, T), lambda b, j: (b, j, 0)),
        scratch_shapes=[pltpu.VMEM((ATT_KV_WIDTH, T), BF16),
                        pltpu.VMEM((T, cols), F32), pltpu.VMEM((T, cols), F32),
                        pltpu.VMEM((T, cols), BF16), pltpu.VMEM((T, cols), BF16),
                        pltpu.VMEM((1, cols), F32), pltpu.VMEM((1, cols), F32)],
        compiler_params=_cparams(("arbitrary", "arbitrary")),
        name="attn",
    )(aqt, ak, avt)


def _out_proj_kernel(x_ref, og_ref, oa_ref, wo_ref, ag_ref, g2_ref, wr_ref, br_ref, tri_ref,
                     h_ref, xn_ref, eid_ref, gate_ref, rank_ref, cnt_ref, run_scr):
    tm = x_ref.shape[0]

    @pl.when(pl.program_id(0) == 0)
    def _():
        run_scr[...] = jnp.zeros_like(run_scr)

    oat = oa_ref[0].astype(F32)
    oan = (oat * lax.rsqrt(jnp.mean(oat * oat, axis=0, keepdims=True) + EPS) * ag_ref[...]).astype(BF16)
    y = (jnp.dot(og_ref[...], wo_ref[:GLA_WIDTH, :], preferred_element_type=F32)
         + lax.dot_general(oan, wo_ref[GLA_WIDTH:, :], (((0,), (0,)), ((), ())),
                           preferred_element_type=F32))
    h = x_ref[...] + y
    h_ref[...] = h
    xn = _rms(h, g2_ref[...])
    xn_ref[...] = xn

    lt = lax.dot_general(wr_ref[...], xn.astype(BF16), (((1,), (1,)), ((), ())),
                         preferred_element_type=F32) + br_ref[...]
    iota8 = lax.broadcasted_iota(jnp.int32, (N_GROUPS, tm), 0)

    def first_argmax(v):
        top = jnp.max(v, axis=0, keepdims=True)
        idx = jnp.min(jnp.where(v == top, iota8, N_GROUPS), axis=0, keepdims=True)
        return top, idx

    gl = lt[0:N_GROUPS]
    gmax, gidx = first_argmax(gl)
    gw = 1.0 / jnp.sum(jnp.exp(gl - gmax), axis=0, keepdims=True)
    esel = jnp.zeros((EXPERTS_PER_GROUP, tm), F32)
    for g in range(N_GROUPS):
        lo = N_GROUPS + g * EXPERTS_PER_GROUP
        esel = jnp.where(gidx == g, lt[lo:lo + EXPERTS_PER_GROUP], esel)
    v1, i1 = first_argmax(esel)
    rest = jnp.where(iota8 == i1, -jnp.inf, esel)
    v2, i2 = first_argmax(rest)
    t = jnp.exp(v2 - v1)
    den = 1.0 + t
    e1 = gidx * EXPERTS_PER_GROUP + i1
    e2 = gidx * EXPERTS_PER_GROUP + i2
    eid_ref[0:1, :] = e1
    eid_ref[1:2, :] = e2
    gate_ref[0:1, :] = gw * (1.0 / den)
    gate_ref[1:2, :] = gw * (t / den)

    iota_e = lax.broadcasted_iota(jnp.int32, (N_EXPERTS, tm), 0)
    oh1 = iota_e == e1
    oh2 = iota_e == e2
    both = jnp.where(oh1, 1.0, jnp.where(oh2, 1.0, 0.0)).astype(BF16)
    prefix = jnp.dot(both, tri_ref[...], preferred_element_type=F32)
    run = run_scr[...]
    base = prefix - 1.0 + run[:, 0:1]
    rank_ref[0:1, :] = jnp.sum(jnp.where(oh1, base, 0.0), axis=0, keepdims=True).astype(jnp.int32)
    rank_ref[1:2, :] = jnp.sum(jnp.where(oh2, base, 0.0), axis=0, keepdims=True).astype(jnp.int32)
    run = run + prefix[:, tm - 1:tm]
    run_scr[...] = run
    cnt_ref[...] = run


def _out_proj(xt, o_gla, o_att_t, wo, ag, g2, wr, br, tm):
    N = xt.shape[0]
    tpb = o_att_t.shape[2] // tm
    const = lambda shape: pl.BlockSpec(shape, lambda i: (0,) * len(shape))
    rows = lambda width: pl.BlockSpec((tm, width), lambda i: (i, 0))
    cols = pl.BlockSpec((TOP_K, tm), lambda i: (0, i))
    tri = jnp.asarray(np.triu(np.ones((tm, tm), np.float32)), BF16)
    return pl.pallas_call(
        _out_proj_kernel,
        out_shape=(jax.ShapeDtypeStruct((N, D_MODEL), F32),
                   jax.ShapeDtypeStruct((N, D_MODEL), F32),
                   jax.ShapeDtypeStruct((TOP_K, N), jnp.int32),
                   jax.ShapeDtypeStruct((TOP_K, N), F32),
                   jax.ShapeDtypeStruct((TOP_K, N), jnp.int32),
                   jax.ShapeDtypeStruct((N_EXPERTS, LANES), F32)),
        grid=(N // tm,),
        in_specs=[rows(D_MODEL), rows(GLA_WIDTH),
                  pl.BlockSpec((1, ATT_WIDTH, tm), lambda i: (i // tpb, 0, i % tpb)),
                  const((D_MODEL, D_MODEL)),
                  const((ATT_WIDTH, 1)), const((1, D_MODEL)), const((LANES, D_MODEL)),
                  const((LANES, 1)), const((tm, tm))],
        out_specs=(rows(D_MODEL), rows(D_MODEL), cols, cols, cols, const((N_EXPERTS, LANES))),
        scratch_shapes=[pltpu.VMEM((N_EXPERTS, LANES), F32)],
        compiler_params=_cparams(("arbitrary",)),
        name="out_proj",
    )(xt, o_gla, o_att_t, wo, ag, g2, wr, br, tri)


D_PACKED = D_MODEL // 2


def _pack_pairs(x):
    w = x.shape[1] // 2
    bits = pltpu.bitcast(x.astype(BF16).astype(F32), jnp.uint32)
    return bits[:, :w] | (bits[:, w:] >> 16)


def _unpack_pairs(words):
    hi = pltpu.bitcast(words & jnp.uint32(0xFFFF0000), F32)
    lo = pltpu.bitcast(words << 16, F32)
    return jnp.concatenate([hi, lo], axis=1)


def _dispatch_kernel(zs_ref, nu_ref, dest_ref, xn_ref, xs_hbm, packed, zero_buf, zero_sem, row_sem):
    tn = dest_ref.shape[1]
    i = pl.program_id(0)
    packed[...] = _pack_pairs(xn_ref[...])

    def fill(row0):
        return pltpu.make_async_copy(
            zero_buf, xs_hbm.at[pl.ds(pl.multiple_of(row0, MOE_BLOCK), MOE_BLOCK)], zero_sem)

    @pl.when(i == 0)
    def _():
        zero_buf[...] = jnp.zeros_like(zero_buf)
        fills = [fill(zs_ref[e]) for e in range(N_EXPERTS)]
        for f in fills:
            f.start()
        for f in fills:
            f.wait()

        def fill_tail(blk, carry):
            f = fill(blk * MOE_BLOCK)
            f.start()
            f.wait()
            return carry

        lax.fori_loop(nu_ref[0], xs_hbm.shape[0] // MOE_BLOCK, fill_tail, 0)

    def issue(g, carry):
        tile = packed.at[pl.ds(pl.multiple_of(g * 8, 8), 8)]
        for s in range(8):
            for k in range(TOP_K):
                pltpu.make_async_copy(tile.at[pl.ds(s, 1)], xs_hbm.at[pl.ds(dest_ref[k, g * 8 + s], 1)],
                                      row_sem).start(priority=k % 2)
        return carry

    lax.fori_loop(0, tn // 8, issue, 0)
    for k in range(TOP_K):
        pltpu.make_async_copy(packed, xs_hbm.at[pl.ds(0, tn)], row_sem).wait()


def _dispatch(zero_starts, n_used, dest, xn, P, tn):
    N = xn.shape[0]
    grid_spec = pltpu.PrefetchScalarGridSpec(
        num_scalar_prefetch=2,
        grid=(N // tn,),
        in_specs=[pl.BlockSpec((TOP_K, tn), lambda i, zs, nu: (0, i), memory_space=pltpu.SMEM),
                  pl.BlockSpec((tn, D_MODEL), lambda i, zs, nu: (i, 0))],
        out_specs=pl.BlockSpec(memory_space=pl.ANY),
        scratch_shapes=[pltpu.VMEM((tn, D_PACKED), jnp.uint32),
                        pltpu.VMEM((MOE_BLOCK, D_PACKED), jnp.uint32),
                        pltpu.SemaphoreType.DMA, pltpu.SemaphoreType.DMA],
    )
    return pl.pallas_call(
        _dispatch_kernel,
        out_shape=jax.ShapeDtypeStruct((P, D_PACKED), jnp.uint32),
        grid_spec=grid_spec,
        compiler_params=_cparams(("arbitrary",)),
        name="dispatch",
    )(zero_starts, n_used, dest, xn)


def _moe_ffn_kernel(be_ref, nu_ref, xs_ref, wg_ref, wu_ref, wd_ref, ys_ref, wg_s, wu_s, wd_s):
    i = pl.program_id(0)
    used = i < nu_ref[0]

    @pl.when(used)
    def _():
        new_expert = jnp.logical_or(i == 0, be_ref[i] != be_ref[jnp.maximum(i - 1, 0)])

        @pl.when(new_expert)
        def _():
            wg_s[...] = wg_ref[0].astype(BF16)
            wu_s[...] = wu_ref[0].astype(BF16)
            wd_s[...] = wd_ref[0].astype(BF16)

        xb = _unpack_pairs(xs_ref[...]).astype(BF16)
        hg = jnp.dot(xb, wg_s[...], preferred_element_type=F32)
        hu = jnp.dot(xb, wu_s[...], preferred_element_type=F32)
        hm = (hg * _sigmoid(hg) * hu).astype(BF16)
        ys_ref[...] = _pack_pairs(jnp.dot(hm, wd_s[...], preferred_element_type=F32))

    @pl.when(jnp.logical_not(used))
    def _():
        ys_ref[...] = jnp.zeros_like(ys_ref)


def _moe_ffn(blk_expert, n_used, xs, w_gate, w_up, w_down):
    P = xs.shape[0]
    nblk = P // MOE_BLOCK
    grid_spec = pltpu.PrefetchScalarGridSpec(
        num_scalar_prefetch=2,
        grid=(nblk,),
        in_specs=[pl.BlockSpec((MOE_BLOCK, D_PACKED), lambda i, be, nu: (i, 0)),
                  pl.BlockSpec((1, D_MODEL, D_EXPERT), lambda i, be, nu: (be[i], 0, 0)),
                  pl.BlockSpec((1, D_MODEL, D_EXPERT), lambda i, be, nu: (be[i], 0, 0)),
                  pl.BlockSpec((1, D_EXPERT, D_MODEL), lambda i, be, nu: (be[i], 0, 0))],
        out_specs=pl.BlockSpec((MOE_BLOCK, D_PACKED), lambda i, be, nu: (i, 0)),
        scratch_shapes=[pltpu.VMEM((D_MODEL, D_EXPERT), BF16),
                        pltpu.VMEM((D_MODEL, D_EXPERT), BF16),
                        pltpu.VMEM((D_EXPERT, D_MODEL), BF16)],
    )
    return pl.pallas_call(
        _moe_ffn_kernel,
        out_shape=jax.ShapeDtypeStruct((P, D_PACKED), jnp.uint32),
        grid_spec=grid_spec,
        compiler_params=_cparams(("arbitrary",)),
        name="moe_ffn",
    )(blk_expert, n_used, xs, w_gate, w_up, w_down)


def _combine_kernel(dcur_ref, dnext_ref, h_ref, gate_ref, fg_ref, ys_hbm, o_ref, ybuf, sems, *, final):
    tm = h_ref.shape[0]
    i = pl.program_id(0)
    n = pl.num_programs(0)

    def gather(dref, slot):
        def issue(g, carry):
            for k in range(TOP_K):
                tile = ybuf.at[slot, pl.ds(pl.multiple_of(k * tm + g * 8, 8), 8)]
                for s in range(8):
                    pltpu.make_async_copy(ys_hbm.at[pl.ds(dref[k, g * 8 + s], 1)],
                                          tile.at[pl.ds(s, 1)], sems.at[slot]).start(priority=k % 2)
            return carry

        lax.fori_loop(0, tm // 8, issue, 0)

    @pl.when(i == 0)
    def _():
        gather(dcur_ref, 0)

    @pl.when(i + 1 < n)
    def _():
        gather(dnext_ref, (i + 1) % 2)

    slot = i % 2
    pltpu.make_async_copy(ys_hbm.at[pl.ds(0, TOP_K * tm)], ybuf.at[slot], sems.at[slot]).wait()
    g = gate_ref[...]
    hh = h_ref[...]
    for k in range(TOP_K):
        hh = hh + _unpack_pairs(ybuf[slot, pl.ds(k * tm, tm), :]) * g[:, k:k + 1]
    o_ref[...] = _rms(hh, fg_ref[...]) if final else hh


def _combine(h, ys, dest, gate_t, fg, tm, final):
    N = h.shape[0]
    n = N // tm
    rows = pl.BlockSpec((tm, D_MODEL), lambda i: (i, 0))
    return pl.pallas_call(
        functools.partial(_combine_kernel, final=final),
        out_shape=jax.ShapeDtypeStruct((N, D_MODEL), F32),
        grid=(n,),
        in_specs=[pl.BlockSpec((TOP_K, tm), lambda i: (0, i), memory_space=pltpu.SMEM),
                  pl.BlockSpec((TOP_K, tm), lambda i: (0, jnp.minimum(i + 1, n - 1)), memory_space=pltpu.SMEM),
                  rows, pl.BlockSpec((tm, TOP_K), lambda i: (i, 0)),
                  pl.BlockSpec((1, D_MODEL), lambda i: (0, 0)),
                  pl.BlockSpec(memory_space=pl.ANY)],
        out_specs=rows,
        scratch_shapes=[pltpu.VMEM((2, TOP_K * tm, D_PACKED), jnp.uint32), pltpu.SemaphoreType.DMA((2,))],
        compiler_params=_cparams(("arbitrary",)),
        name="combine",
    )(dest, dest, h, gate_t, fg, ys)


def _reorder_w_in(w_in):
    n_gla = 2 * GLA_KEY_WIDTH + 2 * GLA_WIDTH
    z0 = n_gla
    a0 = z0 + 2 * GLA_GATE_RANK
    pad = jnp.zeros((D_MODEL, Z_PAD - 2 * GLA_GATE_RANK), w_in.dtype)
    return jnp.concatenate([w_in[:, :n_gla], w_in[:, a0:], w_in[:, z0:a0], pad], axis=1)


def kernel(x, norm1_gain, w_in, gla_up_fwd, gla_up_fwd_bias, gla_up_bwd, gla_up_bwd_bias, gla_out_gain, q_norm_gain, k_norm_gain, att_out_gain, w_out, norm2_gain, w_group, b_group, w_expert, b_expert, w_gate, w_up, w_down, final_gain):
    B, T, D = x.shape
    N = B * T
    depth = norm1_gain.shape[0]
    tm = min(512, T)
    tq = 128
    key_block = min(512, T)
    h = x.reshape(N, D)
    cos, se, so = _rope_tables(T)
    head_mean = jnp.asarray(
        np.kron(np.eye(ATT_HEADS, dtype=np.float32),
                np.full((ATT_HEAD_DIM, ATT_HEAD_DIM), 1.0 / ATT_HEAD_DIM, np.float32)), BF16)
    for l in range(depth):
        w = _reorder_w_in(w_in[l]).astype(BF16)
        r = GLA_GATE_RANK
        up = jnp.zeros((Z_PAD, 2 * GLA_KEY_WIDTH), F32)
        up = up.at[:r, :GLA_KEY_WIDTH].set(gla_up_fwd[l]).at[r:2 * r, GLA_KEY_WIDTH:].set(gla_up_bwd[l])
        upb = jnp.concatenate([gla_up_fwd_bias[l], gla_up_bwd_bias[l]])[None, :]
        qg = jnp.tile(q_norm_gain[l], ATT_HEADS)[None, :]
        kg = jnp.tile(k_norm_gain[l], ATT_KV_HEADS)[None, :]
        gq, gk, gv, gg, laf, lab, aqt, ak, avt = _in_proj(
            h, B, T, tm, norm1_gain[l][None, :], w, up.astype(BF16), upb, qg, kg, head_mean, cos, se, so)
        o_gla = _gla(gq, gk, gv, gg, laf, lab, gla_out_gain[l][None, :], B, T)
        o_att_t = _attn(aqt, ak, avt, B, T, tq, key_block)

        wr = jnp.zeros((LANES, D), F32)
        wr = wr.at[:N_GROUPS].set(w_group[l].T).at[N_GROUPS:N_GROUPS + N_EXPERTS].set(w_expert[l].T)
        br = jnp.zeros((LANES, 1), F32)
        br = br.at[:N_GROUPS, 0].set(b_group[l]).at[N_GROUPS:N_GROUPS + N_EXPERTS, 0].set(b_expert[l])
        h, xn, eid, gate, rank, cnt = _out_proj(
            h, o_gla, o_att_t, w_out[l].astype(BF16), att_out_gain[l][:, None], norm2_gain[l][None, :],
            wr.astype(BF16), br, tm)

        counts = cnt[:, 0].astype(jnp.int32)
        padded = (counts + MOE_BLOCK - 1) // MOE_BLOCK * MOE_BLOCK
        pad_ends = jnp.cumsum(padded)
        pad_starts = pad_ends - padded
        seg_start = jnp.sum(jnp.where(eid[:, :, None] == jnp.arange(N_EXPERTS, dtype=jnp.int32),
                                      pad_starts.astype(jnp.int32), 0), axis=-1)
        dest = seg_start + rank
        P = N * TOP_K + N_EXPERTS * MOE_BLOCK
        nblk = P // MOE_BLOCK
        blk_start = jnp.arange(nblk, dtype=jnp.int32) * MOE_BLOCK
        blk_expert = jnp.minimum(
            jnp.sum((pad_ends[None, :] <= blk_start[:, None]).astype(jnp.int32), axis=1), N_EXPERTS - 1)
        n_used = (pad_ends[-1:] // MOE_BLOCK).astype(jnp.int32)

        zero_starts = jnp.maximum(pad_ends - MOE_BLOCK, 0).astype(jnp.int32)
        xs = _dispatch(zero_starts, n_used, dest, xn, P, tm)
        ys = _moe_ffn(blk_expert, n_used, xs, w_gate[l], w_up[l], w_down[l])
        h = _combine(h, ys, dest, gate.T, final_gain[None, :], tm, l == depth - 1)
    return h.reshape(B, T, D)
```

```python
import functools

import numpy as np
import jax
import jax.numpy as jnp
from jax import lax
from jax.experimental import pallas as pl
from jax.experimental.pallas import tpu as pltpu

F32 = jnp.float32
BF16 = jnp.bfloat16

D_MODEL = 1024
EPS = 1e-6
GRID_W = 64

GLA_HEADS = 4
GLA_DK = 64
GLA_DV = 128
GLA_KEY_WIDTH = GLA_HEADS * GLA_DK
GLA_WIDTH = GLA_HEADS * GLA_DV
GLA_GATE_RANK = 16
GLA_GATE_NORM = 16.0
GLA_CHUNK = 64

ATT_HEADS = 8
ATT_KV_HEADS = 2
ATT_HEAD_DIM = 64
ATT_GROUP = ATT_HEADS // ATT_KV_HEADS
ATT_WIDTH = ATT_HEADS * ATT_HEAD_DIM
ATT_KV_WIDTH = ATT_KV_HEADS * ATT_HEAD_DIM
ROPE_THETA = 10000.0
LOG2_E = 1.4426950408889634

N_GROUPS = 8
EXPERTS_PER_GROUP = 8
N_EXPERTS = N_GROUPS * EXPERTS_PER_GROUP
TOP_K = 2
D_EXPERT = 512
MOE_BLOCK = 256

LANES = 128
Z_PAD = LANES

_OFF_GQ = 0
_OFF_GK = _OFF_GQ + GLA_KEY_WIDTH
_OFF_GV = _OFF_GK + GLA_KEY_WIDTH
_OFF_GG = _OFF_GV + GLA_WIDTH
_OFF_AQ = _OFF_GG + GLA_WIDTH
_OFF_AK = _OFF_AQ + ATT_WIDTH
_OFF_AV = _OFF_AK + ATT_KV_WIDTH
_OFF_Z = _OFF_AV + ATT_KV_WIDTH
D_IN_PAD = _OFF_Z + Z_PAD

VMEM_LIMIT = 56 * 1024 * 1024


def _cparams(semantics):
    return pltpu.CompilerParams(dimension_semantics=semantics, vmem_limit_bytes=VMEM_LIMIT)


def _rms(x, gain):
    return x * lax.rsqrt(jnp.mean(x * x, axis=-1, keepdims=True) + EPS) * gain


def _sigmoid(x):
    return 1.0 / (1.0 + jnp.exp(-x))


def _head_norm_rope(x, head_mean, gain, cos, sin_even, sin_odd):
    w = x.shape[1]
    ms = jnp.dot((x * x).astype(BF16), head_mean, preferred_element_type=F32)
    xn = x * lax.rsqrt(ms + EPS) * gain
    reps = w // LANES
    tile = lambda t: t if reps == 1 else jnp.concatenate([t] * reps, axis=1)
    nxt = pltpu.roll(xn, w - 1, 1)
    prv = pltpu.roll(xn, 1, 1)
    return xn * tile(cos) + nxt * tile(sin_even) + prv * tile(sin_odd)


def _in_proj_kernel(x_ref, g1_ref, w_ref, up_ref, upb_ref, qg_ref, kg_ref, hm_ref,
                    cos_ref, se_ref, so_ref,
                    gq_ref, gk_ref, gv_ref, gg_ref, laf_ref, lab_ref, aqt_ref, ak_ref, avt_ref):
    u = _rms(x_ref[...], g1_ref[...]).astype(BF16)

    def proj(lo, hi):
        return jnp.dot(u, w_ref[:, lo:hi], preferred_element_type=F32)

    cos, se, so = cos_ref[...], se_ref[...], so_ref[...]
    hm = hm_ref[...]
    r_q = proj(_OFF_AQ, _OFF_AK)
    r_gqk = proj(_OFF_GQ, _OFF_GV)
    q = _head_norm_rope(r_q, hm, qg_ref[...], cos, se, so)
    aqt_ref[0] = (q * (ATT_HEAD_DIM ** -0.5 * LOG2_E)).T.astype(BF16)
    gq_ref[...] = r_gqk[:, :GLA_KEY_WIDTH].astype(BF16)
    gk_ref[...] = r_gqk[:, GLA_KEY_WIDTH:].astype(BF16)

    r_kvz = proj(_OFF_AK, D_IN_PAD)
    r_gv = proj(_OFF_GV, _OFF_GG)
    k = _head_norm_rope(r_kvz[:, :ATT_KV_WIDTH], hm[:ATT_KV_WIDTH, :ATT_KV_WIDTH], kg_ref[...], cos, se, so)
    ak_ref[...] = k.astype(BF16)
    avt_ref[0] = r_kvz[:, ATT_KV_WIDTH:2 * ATT_KV_WIDTH].T.astype(BF16)
    gv_ref[...] = r_gv.astype(BF16)

    z = r_kvz[:, 2 * ATT_KV_WIDTH:].astype(BF16)
    zl = jnp.dot(z, up_ref[...], preferred_element_type=F32) + upb_ref[...]
    r_gg = proj(_OFF_GG, _OFF_AQ)
    la = (jnp.minimum(zl, 0.0) - jnp.log(1.0 + jnp.exp(-jnp.abs(zl)))) * (1.0 / GLA_GATE_NORM)
    laf_ref[...] = la[:, :GLA_KEY_WIDTH]
    lab_ref[...] = la[:, GLA_KEY_WIDTH:]
    gg_ref[...] = r_gg.astype(BF16)


def _rope_tables(T):
    t = np.arange(T)
    row = (t // GRID_W).astype(np.float32)
    col = (t % GRID_W).astype(np.float32)
    axis_dim = ATT_HEAD_DIM // 2
    inv_freq = (ROPE_THETA ** (-np.arange(0, axis_dim, 2, dtype=np.float32) / axis_dim)).astype(np.float32)
    ang = np.concatenate([row[:, None] * inv_freq, col[:, None] * inv_freq], axis=-1)
    ang = np.repeat(ang, 2, axis=1)
    ang = np.tile(ang, (1, LANES // ATT_HEAD_DIM))
    even = (np.arange(LANES) % 2 == 0)[None, :]
    cos = np.cos(ang)
    sin = np.sin(ang)
    return (jnp.asarray(cos, F32), jnp.asarray(np.where(even, -sin, 0.0), F32),
            jnp.asarray(np.where(even, 0.0, sin), F32))


def _in_proj(xt, B, T, tm, g1, w, up, upb, qg, kg, hm, cos, se, so):
    N = xt.shape[0]
    tpb = T // tm
    const = lambda shape: pl.BlockSpec(shape, lambda i: (0,) * len(shape))
    rows = lambda width: pl.BlockSpec((tm, width), lambda i: (i, 0))
    pos = pl.BlockSpec((tm, LANES), lambda i: (i % tpb, 0))
    out_shape = (
        jax.ShapeDtypeStruct((N, GLA_KEY_WIDTH), BF16),
        jax.ShapeDtypeStruct((N, GLA_KEY_WIDTH), BF16),
        jax.ShapeDtypeStruct((N, GLA_WIDTH), BF16),
        jax.ShapeDtypeStruct((N, GLA_WIDTH), BF16),
        jax.ShapeDtypeStruct((N, GLA_KEY_WIDTH), F32),
        jax.ShapeDtypeStruct((N, GLA_KEY_WIDTH), F32),
        jax.ShapeDtypeStruct((B, ATT_WIDTH, T), BF16),
        jax.ShapeDtypeStruct((N, ATT_KV_WIDTH), BF16),
        jax.ShapeDtypeStruct((B, ATT_KV_WIDTH, T), BF16),
    )
    cols = lambda width: pl.BlockSpec((1, width, tm), lambda i: (i // tpb, 0, i % tpb))
    out_specs = (
        rows(GLA_KEY_WIDTH), rows(GLA_KEY_WIDTH), rows(GLA_WIDTH), rows(GLA_WIDTH),
        rows(GLA_KEY_WIDTH), rows(GLA_KEY_WIDTH),
        cols(ATT_WIDTH), rows(ATT_KV_WIDTH), cols(ATT_KV_WIDTH),
    )
    return pl.pallas_call(
        _in_proj_kernel,
        out_shape=out_shape,
        grid=(N // tm,),
        in_specs=[rows(D_MODEL), const((1, D_MODEL)), const((D_MODEL, D_IN_PAD)),
                  const((Z_PAD, 2 * GLA_KEY_WIDTH)), const((1, 2 * GLA_KEY_WIDTH)),
                  const((1, ATT_WIDTH)), const((1, ATT_KV_WIDTH)), const((ATT_WIDTH, ATT_WIDTH)),
                  pos, pos, pos],
        out_specs=out_specs,
        compiler_params=_cparams(("arbitrary",)),
        name="in_proj",
    )(xt, g1, w, up, upb, qg, kg, hm, cos, se, so)


GLA_GROUP = 4


def _gla_kernel(q_ref, k_ref, v_ref, g_ref, laf_ref, lab_ref, gain_ref, o_ref,
                part_scr, qif_scr, qib_scr, kvf_scr, kvb_scr, decf_scr, decb_scr, lf_scr, lb_scr):
    C, G = GLA_CHUNK, GLA_GROUP
    R = C * G
    T = q_ref.shape[0]
    n_chunks = T // C
    gain = gain_ref[...]
    row = lax.broadcasted_iota(jnp.int32, (C, C), 0)
    col = lax.broadcasted_iota(jnp.int32, (C, C), 1)
    tril = row >= col
    triu = row <= col
    rr = lax.broadcasted_iota(jnp.int32, (R, R), 0)
    cc = lax.broadcasted_iota(jnp.int32, (R, R), 1)
    same_chunk = (rr // C) == (cc // C)
    lf_scr[...] = jnp.where(same_chunk & (rr >= cc), 1.0, 0.0).astype(BF16)
    lb_scr[...] = jnp.where(same_chunk & (rr <= cc), 1.0, 0.0).astype(BF16)

    directions = ((laf_ref, lf_scr, tril, C // 2 - 1, C - 1, qif_scr, kvf_scr, decf_scr),
                  (lab_ref, lb_scr, triu, C // 2, 0, qib_scr, kvb_scr, decb_scr))
    heads = tuple((slice(h * GLA_DK, (h + 1) * GLA_DK), slice(h * GLA_DV, (h + 1) * GLA_DV)) for h in range(2))
    nt = (((1,), (1,)), ((), ()))
    tn = (((0,), (0,)), ((), ()))

    def intra(sb, carry):
        r0 = pl.multiple_of(sb * R, R)
        q2 = q_ref[pl.ds(r0, R), :].astype(F32) * (GLA_DK ** -0.5)
        k2 = k_ref[pl.ds(r0, R), :].astype(F32)
        v2 = v_ref[pl.ds(r0, R), :]
        cums = []
        for la_ref, tri_scr, *_ in directions:
            la = la_ref[pl.ds(r0, R), :]
            la_hi = la.astype(BF16)
            la_lo = (la - la_hi.astype(F32)).astype(BF16)
            tri = tri_scr[...]
            cums.append(jnp.dot(tri, la_hi, preferred_element_type=F32)
                        + jnp.dot(tri, la_lo, preferred_element_type=F32))
        scaled = []
        for b, (_, _, mask, i_ref, i_last, qi_scr, kv_scr, dec_scr) in zip(cums, directions):
            for c in range(G):
                rows = slice(c * C, (c + 1) * C)
                bc, qc, kc = b[rows], q2[rows], k2[rows]
                b_ref = bc[i_ref:i_ref + 1, :]
                b_last = bc[i_last:i_last + 1, :]
                qf = (qc * jnp.exp(bc - b_ref)).astype(BF16)
                kf = (kc * jnp.exp(b_ref - bc)).astype(BF16)
                qi_scr[pl.ds(r0 + c * C, C), :] = (qc * jnp.exp(bc)).astype(BF16)
                kl = (kc * jnp.exp(b_last - bc)).astype(BF16)
                decay = jnp.exp(b_last)
                for h, (ks, _) in enumerate(heads):
                    dec_scr[sb * G + c, h] = decay[:, ks]
                scaled.append((c, mask, kv_scr, qf, kf, kl))
        raw = [[lax.dot_general(qf[:, ks], kf[:, ks], nt, preferred_element_type=F32) for ks, _ in heads]
               for _, _, _, qf, kf, _ in scaled]
        o_intra = [[None, None] for _ in range(G)]
        for (c, mask, kv_scr, _, _, kl), sc2 in zip(scaled, raw):
            rows = slice(c * C, (c + 1) * C)
            for h, (ks, vs) in enumerate(heads):
                vh = v2[rows, vs]
                sc = jnp.where(mask, sc2[h], 0.0).astype(BF16)
                oi = jnp.dot(sc, vh, preferred_element_type=F32)
                o_intra[c][h] = oi if o_intra[c][h] is None else o_intra[c][h] + oi
                kv_scr[sb * G + c, h] = lax.dot_general(vh, kl[:, ks], tn, preferred_element_type=F32)
        for c in range(G):
            part_scr[pl.ds(r0 + c * C, C), :] = jnp.concatenate(o_intra[c], axis=1)
        return carry

    lax.fori_loop(0, n_chunks // G, intra, 0)

    def scan(n, carry):
        sf, sb = carry
        nb = n_chunks - 1 - n
        new_f, new_b = [], []
        for h in range(2):
            kv = kvf_scr[n, h]
            kvf_scr[n, h] = sf[h]
            new_f.append(decf_scr[n, h] * sf[h] + kv)
            kv = kvb_scr[nb, h]
            kvb_scr[nb, h] = sb[h]
            new_b.append(decb_scr[nb, h] * sb[h] + kv)
        return tuple(new_f), tuple(new_b)

    zero_state = tuple(jnp.zeros((GLA_DV, GLA_DK), F32) for _ in range(2))
    lax.fori_loop(0, n_chunks, scan, (zero_state, zero_state), unroll=4)

    def finish(sb, carry):
        r0 = pl.multiple_of(sb * R, R)
        g = g_ref[pl.ds(r0, R), :].astype(F32)
        part = part_scr[pl.ds(r0, R), :]
        qf = qif_scr[pl.ds(r0, R), :]
        qb = qib_scr[pl.ds(r0, R), :]
        inter = []
        for c in range(G):
            rows = slice(c * C, (c + 1) * C)
            inter.append([
                lax.dot_general(qf[rows, ks], kvf_scr[sb * G + c, h].astype(BF16), nt, preferred_element_type=F32)
                + lax.dot_general(qb[rows, ks], kvb_scr[sb * G + c, h].astype(BF16), nt, preferred_element_type=F32)
                for h, (ks, _) in enumerate(heads)])
        gate = g * _sigmoid(g)
        for c in range(G):
            rows = slice(c * C, (c + 1) * C)
            normed = jnp.concatenate([_rms(part[rows, vs] + inter[c][h], gain) for h, (_, vs) in enumerate(heads)],
                                     axis=1)
            o_ref[pl.ds(r0 + c * C, C), :] = (normed * gate[rows]).astype(BF16)
        return carry

    lax.fori_loop(0, n_chunks // G, finish, 0)


def _gla(gq, gk, gv, gg, laf, lab, gain, B, T):
    N = gq.shape[0]
    pairs = GLA_HEADS // 2
    n_chunks = T // GLA_CHUNK
    assert n_chunks % GLA_GROUP == 0
    group_rows = GLA_CHUNK * GLA_GROUP
    kspec = pl.BlockSpec((T, 2 * GLA_DK), lambda b, p: (b, p))
    vspec = pl.BlockSpec((T, 2 * GLA_DV), lambda b, p: (b, p))
    state = pltpu.VMEM((n_chunks, 2, GLA_DV, GLA_DK), F32)
    decay = pltpu.VMEM((n_chunks, 2, 1, GLA_DK), F32)
    return pl.pallas_call(
        _gla_kernel,
        out_shape=jax.ShapeDtypeStruct((N, GLA_WIDTH), BF16),
        grid=(B, pairs),
        in_specs=[kspec, kspec, vspec, vspec, kspec, kspec,
                  pl.BlockSpec((1, GLA_DV), lambda b, p: (0, 0))],
        out_specs=vspec,
        scratch_shapes=[pltpu.VMEM((T, 2 * GLA_DV), F32),
                        pltpu.VMEM((T, 2 * GLA_DK), BF16), pltpu.VMEM((T, 2 * GLA_DK), BF16),
                        state, state, decay, decay,
                        pltpu.VMEM((group_rows, group_rows), BF16), pltpu.VMEM((group_rows, group_rows), BF16)],
        compiler_params=_cparams(("arbitrary", "arbitrary")),
        name="gla",
    )(gq, gk, gv, gg, laf, lab, gain)


def _attn_kernel(qt_ref, k_ref, vt_ref, ot_ref, vext_scr, s0_scr, s1_scr, p0_scr, p1_scr, m0_scr, m1_scr,
                 *, tq, key_block, steps_per_iter):
    j = pl.program_id(1)
    T = k_ref.shape[0]
    nq = T // tq
    dh = ATT_HEAD_DIM
    cols = ATT_GROUP * tq
    first = j == 0

    row = lax.broadcasted_iota(jnp.int32, (ATT_KV_WIDTH, T), 0)
    own = (row >= j * dh) & (row < (j + 1) * dh)
    vext_scr[...] = jnp.where(own, vt_ref[0], jnp.ones((), BF16))

    s_scr, p_scr, m_scr = (s0_scr, s1_scr), (p0_scr, p1_scr), (m0_scr, m1_scr)

    def step(blk, sa, do_scores=True, do_probs=True, do_out=True):
        sb = 1 - sa
        if do_scores:
            c0 = pl.multiple_of(blk * tq, tq)
            q4 = jnp.concatenate([qt_ref[0, h * dh:(h + 1) * dh, pl.ds(c0, tq)] for h in range(ATT_GROUP)],
                                 axis=1)
            zeros = jnp.zeros_like(q4)
            qe = jnp.where(first, jnp.concatenate([q4, zeros], axis=0), jnp.concatenate([zeros, q4], axis=0))
            m = jnp.full((8, cols), -jnp.inf, F32)
        if do_probs:
            mx = m_scr[sb][...]
        if do_out:
            acc = jnp.zeros((ATT_KV_WIDTH, cols), F32)
        for kb in range(T // key_block):
            ks = slice(kb * key_block, (kb + 1) * key_block)
            if do_scores:
                s = jnp.dot(k_ref[ks, :], qe, preferred_element_type=F32)
                s_scr[sa][ks, :] = s
                m = jnp.maximum(m, jnp.max(s.reshape(key_block // 8, 8, cols), axis=0))
            if do_probs:
                p_scr[sb][ks, :] = jnp.exp2(s_scr[sb][ks, :] - mx).astype(BF16)
            if do_out:
                acc = acc + jnp.dot(vext_scr[:, ks], p_scr[sa][ks, :], preferred_element_type=F32)
        if do_scores:
            m_scr[sa][...] = jnp.max(m, axis=0, keepdims=True)
        if do_out:
            num = jnp.where(first, acc[:dh], acc[dh:])
            den = jnp.where(first, acc[dh:dh + 1], acc[0:1])
            o = (num / den).astype(BF16)
            c0 = pl.multiple_of((blk - 2) * tq, tq)
            for h in range(ATT_GROUP):
                ot_ref[0, h * dh:(h + 1) * dh, pl.ds(c0, tq)] = o[:, h * tq:(h + 1) * tq]

    step(0, 0, do_probs=False, do_out=False)
    step(1, 1, do_out=False)

    def steady(it, carry):
        for u in range(steps_per_iter):
            pl.when(it >= 0)(functools.partial(step, 2 + steps_per_iter * it + u, u % 2))
        return carry

    lax.fori_loop(0, (nq - 2) // steps_per_iter, steady, 0)
    step(nq, 0, do_scores=False)
    step(nq + 1, 1, do_scores=False, do_probs=False)


def _attn(aqt, ak, avt, B, T, tq, key_block):
    assert T % (2 * tq) == 0 and tq % LANES == 0
    gw = ATT_GROUP * ATT_HEAD_DIM
    cols = ATT_GROUP * tq
    steps_per_iter = 2
    return pl.pallas_call(
        functools.partial(_attn_kernel, tq=tq, key_block=key_block, steps_per_iter=steps_per_iter),
        out_shape=jax.ShapeDtypeStruct((B, ATT_WIDTH, T), BF16),
        grid=(B, ATT_KV_HEADS),
        in_specs=[pl.BlockSpec((1, gw, T), lambda b, j: (b, j, 0)),
                  pl.BlockSpec((T, ATT_KV_WIDTH), lambda b, j: (b, 0)),
                  pl.BlockSpec((1, ATT_KV_WIDTH, T), lambda b, j: (b, 0, 0))],
        out_specs=pl.BlockSpec((1, gw, T), lambda b, j: (b, j, 0)),
        scratch_shapes=[pltpu.VMEM((ATT_KV_WIDTH, T), BF16),
                        pltpu.VMEM((T, cols), F32), pltpu.VMEM((T, cols), F32),
                        pltpu.VMEM((T, cols), BF16), pltpu.VMEM((T, cols), BF16),
                        pltpu.VMEM((1, cols), F32), pltpu.VMEM((1, cols), F32)],
        compiler_params=_cparams(("arbitrary", "arbitrary")),
        name="attn",
    )(aqt, ak, avt)


def _out_proj_kernel(x_ref, og_ref, oa_ref, wo_ref, ag_ref, g2_ref, wr_ref, br_ref, tri_ref,
                     h_ref, xn_ref, eid_ref, gate_ref, rank_ref, cnt_ref, run_scr):
    tm = x_ref.shape[0]

    @pl.when(pl.program_id(0) == 0)
    def _():
        run_scr[...] = jnp.zeros_like(run_scr)

    oat = oa_ref[0].astype(F32)
    oan = (oat * lax.rsqrt(jnp.mean(oat * oat, axis=0, keepdims=True) + EPS) * ag_ref[...]).astype(BF16)
    y = (jnp.dot(og_ref[...], wo_ref[:GLA_WIDTH, :], preferred_element_type=F32)
         + lax.dot_general(oan, wo_ref[GLA_WIDTH:, :], (((0,), (0,)), ((), ())),
                           preferred_element_type=F32))
    h = x_ref[...] + y
    h_ref[...] = h
    xn = _rms(h, g2_ref[...])
    xn_ref[...] = xn

    lt = lax.dot_general(wr_ref[...], xn.astype(BF16), (((1,), (1,)), ((), ())),
                         preferred_element_type=F32) + br_ref[...]
    iota8 = lax.broadcasted_iota(jnp.int32, (N_GROUPS, tm), 0)

    def first_argmax(v):
        top = jnp.max(v, axis=0, keepdims=True)
        idx = jnp.min(jnp.where(v == top, iota8, N_GROUPS), axis=0, keepdims=True)
        return top, idx

    gl = lt[0:N_GROUPS]
    gmax, gidx = first_argmax(gl)
    gw = 1.0 / jnp.sum(jnp.exp(gl - gmax), axis=0, keepdims=True)
    esel = jnp.zeros((EXPERTS_PER_GROUP, tm), F32)
    for g in range(N_GROUPS):
        lo = N_GROUPS + g * EXPERTS_PER_GROUP
        esel = jnp.where(gidx == g, lt[lo:lo + EXPERTS_PER_GROUP], esel)
    v1, i1 = first_argmax(esel)
    rest = jnp.where(iota8 == i1, -jnp.inf, esel)
    v2, i2 = first_argmax(rest)
    t = jnp.exp(v2 - v1)
    den = 1.0 + t
    e1 = gidx * EXPERTS_PER_GROUP + i1
    e2 = gidx * EXPERTS_PER_GROUP + i2
    eid_ref[0:1, :] = e1
    eid_ref[1:2, :] = e2
    gate_ref[0:1, :] = gw * (1.0 / den)
    gate_ref[1:2, :] = gw * (t / den)

    iota_e = lax.broadcasted_iota(jnp.int32, (N_EXPERTS, tm), 0)
    oh1 = iota_e == e1
    oh2 = iota_e == e2
    both = jnp.where(oh1, 1.0, jnp.where(oh2, 1.0, 0.0)).astype(BF16)
    prefix = jnp.dot(both, tri_ref[...], preferred_element_type=F32)
    run = run_scr[...]
    base = prefix - 1.0 + run[:, 0:1]
    rank_ref[0:1, :] = jnp.sum(jnp.where(oh1, base, 0.0), axis=0, keepdims=True).astype(jnp.int32)
    rank_ref[1:2, :] = jnp.sum(jnp.where(oh2, base, 0.0), axis=0, keepdims=True).astype(jnp.int32)
    run = run + prefix[:, tm - 1:tm]
    run_scr[...] = run
    cnt_ref[...] = run


def _out_proj(xt, o_gla, o_att_t, wo, ag, g2, wr, br, tm):
    N = xt.shape[0]
    tpb = o_att_t.shape[2] // tm
    const = lambda shape: pl.BlockSpec(shape, lambda i: (0,) * len(shape))
    rows = lambda width: pl.BlockSpec((tm, width), lambda i: (i, 0))
    cols = pl.BlockSpec((TOP_K, tm), lambda i: (0, i))
    tri = jnp.asarray(np.triu(np.ones((tm, tm), np.float32)), BF16)
    return pl.pallas_call(
        _out_proj_kernel,
        out_shape=(jax.ShapeDtypeStruct((N, D_MODEL), F32),
                   jax.ShapeDtypeStruct((N, D_MODEL), F32),
                   jax.ShapeDtypeStruct((TOP_K, N), jnp.int32),
                   jax.ShapeDtypeStruct((TOP_K, N), F32),
                   jax.ShapeDtypeStruct((TOP_K, N), jnp.int32),
                   jax.ShapeDtypeStruct((N_EXPERTS, LANES), F32)),
        grid=(N // tm,),
        in_specs=[rows(D_MODEL), rows(GLA_WIDTH),
                  pl.BlockSpec((1, ATT_WIDTH, tm), lambda i: (i // tpb, 0, i % tpb)),
                  const((D_MODEL, D_MODEL)),
                  const((ATT_WIDTH, 1)), const((1, D_MODEL)), const((LANES, D_MODEL)),
                  const((LANES, 1)), const((tm, tm))],
        out_specs=(rows(D_MODEL), rows(D_MODEL), cols, cols, cols, const((N_EXPERTS, LANES))),
        scratch_shapes=[pltpu.VMEM((N_EXPERTS, LANES), F32)],
        compiler_params=_cparams(("arbitrary",)),
        name="out_proj",
    )(xt, o_gla, o_att_t, wo, ag, g2, wr, br, tri)


D_PACKED = D_MODEL // 2


def _pack_pairs(x):
    w = x.shape[1] // 2
    bits = pltpu.bitcast(x.astype(BF16).astype(F32), jnp.uint32)
    return bits[:, :w] | (bits[:, w:] >> 16)


def _unpack_pairs(words):
    hi = pltpu.bitcast(words & jnp.uint32(0xFFFF0000), F32)
    lo = pltpu.bitcast(words << 16, F32)
    return jnp.concatenate([hi, lo], axis=1)


def _dispatch_kernel(zs_ref, nu_ref, *refs):
    dest_refs = refs[:TOP_K]
    xn_ref, xs_hbm, packed, zero_buf, zero_sem, row_sem = refs[TOP_K:]
    tn = xn_ref.shape[0]
    i = pl.program_id(0)
    packed[...] = _pack_pairs(xn_ref[...])

    def fill(row0):
        return pltpu.make_async_copy(
            zero_buf, xs_hbm.at[pl.ds(pl.multiple_of(row0, MOE_BLOCK), MOE_BLOCK)], zero_sem)

    @pl.when(i == 0)
    def _():
        zero_buf[...] = jnp.zeros_like(zero_buf)
        fills = [fill(zs_ref[e]) for e in range(N_EXPERTS)]
        for f in fills:
            f.start()
        for f in fills:
            f.wait()

        def fill_tail(blk, carry):
            f = fill(blk * MOE_BLOCK)
            f.start()
            f.wait()
            return carry

        lax.fori_loop(nu_ref[0], xs_hbm.shape[0] // MOE_BLOCK, fill_tail, 0)

    def issue(g, carry):
        tile = packed.at[pl.ds(pl.multiple_of(g * 8, 8), 8)]
        for s in range(8):
            for k in range(TOP_K):
                pltpu.make_async_copy(tile.at[pl.ds(s, 1)], xs_hbm.at[pl.ds(dest_refs[k][g * 8 + s], 1)],
                                      row_sem).start(priority=k % 2)
        return carry

    lax.fori_loop(0, tn // 8, issue, 0)
    for k in range(TOP_K):
        pltpu.make_async_copy(packed, xs_hbm.at[pl.ds(0, tn)], row_sem).wait()


def _dest_specs(n_steps, tn, index):
    return [pl.BlockSpec((tn,), lambda i, *_, k=k: (k * n_steps + index(i),), memory_space=pltpu.SMEM)
            for k in range(TOP_K)]


def _dispatch(zero_starts, n_used, dest_flat, xn, P, tn):
    N = xn.shape[0]
    grid_spec = pltpu.PrefetchScalarGridSpec(
        num_scalar_prefetch=2,
        grid=(N // tn,),
        in_specs=_dest_specs(N // tn, tn, lambda i: i) + [pl.BlockSpec((tn, D_MODEL), lambda i, zs, nu: (i, 0))],
        out_specs=pl.BlockSpec(memory_space=pl.ANY),
        scratch_shapes=[pltpu.VMEM((tn, D_PACKED), jnp.uint32),
                        pltpu.VMEM((MOE_BLOCK, D_PACKED), jnp.uint32),
                        pltpu.SemaphoreType.DMA, pltpu.SemaphoreType.DMA],
    )
    return pl.pallas_call(
        _dispatch_kernel,
        out_shape=jax.ShapeDtypeStruct((P, D_PACKED), jnp.uint32),
        grid_spec=grid_spec,
        compiler_params=_cparams(("arbitrary",)),
        name="dispatch",
    )(zero_starts, n_used, *([dest_flat] * TOP_K), xn)


def _moe_ffn_kernel(be_ref, nu_ref, seg_ref, nxt_ref, xs_ref, wg_hbm, wu_hbm, wd_hbm, ys_ref,
                    wg_f, wu_f, wd_f, wg_s, wu_s, wd_s, sems):
    i = pl.program_id(0)
    used = i < nu_ref[0]

    def weights(expert, slot):
        return [pltpu.make_async_copy(src.at[expert], dst.at[slot], sems.at[slot])
                for src, dst in ((wg_hbm, wg_f), (wu_hbm, wu_f), (wd_hbm, wd_f))]

    @pl.when(used)
    def _():
        new_expert = jnp.logical_or(i == 0, be_ref[i] != be_ref[jnp.maximum(i - 1, 0)])

        @pl.when(new_expert)
        def _():
            slot = seg_ref[i] % 2

            @pl.when(i == 0)
            def _():
                for cp in weights(be_ref[0], 0):
                    cp.start()

            for cp in weights(be_ref[i], slot):
                cp.wait()

            @pl.when(nxt_ref[i] >= 0)
            def _():
                for cp in weights(nxt_ref[i], 1 - slot):
                    cp.start()

            wg_s[...] = wg_f[slot].astype(BF16)
            wu_s[...] = wu_f[slot].astype(BF16)
            wd_s[...] = wd_f[slot].astype(BF16)

        xb = _unpack_pairs(xs_ref[...]).astype(BF16)
        hg = jnp.dot(xb, wg_s[...], preferred_element_type=F32)
        hu = jnp.dot(xb, wu_s[...], preferred_element_type=F32)
        hm = (hg * _sigmoid(hg) * hu).astype(BF16)
        ys_ref[...] = _pack_pairs(jnp.dot(hm, wd_s[...], preferred_element_type=F32))

    @pl.when(jnp.logical_not(used))
    def _():
        ys_ref[...] = jnp.zeros_like(ys_ref)


def _moe_ffn(blk_expert, n_used, xs, w_gate, w_up, w_down):
    P = xs.shape[0]
    nblk = P // MOE_BLOCK
    blk = jnp.arange(nblk, dtype=jnp.int32)
    starts = jnp.logical_and(jnp.concatenate([jnp.ones((1,), bool), blk_expert[1:] != blk_expert[:-1]]),
                             blk < n_used[0])
    seg = jnp.cumsum(starts.astype(jnp.int32)) - 1
    next_start = lax.cummin(jnp.where(starts, blk, nblk), reverse=True)
    next_start = jnp.concatenate([next_start[1:], jnp.full((1,), nblk, jnp.int32)])
    nxt = jnp.where(next_start < nblk, blk_expert[jnp.minimum(next_start, nblk - 1)], -1).astype(jnp.int32)
    row_block = pl.BlockSpec((MOE_BLOCK, D_PACKED), lambda i, *_: (i, 0))
    grid_spec = pltpu.PrefetchScalarGridSpec(
        num_scalar_prefetch=4,
        grid=(nblk,),
        in_specs=[row_block] + [pl.BlockSpec(memory_space=pl.ANY)] * 3,
        out_specs=row_block,
        scratch_shapes=[pltpu.VMEM((2, D_MODEL, D_EXPERT), F32),
                        pltpu.VMEM((2, D_MODEL, D_EXPERT), F32),
                        pltpu.VMEM((2, D_EXPERT, D_MODEL), F32),
                        pltpu.VMEM((D_MODEL, D_EXPERT), BF16),
                        pltpu.VMEM((D_MODEL, D_EXPERT), BF16),
                        pltpu.VMEM((D_EXPERT, D_MODEL), BF16),
                        pltpu.SemaphoreType.DMA((2,))],
    )
    return pl.pallas_call(
        _moe_ffn_kernel,
        out_shape=jax.ShapeDtypeStruct((P, D_PACKED), jnp.uint32),
        grid_spec=grid_spec,
        compiler_params=_cparams(("arbitrary",)),
        name="moe_ffn",
    )(blk_expert, n_used, seg, nxt, xs, w_gate, w_up, w_down)


def _combine_kernel(*refs, final):
    dcur_refs, dnext_refs = refs[:TOP_K], refs[TOP_K:2 * TOP_K]
    h_ref, gate_ref, fg_ref, ys_hbm, o_ref, ybuf, sems = refs[2 * TOP_K:]
    tm = h_ref.shape[0]
    i = pl.program_id(0)
    n = pl.num_programs(0)

    def gather(drefs, slot):
        def issue(g, carry):
            for k in range(TOP_K):
                tile = ybuf.at[slot, pl.ds(pl.multiple_of(k * tm + g * 8, 8), 8)]
                for s in range(8):
                    pltpu.make_async_copy(ys_hbm.at[pl.ds(drefs[k][g * 8 + s], 1)],
                                          tile.at[pl.ds(s, 1)], sems.at[slot]).start(priority=k % 2)
            return carry

        lax.fori_loop(0, tm // 8, issue, 0)

    @pl.when(i == 0)
    def _():
        gather(dcur_refs, 0)

    @pl.when(i + 1 < n)
    def _():
        gather(dnext_refs, (i + 1) % 2)

    slot = i % 2
    pltpu.make_async_copy(ys_hbm.at[pl.ds(0, TOP_K * tm)], ybuf.at[slot], sems.at[slot]).wait()
    g = gate_ref[...]
    hh = h_ref[...]
    for k in range(TOP_K):
        hh = hh + _unpack_pairs(ybuf[slot, pl.ds(k * tm, tm), :]) * g[:, k:k + 1]
    o_ref[...] = _rms(hh, fg_ref[...]) if final else hh


def _combine(h, ys, dest_flat, gate_t, fg, tm, final):
    N = h.shape[0]
    n = N // tm
    rows = pl.BlockSpec((tm, D_MODEL), lambda i: (i, 0))
    return pl.pallas_call(
        functools.partial(_combine_kernel, final=final),
        out_shape=jax.ShapeDtypeStruct((N, D_MODEL), F32),
        grid=(n,),
        in_specs=_dest_specs(n, tm, lambda i: i) + _dest_specs(n, tm, lambda i: jnp.minimum(i + 1, n - 1)) + [
                  rows, pl.BlockSpec((tm, TOP_K), lambda i: (i, 0)),
                  pl.BlockSpec((1, D_MODEL), lambda i: (0, 0)),
                  pl.BlockSpec(memory_space=pl.ANY)],
        out_specs=rows,
        scratch_shapes=[pltpu.VMEM((2, TOP_K * tm, D_PACKED), jnp.uint32), pltpu.SemaphoreType.DMA((2,))],
        compiler_params=_cparams(("arbitrary",)),
        name="combine",
    )(*([dest_flat] * (2 * TOP_K)), h, gate_t, fg, ys)


def _reorder_w_in(w_in):
    n_gla = 2 * GLA_KEY_WIDTH + 2 * GLA_WIDTH
    z0 = n_gla
    a0 = z0 + 2 * GLA_GATE_RANK
    pad = jnp.zeros((D_MODEL, Z_PAD - 2 * GLA_GATE_RANK), w_in.dtype)
    return jnp.concatenate([w_in[:, :n_gla], w_in[:, a0:], w_in[:, z0:a0], pad], axis=1)


def kernel(x, norm1_gain, w_in, gla_up_fwd, gla_up_fwd_bias, gla_up_bwd, gla_up_bwd_bias, gla_out_gain, q_norm_gain, k_norm_gain, att_out_gain, w_out, norm2_gain, w_group, b_group, w_expert, b_expert, w_gate, w_up, w_down, final_gain):
    B, T, D = x.shape
    N = B * T
    depth = norm1_gain.shape[0]
    tm = min(512, T)
    tq = 128
    key_block = min(512, T)
    h = x.reshape(N, D)
    cos, se, so = _rope_tables(T)
    head_mean = jnp.asarray(
        np.kron(np.eye(ATT_HEADS, dtype=np.float32),
                np.full((ATT_HEAD_DIM, ATT_HEAD_DIM), 1.0 / ATT_HEAD_DIM, np.float32)), BF16)
    for l in range(depth):
        w = _reorder_w_in(w_in[l]).astype(BF16)
        r = GLA_GATE_RANK
        up = jnp.zeros((Z_PAD, 2 * GLA_KEY_WIDTH), F32)
        up = up.at[:r, :GLA_KEY_WIDTH].set(gla_up_fwd[l]).at[r:2 * r, GLA_KEY_WIDTH:].set(gla_up_bwd[l])
        upb = jnp.concatenate([gla_up_fwd_bias[l], gla_up_bwd_bias[l]])[None, :]
        qg = jnp.tile(q_norm_gain[l], ATT_HEADS)[None, :]
        kg = jnp.tile(k_norm_gain[l], ATT_KV_HEADS)[None, :]
        gq, gk, gv, gg, laf, lab, aqt, ak, avt = _in_proj(
            h, B, T, tm, norm1_gain[l][None, :], w, up.astype(BF16), upb, qg, kg, head_mean, cos, se, so)
        o_gla = _gla(gq, gk, gv, gg, laf, lab, gla_out_gain[l][None, :], B, T)
        o_att_t = _attn(aqt, ak, avt, B, T, tq, key_block)

        wr = jnp.zeros((LANES, D), F32)
        wr = wr.at[:N_GROUPS].set(w_group[l].T).at[N_GROUPS:N_GROUPS + N_EXPERTS].set(w_expert[l].T)
        br = jnp.zeros((LANES, 1), F32)
        br = br.at[:N_GROUPS, 0].set(b_group[l]).at[N_GROUPS:N_GROUPS + N_EXPERTS, 0].set(b_expert[l])
        h, xn, eid, gate, rank, cnt = _out_proj(
            h, o_gla, o_att_t, w_out[l].astype(BF16), att_out_gain[l][:, None], norm2_gain[l][None, :],
            wr.astype(BF16), br, tm)

        counts = cnt[:, 0].astype(jnp.int32)
        padded = (counts + MOE_BLOCK - 1) // MOE_BLOCK * MOE_BLOCK
        pad_ends = jnp.cumsum(padded)
        pad_starts = pad_ends - padded
        seg_start = jnp.sum(jnp.where(eid[:, :, None] == jnp.arange(N_EXPERTS, dtype=jnp.int32),
                                      pad_starts.astype(jnp.int32), 0), axis=-1)
        dest = (seg_start + rank).reshape(-1)
        P = N * TOP_K + N_EXPERTS * MOE_BLOCK
        nblk = P // MOE_BLOCK
        blk_start = jnp.arange(nblk, dtype=jnp.int32) * MOE_BLOCK
        blk_expert = jnp.minimum(
            jnp.sum((pad_ends[None, :] <= blk_start[:, None]).astype(jnp.int32), axis=1), N_EXPERTS - 1)
        n_used = (pad_ends[-1:] // MOE_BLOCK).astype(jnp.int32)

        zero_starts = jnp.maximum(pad_ends - MOE_BLOCK, 0).astype(jnp.int32)
        xs = _dispatch(zero_starts, n_used, dest, xn, P, tm)
        ys = _moe_ffn(blk_expert, n_used, xs, w_gate[l], w_up[l], w_down[l])
        h = _combine(h, ys, dest, gate.T, final_gain[None, :], tm, l == depth - 1)
    return h.reshape(B, T, D)
```

```python
import functools

import numpy as np
import jax
import jax.numpy as jnp
from jax import lax
from jax.experimental import pallas as pl
from jax.experimental.pallas import tpu as pltpu

F32 = jnp.float32
BF16 = jnp.bfloat16

D_MODEL = 1024
EPS = 1e-6
GRID_W = 64

GLA_HEADS = 4
GLA_DK = 64
GLA_DV = 128
GLA_KEY_WIDTH = GLA_HEADS * GLA_DK
GLA_WIDTH = GLA_HEADS * GLA_DV
GLA_GATE_RANK = 16
GLA_GATE_NORM = 16.0
GLA_CHUNK = 64

ATT_HEADS = 8
ATT_KV_HEADS = 2
ATT_HEAD_DIM = 64
ATT_GROUP = ATT_HEADS // ATT_KV_HEADS
ATT_WIDTH = ATT_HEADS * ATT_HEAD_DIM
ATT_KV_WIDTH = ATT_KV_HEADS * ATT_HEAD_DIM
ROPE_THETA = 10000.0
LOG2_E = 1.4426950408889634

N_GROUPS = 8
EXPERTS_PER_GROUP = 8
N_EXPERTS = N_GROUPS * EXPERTS_PER_GROUP
TOP_K = 2
D_EXPERT = 512
MOE_BLOCK = 256

LANES = 128
Z_PAD = LANES

_OFF_GQ = 0
_OFF_GK = _OFF_GQ + GLA_KEY_WIDTH
_OFF_GV = _OFF_GK + GLA_KEY_WIDTH
_OFF_GG = _OFF_GV + GLA_WIDTH
_OFF_AQ = _OFF_GG + GLA_WIDTH
_OFF_AK = _OFF_AQ + ATT_WIDTH
_OFF_AV = _OFF_AK + ATT_KV_WIDTH
_OFF_Z = _OFF_AV + ATT_KV_WIDTH
D_IN_PAD = _OFF_Z + Z_PAD

VMEM_LIMIT = 56 * 1024 * 1024


def _cparams(semantics):
    return pltpu.CompilerParams(dimension_semantics=semantics, vmem_limit_bytes=VMEM_LIMIT)


def _rms(x, gain):
    return x * lax.rsqrt(jnp.mean(x * x, axis=-1, keepdims=True) + EPS) * gain


def _sigmoid(x):
    return 1.0 / (1.0 + jnp.exp(-x))


def _head_norm_rope(x, head_mean, gain, cos, sin_even, sin_odd):
    w = x.shape[1]
    ms = jnp.dot((x * x).astype(BF16), head_mean, preferred_element_type=F32)
    xn = x * lax.rsqrt(ms + EPS) * gain
    reps = w // LANES
    tile = lambda t: t if reps == 1 else jnp.concatenate([t] * reps, axis=1)
    nxt = pltpu.roll(xn, w - 1, 1)
    prv = pltpu.roll(xn, 1, 1)
    return xn * tile(cos) + nxt * tile(sin_even) + prv * tile(sin_odd)


def _in_proj_kernel(x_ref, g1_ref, w_ref, up_ref, upb_ref, qg_ref, kg_ref, hm_ref,
                    cos_ref, se_ref, so_ref,
                    gq_ref, gk_ref, gv_ref, gg_ref, laf_ref, lab_ref, aqt_ref, ak_ref, avt_ref):
    u = _rms(x_ref[...], g1_ref[...]).astype(BF16)

    def proj(lo, hi):
        return jnp.dot(u, w_ref[:, lo:hi], preferred_element_type=F32)

    cos, se, so = cos_ref[...], se_ref[...], so_ref[...]
    hm = hm_ref[...]
    r_q = proj(_OFF_AQ, _OFF_AK)
    r_gqk = proj(_OFF_GQ, _OFF_GV)
    q = _head_norm_rope(r_q, hm, qg_ref[...], cos, se, so)
    aqt_ref[0] = (q * (ATT_HEAD_DIM ** -0.5 * LOG2_E)).T.astype(BF16)
    gq_ref[...] = r_gqk[:, :GLA_KEY_WIDTH].astype(BF16)
    gk_ref[...] = r_gqk[:, GLA_KEY_WIDTH:].astype(BF16)

    r_kvz = proj(_OFF_AK, D_IN_PAD)
    r_gv = proj(_OFF_GV, _OFF_GG)
    k = _head_norm_rope(r_kvz[:, :ATT_KV_WIDTH], hm[:ATT_KV_WIDTH, :ATT_KV_WIDTH], kg_ref[...], cos, se, so)
    ak_ref[...] = k.astype(BF16)
    avt_ref[0] = r_kvz[:, ATT_KV_WIDTH:2 * ATT_KV_WIDTH].T.astype(BF16)
    gv_ref[...] = r_gv.astype(BF16)

    z = r_kvz[:, 2 * ATT_KV_WIDTH:].astype(BF16)
    zl = jnp.dot(z, up_ref[...], preferred_element_type=F32) + upb_ref[...]
    r_gg = proj(_OFF_GG, _OFF_AQ)
    la = (jnp.minimum(zl, 0.0) - jnp.log(1.0 + jnp.exp(-jnp.abs(zl)))) * (1.0 / GLA_GATE_NORM)
    laf_ref[...] = la[:, :GLA_KEY_WIDTH]
    lab_ref[...] = la[:, GLA_KEY_WIDTH:]
    gg_ref[...] = r_gg.astype(BF16)


def _rope_tables(T):
    t = np.arange(T)
    row = (t // GRID_W).astype(np.float32)
    col = (t % GRID_W).astype(np.float32)
    axis_dim = ATT_HEAD_DIM // 2
    inv_freq = (ROPE_THETA ** (-np.arange(0, axis_dim, 2, dtype=np.float32) / axis_dim)).astype(np.float32)
    ang = np.concatenate([row[:, None] * inv_freq, col[:, None] * inv_freq], axis=-1)
    ang = np.repeat(ang, 2, axis=1)
    ang = np.tile(ang, (1, LANES // ATT_HEAD_DIM))
    even = (np.arange(LANES) % 2 == 0)[None, :]
    cos = np.cos(ang)
    sin = np.sin(ang)
    return (jnp.asarray(cos, F32), jnp.asarray(np.where(even, -sin, 0.0), F32),
            jnp.asarray(np.where(even, 0.0, sin), F32))


def _in_proj(xt, B, T, tm, g1, w, up, upb, qg, kg, hm, cos, se, so):
    N = xt.shape[0]
    tpb = T // tm
    const = lambda shape: pl.BlockSpec(shape, lambda i: (0,) * len(shape))
    rows = lambda width: pl.BlockSpec((tm, width), lambda i: (i, 0))
    pos = pl.BlockSpec((tm, LANES), lambda i: (i % tpb, 0))
    out_shape = (
        jax.ShapeDtypeStruct((N, GLA_KEY_WIDTH), BF16),
        jax.ShapeDtypeStruct((N, GLA_KEY_WIDTH), BF16),
        jax.ShapeDtypeStruct((N, GLA_WIDTH), BF16),
        jax.ShapeDtypeStruct((N, GLA_WIDTH), BF16),
        jax.ShapeDtypeStruct((N, GLA_KEY_WIDTH), F32),
        jax.ShapeDtypeStruct((N, GLA_KEY_WIDTH), F32),
        jax.ShapeDtypeStruct((B, ATT_WIDTH, T), BF16),
        jax.ShapeDtypeStruct((N, ATT_KV_WIDTH), BF16),
        jax.ShapeDtypeStruct((B, ATT_KV_WIDTH, T), BF16),
    )
    cols = lambda width: pl.BlockSpec((1, width, tm), lambda i: (i // tpb, 0, i % tpb))
    out_specs = (
        rows(GLA_KEY_WIDTH), rows(GLA_KEY_WIDTH), rows(GLA_WIDTH), rows(GLA_WIDTH),
        rows(GLA_KEY_WIDTH), rows(GLA_KEY_WIDTH),
        cols(ATT_WIDTH), rows(ATT_KV_WIDTH), cols(ATT_KV_WIDTH),
    )
    return pl.pallas_call(
        _in_proj_kernel,
        out_shape=out_shape,
        grid=(N // tm,),
        in_specs=[rows(D_MODEL), const((1, D_MODEL)), const((D_MODEL, D_IN_PAD)),
                  const((Z_PAD, 2 * GLA_KEY_WIDTH)), const((1, 2 * GLA_KEY_WIDTH)),
                  const((1, ATT_WIDTH)), const((1, ATT_KV_WIDTH)), const((ATT_WIDTH, ATT_WIDTH)),
                  pos, pos, pos],
        out_specs=out_specs,
        compiler_params=_cparams(("arbitrary",)),
        name="in_proj",
    )(xt, g1, w, up, upb, qg, kg, hm, cos, se, so)


GLA_GROUP = 4
GLA_FINISH_GROUP = 8


def _gla_kernel(q_ref, k_ref, v_ref, g_ref, laf_ref, lab_ref, gain_ref, o_ref,
                part_scr, qif_scr, qib_scr, kvf_scr, kvb_scr, decf_scr, decb_scr, lf_scr, lb_scr):
    C, G = GLA_CHUNK, GLA_GROUP
    R = C * G
    T = q_ref.shape[0]
    n_chunks = T // C
    gain = gain_ref[...]
    row = lax.broadcasted_iota(jnp.int32, (C, C), 0)
    col = lax.broadcasted_iota(jnp.int32, (C, C), 1)
    tril = row >= col
    triu = row <= col
    rr = lax.broadcasted_iota(jnp.int32, (R, R), 0)
    cc = lax.broadcasted_iota(jnp.int32, (R, R), 1)
    same_chunk = (rr // C) == (cc // C)
    lf_scr[...] = jnp.where(same_chunk & (rr >= cc), 1.0, 0.0).astype(BF16)
    lb_scr[...] = jnp.where(same_chunk & (rr <= cc), 1.0, 0.0).astype(BF16)

    directions = ((laf_ref, lf_scr, tril, C // 2 - 1, C - 1, qif_scr, kvf_scr, decf_scr),
                  (lab_ref, lb_scr, triu, C // 2, 0, qib_scr, kvb_scr, decb_scr))
    heads = tuple((slice(h * GLA_DK, (h + 1) * GLA_DK), slice(h * GLA_DV, (h + 1) * GLA_DV)) for h in range(2))
    nt = (((1,), (1,)), ((), ()))
    tn = (((0,), (0,)), ((), ()))

    def intra(sb, carry):
        r0 = pl.multiple_of(sb * R, R)
        q2 = q_ref[pl.ds(r0, R), :].astype(F32) * (GLA_DK ** -0.5)
        k2 = k_ref[pl.ds(r0, R), :].astype(F32)
        v2 = v_ref[pl.ds(r0, R), :]
        cums = []
        for la_ref, tri_scr, *_ in directions:
            la = la_ref[pl.ds(r0, R), :]
            la_hi = la.astype(BF16)
            la_lo = (la - la_hi.astype(F32)).astype(BF16)
            tri = tri_scr[...]
            cums.append(jnp.dot(tri, la_hi, preferred_element_type=F32)
                        + jnp.dot(tri, la_lo, preferred_element_type=F32))
        scaled = []
        for b, (_, _, mask, i_ref, i_last, qi_scr, kv_scr, dec_scr) in zip(cums, directions):
            for c in range(G):
                rows = slice(c * C, (c + 1) * C)
                bc, qc, kc = b[rows], q2[rows], k2[rows]
                b_ref = bc[i_ref:i_ref + 1, :]
                b_last = bc[i_last:i_last + 1, :]
                qf = (qc * jnp.exp(bc - b_ref)).astype(BF16)
                kf = (kc * jnp.exp(b_ref - bc)).astype(BF16)
                qi_scr[pl.ds(r0 + c * C, C), :] = (qc * jnp.exp(bc)).astype(BF16)
                kl = (kc * jnp.exp(b_last - bc)).astype(BF16)
                decay = jnp.exp(b_last)
                for h, (ks, _) in enumerate(heads):
                    dec_scr[sb * G + c, h] = decay[:, ks]
                scaled.append((c, mask, kv_scr, qf, kf, kl))
        raw = [[lax.dot_general(qf[:, ks], kf[:, ks], nt, preferred_element_type=F32) for ks, _ in heads]
               for _, _, _, qf, kf, _ in scaled]
        o_intra = [[None, None] for _ in range(G)]
        for (c, mask, kv_scr, _, _, kl), sc2 in zip(scaled, raw):
            rows = slice(c * C, (c + 1) * C)
            for h, (ks, vs) in enumerate(heads):
                vh = v2[rows, vs]
                sc = jnp.where(mask, sc2[h], 0.0).astype(BF16)
                oi = jnp.dot(sc, vh, preferred_element_type=F32)
                o_intra[c][h] = oi if o_intra[c][h] is None else o_intra[c][h] + oi
                kv_scr[sb * G + c, h] = lax.dot_general(vh, kl[:, ks], tn, preferred_element_type=F32)
        for c in range(G):
            part_scr[pl.ds(r0 + c * C, C), :] = jnp.concatenate(o_intra[c], axis=1)
        return carry

    lax.fori_loop(0, n_chunks // G, intra, 0)

    def scan(n, carry):
        sf, sb = carry
        nb = n_chunks - 1 - n
        new_f, new_b = [], []
        for h in range(2):
            kv = kvf_scr[n, h]
            kvf_scr[n, h] = sf[h]
            new_f.append(decf_scr[n, h] * sf[h] + kv)
            kv = kvb_scr[nb, h]
            kvb_scr[nb, h] = sb[h]
            new_b.append(decb_scr[nb, h] * sb[h] + kv)
        return tuple(new_f), tuple(new_b)

    zero_state = tuple(jnp.zeros((GLA_DV, GLA_DK), F32) for _ in range(2))
    lax.fori_loop(0, n_chunks, scan, (zero_state, zero_state), unroll=4)

    GF = GLA_FINISH_GROUP
    RF = C * GF

    def finish(sb, carry):
        r0 = pl.multiple_of(sb * RF, RF)
        g = g_ref[pl.ds(r0, RF), :].astype(F32)
        part = part_scr[pl.ds(r0, RF), :]
        qf = qif_scr[pl.ds(r0, RF), :]
        qb = qib_scr[pl.ds(r0, RF), :]
        inter = []
        for c in range(GF):
            rows = slice(c * C, (c + 1) * C)
            inter.append([
                lax.dot_general(qf[rows, ks], kvf_scr[sb * GF + c, h].astype(BF16), nt, preferred_element_type=F32)
                + lax.dot_general(qb[rows, ks], kvb_scr[sb * GF + c, h].astype(BF16), nt, preferred_element_type=F32)
                for h, (ks, _) in enumerate(heads)])
        gate = g * _sigmoid(g)
        for c in range(GF):
            rows = slice(c * C, (c + 1) * C)
            normed = jnp.concatenate([_rms(part[rows, vs] + inter[c][h], gain) for h, (_, vs) in enumerate(heads)],
                                     axis=1)
            o_ref[pl.ds(r0 + c * C, C), :] = (normed * gate[rows]).astype(BF16)
        return carry

    lax.fori_loop(0, n_chunks // GF, finish, 0)


def _gla(gq, gk, gv, gg, laf, lab, gain, B, T):
    N = gq.shape[0]
    pairs = GLA_HEADS // 2
    n_chunks = T // GLA_CHUNK
    assert n_chunks % GLA_GROUP == 0 and n_chunks % GLA_FINISH_GROUP == 0
    group_rows = GLA_CHUNK * GLA_GROUP
    kspec = pl.BlockSpec((T, 2 * GLA_DK), lambda b, p: (b, p))
    vspec = pl.BlockSpec((T, 2 * GLA_DV), lambda b, p: (b, p))
    state = pltpu.VMEM((n_chunks, 2, GLA_DV, GLA_DK), F32)
    decay = pltpu.VMEM((n_chunks, 2, 1, GLA_DK), F32)
    return pl.pallas_call(
        _gla_kernel,
        out_shape=jax.ShapeDtypeStruct((N, GLA_WIDTH), BF16),
        grid=(B, pairs),
        in_specs=[kspec, kspec, vspec, vspec, kspec, kspec,
                  pl.BlockSpec((1, GLA_DV), lambda b, p: (0, 0))],
        out_specs=vspec,
        scratch_shapes=[pltpu.VMEM((T, 2 * GLA_DV), F32),
                        pltpu.VMEM((T, 2 * GLA_DK), BF16), pltpu.VMEM((T, 2 * GLA_DK), BF16),
                        state, state, decay, decay,
                        pltpu.VMEM((group_rows, group_rows), BF16), pltpu.VMEM((group_rows, group_rows), BF16)],
        compiler_params=_cparams(("arbitrary", "arbitrary")),
        name="gla",
    )(gq, gk, gv, gg, laf, lab, gain)


def _attn_kernel(qt_ref, k_ref, vt_ref, ot_ref, vext_scr, s0_scr, s1_scr, p0_scr, p1_scr, m0_scr, m1_scr,
                 *, tq, key_block, steps_per_iter):
    j = pl.program_id(1)
    T = k_ref.shape[0]
    nq = T // tq
    dh = ATT_HEAD_DIM
    cols = ATT_GROUP * tq
    first = j == 0

    row = lax.broadcasted_iota(jnp.int32, (ATT_KV_WIDTH, T), 0)
    own = (row >= j * dh) & (row < (j + 1) * dh)
    vext_scr[...] = jnp.where(own, vt_ref[0], jnp.ones((), BF16))

    s_scr, p_scr, m_scr = (s0_scr, s1_scr), (p0_scr, p1_scr), (m0_scr, m1_scr)

    def step(blk, sa, do_scores=True, do_probs=True, do_out=True):
        sb = 1 - sa
        if do_scores:
            c0 = pl.multiple_of(blk * tq, tq)
            q4 = jnp.concatenate([qt_ref[0, h * dh:(h + 1) * dh, pl.ds(c0, tq)] for h in range(ATT_GROUP)],
                                 axis=1)
            zeros = jnp.zeros_like(q4)
            qe = jnp.where(first, jnp.concatenate([q4, zeros], axis=0), jnp.concatenate([zeros, q4], axis=0))
            m = jnp.full((8, cols), -jnp.inf, F32)
        if do_probs:
            mx = m_scr[sb][...]
        if do_out:
            acc = jnp.zeros((ATT_KV_WIDTH, cols), F32)
        for kb in range(T // key_block):
            ks = slice(kb * key_block, (kb + 1) * key_block)
            if do_scores:
                s = jnp.dot(k_ref[ks, :], qe, preferred_element_type=F32)
                s_scr[sa][ks, :] = s
                m = jnp.maximum(m, jnp.max(s.reshape(key_block // 8, 8, cols), axis=0))
            if do_probs:
                p_scr[sb][ks, :] = jnp.exp2(s_scr[sb][ks, :] - mx).astype(BF16)
            if do_out:
                acc = acc + jnp.dot(vext_scr[:, ks], p_scr[sa][ks, :], preferred_element_type=F32)
        if do_scores:
            m_scr[sa][...] = jnp.max(m, axis=0, keepdims=True)
        if do_out:
            num = jnp.where(first, acc[:dh], acc[dh:])
            den = jnp.where(first, acc[dh:dh + 1], acc[0:1])
            o = (num / den).astype(BF16)
            c0 = pl.multiple_of((blk - 2) * tq, tq)
            for h in range(ATT_GROUP):
                ot_ref[0, h * dh:(h + 1) * dh, pl.ds(c0, tq)] = o[:, h * tq:(h + 1) * tq]

    step(0, 0, do_probs=False, do_out=False)
    step(1, 1, do_out=False)

    def steady(it, carry):
        for u in range(steps_per_iter):
            pl.when(it >= 0)(functools.partial(step, 2 + steps_per_iter * it + u, u % 2))
        return carry

    lax.fori_loop(0, (nq - 2) // steps_per_iter, steady, 0)
    step(nq, 0, do_scores=False)
    step(nq + 1, 1, do_scores=False, do_probs=False)


def _attn(aqt, ak, avt, B, T, tq, key_block):
    assert T % (2 * tq) == 0 and tq % LANES == 0
    gw = ATT_GROUP * ATT_HEAD_DIM
    cols = ATT_GROUP * tq
    steps_per_iter = 2
    return pl.pallas_call(
        functools.partial(_attn_kernel, tq=tq, key_block=key_block, steps_per_iter=steps_per_iter),
        out_shape=jax.ShapeDtypeStruct((B, ATT_WIDTH, T), BF16),
        grid=(B, ATT_KV_HEADS),
        in_specs=[pl.BlockSpec((1, gw, T), lambda b, j: (b, j, 0)),
                  pl.BlockSpec((T, ATT_KV_WIDTH), lambda b, j: (b, 0)),
                  pl.BlockSpec((1, ATT_KV_WIDTH, T), lambda b, j: (b, 0, 0))],
        out_specs=pl.BlockSpec((1, gw, T), lambda b, j: (b, j, 0)),
        scratch_shapes=[pltpu.VMEM((ATT_KV_WIDTH, T), BF16),
                        pltpu.VMEM((T, cols), F32), pltpu.VMEM((T, cols), F32),
                        pltpu.VMEM((T, cols), BF16), pltpu.VMEM((T, cols), BF16),
                        pltpu.VMEM((1, cols), F32), pltpu.VMEM((1, cols), F32)],
        compiler_params=_cparams(("arbitrary", "arbitrary")),
        name="attn",
    )(aqt, ak, avt)


def _out_proj_kernel(x_ref, og_ref, oa_ref, wo_ref, ag_ref, g2_ref, wr_ref, br_ref, tri_ref,
                     h_ref, xn_ref, eid_ref, gate_ref, rank_ref, cnt_ref, run_scr):
    tm = x_ref.shape[0]

    @pl.when(pl.program_id(0) == 0)
    def _():
        run_scr[...] = jnp.zeros_like(run_scr)

    def branch_norm(ts):
        oat = oa_ref[0, :, ts].astype(F32)
        return (oat * lax.rsqrt(jnp.mean(oat * oat, axis=0, keepdims=True) + EPS) * ag_ref[...]).astype(BF16)

    def mix(ts, oan):
        return (jnp.dot(og_ref[ts, :], wo_ref[:GLA_WIDTH, :], preferred_element_type=F32)
                + lax.dot_general(oan, wo_ref[GLA_WIDTH:, :], (((0,), (0,)), ((), ())),
                                  preferred_element_type=F32))

    def residual_norm(ts, y):
        h = x_ref[ts, :] + y
        h_ref[ts, :] = h
        xn = _rms(h, g2_ref[...])
        xn_ref[ts, :] = xn
        return xn.astype(BF16)

    def logits(xnb):
        return lax.dot_general(wr_ref[...], xnb, (((1,), (1,)), ((), ())), preferred_element_type=F32) + br_ref[...]

    whole = slice(0, tm)
    lt = logits(residual_norm(whole, mix(whole, branch_norm(whole))))
    iota8 = lax.broadcasted_iota(jnp.int32, (N_GROUPS, tm), 0)

    def first_argmax(v):
        top = jnp.max(v, axis=0, keepdims=True)
        idx = jnp.min(jnp.where(v == top, iota8, N_GROUPS), axis=0, keepdims=True)
        return top, idx

    gl = lt[0:N_GROUPS]
    gmax, gidx = first_argmax(gl)
    gw = 1.0 / jnp.sum(jnp.exp(gl - gmax), axis=0, keepdims=True)
    esel = jnp.zeros((EXPERTS_PER_GROUP, tm), F32)
    for g in range(N_GROUPS):
        lo = N_GROUPS + g * EXPERTS_PER_GROUP
        esel = jnp.where(gidx == g, lt[lo:lo + EXPERTS_PER_GROUP], esel)
    v1, i1 = first_argmax(esel)
    rest = jnp.where(iota8 == i1, -jnp.inf, esel)
    v2, i2 = first_argmax(rest)
    t = jnp.exp(v2 - v1)
    den = 1.0 + t
    e1 = gidx * EXPERTS_PER_GROUP + i1
    e2 = gidx * EXPERTS_PER_GROUP + i2
    eid_ref[0:1, :] = e1
    eid_ref[1:2, :] = e2
    gate_ref[0:1, :] = gw * (1.0 / den)
    gate_ref[1:2, :] = gw * (t / den)

    iota_e = lax.broadcasted_iota(jnp.int32, (N_EXPERTS, tm), 0)
    oh1 = iota_e == e1
    oh2 = iota_e == e2
    both = jnp.where(oh1, 1.0, jnp.where(oh2, 1.0, 0.0)).astype(BF16)
    prefix = jnp.dot(both, tri_ref[...], preferred_element_type=F32)
    run = run_scr[...]
    base = prefix - 1.0 + run[:, 0:1]
    rank_ref[0:1, :] = jnp.sum(jnp.where(oh1, base, 0.0), axis=0, keepdims=True).astype(jnp.int32)
    rank_ref[1:2, :] = jnp.sum(jnp.where(oh2, base, 0.0), axis=0, keepdims=True).astype(jnp.int32)
    run = run + prefix[:, tm - 1:tm]
    run_scr[...] = run
    cnt_ref[...] = run


def _out_proj(xt, o_gla, o_att_t, wo, ag, g2, wr, br, tm):
    N = xt.shape[0]
    tpb = o_att_t.shape[2] // tm
    const = lambda shape: pl.BlockSpec(shape, lambda i: (0,) * len(shape))
    rows = lambda width: pl.BlockSpec((tm, width), lambda i: (i, 0))
    cols = pl.BlockSpec((TOP_K, tm), lambda i: (0, i))
    tri = jnp.asarray(np.triu(np.ones((tm, tm), np.float32)), BF16)
    return pl.pallas_call(
        _out_proj_kernel,
        out_shape=(jax.ShapeDtypeStruct((N, D_MODEL), F32),
                   jax.ShapeDtypeStruct((N, D_MODEL), F32),
                   jax.ShapeDtypeStruct((TOP_K, N), jnp.int32),
                   jax.ShapeDtypeStruct((TOP_K, N), F32),
                   jax.ShapeDtypeStruct((TOP_K, N), jnp.int32),
                   jax.ShapeDtypeStruct((N_EXPERTS, LANES), F32)),
        grid=(N // tm,),
        in_specs=[rows(D_MODEL), rows(GLA_WIDTH),
                  pl.BlockSpec((1, ATT_WIDTH, tm), lambda i: (i // tpb, 0, i % tpb)),
                  const((D_MODEL, D_MODEL)),
                  const((ATT_WIDTH, 1)), const((1, D_MODEL)), const((LANES, D_MODEL)),
                  const((LANES, 1)), const((tm, tm))],
        out_specs=(rows(D_MODEL), rows(D_MODEL), cols, cols, cols, const((N_EXPERTS, LANES))),
        scratch_shapes=[pltpu.VMEM((N_EXPERTS, LANES), F32)],
        compiler_params=_cparams(("arbitrary",)),
        name="out_proj",
    )(xt, o_gla, o_att_t, wo, ag, g2, wr, br, tri)


D_PACKED = D_MODEL // 2


def _pack_pairs(x):
    w = x.shape[1] // 2
    bits = pltpu.bitcast(x.astype(BF16).astype(F32), jnp.uint32)
    return bits[:, :w] | (bits[:, w:] >> 16)


def _unpack_pairs(words):
    hi = pltpu.bitcast(words & jnp.uint32(0xFFFF0000), F32)
    lo = pltpu.bitcast(words << 16, F32)
    return jnp.concatenate([hi, lo], axis=1)


def _dispatch_kernel(zs_ref, nu_ref, *refs):
    dest_refs = refs[:TOP_K]
    xn_ref, xs_hbm, packed, zero_buf, zero_sem, row_sem = refs[TOP_K:]
    tn = xn_ref.shape[0]
    i = pl.program_id(0)
    packed[...] = _pack_pairs(xn_ref[...])

    def fill(row0):
        return pltpu.make_async_copy(
            zero_buf, xs_hbm.at[pl.ds(pl.multiple_of(row0, MOE_BLOCK), MOE_BLOCK)], zero_sem)

    @pl.when(i == 0)
    def _():
        zero_buf[...] = jnp.zeros_like(zero_buf)
        fills = [fill(zs_ref[e]) for e in range(N_EXPERTS)]
        for f in fills:
            f.start()
        for f in fills:
            f.wait()

        def fill_tail(blk, carry):
            f = fill(blk * MOE_BLOCK)
            f.start()
            f.wait()
            return carry

        lax.fori_loop(nu_ref[0], xs_hbm.shape[0] // MOE_BLOCK, fill_tail, 0)

    def issue(g, carry):
        tile = packed.at[pl.ds(pl.multiple_of(g * 8, 8), 8)]
        for s in range(8):
            for k in range(TOP_K):
                pltpu.make_async_copy(tile.at[pl.ds(s, 1)], xs_hbm.at[pl.ds(dest_refs[k][g * 8 + s], 1)],
                                      row_sem).start(priority=k % 2)
        return carry

    lax.fori_loop(0, tn // 8, issue, 0)
    for k in range(TOP_K):
        pltpu.make_async_copy(packed, xs_hbm.at[pl.ds(0, tn)], row_sem).wait()


def _dest_specs(n_steps, tn, index):
    return [pl.BlockSpec((tn,), lambda i, *_, k=k: (k * n_steps + index(i),), memory_space=pltpu.SMEM)
            for k in range(TOP_K)]


def _dispatch(zero_starts, n_used, dest_flat, xn, P, tn):
    N = xn.shape[0]
    grid_spec = pltpu.PrefetchScalarGridSpec(
        num_scalar_prefetch=2,
        grid=(N // tn,),
        in_specs=_dest_specs(N // tn, tn, lambda i: i) + [pl.BlockSpec((tn, D_MODEL), lambda i, zs, nu: (i, 0))],
        out_specs=pl.BlockSpec(memory_space=pl.ANY),
        scratch_shapes=[pltpu.VMEM((tn, D_PACKED), jnp.uint32),
                        pltpu.VMEM((MOE_BLOCK, D_PACKED), jnp.uint32),
                        pltpu.SemaphoreType.DMA, pltpu.SemaphoreType.DMA],
    )
    return pl.pallas_call(
        _dispatch_kernel,
        out_shape=jax.ShapeDtypeStruct((P, D_PACKED), jnp.uint32),
        grid_spec=grid_spec,
        compiler_params=_cparams(("arbitrary",)),
        name="dispatch",
    )(zero_starts, n_used, *([dest_flat] * TOP_K), xn)


def _moe_ffn_kernel(be_ref, nu_ref, seg_ref, nxt_ref, xs_ref, wg_hbm, wu_hbm, wd_hbm, ys_ref,
                    wg_f, wu_f, wd_f, wg_s, wu_s, wd_s, sems):
    i = pl.program_id(0)
    used = i < nu_ref[0]

    def weights(expert, slot):
        return [pltpu.make_async_copy(src.at[expert], dst.at[slot], sems.at[slot])
                for src, dst in ((wg_hbm, wg_f), (wu_hbm, wu_f), (wd_hbm, wd_f))]

    @pl.when(used)
    def _():
        new_expert = jnp.logical_or(i == 0, be_ref[i] != be_ref[jnp.maximum(i - 1, 0)])

        @pl.when(new_expert)
        def _():
            slot = seg_ref[i] % 2

            @pl.when(i == 0)
            def _():
                for cp in weights(be_ref[0], 0):
                    cp.start()

            for cp in weights(be_ref[i], slot):
                cp.wait()

            @pl.when(nxt_ref[i] >= 0)
            def _():
                for cp in weights(nxt_ref[i], 1 - slot):
                    cp.start()

            wg_s[...] = wg_f[slot].astype(BF16)
            wu_s[...] = wu_f[slot].astype(BF16)
            wd_s[...] = wd_f[slot].astype(BF16)

        xb = _unpack_pairs(xs_ref[...]).astype(BF16)
        hg = jnp.dot(xb, wg_s[...], preferred_element_type=F32)
        hu = jnp.dot(xb, wu_s[...], preferred_element_type=F32)
        hm = (hg * _sigmoid(hg) * hu).astype(BF16)
        ys_ref[...] = _pack_pairs(jnp.dot(hm, wd_s[...], preferred_element_type=F32))

    @pl.when(jnp.logical_not(used))
    def _():
        ys_ref[...] = jnp.zeros_like(ys_ref)


def _moe_ffn(blk_expert, n_used, xs, w_gate, w_up, w_down):
    P = xs.shape[0]
    nblk = P // MOE_BLOCK
    blk = jnp.arange(nblk, dtype=jnp.int32)
    starts = jnp.logical_and(jnp.concatenate([jnp.ones((1,), bool), blk_expert[1:] != blk_expert[:-1]]),
                             blk < n_used[0])
    seg = jnp.cumsum(starts.astype(jnp.int32)) - 1
    next_start = lax.cummin(jnp.where(starts, blk, nblk), reverse=True)
    next_start = jnp.concatenate([next_start[1:], jnp.full((1,), nblk, jnp.int32)])
    nxt = jnp.where(next_start < nblk, blk_expert[jnp.minimum(next_start, nblk - 1)], -1).astype(jnp.int32)
    row_block = pl.BlockSpec((MOE_BLOCK, D_PACKED), lambda i, *_: (i, 0))
    grid_spec = pltpu.PrefetchScalarGridSpec(
        num_scalar_prefetch=4,
        grid=(nblk,),
        in_specs=[row_block] + [pl.BlockSpec(memory_space=pl.ANY)] * 3,
        out_specs=row_block,
        scratch_shapes=[pltpu.VMEM((2, D_MODEL, D_EXPERT), F32),
                        pltpu.VMEM((2, D_MODEL, D_EXPERT), F32),
                        pltpu.VMEM((2, D_EXPERT, D_MODEL), F32),
                        pltpu.VMEM((D_MODEL, D_EXPERT), BF16),
                        pltpu.VMEM((D_MODEL, D_EXPERT), BF16),
                        pltpu.VMEM((D_EXPERT, D_MODEL), BF16),
                        pltpu.SemaphoreType.DMA((2,))],
    )
    return pl.pallas_call(
        _moe_ffn_kernel,
        out_shape=jax.ShapeDtypeStruct((P, D_PACKED), jnp.uint32),
        grid_spec=grid_spec,
        compiler_params=_cparams(("arbitrary",)),
        name="moe_ffn",
    )(blk_expert, n_used, seg, nxt, xs, w_gate, w_up, w_down)


COMBINE_ROWS = 32

def _combine_kernel(*refs, final):
    dcur_refs, dnext_refs = refs[:TOP_K], refs[TOP_K:2 * TOP_K]
    h_ref, gate_ref, fg_ref, ys_hbm, o_ref, ybuf, sems = refs[2 * TOP_K:]
    tm = h_ref.shape[0]
    i = pl.program_id(0)
    n = pl.num_programs(0)

    slot = i % 2
    fg = fg_ref[...]

    def issue(drefs, to_slot, g):
        for k in range(TOP_K):
            tile = ybuf.at[to_slot, pl.ds(pl.multiple_of(k * tm + g * 8, 8), 8)]
            for s in range(8):
                pltpu.make_async_copy(ys_hbm.at[pl.ds(drefs[k][g * 8 + s], 1)],
                                      tile.at[pl.ds(s, 1)], sems.at[to_slot]).start(priority=k % 2)

    tiles = COMBINE_ROWS // 8

    def load_rows(j):
        r0 = pl.multiple_of(j * COMBINE_ROWS, COMBINE_ROWS)
        packed = [ybuf[slot, pl.ds(pl.multiple_of(k * tm + j * COMBINE_ROWS, COMBINE_ROWS), COMBINE_ROWS), :]
                  for k in range(TOP_K)]
        return r0, gate_ref[pl.ds(r0, COMBINE_ROWS), :], h_ref[pl.ds(r0, COMBINE_ROWS), :], packed

    def finish_rows(loaded):
        r0, gate, hh, packed = loaded
        for k in range(TOP_K):
            hh = hh + _unpack_pairs(packed[k]) * gate[:, k:k + 1]
        o_ref[pl.ds(r0, COMBINE_ROWS), :] = _rms(hh, fg) if final else hh

    @pl.when(i == 0)
    def _():
        lax.fori_loop(0, tm // 8, lambda g, c: (issue(dcur_refs, 0, g), c)[1], 0)

    pltpu.make_async_copy(ys_hbm.at[pl.ds(0, TOP_K * tm)], ybuf.at[slot], sems.at[slot]).wait()

    @pl.when(i + 1 < n)
    def _():
        def body(j, c):
            loaded = load_rows(j)
            for u in range(tiles):
                issue(dnext_refs, 1 - slot, j * tiles + u)
            finish_rows(loaded)
            return c

        lax.fori_loop(0, tm // COMBINE_ROWS, body, 0)

    @pl.when(i + 1 == n)
    def _():
        lax.fori_loop(0, tm // COMBINE_ROWS, lambda j, c: (finish_rows(load_rows(j)), c)[1], 0)


def _combine(h, ys, dest_flat, gate_t, fg, tm, final):
    N = h.shape[0]
    n = N // tm
    rows = pl.BlockSpec((tm, D_MODEL), lambda i: (i, 0))
    return pl.pallas_call(
        functools.partial(_combine_kernel, final=final),
        out_shape=jax.ShapeDtypeStruct((N, D_MODEL), F32),
        grid=(n,),
        in_specs=_dest_specs(n, tm, lambda i: i) + _dest_specs(n, tm, lambda i: jnp.minimum(i + 1, n - 1)) + [
                  rows, pl.BlockSpec((tm, TOP_K), lambda i: (i, 0)),
                  pl.BlockSpec((1, D_MODEL), lambda i: (0, 0)),
                  pl.BlockSpec(memory_space=pl.ANY)],
        out_specs=rows,
        scratch_shapes=[pltpu.VMEM((2, TOP_K * tm, D_PACKED), jnp.uint32), pltpu.SemaphoreType.DMA((2,))],
        compiler_params=_cparams(("arbitrary",)),
        name="combine",
    )(*([dest_flat] * (2 * TOP_K)), h, gate_t, fg, ys)


def _reorder_w_in(w_in):
    n_gla = 2 * GLA_KEY_WIDTH + 2 * GLA_WIDTH
    z0 = n_gla
    a0 = z0 + 2 * GLA_GATE_RANK
    pad = jnp.zeros((D_MODEL, Z_PAD - 2 * GLA_GATE_RANK), w_in.dtype)
    return jnp.concatenate([w_in[:, :n_gla], w_in[:, a0:], w_in[:, z0:a0], pad], axis=1)


def kernel(x, norm1_gain, w_in, gla_up_fwd, gla_up_fwd_bias, gla_up_bwd, gla_up_bwd_bias, gla_out_gain, q_norm_gain, k_norm_gain, att_out_gain, w_out, norm2_gain, w_group, b_group, w_expert, b_expert, w_gate, w_up, w_down, final_gain):
    B, T, D = x.shape
    N = B * T
    depth = norm1_gain.shape[0]
    tm = min(512, T)
    tq = 128
    key_block = min(512, T)
    h = x.reshape(N, D)
    cos, se, so = _rope_tables(T)
    head_mean = jnp.asarray(
        np.kron(np.eye(ATT_HEADS, dtype=np.float32),
                np.full((ATT_HEAD_DIM, ATT_HEAD_DIM), 1.0 / ATT_HEAD_DIM, np.float32)), BF16)
    for l in range(depth):
        w = _reorder_w_in(w_in[l]).astype(BF16)
        r = GLA_GATE_RANK
        up = jnp.zeros((Z_PAD, 2 * GLA_KEY_WIDTH), F32)
        up = up.at[:r, :GLA_KEY_WIDTH].set(gla_up_fwd[l]).at[r:2 * r, GLA_KEY_WIDTH:].set(gla_up_bwd[l])
        upb = jnp.concatenate([gla_up_fwd_bias[l], gla_up_bwd_bias[l]])[None, :]
        qg = jnp.tile(q_norm_gain[l], ATT_HEADS)[None, :]
        kg = jnp.tile(k_norm_gain[l], ATT_KV_HEADS)[None, :]
        gq, gk, gv, gg, laf, lab, aqt, ak, avt = _in_proj(
            h, B, T, tm, norm1_gain[l][None, :], w, up.astype(BF16), upb, qg, kg, head_mean, cos, se, so)
        o_gla = _gla(gq, gk, gv, gg, laf, lab, gla_out_gain[l][None, :], B, T)
        o_att_t = _attn(aqt, ak, avt, B, T, tq, key_block)

        wr = jnp.zeros((LANES, D), F32)
        wr = wr.at[:N_GROUPS].set(w_group[l].T).at[N_GROUPS:N_GROUPS + N_EXPERTS].set(w_expert[l].T)
        br = jnp.zeros((LANES, 1), F32)
        br = br.at[:N_GROUPS, 0].set(b_group[l]).at[N_GROUPS:N_GROUPS + N_EXPERTS, 0].set(b_expert[l])
        h, xn, eid, gate, rank, cnt = _out_proj(
            h, o_gla, o_att_t, w_out[l].astype(BF16), att_out_gain[l][:, None], norm2_gain[l][None, :],
            wr.astype(BF16), br, tm)

        counts = cnt[:, 0].astype(jnp.int32)
        padded = (counts + MOE_BLOCK - 1) // MOE_BLOCK * MOE_BLOCK
        pad_ends = jnp.cumsum(padded)
        pad_starts = pad_ends - padded
        seg_start = jnp.sum(jnp.where(eid[:, :, None] == jnp.arange(N_EXPERTS, dtype=jnp.int32),
                                      pad_starts.astype(jnp.int32), 0), axis=-1)
        dest = (seg_start + rank).reshape(-1)
        P = N * TOP_K + N_EXPERTS * MOE_BLOCK
        nblk = P // MOE_BLOCK
        blk_start = jnp.arange(nblk, dtype=jnp.int32) * MOE_BLOCK
        blk_expert = jnp.minimum(
            jnp.sum((pad_ends[None, :] <= blk_start[:, None]).astype(jnp.int32), axis=1), N_EXPERTS - 1)
        n_used = (pad_ends[-1:] // MOE_BLOCK).astype(jnp.int32)

        zero_starts = jnp.maximum(pad_ends - MOE_BLOCK, 0).astype(jnp.int32)
        xs = _dispatch(zero_starts, n_used, dest, xn, P, tm)
        ys = _moe_ffn(blk_expert, n_used, xs, w_gate[l], w_up[l], w_down[l])
        h = _combine(h, ys, dest, gate.T, final_gain[None, :], tm, l == depth - 1)
    return h.reshape(B, T, D)
```

```python
import functools

import numpy as np
import jax
import jax.numpy as jnp
from jax import lax
from jax.experimental import pallas as pl
from jax.experimental.pallas import tpu as pltpu

F32 = jnp.float32
BF16 = jnp.bfloat16

D_MODEL = 1024
EPS = 1e-6
GRID_W = 64

GLA_HEADS = 4
GLA_DK = 64
GLA_DV = 128
GLA_KEY_WIDTH = GLA_HEADS * GLA_DK
GLA_WIDTH = GLA_HEADS * GLA_DV
GLA_GATE_RANK = 16
GLA_GATE_NORM = 16.0
GLA_CHUNK = 64

ATT_HEADS = 8
ATT_KV_HEADS = 2
ATT_HEAD_DIM = 64
ATT_GROUP = ATT_HEADS // ATT_KV_HEADS
ATT_WIDTH = ATT_HEADS * ATT_HEAD_DIM
ATT_KV_WIDTH = ATT_KV_HEADS * ATT_HEAD_DIM
ROPE_THETA = 10000.0
LOG2_E = 1.4426950408889634

N_GROUPS = 8
EXPERTS_PER_GROUP = 8
N_EXPERTS = N_GROUPS * EXPERTS_PER_GROUP
TOP_K = 2
D_EXPERT = 512
MOE_BLOCK = 256

LANES = 128
Z_PAD = LANES

_OFF_GQ = 0
_OFF_GK = _OFF_GQ + GLA_KEY_WIDTH
_OFF_GV = _OFF_GK + GLA_KEY_WIDTH
_OFF_GG = _OFF_GV + GLA_WIDTH
_OFF_AQ = _OFF_GG + GLA_WIDTH
_OFF_AK = _OFF_AQ + ATT_WIDTH
_OFF_AV = _OFF_AK + ATT_KV_WIDTH
_OFF_Z = _OFF_AV + ATT_KV_WIDTH
D_IN_PAD = _OFF_Z + Z_PAD

VMEM_LIMIT = 56 * 1024 * 1024


def _cparams(semantics):
    return pltpu.CompilerParams(dimension_semantics=semantics, vmem_limit_bytes=VMEM_LIMIT)


def _rms(x, gain):
    return x * lax.rsqrt(jnp.mean(x * x, axis=-1, keepdims=True) + EPS) * gain


def _sigmoid(x):
    return 1.0 / (1.0 + jnp.exp(-x))


def _head_norm_rope(x, head_mean, gain, cos, sin_even, sin_odd):
    w = x.shape[1]
    ms = jnp.dot((x * x).astype(BF16), head_mean, preferred_element_type=F32)
    xn = x * lax.rsqrt(ms + EPS) * gain
    reps = w // LANES
    tile = lambda t: t if reps == 1 else jnp.concatenate([t] * reps, axis=1)
    nxt = pltpu.roll(xn, w - 1, 1)
    prv = pltpu.roll(xn, 1, 1)
    return xn * tile(cos) + nxt * tile(sin_even) + prv * tile(sin_odd)


def _in_proj_kernel(x_ref, g1_ref, w_ref, up_ref, upb_ref, qg_ref, kg_ref, hm_ref,
                    cos_ref, se_ref, so_ref,
                    gq_ref, gk_ref, gv_ref, gg_ref, laf_ref, lab_ref, aqt_ref, ak_ref, avt_ref):
    u = _rms(x_ref[...], g1_ref[...]).astype(BF16)

    def proj(lo, hi):
        return jnp.dot(u, w_ref[:, lo:hi], preferred_element_type=F32)

    cos, se, so = cos_ref[...], se_ref[...], so_ref[...]
    hm = hm_ref[...]
    r_q = proj(_OFF_AQ, _OFF_AK)
    r_gqk = proj(_OFF_GQ, _OFF_GV)
    q = _head_norm_rope(r_q, hm, qg_ref[...], cos, se, so)
    aqt_ref[0] = (q * (ATT_HEAD_DIM ** -0.5 * LOG2_E)).T.astype(BF16)
    gq_ref[...] = r_gqk[:, :GLA_KEY_WIDTH].astype(BF16)
    gk_ref[...] = r_gqk[:, GLA_KEY_WIDTH:].astype(BF16)

    r_kvz = proj(_OFF_AK, D_IN_PAD)
    r_gv = proj(_OFF_GV, _OFF_GG)
    k = _head_norm_rope(r_kvz[:, :ATT_KV_WIDTH], hm[:ATT_KV_WIDTH, :ATT_KV_WIDTH], kg_ref[...], cos, se, so)
    ak_ref[...] = k.astype(BF16)
    avt_ref[0] = r_kvz[:, ATT_KV_WIDTH:2 * ATT_KV_WIDTH].T.astype(BF16)
    gv_ref[...] = r_gv.astype(BF16)

    z = r_kvz[:, 2 * ATT_KV_WIDTH:].astype(BF16)
    zl = jnp.dot(z, up_ref[...], preferred_element_type=F32) + upb_ref[...]
    r_gg = proj(_OFF_GG, _OFF_AQ)
    la = (jnp.minimum(zl, 0.0) - jnp.log(1.0 + jnp.exp(-jnp.abs(zl)))) * (1.0 / GLA_GATE_NORM)
    laf_ref[...] = la[:, :GLA_KEY_WIDTH]
    lab_ref[...] = la[:, GLA_KEY_WIDTH:]
    gg_ref[...] = r_gg.astype(BF16)


def _rope_tables(T):
    t = np.arange(T)
    row = (t // GRID_W).astype(np.float32)
    col = (t % GRID_W).astype(np.float32)
    axis_dim = ATT_HEAD_DIM // 2
    inv_freq = (ROPE_THETA ** (-np.arange(0, axis_dim, 2, dtype=np.float32) / axis_dim)).astype(np.float32)
    ang = np.concatenate([row[:, None] * inv_freq, col[:, None] * inv_freq], axis=-1)
    ang = np.repeat(ang, 2, axis=1)
    ang = np.tile(ang, (1, LANES // ATT_HEAD_DIM))
    even = (np.arange(LANES) % 2 == 0)[None, :]
    cos = np.cos(ang)
    sin = np.sin(ang)
    return (jnp.asarray(cos, F32), jnp.asarray(np.where(even, -sin, 0.0), F32),
            jnp.asarray(np.where(even, 0.0, sin), F32))


def _in_proj(xt, B, T, tm, g1, w, up, upb, qg, kg, hm, cos, se, so):
    N = xt.shape[0]
    tpb = T // tm
    const = lambda shape: pl.BlockSpec(shape, lambda i: (0,) * len(shape))
    rows = lambda width: pl.BlockSpec((tm, width), lambda i: (i, 0))
    pos = pl.BlockSpec((tm, LANES), lambda i: (i % tpb, 0))
    out_shape = (
        jax.ShapeDtypeStruct((N, GLA_KEY_WIDTH), BF16),
        jax.ShapeDtypeStruct((N, GLA_KEY_WIDTH), BF16),
        jax.ShapeDtypeStruct((N, GLA_WIDTH), BF16),
        jax.ShapeDtypeStruct((N, GLA_WIDTH), BF16),
        jax.ShapeDtypeStruct((N, GLA_KEY_WIDTH), F32),
        jax.ShapeDtypeStruct((N, GLA_KEY_WIDTH), F32),
        jax.ShapeDtypeStruct((B, ATT_WIDTH, T), BF16),
        jax.ShapeDtypeStruct((N, ATT_KV_WIDTH), BF16),
        jax.ShapeDtypeStruct((B, ATT_KV_WIDTH, T), BF16),
    )
    cols = lambda width: pl.BlockSpec((1, width, tm), lambda i: (i // tpb, 0, i % tpb))
    out_specs = (
        rows(GLA_KEY_WIDTH), rows(GLA_KEY_WIDTH), rows(GLA_WIDTH), rows(GLA_WIDTH),
        rows(GLA_KEY_WIDTH), rows(GLA_KEY_WIDTH),
        cols(ATT_WIDTH), rows(ATT_KV_WIDTH), cols(ATT_KV_WIDTH),
    )
    return pl.pallas_call(
        _in_proj_kernel,
        out_shape=out_shape,
        grid=(N // tm,),
        in_specs=[rows(D_MODEL), const((1, D_MODEL)), const((D_MODEL, D_IN_PAD)),
                  const((Z_PAD, 2 * GLA_KEY_WIDTH)), const((1, 2 * GLA_KEY_WIDTH)),
                  const((1, ATT_WIDTH)), const((1, ATT_KV_WIDTH)), const((ATT_WIDTH, ATT_WIDTH)),
                  pos, pos, pos],
        out_specs=out_specs,
        compiler_params=_cparams(("arbitrary",)),
        name="in_proj",
    )(xt, g1, w, up, upb, qg, kg, hm, cos, se, so)


GLA_GROUP = 4
GLA_FINISH_GROUP = 16


def _gla_kernel(q_ref, k_ref, v_ref, g_ref, laf_ref, lab_ref, gain_ref, o_ref,
                part_scr, qif_scr, qib_scr, kvf_scr, kvb_scr, decf_scr, decb_scr, lf_scr, lb_scr):
    C, G = GLA_CHUNK, GLA_GROUP
    R = C * G
    T = q_ref.shape[0]
    n_chunks = T // C
    gain = gain_ref[...]
    row = lax.broadcasted_iota(jnp.int32, (C, C), 0)
    col = lax.broadcasted_iota(jnp.int32, (C, C), 1)
    tril = row >= col
    triu = row <= col
    rr = lax.broadcasted_iota(jnp.int32, (R, R), 0)
    cc = lax.broadcasted_iota(jnp.int32, (R, R), 1)
    same_chunk = (rr // C) == (cc // C)
    lf_scr[...] = jnp.where(same_chunk & (rr >= cc), 1.0, 0.0).astype(BF16)
    lb_scr[...] = jnp.where(same_chunk & (rr <= cc), 1.0, 0.0).astype(BF16)

    directions = ((laf_ref, lf_scr, tril, C // 2 - 1, C - 1, qif_scr, kvf_scr, decf_scr),
                  (lab_ref, lb_scr, triu, C // 2, 0, qib_scr, kvb_scr, decb_scr))
    heads = tuple((slice(h * GLA_DK, (h + 1) * GLA_DK), slice(h * GLA_DV, (h + 1) * GLA_DV)) for h in range(2))
    nt = (((1,), (1,)), ((), ()))
    tn = (((0,), (0,)), ((), ()))

    def intra(sb, carry):
        r0 = pl.multiple_of(sb * R, R)
        q2 = q_ref[pl.ds(r0, R), :].astype(F32) * (GLA_DK ** -0.5)
        k2 = k_ref[pl.ds(r0, R), :].astype(F32)
        v2 = v_ref[pl.ds(r0, R), :]
        cums = []
        for la_ref, tri_scr, *_ in directions:
            la = la_ref[pl.ds(r0, R), :]
            la_hi = la.astype(BF16)
            la_lo = (la - la_hi.astype(F32)).astype(BF16)
            both = jnp.dot(tri_scr[...], jnp.concatenate([la_hi, la_lo], axis=1), preferred_element_type=F32)
            cums.append(both[:, :2 * GLA_DK] + both[:, 2 * GLA_DK:])
        scaled = []
        for b, (_, _, mask, i_ref, i_last, qi_scr, kv_scr, dec_scr) in zip(cums, directions):
            for c in range(G):
                rows = slice(c * C, (c + 1) * C)
                bc, qc, kc = b[rows], q2[rows], k2[rows]
                b_ref = bc[i_ref:i_ref + 1, :]
                b_last = bc[i_last:i_last + 1, :]
                qf = (qc * jnp.exp(bc - b_ref)).astype(BF16)
                kf = (kc * jnp.exp(b_ref - bc)).astype(BF16)
                qi_scr[pl.ds(r0 + c * C, C), :] = (qc * jnp.exp(bc)).astype(BF16)
                kl = (kc * jnp.exp(b_last - bc)).astype(BF16)
                decay = jnp.exp(b_last)
                for h, (ks, _) in enumerate(heads):
                    dec_scr[sb * G + c, h] = decay[:, ks]
                scaled.append((c, mask, kv_scr, qf, kf, kl))
        raw = [[lax.dot_general(qf[:, ks], kf[:, ks], nt, preferred_element_type=F32) for ks, _ in heads]
               for _, _, _, qf, kf, _ in scaled]
        o_intra = [[None, None] for _ in range(G)]
        for (c, mask, kv_scr, _, _, kl), sc2 in zip(scaled, raw):
            rows = slice(c * C, (c + 1) * C)
            for h, (ks, vs) in enumerate(heads):
                vh = v2[rows, vs]
                sc = jnp.where(mask, sc2[h], 0.0).astype(BF16)
                oi = jnp.dot(sc, vh, preferred_element_type=F32)
                o_intra[c][h] = oi if o_intra[c][h] is None else o_intra[c][h] + oi
                kv_scr[sb * G + c, h] = lax.dot_general(vh, kl[:, ks], tn, preferred_element_type=F32)
        for c in range(G):
            part_scr[pl.ds(r0 + c * C, C), :] = jnp.concatenate(o_intra[c], axis=1)
        return carry

    lax.fori_loop(0, n_chunks // G, intra, 0)

    def scan(n, carry):
        sf, sb = carry
        nb = n_chunks - 1 - n
        new_f, new_b = [], []
        for h in range(2):
            kv = kvf_scr[n, h]
            kvf_scr[n, h] = sf[h]
            new_f.append(decf_scr[n, h] * sf[h] + kv)
            kv = kvb_scr[nb, h]
            kvb_scr[nb, h] = sb[h]
            new_b.append(decb_scr[nb, h] * sb[h] + kv)
        return tuple(new_f), tuple(new_b)

    zero_state = tuple(jnp.zeros((GLA_DV, GLA_DK), F32) for _ in range(2))
    lax.fori_loop(0, n_chunks, scan, (zero_state, zero_state), unroll=4)

    GF = min(GLA_FINISH_GROUP, n_chunks)
    RF = C * GF

    def finish(sb, carry):
        r0 = pl.multiple_of(sb * RF, RF)
        g = g_ref[pl.ds(r0, RF), :].astype(F32)
        part = part_scr[pl.ds(r0, RF), :]
        qf = qif_scr[pl.ds(r0, RF), :]
        qb = qib_scr[pl.ds(r0, RF), :]
        inter = []
        for c in range(GF):
            rows = slice(c * C, (c + 1) * C)
            inter.append([
                lax.dot_general(qf[rows, ks], kvf_scr[sb * GF + c, h].astype(BF16), nt, preferred_element_type=F32)
                + lax.dot_general(qb[rows, ks], kvb_scr[sb * GF + c, h].astype(BF16), nt, preferred_element_type=F32)
                for h, (ks, _) in enumerate(heads)])
        gate = g * _sigmoid(g)
        for c in range(GF):
            rows = slice(c * C, (c + 1) * C)
            normed = jnp.concatenate([_rms(part[rows, vs] + inter[c][h], gain) for h, (_, vs) in enumerate(heads)],
                                     axis=1)
            o_ref[pl.ds(r0 + c * C, C), :] = (normed * gate[rows]).astype(BF16)
        return carry

    lax.fori_loop(0, n_chunks // GF, finish, 0)


def _gla(gq, gk, gv, gg, laf, lab, gain, B, T):
    N = gq.shape[0]
    pairs = GLA_HEADS // 2
    n_chunks = T // GLA_CHUNK
    assert n_chunks % GLA_GROUP == 0 and n_chunks % min(GLA_FINISH_GROUP, n_chunks) == 0
    group_rows = GLA_CHUNK * GLA_GROUP
    kspec = pl.BlockSpec((T, 2 * GLA_DK), lambda b, p: (b, p))
    vspec = pl.BlockSpec((T, 2 * GLA_DV), lambda b, p: (b, p))
    state = pltpu.VMEM((n_chunks, 2, GLA_DV, GLA_DK), F32)
    decay = pltpu.VMEM((n_chunks, 2, 1, GLA_DK), F32)
    return pl.pallas_call(
        _gla_kernel,
        out_shape=jax.ShapeDtypeStruct((N, GLA_WIDTH), BF16),
        grid=(B, pairs),
        in_specs=[kspec, kspec, vspec, vspec, kspec, kspec,
                  pl.BlockSpec((1, GLA_DV), lambda b, p: (0, 0))],
        out_specs=vspec,
        scratch_shapes=[pltpu.VMEM((T, 2 * GLA_DV), F32),
                        pltpu.VMEM((T, 2 * GLA_DK), BF16), pltpu.VMEM((T, 2 * GLA_DK), BF16),
                        state, state, decay, decay,
                        pltpu.VMEM((group_rows, group_rows), BF16), pltpu.VMEM((group_rows, group_rows), BF16)],
        compiler_params=_cparams(("arbitrary", "arbitrary")),
        name="gla",
    )(gq, gk, gv, gg, laf, lab, gain)


def _attn_kernel(qt_ref, k_ref, vt_ref, ot_ref, vext_scr, s0_scr, s1_scr, p0_scr, p1_scr, m0_scr, m1_scr,
                 *, tq, key_block, steps_per_iter):
    j = pl.program_id(1)
    T = k_ref.shape[0]
    nq = T // tq
    dh = ATT_HEAD_DIM
    cols = ATT_GROUP * tq
    first = j == 0

    row = lax.broadcasted_iota(jnp.int32, (ATT_KV_WIDTH, T), 0)
    own = (row >= j * dh) & (row < (j + 1) * dh)
    vext_scr[...] = jnp.where(own, vt_ref[0], jnp.ones((), BF16))

    s_scr, p_scr, m_scr = (s0_scr, s1_scr), (p0_scr, p1_scr), (m0_scr, m1_scr)

    def step(blk, sa, do_scores=True, do_probs=True, do_out=True):
        sb = 1 - sa
        if do_scores:
            c0 = pl.multiple_of(blk * tq, tq)
            q4 = jnp.concatenate([qt_ref[0, h * dh:(h + 1) * dh, pl.ds(c0, tq)] for h in range(ATT_GROUP)],
                                 axis=1)
            zeros = jnp.zeros_like(q4)
            qe = jnp.where(first, jnp.concatenate([q4, zeros], axis=0), jnp.concatenate([zeros, q4], axis=0))
            m = jnp.full((8, cols), -jnp.inf, F32)
        if do_probs:
            mx = m_scr[sb][...]
        if do_out:
            acc = jnp.zeros((ATT_KV_WIDTH, cols), F32)
        for kb in range(T // key_block):
            ks = slice(kb * key_block, (kb + 1) * key_block)
            if do_scores:
                s = jnp.dot(k_ref[ks, :], qe, preferred_element_type=F32)
                s_scr[sa][ks, :] = s
                m = jnp.maximum(m, jnp.max(s.reshape(key_block // 8, 8, cols), axis=0))
            if do_probs:
                p_scr[sb][ks, :] = jnp.exp2(s_scr[sb][ks, :] - mx).astype(BF16)
            if do_out:
                acc = acc + jnp.dot(vext_scr[:, ks], p_scr[sa][ks, :], preferred_element_type=F32)
        if do_scores:
            m_scr[sa][...] = jnp.max(m, axis=0, keepdims=True)
        if do_out:
            num = jnp.where(first, acc[:dh], acc[dh:])
            den = jnp.where(first, acc[dh:dh + 1], acc[0:1])
            o = (num / den).astype(BF16)
            c0 = pl.multiple_of((blk - 2) * tq, tq)
            for h in range(ATT_GROUP):
                ot_ref[0, h * dh:(h + 1) * dh, pl.ds(c0, tq)] = o[:, h * tq:(h + 1) * tq]

    step(0, 0, do_probs=False, do_out=False)
    step(1, 1, do_out=False)

    def steady(it, carry):
        for u in range(steps_per_iter):
            pl.when(it >= 0)(functools.partial(step, 2 + steps_per_iter * it + u, u % 2))
        return carry

    lax.fori_loop(0, (nq - 2) // steps_per_iter, steady, 0)
    step(nq, 0, do_scores=False)
    step(nq + 1, 1, do_scores=False, do_probs=False)


def _attn(aqt, ak, avt, B, T, tq, key_block):
    assert T % (2 * tq) == 0 and tq % LANES == 0
    gw = ATT_GROUP * ATT_HEAD_DIM
    cols = ATT_GROUP * tq
    steps_per_iter = 2
    return pl.pallas_call(
        functools.partial(_attn_kernel, tq=tq, key_block=key_block, steps_per_iter=steps_per_iter),
        out_shape=jax.ShapeDtypeStruct((B, ATT_WIDTH, T), BF16),
        grid=(B, ATT_KV_HEADS),
        in_specs=[pl.BlockSpec((1, gw, T), lambda b, j: (b, j, 0)),
                  pl.BlockSpec((T, ATT_KV_WIDTH), lambda b, j: (b, 0)),
                  pl.BlockSpec((1, ATT_KV_WIDTH, T), lambda b, j: (b, 0, 0))],
        out_specs=pl.BlockSpec((1, gw, T), lambda b, j: (b, j, 0)),
        scratch_shapes=[pltpu.VMEM((ATT_KV_WIDTH, T), BF16),
                        pltpu.VMEM((T, cols), F32), pltpu.VMEM((T, cols), F32),
                        pltpu.VMEM((T, cols), BF16), pltpu.VMEM((T, cols), BF16),
                        pltpu.VMEM((1, cols), F32), pltpu.VMEM((1, cols), F32)],
        compiler_params=_cparams(("arbitrary", "arbitrary")),
        name="attn",
    )(aqt, ak, avt)


def _out_proj_kernel(x_ref, og_ref, oa_ref, wo_ref, ag_ref, g2_ref, wr_ref, br_ref, tri_ref,
                     h_ref, xn_ref, eid_ref, gate_ref, rank_ref, cnt_ref, run_scr):
    tm = x_ref.shape[0]

    @pl.when(pl.program_id(0) == 0)
    def _():
        run_scr[...] = jnp.zeros_like(run_scr)

    def branch_norm(ts):
        oat = oa_ref[0, :, ts].astype(F32)
        return (oat * lax.rsqrt(jnp.mean(oat * oat, axis=0, keepdims=True) + EPS) * ag_ref[...]).astype(BF16)

    def mix(ts, oan):
        return (jnp.dot(og_ref[ts, :], wo_ref[:GLA_WIDTH, :], preferred_element_type=F32)
                + lax.dot_general(oan, wo_ref[GLA_WIDTH:, :], (((0,), (0,)), ((), ())),
                                  preferred_element_type=F32))

    def residual_norm(ts, y):
        h = x_ref[ts, :] + y
        h_ref[ts, :] = h
        xn = _rms(h, g2_ref[...])
        xn_ref[ts, :] = _pack_pairs(xn)
        return xn.astype(BF16)

    def logits(xnb):
        return lax.dot_general(wr_ref[...], xnb, (((1,), (1,)), ((), ())), preferred_element_type=F32) + br_ref[...]

    whole = slice(0, tm)
    lt = logits(residual_norm(whole, mix(whole, branch_norm(whole))))
    iota8 = lax.broadcasted_iota(jnp.int32, (N_GROUPS, tm), 0)

    def first_argmax(v):
        top = jnp.max(v, axis=0, keepdims=True)
        idx = jnp.min(jnp.where(v == top, iota8, N_GROUPS), axis=0, keepdims=True)
        return top, idx

    gl = lt[0:N_GROUPS]
    gmax, gidx = first_argmax(gl)
    gw = 1.0 / jnp.sum(jnp.exp(gl - gmax), axis=0, keepdims=True)
    esel = jnp.zeros((EXPERTS_PER_GROUP, tm), F32)
    for g in range(N_GROUPS):
        lo = N_GROUPS + g * EXPERTS_PER_GROUP
        esel = jnp.where(gidx == g, lt[lo:lo + EXPERTS_PER_GROUP], esel)
    v1, i1 = first_argmax(esel)
    rest = jnp.where(iota8 == i1, -jnp.inf, esel)
    v2, i2 = first_argmax(rest)
    t = jnp.exp(v2 - v1)
    den = 1.0 + t
    e1 = gidx * EXPERTS_PER_GROUP + i1
    e2 = gidx * EXPERTS_PER_GROUP + i2
    eid_ref[0:1, :] = e1
    eid_ref[1:2, :] = e2
    gate_ref[0:1, :] = gw * (1.0 / den)
    gate_ref[1:2, :] = gw * (t / den)

    iota_e = lax.broadcasted_iota(jnp.int32, (N_EXPERTS, tm), 0)
    oh1 = iota_e == e1
    oh2 = iota_e == e2
    both = jnp.where(oh1, 1.0, jnp.where(oh2, 1.0, 0.0)).astype(BF16)
    prefix = jnp.dot(both, tri_ref[...], preferred_element_type=F32)
    run = run_scr[...]
    base = prefix - 1.0 + run[:, 0:1]
    rank_ref[0:1, :] = jnp.sum(jnp.where(oh1, base, 0.0), axis=0, keepdims=True).astype(jnp.int32)
    rank_ref[1:2, :] = jnp.sum(jnp.where(oh2, base, 0.0), axis=0, keepdims=True).astype(jnp.int32)
    run = run + prefix[:, tm - 1:tm]
    run_scr[...] = run
    cnt_ref[...] = run


def _out_proj(xt, o_gla, o_att_t, wo, ag, g2, wr, br, tm):
    N = xt.shape[0]
    tpb = o_att_t.shape[2] // tm
    const = lambda shape: pl.BlockSpec(shape, lambda i: (0,) * len(shape))
    rows = lambda width: pl.BlockSpec((tm, width), lambda i: (i, 0))
    cols = pl.BlockSpec((TOP_K, tm), lambda i: (0, i))
    tri = jnp.asarray(np.triu(np.ones((tm, tm), np.float32)), BF16)
    return pl.pallas_call(
        _out_proj_kernel,
        out_shape=(jax.ShapeDtypeStruct((N, D_MODEL), F32),
                   jax.ShapeDtypeStruct((N, D_PACKED), jnp.uint32),
                   jax.ShapeDtypeStruct((TOP_K, N), jnp.int32),
                   jax.ShapeDtypeStruct((TOP_K, N), F32),
                   jax.ShapeDtypeStruct((TOP_K, N), jnp.int32),
                   jax.ShapeDtypeStruct((N_EXPERTS, LANES), F32)),
        grid=(N // tm,),
        in_specs=[rows(D_MODEL), rows(GLA_WIDTH),
                  pl.BlockSpec((1, ATT_WIDTH, tm), lambda i: (i // tpb, 0, i % tpb)),
                  const((D_MODEL, D_MODEL)),
                  const((ATT_WIDTH, 1)), const((1, D_MODEL)), const((LANES, D_MODEL)),
                  const((LANES, 1)), const((tm, tm))],
        out_specs=(rows(D_MODEL), rows(D_PACKED), cols, cols, cols, const((N_EXPERTS, LANES))),
        scratch_shapes=[pltpu.VMEM((N_EXPERTS, LANES), F32)],
        compiler_params=_cparams(("arbitrary",)),
        name="out_proj",
    )(xt, o_gla, o_att_t, wo, ag, g2, wr, br, tri)


D_PACKED = D_MODEL // 2


def _pack_pairs(x):
    w = x.shape[1] // 2
    bits = pltpu.bitcast(x.astype(BF16).astype(F32), jnp.uint32)
    return bits[:, :w] | (bits[:, w:] >> 16)


def _unpack_pairs(words):
    hi = pltpu.bitcast(words & jnp.uint32(0xFFFF0000), F32)
    lo = pltpu.bitcast(words << 16, F32)
    return jnp.concatenate([hi, lo], axis=1)


def _dispatch_kernel(zs_ref, nu_ref, *refs):
    dest_refs = refs[:TOP_K]
    packed, xs_hbm, zero_buf, zero_sem, row_sem = refs[TOP_K:]
    tn = packed.shape[0]
    i = pl.program_id(0)

    def fill(row0):
        return pltpu.make_async_copy(
            zero_buf, xs_hbm.at[pl.ds(pl.multiple_of(row0, MOE_BLOCK), MOE_BLOCK)], zero_sem)

    @pl.when(i == 0)
    def _():
        zero_buf[...] = jnp.zeros_like(zero_buf)
        fills = [fill(zs_ref[e]) for e in range(N_EXPERTS)]
        for f in fills:
            f.start()
        for f in fills:
            f.wait()

        def fill_tail(blk, carry):
            f = fill(blk * MOE_BLOCK)
            f.start()
            f.wait()
            return carry

        lax.fori_loop(nu_ref[0], xs_hbm.shape[0] // MOE_BLOCK, fill_tail, 0)

    def issue(g, carry):
        tile = packed.at[pl.ds(pl.multiple_of(g * 8, 8), 8)]
        for s in range(8):
            for k in range(TOP_K):
                pltpu.make_async_copy(tile.at[pl.ds(s, 1)], xs_hbm.at[pl.ds(dest_refs[k][g * 8 + s], 1)],
                                      row_sem).start(priority=k % 2)
        return carry

    lax.fori_loop(0, tn // 8, issue, 0)
    for k in range(TOP_K):
        pltpu.make_async_copy(packed, xs_hbm.at[pl.ds(0, tn)], row_sem).wait()


def _dest_specs(n_steps, tn, index):
    return [pl.BlockSpec((tn,), lambda i, *_, k=k: (k * n_steps + index(i),), memory_space=pltpu.SMEM)
            for k in range(TOP_K)]


def _dispatch(zero_starts, n_used, dest_flat, xn, P, tn):
    N = xn.shape[0]
    grid_spec = pltpu.PrefetchScalarGridSpec(
        num_scalar_prefetch=2,
        grid=(N // tn,),
        in_specs=_dest_specs(N // tn, tn, lambda i: i) + [pl.BlockSpec((tn, D_PACKED), lambda i, zs, nu: (i, 0))],
        out_specs=pl.BlockSpec(memory_space=pl.ANY),
        scratch_shapes=[pltpu.VMEM((MOE_BLOCK, D_PACKED), jnp.uint32),
                        pltpu.SemaphoreType.DMA, pltpu.SemaphoreType.DMA],
    )
    return pl.pallas_call(
        _dispatch_kernel,
        out_shape=jax.ShapeDtypeStruct((P, D_PACKED), jnp.uint32),
        grid_spec=grid_spec,
        compiler_params=_cparams(("arbitrary",)),
        name="dispatch",
    )(zero_starts, n_used, *([dest_flat] * TOP_K), xn)


def _moe_ffn_kernel(be_ref, nu_ref, seg_ref, nxt_ref, xs_ref, wg_hbm, wu_hbm, wd_hbm, ys_ref,
                    wg_f, wu_f, wd_f, wg_s, wu_s, wd_s, sems):
    i = pl.program_id(0)
    used = i < nu_ref[0]

    def weights(expert, slot):
        return [pltpu.make_async_copy(src.at[expert], dst.at[slot], sems.at[slot])
                for src, dst in ((wg_hbm, wg_f), (wu_hbm, wu_f), (wd_hbm, wd_f))]

    @pl.when(used)
    def _():
        new_expert = jnp.logical_or(i == 0, be_ref[i] != be_ref[jnp.maximum(i - 1, 0)])

        @pl.when(new_expert)
        def _():
            slot = seg_ref[i] % 2

            @pl.when(i == 0)
            def _():
                for cp in weights(be_ref[0], 0):
                    cp.start()

            for cp in weights(be_ref[i], slot):
                cp.wait()

            @pl.when(nxt_ref[i] >= 0)
            def _():
                for cp in weights(nxt_ref[i], 1 - slot):
                    cp.start()

            wg_s[...] = wg_f[slot].astype(BF16)
            wu_s[...] = wu_f[slot].astype(BF16)
            wd_s[...] = wd_f[slot].astype(BF16)

        xb = _unpack_pairs(xs_ref[...]).astype(BF16)
        hg = jnp.dot(xb, wg_s[...], preferred_element_type=F32)
        hu = jnp.dot(xb, wu_s[...], preferred_element_type=F32)
        hm = (hg * _sigmoid(hg) * hu).astype(BF16)
        ys_ref[...] = _pack_pairs(jnp.dot(hm, wd_s[...], preferred_element_type=F32))

    @pl.when(jnp.logical_not(used))
    def _():
        ys_ref[...] = jnp.zeros_like(ys_ref)


def _moe_ffn(blk_expert, n_used, xs, w_gate, w_up, w_down):
    P = xs.shape[0]
    nblk = P // MOE_BLOCK
    blk = jnp.arange(nblk, dtype=jnp.int32)
    starts = jnp.logical_and(jnp.concatenate([jnp.ones((1,), bool), blk_expert[1:] != blk_expert[:-1]]),
                             blk < n_used[0])
    seg = jnp.cumsum(starts.astype(jnp.int32)) - 1
    next_start = lax.cummin(jnp.where(starts, blk, nblk), reverse=True)
    next_start = jnp.concatenate([next_start[1:], jnp.full((1,), nblk, jnp.int32)])
    nxt = jnp.where(next_start < nblk, blk_expert[jnp.minimum(next_start, nblk - 1)], -1).astype(jnp.int32)
    row_block = pl.BlockSpec((MOE_BLOCK, D_PACKED), lambda i, *_: (i, 0))
    grid_spec = pltpu.PrefetchScalarGridSpec(
        num_scalar_prefetch=4,
        grid=(nblk,),
        in_specs=[row_block] + [pl.BlockSpec(memory_space=pl.ANY)] * 3,
        out_specs=row_block,
        scratch_shapes=[pltpu.VMEM((2, D_MODEL, D_EXPERT), F32),
                        pltpu.VMEM((2, D_MODEL, D_EXPERT), F32),
                        pltpu.VMEM((2, D_EXPERT, D_MODEL), F32),
                        pltpu.VMEM((D_MODEL, D_EXPERT), BF16),
                        pltpu.VMEM((D_MODEL, D_EXPERT), BF16),
                        pltpu.VMEM((D_EXPERT, D_MODEL), BF16),
                        pltpu.SemaphoreType.DMA((2,))],
    )
    return pl.pallas_call(
        _moe_ffn_kernel,
        out_shape=jax.ShapeDtypeStruct((P, D_PACKED), jnp.uint32),
        grid_spec=grid_spec,
        compiler_params=_cparams(("arbitrary",)),
        name="moe_ffn",
    )(blk_expert, n_used, seg, nxt, xs, w_gate, w_up, w_down)


COMBINE_ROWS = 32

def _combine_kernel(*refs, final):
    dcur_refs, dnext_refs = refs[:TOP_K], refs[TOP_K:2 * TOP_K]
    h_ref, gate_ref, fg_ref, ys_hbm, o_ref, ybuf, sems = refs[2 * TOP_K:]
    tm = h_ref.shape[0]
    i = pl.program_id(0)
    n = pl.num_programs(0)

    slot = i % 2
    fg = fg_ref[...]

    def issue(drefs, to_slot, g):
        for k in range(TOP_K):
            tile = ybuf.at[to_slot, pl.ds(pl.multiple_of(k * tm + g * 8, 8), 8)]
            for s in range(8):
                pltpu.make_async_copy(ys_hbm.at[pl.ds(drefs[k][g * 8 + s], 1)],
                                      tile.at[pl.ds(s, 1)], sems.at[to_slot]).start(priority=k % 2)

    tiles = COMBINE_ROWS // 8

    def load_rows(j):
        r0 = pl.multiple_of(j * COMBINE_ROWS, COMBINE_ROWS)
        packed = [ybuf[slot, pl.ds(pl.multiple_of(k * tm + j * COMBINE_ROWS, COMBINE_ROWS), COMBINE_ROWS), :]
                  for k in range(TOP_K)]
        return r0, gate_ref[pl.ds(r0, COMBINE_ROWS), :], h_ref[pl.ds(r0, COMBINE_ROWS), :], packed

    def finish_rows(loaded):
        r0, gate, hh, packed = loaded
        for k in range(TOP_K):
            hh = hh + _unpack_pairs(packed[k]) * gate[:, k:k + 1]
        o_ref[pl.ds(r0, COMBINE_ROWS), :] = _rms(hh, fg) if final else hh

    @pl.when(i == 0)
    def _():
        lax.fori_loop(0, tm // 8, lambda g, c: (issue(dcur_refs, 0, g), c)[1], 0)

    pltpu.make_async_copy(ys_hbm.at[pl.ds(0, TOP_K * tm)], ybuf.at[slot], sems.at[slot]).wait()

    @pl.when(i + 1 < n)
    def _():
        def body(j, c):
            loaded = load_rows(j)
            for u in range(tiles):
                issue(dnext_refs, 1 - slot, j * tiles + u)
            finish_rows(loaded)
            return c

        lax.fori_loop(0, tm // COMBINE_ROWS, body, 0)

    @pl.when(i + 1 == n)
    def _():
        lax.fori_loop(0, tm // COMBINE_ROWS, lambda j, c: (finish_rows(load_rows(j)), c)[1], 0)


def _combine(h, ys, dest_flat, gate_t, fg, tm, final):
    N = h.shape[0]
    n = N // tm
    rows = pl.BlockSpec((tm, D_MODEL), lambda i: (i, 0))
    return pl.pallas_call(
        functools.partial(_combine_kernel, final=final),
        out_shape=jax.ShapeDtypeStruct((N, D_MODEL), F32),
        grid=(n,),
        in_specs=_dest_specs(n, tm, lambda i: i) + _dest_specs(n, tm, lambda i: jnp.minimum(i + 1, n - 1)) + [
                  rows, pl.BlockSpec((tm, TOP_K), lambda i: (i, 0)),
                  pl.BlockSpec((1, D_MODEL), lambda i: (0, 0)),
                  pl.BlockSpec(memory_space=pl.ANY)],
        out_specs=rows,
        scratch_shapes=[pltpu.VMEM((2, TOP_K * tm, D_PACKED), jnp.uint32), pltpu.SemaphoreType.DMA((2,))],
        compiler_params=_cparams(("arbitrary",)),
        name="combine",
    )(*([dest_flat] * (2 * TOP_K)), h, gate_t, fg, ys)


def _reorder_w_in(w_in):
    n_gla = 2 * GLA_KEY_WIDTH + 2 * GLA_WIDTH
    z0 = n_gla
    a0 = z0 + 2 * GLA_GATE_RANK
    pad = jnp.zeros((D_MODEL, Z_PAD - 2 * GLA_GATE_RANK), w_in.dtype)
    return jnp.concatenate([w_in[:, :n_gla], w_in[:, a0:], w_in[:, z0:a0], pad], axis=1)


def kernel(x, norm1_gain, w_in, gla_up_fwd, gla_up_fwd_bias, gla_up_bwd, gla_up_bwd_bias, gla_out_gain, q_norm_gain, k_norm_gain, att_out_gain, w_out, norm2_gain, w_group, b_group, w_expert, b_expert, w_gate, w_up, w_down, final_gain):
    B, T, D = x.shape
    N = B * T
    depth = norm1_gain.shape[0]
    tm = min(512, T)
    tq = 128
    key_block = min(256, T)
    h = x.reshape(N, D)
    cos, se, so = _rope_tables(T)
    head_mean = jnp.asarray(
        np.kron(np.eye(ATT_HEADS, dtype=np.float32),
                np.full((ATT_HEAD_DIM, ATT_HEAD_DIM), 1.0 / ATT_HEAD_DIM, np.float32)), BF16)
    for l in range(depth):
        w = _reorder_w_in(w_in[l]).astype(BF16)
        r = GLA_GATE_RANK
        up = jnp.zeros((Z_PAD, 2 * GLA_KEY_WIDTH), F32)
        up = up.at[:r, :GLA_KEY_WIDTH].set(gla_up_fwd[l]).at[r:2 * r, GLA_KEY_WIDTH:].set(gla_up_bwd[l])
        upb = jnp.concatenate([gla_up_fwd_bias[l], gla_up_bwd_bias[l]])[None, :]
        qg = jnp.tile(q_norm_gain[l], ATT_HEADS)[None, :]
        kg = jnp.tile(k_norm_gain[l], ATT_KV_HEADS)[None, :]
        gq, gk, gv, gg, laf, lab, aqt, ak, avt = _in_proj(
            h, B, T, tm, norm1_gain[l][None, :], w, up.astype(BF16), upb, qg, kg, head_mean, cos, se, so)
        o_gla = _gla(gq, gk, gv, gg, laf, lab, gla_out_gain[l][None, :], B, T)
        o_att_t = _attn(aqt, ak, avt, B, T, tq, key_block)

        wr = jnp.zeros((LANES, D), F32)
        wr = wr.at[:N_GROUPS].set(w_group[l].T).at[N_GROUPS:N_GROUPS + N_EXPERTS].set(w_expert[l].T)
        br = jnp.zeros((LANES, 1), F32)
        br = br.at[:N_GROUPS, 0].set(b_group[l]).at[N_GROUPS:N_GROUPS + N_EXPERTS, 0].set(b_expert[l])
        h, xn, eid, gate, rank, cnt = _out_proj(
            h, o_gla, o_att_t, w_out[l].astype(BF16), att_out_gain[l][:, None], norm2_gain[l][None, :],
            wr.astype(BF16), br, tm)

        counts = cnt[:, 0].astype(jnp.int32)
        padded = (counts + MOE_BLOCK - 1) // MOE_BLOCK * MOE_BLOCK
        pad_ends = jnp.cumsum(padded)
        pad_starts = pad_ends - padded
        seg_start = jnp.sum(jnp.where(eid[:, :, None] == jnp.arange(N_EXPERTS, dtype=jnp.int32),
                                      pad_starts.astype(jnp.int32), 0), axis=-1)
        dest = (seg_start + rank).reshape(-1)
        P = N * TOP_K + N_EXPERTS * MOE_BLOCK
        nblk = P // MOE_BLOCK
        blk_start = jnp.arange(nblk, dtype=jnp.int32) * MOE_BLOCK
        blk_expert = jnp.minimum(
            jnp.sum((pad_ends[None, :] <= blk_start[:, None]).astype(jnp.int32), axis=1), N_EXPERTS - 1)
        n_used = (pad_ends[-1:] // MOE_BLOCK).astype(jnp.int32)

        zero_starts = jnp.maximum(pad_ends - MOE_BLOCK, 0).astype(jnp.int32)
        xs = _dispatch(zero_starts, n_used, dest, xn, P, tm)
        ys = _moe_ffn(blk_expert, n_used, xs, w_gate[l], w_up[l], w_down[l])
        h = _combine(h, ys, dest, gate.T, final_gain[None, :], tm, l == depth - 1)
    return h.reshape(B, T, D)
```

```python
import functools

import numpy as np
import jax
import jax.numpy as jnp
from jax import lax
from jax.experimental import pallas as pl
from jax.experimental.pallas import tpu as pltpu

F32 = jnp.float32
BF16 = jnp.bfloat16

D_MODEL = 1024
EPS = 1e-6
GRID_W = 64

GLA_HEADS = 4
GLA_DK = 64
GLA_DV = 128
GLA_KEY_WIDTH = GLA_HEADS * GLA_DK
GLA_WIDTH = GLA_HEADS * GLA_DV
GLA_GATE_RANK = 16
GLA_GATE_NORM = 16.0
GLA_CHUNK = 64

ATT_HEADS = 8
ATT_KV_HEADS = 2
ATT_HEAD_DIM = 64
ATT_GROUP = ATT_HEADS // ATT_KV_HEADS
ATT_WIDTH = ATT_HEADS * ATT_HEAD_DIM
ATT_KV_WIDTH = ATT_KV_HEADS * ATT_HEAD_DIM
ROPE_THETA = 10000.0
LOG2_E = 1.4426950408889634

N_GROUPS = 8
EXPERTS_PER_GROUP = 8
N_EXPERTS = N_GROUPS * EXPERTS_PER_GROUP
TOP_K = 2
D_EXPERT = 512
MOE_BLOCK = 256

LANES = 128
Z_PAD = LANES

_OFF_GQ = 0
_OFF_GK = _OFF_GQ + GLA_KEY_WIDTH
_OFF_GV = _OFF_GK + GLA_KEY_WIDTH
_OFF_GG = _OFF_GV + GLA_WIDTH
_OFF_AQ = _OFF_GG + GLA_WIDTH
_OFF_AK = _OFF_AQ + ATT_WIDTH
_OFF_AV = _OFF_AK + ATT_KV_WIDTH
_OFF_Z = _OFF_AV + ATT_KV_WIDTH
D_IN_PAD = _OFF_Z + Z_PAD

VMEM_LIMIT = 56 * 1024 * 1024


def _cparams(semantics):
    return pltpu.CompilerParams(dimension_semantics=semantics, vmem_limit_bytes=VMEM_LIMIT)


def _rms(x, gain):
    return x * lax.rsqrt(jnp.mean(x * x, axis=-1, keepdims=True) + EPS) * gain


def _sigmoid(x):
    return 1.0 / (1.0 + jnp.exp(-x))


def _head_norm_rope(x, head_mean, gain, cos, sin_even, sin_odd):
    w = x.shape[1]
    ms = jnp.dot((x * x).astype(BF16), head_mean, preferred_element_type=F32)
    xn = x * lax.rsqrt(ms + EPS) * gain
    reps = w // LANES
    tile = lambda t: t if reps == 1 else jnp.concatenate([t] * reps, axis=1)
    nxt = pltpu.roll(xn, w - 1, 1)
    prv = pltpu.roll(xn, 1, 1)
    return xn * tile(cos) + nxt * tile(sin_even) + prv * tile(sin_odd)


def _in_proj_kernel(x_ref, g1_ref, w_ref, up_ref, upb_ref, qg_ref, kg_ref, hm_ref,
                    cos_ref, se_ref, so_ref,
                    gq_ref, gk_ref, gv_ref, gg_ref, laf_ref, lab_ref, aqt_ref, ak_ref, avt_ref):
    u = _rms(x_ref[...], g1_ref[...]).astype(BF16)

    def proj(lo, hi):
        return jnp.dot(u, w_ref[:, lo:hi], preferred_element_type=F32)

    cos, se, so = cos_ref[...], se_ref[...], so_ref[...]
    hm = hm_ref[...]
    r_q = proj(_OFF_AQ, _OFF_AK)
    r_gqk = proj(_OFF_GQ, _OFF_GV)
    q = _head_norm_rope(r_q, hm, qg_ref[...], cos, se, so)
    aqt_ref[0] = (q * (ATT_HEAD_DIM ** -0.5 * LOG2_E)).T.astype(BF16)
    gq_ref[...] = r_gqk[:, :GLA_KEY_WIDTH].astype(BF16)
    gk_ref[...] = r_gqk[:, GLA_KEY_WIDTH:].astype(BF16)

    r_kvz = proj(_OFF_AK, D_IN_PAD)
    r_gv = proj(_OFF_GV, _OFF_GG)
    k = _head_norm_rope(r_kvz[:, :ATT_KV_WIDTH], hm[:ATT_KV_WIDTH, :ATT_KV_WIDTH], kg_ref[...], cos, se, so)
    ak_ref[...] = k.astype(BF16)
    avt_ref[0] = r_kvz[:, ATT_KV_WIDTH:2 * ATT_KV_WIDTH].T.astype(BF16)
    gv_ref[...] = r_gv.astype(BF16)

    z = r_kvz[:, 2 * ATT_KV_WIDTH:].astype(BF16)
    zl = jnp.dot(z, up_ref[...], preferred_element_type=F32) + upb_ref[...]
    r_gg = proj(_OFF_GG, _OFF_AQ)
    la = (jnp.minimum(zl, 0.0) - jnp.log(1.0 + jnp.exp(-jnp.abs(zl)))) * (1.0 / GLA_GATE_NORM)
    laf_ref[...] = la[:, :GLA_KEY_WIDTH]
    lab_ref[...] = la[:, GLA_KEY_WIDTH:]
    gg_ref[...] = r_gg.astype(BF16)


def _rope_tables(T):
    t = np.arange(T)
    row = (t // GRID_W).astype(np.float32)
    col = (t % GRID_W).astype(np.float32)
    axis_dim = ATT_HEAD_DIM // 2
    inv_freq = (ROPE_THETA ** (-np.arange(0, axis_dim, 2, dtype=np.float32) / axis_dim)).astype(np.float32)
    ang = np.concatenate([row[:, None] * inv_freq, col[:, None] * inv_freq], axis=-1)
    ang = np.repeat(ang, 2, axis=1)
    ang = np.tile(ang, (1, LANES // ATT_HEAD_DIM))
    even = (np.arange(LANES) % 2 == 0)[None, :]
    cos = np.cos(ang)
    sin = np.sin(ang)
    return (jnp.asarray(cos, F32), jnp.asarray(np.where(even, -sin, 0.0), F32),
            jnp.asarray(np.where(even, 0.0, sin), F32))


def _in_proj(xt, B, T, tm, g1, w, up, upb, qg, kg, hm, cos, se, so):
    N = xt.shape[0]
    tpb = T // tm
    const = lambda shape: pl.BlockSpec(shape, lambda i: (0,) * len(shape))
    rows = lambda width: pl.BlockSpec((tm, width), lambda i: (i, 0))
    pos = pl.BlockSpec((tm, LANES), lambda i: (i % tpb, 0))
    out_shape = (
        jax.ShapeDtypeStruct((N, GLA_KEY_WIDTH), BF16),
        jax.ShapeDtypeStruct((N, GLA_KEY_WIDTH), BF16),
        jax.ShapeDtypeStruct((N, GLA_WIDTH), BF16),
        jax.ShapeDtypeStruct((N, GLA_WIDTH), BF16),
        jax.ShapeDtypeStruct((N, GLA_KEY_WIDTH), F32),
        jax.ShapeDtypeStruct((N, GLA_KEY_WIDTH), F32),
        jax.ShapeDtypeStruct((B, ATT_WIDTH, T), BF16),
        jax.ShapeDtypeStruct((N, ATT_KV_WIDTH), BF16),
        jax.ShapeDtypeStruct((B, ATT_KV_WIDTH, T), BF16),
    )
    cols = lambda width: pl.BlockSpec((1, width, tm), lambda i: (i // tpb, 0, i % tpb))
    out_specs = (
        rows(GLA_KEY_WIDTH), rows(GLA_KEY_WIDTH), rows(GLA_WIDTH), rows(GLA_WIDTH),
        rows(GLA_KEY_WIDTH), rows(GLA_KEY_WIDTH),
        cols(ATT_WIDTH), rows(ATT_KV_WIDTH), cols(ATT_KV_WIDTH),
    )
    return pl.pallas_call(
        _in_proj_kernel,
        out_shape=out_shape,
        grid=(N // tm,),
        in_specs=[rows(D_MODEL), const((1, D_MODEL)), const((D_MODEL, D_IN_PAD)),
                  const((Z_PAD, 2 * GLA_KEY_WIDTH)), const((1, 2 * GLA_KEY_WIDTH)),
                  const((1, ATT_WIDTH)), const((1, ATT_KV_WIDTH)), const((ATT_WIDTH, ATT_WIDTH)),
                  pos, pos, pos],
        out_specs=out_specs,
        compiler_params=_cparams(("arbitrary",)),
        name="in_proj",
    )(xt, g1, w, up, upb, qg, kg, hm, cos, se, so)


GLA_GROUP = 4
GLA_FINISH_GROUP = 16


def _gla_kernel(q_ref, k_ref, v_ref, g_ref, laf_ref, lab_ref, gain_ref, o_ref,
                part_scr, qif_scr, qib_scr, kvf_scr, kvb_scr, decf_scr, decb_scr, lf_scr, lb_scr):
    C, G = GLA_CHUNK, GLA_GROUP
    R = C * G
    T = q_ref.shape[0]
    n_chunks = T // C
    gain = gain_ref[...]
    row = lax.broadcasted_iota(jnp.int32, (C, C), 0)
    col = lax.broadcasted_iota(jnp.int32, (C, C), 1)
    tril = row >= col
    triu = row <= col
    rr = lax.broadcasted_iota(jnp.int32, (R, R), 0)
    cc = lax.broadcasted_iota(jnp.int32, (R, R), 1)
    same_chunk = (rr // C) == (cc // C)
    lf_scr[...] = jnp.where(same_chunk & (rr >= cc), 1.0, 0.0).astype(BF16)
    lb_scr[...] = jnp.where(same_chunk & (rr <= cc), 1.0, 0.0).astype(BF16)

    directions = ((laf_ref, lf_scr, tril, C // 2 - 1, C - 1, qif_scr, kvf_scr, decf_scr),
                  (lab_ref, lb_scr, triu, C // 2, 0, qib_scr, kvb_scr, decb_scr))
    heads = tuple((slice(h * GLA_DK, (h + 1) * GLA_DK), slice(h * GLA_DV, (h + 1) * GLA_DV)) for h in range(2))
    nt = (((1,), (1,)), ((), ()))
    tn = (((0,), (0,)), ((), ()))

    def intra(sb, carry):
        r0 = pl.multiple_of(sb * R, R)
        q2 = q_ref[pl.ds(r0, R), :].astype(F32) * (GLA_DK ** -0.5)
        k2 = k_ref[pl.ds(r0, R), :].astype(F32)
        v2 = v_ref[pl.ds(r0, R), :]
        cums = []
        for la_ref, tri_scr, *_ in directions:
            la = la_ref[pl.ds(r0, R), :]
            la_hi = la.astype(BF16)
            la_lo = (la - la_hi.astype(F32)).astype(BF16)
            both = jnp.dot(tri_scr[...], jnp.concatenate([la_hi, la_lo], axis=1), preferred_element_type=F32)
            cums.append(both[:, :2 * GLA_DK] + both[:, 2 * GLA_DK:])
        scaled = []
        for b, (_, _, mask, i_ref, i_last, qi_scr, kv_scr, dec_scr) in zip(cums, directions):
            for c in range(G):
                rows = slice(c * C, (c + 1) * C)
                bc, qc, kc = b[rows], q2[rows], k2[rows]
                b_ref = bc[i_ref:i_ref + 1, :]
                b_last = bc[i_last:i_last + 1, :]
                qf = (qc * jnp.exp(bc - b_ref)).astype(BF16)
                kf = (kc * jnp.exp(b_ref - bc)).astype(BF16)
                qi_scr[pl.ds(r0 + c * C, C), :] = (qc * jnp.exp(bc)).astype(BF16)
                kl = (kc * jnp.exp(b_last - bc)).astype(BF16)
                decay = jnp.exp(b_last)
                for h, (ks, _) in enumerate(heads):
                    dec_scr[sb * G + c, h] = decay[:, ks]
                scaled.append((c, mask, kv_scr, qf, kf, kl))
        raw = [[lax.dot_general(qf[:, ks], kf[:, ks], nt, preferred_element_type=F32) for ks, _ in heads]
               for _, _, _, qf, kf, _ in scaled]
        o_intra = [[None, None] for _ in range(G)]
        for (c, mask, kv_scr, _, _, kl), sc2 in zip(scaled, raw):
            rows = slice(c * C, (c + 1) * C)
            for h, (ks, vs) in enumerate(heads):
                vh = v2[rows, vs]
                sc = jnp.where(mask, sc2[h], 0.0).astype(BF16)
                oi = jnp.dot(sc, vh, preferred_element_type=F32)
                o_intra[c][h] = oi if o_intra[c][h] is None else o_intra[c][h] + oi
                kv_scr[sb * G + c, h] = lax.dot_general(vh, kl[:, ks], tn, preferred_element_type=F32)
        for c in range(G):
            part_scr[pl.ds(r0 + c * C, C), :] = jnp.concatenate(o_intra[c], axis=1)
        return carry

    lax.fori_loop(0, n_chunks // G, intra, 0)

    def scan(n, carry):
        sf, sb = carry
        nb = n_chunks - 1 - n
        new_f, new_b = [], []
        for h in range(2):
            kv = kvf_scr[n, h]
            kvf_scr[n, h] = sf[h]
            new_f.append(decf_scr[n, h] * sf[h] + kv)
            kv = kvb_scr[nb, h]
            kvb_scr[nb, h] = sb[h]
            new_b.append(decb_scr[nb, h] * sb[h] + kv)
        return tuple(new_f), tuple(new_b)

    zero_state = tuple(jnp.zeros((GLA_DV, GLA_DK), F32) for _ in range(2))
    lax.fori_loop(0, n_chunks, scan, (zero_state, zero_state), unroll=4)

    GF = min(GLA_FINISH_GROUP, n_chunks)
    RF = C * GF

    def finish(sb, carry):
        r0 = pl.multiple_of(sb * RF, RF)
        g = g_ref[pl.ds(r0, RF), :].astype(F32)
        part = part_scr[pl.ds(r0, RF), :]
        qf = qif_scr[pl.ds(r0, RF), :]
        qb = qib_scr[pl.ds(r0, RF), :]
        inter = []
        for c in range(GF):
            rows = slice(c * C, (c + 1) * C)
            inter.append([
                lax.dot_general(qf[rows, ks], kvf_scr[sb * GF + c, h].astype(BF16), nt, preferred_element_type=F32)
                + lax.dot_general(qb[rows, ks], kvb_scr[sb * GF + c, h].astype(BF16), nt, preferred_element_type=F32)
                for h, (ks, _) in enumerate(heads)])
        gate = g * _sigmoid(g)
        for c in range(GF):
            rows = slice(c * C, (c + 1) * C)
            normed = jnp.concatenate([_rms(part[rows, vs] + inter[c][h], gain) for h, (_, vs) in enumerate(heads)],
                                     axis=1)
            o_ref[pl.ds(r0 + c * C, C), :] = (normed * gate[rows]).astype(BF16)
        return carry

    lax.fori_loop(0, n_chunks // GF, finish, 0)


def _gla(gq, gk, gv, gg, laf, lab, gain, B, T):
    N = gq.shape[0]
    pairs = GLA_HEADS // 2
    n_chunks = T // GLA_CHUNK
    assert n_chunks % GLA_GROUP == 0 and n_chunks % min(GLA_FINISH_GROUP, n_chunks) == 0
    group_rows = GLA_CHUNK * GLA_GROUP
    kspec = pl.BlockSpec((T, 2 * GLA_DK), lambda b, p: (b, p))
    vspec = pl.BlockSpec((T, 2 * GLA_DV), lambda b, p: (b, p))
    state = pltpu.VMEM((n_chunks, 2, GLA_DV, GLA_DK), F32)
    decay = pltpu.VMEM((n_chunks, 2, 1, GLA_DK), F32)
    return pl.pallas_call(
        _gla_kernel,
        out_shape=jax.ShapeDtypeStruct((N, GLA_WIDTH), BF16),
        grid=(B, pairs),
        in_specs=[kspec, kspec, vspec, vspec, kspec, kspec,
                  pl.BlockSpec((1, GLA_DV), lambda b, p: (0, 0))],
        out_specs=vspec,
        scratch_shapes=[pltpu.VMEM((T, 2 * GLA_DV), F32),
                        pltpu.VMEM((T, 2 * GLA_DK), BF16), pltpu.VMEM((T, 2 * GLA_DK), BF16),
                        state, state, decay, decay,
                        pltpu.VMEM((group_rows, group_rows), BF16), pltpu.VMEM((group_rows, group_rows), BF16)],
        compiler_params=_cparams(("arbitrary", "arbitrary")),
        name="gla",
    )(gq, gk, gv, gg, laf, lab, gain)


def _attn_kernel(qt_ref, k_ref, vt_ref, ot_ref, vext_scr, s0_scr, s1_scr, p0_scr, p1_scr, m0_scr, m1_scr,
                 *, tq, key_block, steps_per_iter):
    j = pl.program_id(1)
    T = k_ref.shape[0]
    nq = T // tq
    dh = ATT_HEAD_DIM
    cols = ATT_GROUP * tq
    first = j == 0

    row = lax.broadcasted_iota(jnp.int32, (ATT_KV_WIDTH, T), 0)
    own = (row >= j * dh) & (row < (j + 1) * dh)
    vext_scr[...] = jnp.where(own, vt_ref[0], jnp.ones((), BF16))

    s_scr, p_scr, m_scr = (s0_scr, s1_scr), (p0_scr, p1_scr), (m0_scr, m1_scr)

    def step(blk, sa, do_scores=True, do_probs=True, do_out=True):
        sb = 1 - sa
        if do_scores:
            c0 = pl.multiple_of(blk * tq, tq)
            q4 = jnp.concatenate([qt_ref[0, h * dh:(h + 1) * dh, pl.ds(c0, tq)] for h in range(ATT_GROUP)],
                                 axis=1)
            zeros = jnp.zeros_like(q4)
            qe = jnp.where(first, jnp.concatenate([q4, zeros], axis=0), jnp.concatenate([zeros, q4], axis=0))
            m = jnp.full((8, cols), -jnp.inf, F32)
        if do_probs:
            mx = m_scr[sb][...]
        if do_out:
            acc = jnp.zeros((ATT_KV_WIDTH, cols), F32)
        for kb in range(T // key_block):
            ks = slice(kb * key_block, (kb + 1) * key_block)
            if do_scores:
                s = jnp.dot(k_ref[ks, :], qe, preferred_element_type=F32)
                s_scr[sa][ks, :] = s
                m = jnp.maximum(m, jnp.max(s.reshape(key_block // 8, 8, cols), axis=0))
            if do_probs:
                p_scr[sb][ks, :] = jnp.exp2(s_scr[sb][ks, :] - mx).astype(BF16)
            if do_out:
                acc = acc + jnp.dot(vext_scr[:, ks], p_scr[sa][ks, :], preferred_element_type=F32)
        if do_scores:
            m_scr[sa][...] = jnp.max(m, axis=0, keepdims=True)
        if do_out:
            num = jnp.where(first, acc[:dh], acc[dh:])
            den = jnp.where(first, acc[dh:dh + 1], acc[0:1])
            o = (num / den).astype(BF16)
            c0 = pl.multiple_of((blk - 2) * tq, tq)
            for h in range(ATT_GROUP):
                ot_ref[0, h * dh:(h + 1) * dh, pl.ds(c0, tq)] = o[:, h * tq:(h + 1) * tq]

    step(0, 0, do_probs=False, do_out=False)
    step(1, 1, do_out=False)

    def steady(it, carry):
        for u in range(steps_per_iter):
            pl.when(it >= 0)(functools.partial(step, 2 + steps_per_iter * it + u, u % 2))
        return carry

    lax.fori_loop(0, (nq - 2) // steps_per_iter, steady, 0)
    step(nq, 0, do_scores=False)
    step(nq + 1, 1, do_scores=False, do_probs=False)


def _attn(aqt, ak, avt, B, T, tq, key_block):
    assert T % (2 * tq) == 0 and tq % LANES == 0
    gw = ATT_GROUP * ATT_HEAD_DIM
    cols = ATT_GROUP * tq
    steps_per_iter = 2
    return pl.pallas_call(
        functools.partial(_attn_kernel, tq=tq, key_block=key_block, steps_per_iter=steps_per_iter),
        out_shape=jax.ShapeDtypeStruct((B, ATT_WIDTH, T), BF16),
        grid=(B, ATT_KV_HEADS),
        in_specs=[pl.BlockSpec((1, gw, T), lambda b, j: (b, j, 0)),
                  pl.BlockSpec((T, ATT_KV_WIDTH), lambda b, j: (b, 0)),
                  pl.BlockSpec((1, ATT_KV_WIDTH, T), lambda b, j: (b, 0, 0))],
        out_specs=pl.BlockSpec((1, gw, T), lambda b, j: (b, j, 0)),
        scratch_shapes=[pltpu.VMEM((ATT_KV_WIDTH, T), BF16),
                        pltpu.VMEM((T, cols), F32), pltpu.VMEM((T, cols), F32),
                        pltpu.VMEM((T, cols), BF16), pltpu.VMEM((T, cols), BF16),
                        pltpu.VMEM((1, cols), F32), pltpu.VMEM((1, cols), F32)],
        compiler_params=_cparams(("arbitrary", "arbitrary")),
        name="attn",
    )(aqt, ak, avt)


def _out_proj_kernel(x_ref, og_ref, oa_ref, wo_ref, ag_ref, g2_ref, wr_ref, br_ref, tri_ref,
                     h_ref, xn_ref, eid_ref, gate_ref, rank_ref, cnt_ref, run_scr):
    tm = x_ref.shape[0]

    @pl.when(pl.program_id(0) == 0)
    def _():
        run_scr[...] = jnp.zeros_like(run_scr)

    def branch_norm(ts):
        oat = oa_ref[0, :, ts].astype(F32)
        return (oat * lax.rsqrt(jnp.mean(oat * oat, axis=0, keepdims=True) + EPS) * ag_ref[...]).astype(BF16)

    def mix(ts, oan):
        return (jnp.dot(og_ref[ts, :], wo_ref[:GLA_WIDTH, :], preferred_element_type=F32)
                + lax.dot_general(oan, wo_ref[GLA_WIDTH:, :], (((0,), (0,)), ((), ())),
                                  preferred_element_type=F32))

    def residual_norm(ts, y):
        h = x_ref[ts, :] + y
        h_ref[ts, :] = h
        xn = _rms(h, g2_ref[...])
        xn_ref[ts, :] = _pack_pairs(xn)
        return xn.astype(BF16)

    def logits(xnb):
        return lax.dot_general(wr_ref[...], xnb, (((1,), (1,)), ((), ())), preferred_element_type=F32) + br_ref[...]

    whole = slice(0, tm)
    lt = logits(residual_norm(whole, mix(whole, branch_norm(whole))))
    iota8 = lax.broadcasted_iota(jnp.int32, (N_GROUPS, tm), 0)

    def first_argmax(v):
        top = jnp.max(v, axis=0, keepdims=True)
        idx = jnp.min(jnp.where(v == top, iota8, N_GROUPS), axis=0, keepdims=True)
        return top, idx

    gl = lt[0:N_GROUPS]
    gmax, gidx = first_argmax(gl)
    gw = 1.0 / jnp.sum(jnp.exp(gl - gmax), axis=0, keepdims=True)
    esel = jnp.zeros((EXPERTS_PER_GROUP, tm), F32)
    for g in range(N_GROUPS):
        lo = N_GROUPS + g * EXPERTS_PER_GROUP
        esel = jnp.where(gidx == g, lt[lo:lo + EXPERTS_PER_GROUP], esel)
    v1, i1 = first_argmax(esel)
    rest = jnp.where(iota8 == i1, -jnp.inf, esel)
    v2, i2 = first_argmax(rest)
    t = jnp.exp(v2 - v1)
    den = 1.0 + t
    e1 = gidx * EXPERTS_PER_GROUP + i1
    e2 = gidx * EXPERTS_PER_GROUP + i2
    eid_ref[0:1, :] = e1
    eid_ref[1:2, :] = e2
    gate_ref[0:1, :] = gw * (1.0 / den)
    gate_ref[1:2, :] = gw * (t / den)

    iota_e = lax.broadcasted_iota(jnp.int32, (N_EXPERTS, tm), 0)
    oh1 = iota_e == e1
    oh2 = iota_e == e2
    both = jnp.where(oh1, 1.0, jnp.where(oh2, 1.0, 0.0)).astype(BF16)
    prefix = jnp.dot(both, tri_ref[...], preferred_element_type=F32)
    run = run_scr[...]
    base = prefix - 1.0 + run[:, 0:1]
    rank_ref[0:1, :] = jnp.sum(jnp.where(oh1, base, 0.0), axis=0, keepdims=True).astype(jnp.int32)
    rank_ref[1:2, :] = jnp.sum(jnp.where(oh2, base, 0.0), axis=0, keepdims=True).astype(jnp.int32)
    run = run + prefix[:, tm - 1:tm]
    run_scr[...] = run
    cnt_ref[...] = run


def _out_proj(xt, o_gla, o_att_t, wo, ag, g2, wr, br, tm):
    N = xt.shape[0]
    tpb = o_att_t.shape[2] // tm
    const = lambda shape: pl.BlockSpec(shape, lambda i: (0,) * len(shape))
    rows = lambda width: pl.BlockSpec((tm, width), lambda i: (i, 0))
    cols = pl.BlockSpec((TOP_K, tm), lambda i: (0, i))
    tri = jnp.asarray(np.triu(np.ones((tm, tm), np.float32)), BF16)
    return pl.pallas_call(
        _out_proj_kernel,
        out_shape=(jax.ShapeDtypeStruct((N, D_MODEL), F32),
                   jax.ShapeDtypeStruct((N, D_PACKED), jnp.uint32),
                   jax.ShapeDtypeStruct((TOP_K, N), jnp.int32),
                   jax.ShapeDtypeStruct((TOP_K, N), F32),
                   jax.ShapeDtypeStruct((TOP_K, N), jnp.int32),
                   jax.ShapeDtypeStruct((N_EXPERTS, LANES), F32)),
        grid=(N // tm,),
        in_specs=[rows(D_MODEL), rows(GLA_WIDTH),
                  pl.BlockSpec((1, ATT_WIDTH, tm), lambda i: (i // tpb, 0, i % tpb)),
                  const((D_MODEL, D_MODEL)),
                  const((ATT_WIDTH, 1)), const((1, D_MODEL)), const((LANES, D_MODEL)),
                  const((LANES, 1)), const((tm, tm))],
        out_specs=(rows(D_MODEL), rows(D_PACKED), cols, cols, cols, const((N_EXPERTS, LANES))),
        scratch_shapes=[pltpu.VMEM((N_EXPERTS, LANES), F32)],
        compiler_params=_cparams(("arbitrary",)),
        name="out_proj",
    )(xt, o_gla, o_att_t, wo, ag, g2, wr, br, tri)


D_PACKED = D_MODEL // 2


def _pack_pairs(x):
    w = x.shape[1] // 2
    bits = pltpu.bitcast(x.astype(BF16).astype(F32), jnp.uint32)
    return bits[:, :w] | (bits[:, w:] >> 16)


def _unpack_pairs(words):
    hi = pltpu.bitcast(words & jnp.uint32(0xFFFF0000), F32)
    lo = pltpu.bitcast(words << 16, F32)
    return jnp.concatenate([hi, lo], axis=1)


def _dispatch_kernel(zs_ref, nu_ref, *refs):
    dest_refs = refs[:TOP_K]
    packed = refs[TOP_K]
    w_in = refs[TOP_K + 1:TOP_K + 4]
    xs_hbm = refs[TOP_K + 4]
    w_out = refs[TOP_K + 5:TOP_K + 8]
    zero_buf, zero_sem, row_sem = refs[TOP_K + 8:]
    tn = packed.shape[0]
    i = pl.program_id(0)

    def fill(row0):
        return pltpu.make_async_copy(
            zero_buf, xs_hbm.at[pl.ds(pl.multiple_of(row0, MOE_BLOCK), MOE_BLOCK)], zero_sem)

    @pl.when(i == 0)
    def _():
        zero_buf[...] = jnp.zeros_like(zero_buf)
        fills = [fill(zs_ref[e]) for e in range(N_EXPERTS)]
        for f in fills:
            f.start()
        for f in fills:
            f.wait()

        def fill_tail(blk, carry):
            f = fill(blk * MOE_BLOCK)
            f.start()
            f.wait()
            return carry

        lax.fori_loop(nu_ref[0], xs_hbm.shape[0] // MOE_BLOCK, fill_tail, 0)

    n_iter = tn // 16
    epw = w_in[0].shape[0]

    def issue(it, carry):
        chunks = []
        for src, dst in zip(w_in, w_out):
            rows = src.shape[1] // n_iter
            r0 = pl.multiple_of(it * rows, rows)
            for e in range(epw):
                chunks.append((dst, e, r0, rows, src[e, pl.ds(r0, rows), :]))
        for u in range(2):
            g = it * 2 + u
            tile = packed.at[pl.ds(pl.multiple_of(g * 8, 8), 8)]
            for s in range(8):
                for k in range(TOP_K):
                    pltpu.make_async_copy(tile.at[pl.ds(s, 1)], xs_hbm.at[pl.ds(dest_refs[k][g * 8 + s], 1)],
                                          row_sem).start(priority=k % 2)
        for dst, e, r0, rows, val in chunks:
            dst[e, pl.ds(r0, rows), :] = val.astype(BF16)
        return carry

    lax.fori_loop(0, n_iter, issue, 0)
    for k in range(TOP_K):
        pltpu.make_async_copy(packed, xs_hbm.at[pl.ds(0, tn)], row_sem).wait()


def _dest_specs(n_steps, tn, index):
    return [pl.BlockSpec((tn,), lambda i, *_, k=k: (k * n_steps + index(i),), memory_space=pltpu.SMEM)
            for k in range(TOP_K)]


def _dispatch(zero_starts, n_used, dest_flat, xn, weights, P, tn):
    N = xn.shape[0]
    n_steps = N // tn
    assert N_EXPERTS % n_steps == 0
    epw = N_EXPERTS // n_steps
    w_specs = [pl.BlockSpec((epw,) + w.shape[1:], lambda i, *_: (i, 0, 0)) for w in weights]
    for w in weights:
        assert w.shape[1] % (16 * (tn // 16)) == 0
    grid_spec = pltpu.PrefetchScalarGridSpec(
        num_scalar_prefetch=2,
        grid=(n_steps,),
        in_specs=_dest_specs(n_steps, tn, lambda i: i) + [pl.BlockSpec((tn, D_PACKED), lambda i, zs, nu: (i, 0))]
        + w_specs,
        out_specs=[pl.BlockSpec(memory_space=pl.ANY)] + w_specs,
        scratch_shapes=[pltpu.VMEM((MOE_BLOCK, D_PACKED), jnp.uint32),
                        pltpu.SemaphoreType.DMA, pltpu.SemaphoreType.DMA],
    )
    xs, *w_bf16 = pl.pallas_call(
        _dispatch_kernel,
        out_shape=[jax.ShapeDtypeStruct((P, D_PACKED), jnp.uint32)]
        + [jax.ShapeDtypeStruct(w.shape, BF16) for w in weights],
        grid_spec=grid_spec,
        compiler_params=_cparams(("arbitrary",)),
        name="dispatch",
    )(zero_starts, n_used, *([dest_flat] * TOP_K), xn, *weights)
    return xs, w_bf16


def _moe_ffn_kernel(be_ref, nu_ref, seg_ref, nxt_ref, xs_ref, wg_hbm, wu_hbm, wd_hbm, ys_ref,
                    wg_b, wu_b, wd_b, sems):
    i = pl.program_id(0)
    used = i < nu_ref[0]

    def weights(expert, slot):
        return [pltpu.make_async_copy(src.at[expert], dst.at[slot], sems.at[slot])
                for src, dst in ((wg_hbm, wg_b), (wu_hbm, wu_b), (wd_hbm, wd_b))]

    @pl.when(used)
    def _():
        new_expert = jnp.logical_or(i == 0, be_ref[i] != be_ref[jnp.maximum(i - 1, 0)])
        slot = seg_ref[i] % 2

        @pl.when(new_expert)
        def _():
            @pl.when(i == 0)
            def _():
                for cp in weights(be_ref[0], 0):
                    cp.start()

            for cp in weights(be_ref[i], slot):
                cp.wait()

            @pl.when(nxt_ref[i] >= 0)
            def _():
                for cp in weights(nxt_ref[i], 1 - slot):
                    cp.start()

        xb = _unpack_pairs(xs_ref[...]).astype(BF16)
        hg = jnp.dot(xb, wg_b[slot], preferred_element_type=F32)
        hu = jnp.dot(xb, wu_b[slot], preferred_element_type=F32)
        hm = (hg * _sigmoid(hg) * hu).astype(BF16)
        ys_ref[...] = _pack_pairs(jnp.dot(hm, wd_b[slot], preferred_element_type=F32))

    @pl.when(jnp.logical_not(used))
    def _():
        ys_ref[...] = jnp.zeros_like(ys_ref)


def _moe_ffn(blk_expert, n_used, xs, w_gate, w_up, w_down):
    P = xs.shape[0]
    nblk = P // MOE_BLOCK
    blk = jnp.arange(nblk, dtype=jnp.int32)
    starts = jnp.logical_and(jnp.concatenate([jnp.ones((1,), bool), blk_expert[1:] != blk_expert[:-1]]),
                             blk < n_used[0])
    seg = jnp.cumsum(starts.astype(jnp.int32)) - 1
    next_start = lax.cummin(jnp.where(starts, blk, nblk), reverse=True)
    next_start = jnp.concatenate([next_start[1:], jnp.full((1,), nblk, jnp.int32)])
    nxt = jnp.where(next_start < nblk, blk_expert[jnp.minimum(next_start, nblk - 1)], -1).astype(jnp.int32)
    row_block = pl.BlockSpec((MOE_BLOCK, D_PACKED), lambda i, *_: (i, 0))
    grid_spec = pltpu.PrefetchScalarGridSpec(
        num_scalar_prefetch=4,
        grid=(nblk,),
        in_specs=[row_block] + [pl.BlockSpec(memory_space=pl.ANY)] * 3,
        out_specs=row_block,
        scratch_shapes=[pltpu.VMEM((2, D_MODEL, D_EXPERT), BF16),
                        pltpu.VMEM((2, D_MODEL, D_EXPERT), BF16),
                        pltpu.VMEM((2, D_EXPERT, D_MODEL), BF16),
                        pltpu.SemaphoreType.DMA((2,))],
    )
    return pl.pallas_call(
        _moe_ffn_kernel,
        out_shape=jax.ShapeDtypeStruct((P, D_PACKED), jnp.uint32),
        grid_spec=grid_spec,
        compiler_params=_cparams(("arbitrary",)),
        name="moe_ffn",
    )(blk_expert, n_used, seg, nxt, xs, w_gate, w_up, w_down)


COMBINE_ROWS = 32

def _combine_kernel(*refs, final):
    dcur_refs, dnext_refs = refs[:TOP_K], refs[TOP_K:2 * TOP_K]
    h_ref, gate_ref, fg_ref, ys_hbm, o_ref, ybuf, sems = refs[2 * TOP_K:]
    tm = h_ref.shape[0]
    i = pl.program_id(0)
    n = pl.num_programs(0)

    slot = i % 2
    fg = fg_ref[...]

    def issue(drefs, to_slot, g):
        for k in range(TOP_K):
            tile = ybuf.at[to_slot, pl.ds(pl.multiple_of(k * tm + g * 8, 8), 8)]
            for s in range(8):
                pltpu.make_async_copy(ys_hbm.at[pl.ds(drefs[k][g * 8 + s], 1)],
                                      tile.at[pl.ds(s, 1)], sems.at[to_slot]).start(priority=k % 2)

    tiles = COMBINE_ROWS // 8

    def load_rows(j):
        r0 = pl.multiple_of(j * COMBINE_ROWS, COMBINE_ROWS)
        packed = [ybuf[slot, pl.ds(pl.multiple_of(k * tm + j * COMBINE_ROWS, COMBINE_ROWS), COMBINE_ROWS), :]
                  for k in range(TOP_K)]
        return r0, gate_ref[pl.ds(r0, COMBINE_ROWS), :], h_ref[pl.ds(r0, COMBINE_ROWS), :], packed

    def finish_rows(loaded):
        r0, gate, hh, packed = loaded
        for k in range(TOP_K):
            hh = hh + _unpack_pairs(packed[k]) * gate[:, k:k + 1]
        o_ref[pl.ds(r0, COMBINE_ROWS), :] = _rms(hh, fg) if final else hh

    @pl.when(i == 0)
    def _():
        lax.fori_loop(0, tm // 8, lambda g, c: (issue(dcur_refs, 0, g), c)[1], 0)

    pltpu.make_async_copy(ys_hbm.at[pl.ds(0, TOP_K * tm)], ybuf.at[slot], sems.at[slot]).wait()

    @pl.when(i + 1 < n)
    def _():
        def body(j, c):
            loaded = load_rows(j)
            for u in range(tiles):
                issue(dnext_refs, 1 - slot, j * tiles + u)
            finish_rows(loaded)
            return c

        lax.fori_loop(0, tm // COMBINE_ROWS, body, 0)

    @pl.when(i + 1 == n)
    def _():
        lax.fori_loop(0, tm // COMBINE_ROWS, lambda j, c: (finish_rows(load_rows(j)), c)[1], 0)


def _combine(h, ys, dest_flat, gate_t, fg, tm, final):
    N = h.shape[0]
    n = N // tm
    rows = pl.BlockSpec((tm, D_MODEL), lambda i: (i, 0))
    return pl.pallas_call(
        functools.partial(_combine_kernel, final=final),
        out_shape=jax.ShapeDtypeStruct((N, D_MODEL), F32),
        grid=(n,),
        in_specs=_dest_specs(n, tm, lambda i: i) + _dest_specs(n, tm, lambda i: jnp.minimum(i + 1, n - 1)) + [
                  rows, pl.BlockSpec((tm, TOP_K), lambda i: (i, 0)),
                  pl.BlockSpec((1, D_MODEL), lambda i: (0, 0)),
                  pl.BlockSpec(memory_space=pl.ANY)],
        out_specs=rows,
        scratch_shapes=[pltpu.VMEM((2, TOP_K * tm, D_PACKED), jnp.uint32), pltpu.SemaphoreType.DMA((2,))],
        compiler_params=_cparams(("arbitrary",)),
        name="combine",
    )(*([dest_flat] * (2 * TOP_K)), h, gate_t, fg, ys)


def _reorder_w_in(w_in):
    n_gla = 2 * GLA_KEY_WIDTH + 2 * GLA_WIDTH
    z0 = n_gla
    a0 = z0 + 2 * GLA_GATE_RANK
    pad = jnp.zeros((D_MODEL, Z_PAD - 2 * GLA_GATE_RANK), w_in.dtype)
    return jnp.concatenate([w_in[:, :n_gla], w_in[:, a0:], w_in[:, z0:a0], pad], axis=1)


def kernel(x, norm1_gain, w_in, gla_up_fwd, gla_up_fwd_bias, gla_up_bwd, gla_up_bwd_bias, gla_out_gain, q_norm_gain, k_norm_gain, att_out_gain, w_out, norm2_gain, w_group, b_group, w_expert, b_expert, w_gate, w_up, w_down, final_gain):
    B, T, D = x.shape
    N = B * T
    depth = norm1_gain.shape[0]
    tm = min(512, T)
    tq = 128
    key_block = min(256, T)
    h = x.reshape(N, D)
    cos, se, so = _rope_tables(T)
    head_mean = jnp.asarray(
        np.kron(np.eye(ATT_HEADS, dtype=np.float32),
                np.full((ATT_HEAD_DIM, ATT_HEAD_DIM), 1.0 / ATT_HEAD_DIM, np.float32)), BF16)
    for l in range(depth):
        w = _reorder_w_in(w_in[l]).astype(BF16)
        r = GLA_GATE_RANK
        up = jnp.zeros((Z_PAD, 2 * GLA_KEY_WIDTH), F32)
        up = up.at[:r, :GLA_KEY_WIDTH].set(gla_up_fwd[l]).at[r:2 * r, GLA_KEY_WIDTH:].set(gla_up_bwd[l])
        upb = jnp.concatenate([gla_up_fwd_bias[l], gla_up_bwd_bias[l]])[None, :]
        qg = jnp.tile(q_norm_gain[l], ATT_HEADS)[None, :]
        kg = jnp.tile(k_norm_gain[l], ATT_KV_HEADS)[None, :]
        gq, gk, gv, gg, laf, lab, aqt, ak, avt = _in_proj(
            h, B, T, tm, norm1_gain[l][None, :], w, up.astype(BF16), upb, qg, kg, head_mean, cos, se, so)
        o_gla = _gla(gq, gk, gv, gg, laf, lab, gla_out_gain[l][None, :], B, T)
        o_att_t = _attn(aqt, ak, avt, B, T, tq, key_block)

        wr = jnp.zeros((LANES, D), F32)
        wr = wr.at[:N_GROUPS].set(w_group[l].T).at[N_GROUPS:N_GROUPS + N_EXPERTS].set(w_expert[l].T)
        br = jnp.zeros((LANES, 1), F32)
        br = br.at[:N_GROUPS, 0].set(b_group[l]).at[N_GROUPS:N_GROUPS + N_EXPERTS, 0].set(b_expert[l])
        h, xn, eid, gate, rank, cnt = _out_proj(
            h, o_gla, o_att_t, w_out[l].astype(BF16), att_out_gain[l][:, None], norm2_gain[l][None, :],
            wr.astype(BF16), br, tm)

        counts = cnt[:, 0].astype(jnp.int32)
        padded = (counts + MOE_BLOCK - 1) // MOE_BLOCK * MOE_BLOCK
        pad_ends = jnp.cumsum(padded)
        pad_starts = pad_ends - padded
        seg_start = jnp.sum(jnp.where(eid[:, :, None] == jnp.arange(N_EXPERTS, dtype=jnp.int32),
                                      pad_starts.astype(jnp.int32), 0), axis=-1)
        dest = (seg_start + rank).reshape(-1)
        P = N * TOP_K + N_EXPERTS * MOE_BLOCK
        nblk = P // MOE_BLOCK
        blk_start = jnp.arange(nblk, dtype=jnp.int32) * MOE_BLOCK
        blk_expert = jnp.minimum(
            jnp.sum((pad_ends[None, :] <= blk_start[:, None]).astype(jnp.int32), axis=1), N_EXPERTS - 1)
        n_used = (pad_ends[-1:] // MOE_BLOCK).astype(jnp.int32)

        zero_starts = jnp.maximum(pad_ends - MOE_BLOCK, 0).astype(jnp.int32)
        xs, expert_w = _dispatch(zero_starts, n_used, dest, xn, (w_gate[l], w_up[l], w_down[l]), P, tm)
        ys = _moe_ffn(blk_expert, n_used, xs, *expert_w)
        h = _combine(h, ys, dest, gate.T, final_gain[None, :], tm, l == depth - 1)
    return h.reshape(B, T, D)
```

```python
import functools

import numpy as np
import jax
import jax.numpy as jnp
from jax import lax
from jax.experimental import pallas as pl
from jax.experimental.pallas import tpu as pltpu

F32 = jnp.float32
BF16 = jnp.bfloat16

D_MODEL = 1024
EPS = 1e-6
GRID_W = 64

GLA_HEADS = 4
GLA_DK = 64
GLA_DV = 128
GLA_KEY_WIDTH = GLA_HEADS * GLA_DK
GLA_WIDTH = GLA_HEADS * GLA_DV
GLA_GATE_RANK = 16
GLA_GATE_NORM = 16.0
GLA_CHUNK = 64

ATT_HEADS = 8
ATT_KV_HEADS = 2
ATT_HEAD_DIM = 64
ATT_GROUP = ATT_HEADS // ATT_KV_HEADS
ATT_WIDTH = ATT_HEADS * ATT_HEAD_DIM
ATT_KV_WIDTH = ATT_KV_HEADS * ATT_HEAD_DIM
ROPE_THETA = 10000.0
LOG2_E = 1.4426950408889634

N_GROUPS = 8
EXPERTS_PER_GROUP = 8
N_EXPERTS = N_GROUPS * EXPERTS_PER_GROUP
TOP_K = 2
D_EXPERT = 512
MOE_BLOCK = 256

LANES = 128
Z_PAD = LANES

_OFF_GQ = 0
_OFF_GK = _OFF_GQ + GLA_KEY_WIDTH
_OFF_GV = _OFF_GK + GLA_KEY_WIDTH
_OFF_GG = _OFF_GV + GLA_WIDTH
_OFF_AQ = _OFF_GG + GLA_WIDTH
_OFF_AK = _OFF_AQ + ATT_WIDTH
_OFF_AV = _OFF_AK + ATT_KV_WIDTH
_OFF_Z = _OFF_AV + ATT_KV_WIDTH
D_IN_PAD = _OFF_Z + Z_PAD

VMEM_LIMIT = 56 * 1024 * 1024


def _cparams(semantics):
    return pltpu.CompilerParams(dimension_semantics=semantics, vmem_limit_bytes=VMEM_LIMIT)


def _rms(x, gain):
    return x * lax.rsqrt(jnp.mean(x * x, axis=-1, keepdims=True) + EPS) * gain


def _sigmoid(x):
    return 1.0 / (1.0 + jnp.exp(-x))


def _head_norm_rope(x, head_mean, gain, cos, sin_even, sin_odd):
    w = x.shape[1]
    ms = jnp.dot((x * x).astype(BF16), head_mean, preferred_element_type=F32)
    xn = x * lax.rsqrt(ms + EPS) * gain
    reps = w // LANES
    tile = lambda t: t if reps == 1 else jnp.concatenate([t] * reps, axis=1)
    nxt = pltpu.roll(xn, w - 1, 1)
    prv = pltpu.roll(xn, 1, 1)
    return xn * tile(cos) + nxt * tile(sin_even) + prv * tile(sin_odd)


def _in_proj_kernel(x_ref, g1_ref, w_ref, up_ref, upb_ref, qg_ref, kg_ref, hm_ref,
                    cos_ref, se_ref, so_ref,
                    gq_ref, gk_ref, gv_ref, gg_ref, laf_ref, lab_ref, aqt_ref, ak_ref, avt_ref):
    u = _rms(x_ref[...], g1_ref[...]).astype(BF16)

    def proj(lo, hi):
        return jnp.dot(u, w_ref[:, lo:hi], preferred_element_type=F32)

    cos, se, so = cos_ref[...], se_ref[...], so_ref[...]
    hm = hm_ref[...]
    r_q = proj(_OFF_AQ, _OFF_AK)
    r_gqk = proj(_OFF_GQ, _OFF_GV)
    q = _head_norm_rope(r_q, hm, qg_ref[...], cos, se, so)
    aqt_ref[0] = (q * (ATT_HEAD_DIM ** -0.5 * LOG2_E)).T.astype(BF16)
    gq_ref[...] = r_gqk[:, :GLA_KEY_WIDTH].astype(BF16)
    gk_ref[...] = r_gqk[:, GLA_KEY_WIDTH:].astype(BF16)

    r_kvz = proj(_OFF_AK, D_IN_PAD)
    r_gv = proj(_OFF_GV, _OFF_GG)
    k = _head_norm_rope(r_kvz[:, :ATT_KV_WIDTH], hm[:ATT_KV_WIDTH, :ATT_KV_WIDTH], kg_ref[...], cos, se, so)
    ak_ref[...] = k.astype(BF16)
    avt_ref[0] = r_kvz[:, ATT_KV_WIDTH:2 * ATT_KV_WIDTH].T.astype(BF16)
    gv_ref[...] = r_gv.astype(BF16)

    z = r_kvz[:, 2 * ATT_KV_WIDTH:].astype(BF16)
    zl = jnp.dot(z, up_ref[...], preferred_element_type=F32) + upb_ref[...]
    r_gg = proj(_OFF_GG, _OFF_AQ)
    la = (jnp.minimum(zl, 0.0) - jnp.log(1.0 + jnp.exp(-jnp.abs(zl)))) * (1.0 / GLA_GATE_NORM)
    laf_ref[...] = la[:, :GLA_KEY_WIDTH]
    lab_ref[...] = la[:, GLA_KEY_WIDTH:]
    gg_ref[...] = r_gg.astype(BF16)


def _rope_tables(T):
    t = np.arange(T)
    row = (t // GRID_W).astype(np.float32)
    col = (t % GRID_W).astype(np.float32)
    axis_dim = ATT_HEAD_DIM // 2
    inv_freq = (ROPE_THETA ** (-np.arange(0, axis_dim, 2, dtype=np.float32) / axis_dim)).astype(np.float32)
    ang = np.concatenate([row[:, None] * inv_freq, col[:, None] * inv_freq], axis=-1)
    ang = np.repeat(ang, 2, axis=1)
    ang = np.tile(ang, (1, LANES // ATT_HEAD_DIM))
    even = (np.arange(LANES) % 2 == 0)[None, :]
    cos = np.cos(ang)
    sin = np.sin(ang)
    return (jnp.asarray(cos, F32), jnp.asarray(np.where(even, -sin, 0.0), F32),
            jnp.asarray(np.where(even, 0.0, sin), F32))


def _in_proj(xt, B, T, tm, g1, w, up, upb, qg, kg, hm, cos, se, so):
    N = xt.shape[0]
    tpb = T // tm
    const = lambda shape: pl.BlockSpec(shape, lambda i: (0,) * len(shape))
    rows = lambda width: pl.BlockSpec((tm, width), lambda i: (i, 0))
    pos = pl.BlockSpec((tm, LANES), lambda i: (i % tpb, 0))
    out_shape = (
        jax.ShapeDtypeStruct((N, GLA_KEY_WIDTH), BF16),
        jax.ShapeDtypeStruct((N, GLA_KEY_WIDTH), BF16),
        jax.ShapeDtypeStruct((N, GLA_WIDTH), BF16),
        jax.ShapeDtypeStruct((N, GLA_WIDTH), BF16),
        jax.ShapeDtypeStruct((N, GLA_KEY_WIDTH), F32),
        jax.ShapeDtypeStruct((N, GLA_KEY_WIDTH), F32),
        jax.ShapeDtypeStruct((B, ATT_WIDTH, T), BF16),
        jax.ShapeDtypeStruct((N, ATT_KV_WIDTH), BF16),
        jax.ShapeDtypeStruct((B, ATT_KV_WIDTH, T), BF16),
    )
    cols = lambda width: pl.BlockSpec((1, width, tm), lambda i: (i // tpb, 0, i % tpb))
    out_specs = (
        rows(GLA_KEY_WIDTH), rows(GLA_KEY_WIDTH), rows(GLA_WIDTH), rows(GLA_WIDTH),
        rows(GLA_KEY_WIDTH), rows(GLA_KEY_WIDTH),
        cols(ATT_WIDTH), rows(ATT_KV_WIDTH), cols(ATT_KV_WIDTH),
    )
    return pl.pallas_call(
        _in_proj_kernel,
        out_shape=out_shape,
        grid=(N // tm,),
        in_specs=[rows(D_MODEL), const((1, D_MODEL)), const((D_MODEL, D_IN_PAD)),
                  const((Z_PAD, 2 * GLA_KEY_WIDTH)), const((1, 2 * GLA_KEY_WIDTH)),
                  const((1, ATT_WIDTH)), const((1, ATT_KV_WIDTH)), const((ATT_WIDTH, ATT_WIDTH)),
                  pos, pos, pos],
        out_specs=out_specs,
        compiler_params=_cparams(("arbitrary",)),
        name="in_proj",
    )(xt, g1, w, up, upb, qg, kg, hm, cos, se, so)


GLA_GROUP = 4
GLA_FINISH_GROUP = 16


def _gla_kernel(q_ref, k_ref, v_ref, g_ref, laf_ref, lab_ref, gain_ref, o_ref,
                part_scr, qif_scr, qib_scr, kvf_scr, kvb_scr, decf_scr, decb_scr, lf_scr, lb_scr):
    C, G = GLA_CHUNK, GLA_GROUP
    R = C * G
    T = q_ref.shape[0]
    n_chunks = T // C
    gain = gain_ref[...]
    row = lax.broadcasted_iota(jnp.int32, (C, C), 0)
    col = lax.broadcasted_iota(jnp.int32, (C, C), 1)
    tril = row >= col
    triu = row <= col
    rr = lax.broadcasted_iota(jnp.int32, (R, R), 0)
    cc = lax.broadcasted_iota(jnp.int32, (R, R), 1)
    same_chunk = (rr // C) == (cc // C)
    lf_scr[...] = jnp.where(same_chunk & (rr >= cc), 1.0, 0.0).astype(BF16)
    lb_scr[...] = jnp.where(same_chunk & (rr <= cc), 1.0, 0.0).astype(BF16)

    directions = ((laf_ref, lf_scr, tril, C // 2 - 1, C - 1, qif_scr, kvf_scr, decf_scr),
                  (lab_ref, lb_scr, triu, C // 2, 0, qib_scr, kvb_scr, decb_scr))
    heads = tuple((slice(h * GLA_DK, (h + 1) * GLA_DK), slice(h * GLA_DV, (h + 1) * GLA_DV)) for h in range(2))
    nt = (((1,), (1,)), ((), ()))
    tn = (((0,), (0,)), ((), ()))

    def intra(sb, carry):
        r0 = pl.multiple_of(sb * R, R)
        q2 = q_ref[pl.ds(r0, R), :].astype(F32) * (GLA_DK ** -0.5)
        k2 = k_ref[pl.ds(r0, R), :].astype(F32)
        v2 = v_ref[pl.ds(r0, R), :]
        cums = []
        for la_ref, tri_scr, *_ in directions:
            la = la_ref[pl.ds(r0, R), :]
            la_hi = la.astype(BF16)
            la_lo = (la - la_hi.astype(F32)).astype(BF16)
            both = jnp.dot(tri_scr[...], jnp.concatenate([la_hi, la_lo], axis=1), preferred_element_type=F32)
            cums.append(both[:, :2 * GLA_DK] + both[:, 2 * GLA_DK:])
        scaled = []
        for b, (_, _, mask, i_ref, i_last, qi_scr, kv_scr, dec_scr) in zip(cums, directions):
            for c in range(G):
                rows = slice(c * C, (c + 1) * C)
                bc, qc, kc = b[rows], q2[rows], k2[rows]
                b_ref = bc[i_ref:i_ref + 1, :]
                b_last = bc[i_last:i_last + 1, :]
                qf = (qc * jnp.exp(bc - b_ref)).astype(BF16)
                kf = (kc * jnp.exp(b_ref - bc)).astype(BF16)
                qi_scr[pl.ds(r0 + c * C, C), :] = (qc * jnp.exp(bc)).astype(BF16)
                kl = (kc * jnp.exp(b_last - bc)).astype(BF16)
                decay = jnp.exp(b_last)
                for h, (ks, _) in enumerate(heads):
                    dec_scr[sb * G + c, h] = decay[:, ks]
                scaled.append((c, mask, kv_scr, qf, kf, kl))
        raw = [[lax.dot_general(qf[:, ks], kf[:, ks], nt, preferred_element_type=F32) for ks, _ in heads]
               for _, _, _, qf, kf, _ in scaled]
        o_intra = [[None, None] for _ in range(G)]
        for (c, mask, kv_scr, _, _, kl), sc2 in zip(scaled, raw):
            rows = slice(c * C, (c + 1) * C)
            for h, (ks, vs) in enumerate(heads):
                vh = v2[rows, vs]
                sc = jnp.where(mask, sc2[h], 0.0).astype(BF16)
                oi = jnp.dot(sc, vh, preferred_element_type=F32)
                o_intra[c][h] = oi if o_intra[c][h] is None else o_intra[c][h] + oi
                kv_scr[sb * G + c, h] = lax.dot_general(vh, kl[:, ks], tn, preferred_element_type=F32)
        for c in range(G):
            part_scr[pl.ds(r0 + c * C, C), :] = jnp.concatenate(o_intra[c], axis=1)
        return carry

    lax.fori_loop(0, n_chunks // G, intra, 0)

    def scan(n, carry):
        sf, sb = carry
        nb = n_chunks - 1 - n
        new_f, new_b = [], []
        for h in range(2):
            kv = kvf_scr[n, h]
            kvf_scr[n, h] = sf[h]
            new_f.append(decf_scr[n, h] * sf[h] + kv)
            kv = kvb_scr[nb, h]
            kvb_scr[nb, h] = sb[h]
            new_b.append(decb_scr[nb, h] * sb[h] + kv)
        return tuple(new_f), tuple(new_b)

    zero_state = tuple(jnp.zeros((GLA_DV, GLA_DK), F32) for _ in range(2))
    lax.fori_loop(0, n_chunks, scan, (zero_state, zero_state), unroll=4)

    GF = min(GLA_FINISH_GROUP, n_chunks)
    RF = C * GF

    def finish(sb, carry):
        r0 = pl.multiple_of(sb * RF, RF)
        g = g_ref[pl.ds(r0, RF), :].astype(F32)
        part = part_scr[pl.ds(r0, RF), :]
        qf = qif_scr[pl.ds(r0, RF), :]
        qb = qib_scr[pl.ds(r0, RF), :]
        inter = []
        for c in range(GF):
            rows = slice(c * C, (c + 1) * C)
            inter.append([
                lax.dot_general(qf[rows, ks], kvf_scr[sb * GF + c, h].astype(BF16), nt, preferred_element_type=F32)
                + lax.dot_general(qb[rows, ks], kvb_scr[sb * GF + c, h].astype(BF16), nt, preferred_element_type=F32)
                for h, (ks, _) in enumerate(heads)])
        gate = g * _sigmoid(g)
        for c in range(GF):
            rows = slice(c * C, (c + 1) * C)
            normed = jnp.concatenate([_rms(part[rows, vs] + inter[c][h], gain) for h, (_, vs) in enumerate(heads)],
                                     axis=1)
            o_ref[pl.ds(r0 + c * C, C), :] = (normed * gate[rows]).astype(BF16)
        return carry

    lax.fori_loop(0, n_chunks // GF, finish, 0)


def _gla(gq, gk, gv, gg, laf, lab, gain, B, T):
    N = gq.shape[0]
    pairs = GLA_HEADS // 2
    n_chunks = T // GLA_CHUNK
    assert n_chunks % GLA_GROUP == 0 and n_chunks % min(GLA_FINISH_GROUP, n_chunks) == 0
    group_rows = GLA_CHUNK * GLA_GROUP
    kspec = pl.BlockSpec((T, 2 * GLA_DK), lambda b, p: (b, p))
    vspec = pl.BlockSpec((T, 2 * GLA_DV), lambda b, p: (b, p))
    state = pltpu.VMEM((n_chunks, 2, GLA_DV, GLA_DK), F32)
    decay = pltpu.VMEM((n_chunks, 2, 1, GLA_DK), F32)
    return pl.pallas_call(
        _gla_kernel,
        out_shape=jax.ShapeDtypeStruct((N, GLA_WIDTH), BF16),
        grid=(B, pairs),
        in_specs=[kspec, kspec, vspec, vspec, kspec, kspec,
                  pl.BlockSpec((1, GLA_DV), lambda b, p: (0, 0))],
        out_specs=vspec,
        scratch_shapes=[pltpu.VMEM((T, 2 * GLA_DV), F32),
                        pltpu.VMEM((T, 2 * GLA_DK), BF16), pltpu.VMEM((T, 2 * GLA_DK), BF16),
                        state, state, decay, decay,
                        pltpu.VMEM((group_rows, group_rows), BF16), pltpu.VMEM((group_rows, group_rows), BF16)],
        compiler_params=_cparams(("arbitrary", "arbitrary")),
        name="gla",
    )(gq, gk, gv, gg, laf, lab, gain)


def _attn_kernel(qt_ref, k_ref, vt_ref, ot_ref, vext_scr, s0_scr, s1_scr, p0_scr, p1_scr, m0_scr, m1_scr,
                 *, tq, key_block, steps_per_iter):
    T = k_ref.shape[0]
    nq = T // tq
    n_blocks = ATT_KV_HEADS * nq
    dh = ATT_HEAD_DIM
    gw = ATT_GROUP * dh
    cols = ATT_GROUP * tq

    row = lax.broadcasted_iota(jnp.int32, (ATT_KV_WIDTH, T), 0)
    for hj in range(ATT_KV_HEADS):
        own = (row >= hj * dh) & (row < (hj + 1) * dh)
        vext_scr[hj] = jnp.where(own, vt_ref[0], jnp.ones((), BF16))

    s_scr, p_scr, m_scr = (s0_scr, s1_scr), (p0_scr, p1_scr), (m0_scr, m1_scr)

    def locate(blk):
        if isinstance(blk, int):
            hj = blk // nq
            return hj, hj == 0, (blk - hj * nq) * tq
        second = blk >= nq
        hj = jnp.where(second, 1, 0)
        return hj, jnp.logical_not(second), pl.multiple_of((blk - hj * nq) * tq, tq)

    def step(blk, sa, do_scores=True, do_probs=True, do_out=True):
        sb = 1 - sa
        if do_scores:
            hj, first, c0 = locate(blk)
            q4 = jnp.concatenate(
                [qt_ref[0, pl.ds(pl.multiple_of(hj * gw + h * dh, dh), dh), pl.ds(c0, tq)] for h in range(ATT_GROUP)],
                axis=1)
            zeros = jnp.zeros_like(q4)
            qe = jnp.where(first, jnp.concatenate([q4, zeros], axis=0), jnp.concatenate([zeros, q4], axis=0))
            m = jnp.full((8, cols), -jnp.inf, F32)
        if do_probs:
            mx = m_scr[sb][...]
        if do_out:
            hj_o, first_o, c0_o = locate(blk - 2)
            acc = jnp.zeros((ATT_KV_WIDTH, cols), F32)
        for kb in range(T // key_block):
            ks = slice(kb * key_block, (kb + 1) * key_block)
            if do_scores:
                s = jnp.dot(k_ref[ks, :], qe, preferred_element_type=F32)
                s_scr[sa][ks, :] = s
                m = jnp.maximum(m, jnp.max(s.reshape(key_block // 8, 8, cols), axis=0))
            if do_probs:
                p_scr[sb][ks, :] = jnp.exp2(s_scr[sb][ks, :] - mx).astype(BF16)
            if do_out:
                acc = acc + jnp.dot(vext_scr[hj_o, :, ks], p_scr[sa][ks, :], preferred_element_type=F32)
        if do_scores:
            m_scr[sa][...] = jnp.max(m, axis=0, keepdims=True)
        if do_out:
            num = jnp.where(first_o, acc[:dh], acc[dh:])
            den = jnp.where(first_o, acc[dh:dh + 1], acc[0:1])
            o = (num / den).astype(BF16)
            for h in range(ATT_GROUP):
                ot_ref[0, pl.ds(pl.multiple_of(hj_o * gw + h * dh, dh), dh), pl.ds(c0_o, tq)] = (
                    o[:, h * tq:(h + 1) * tq])

    step(0, 0, do_probs=False, do_out=False)
    step(1, 1, do_out=False)

    def steady(it, carry):
        for u in range(steps_per_iter):
            pl.when(it >= 0)(functools.partial(step, 2 + steps_per_iter * it + u, u % 2))
        return carry

    lax.fori_loop(0, (n_blocks - 2) // steps_per_iter, steady, 0)
    step(n_blocks, 0, do_scores=False)
    step(n_blocks + 1, 1, do_scores=False, do_probs=False)


def _attn(aqt, ak, avt, B, T, tq, key_block):
    assert T % (2 * tq) == 0 and tq % LANES == 0 and ATT_KV_HEADS == 2
    cols = ATT_GROUP * tq
    steps_per_iter = 2
    return pl.pallas_call(
        functools.partial(_attn_kernel, tq=tq, key_block=key_block, steps_per_iter=steps_per_iter),
        out_shape=jax.ShapeDtypeStruct((B, ATT_WIDTH, T), BF16),
        grid=(B,),
        in_specs=[pl.BlockSpec((1, ATT_WIDTH, T), lambda b: (b, 0, 0)),
                  pl.BlockSpec((T, ATT_KV_WIDTH), lambda b: (b, 0)),
                  pl.BlockSpec((1, ATT_KV_WIDTH, T), lambda b: (b, 0, 0))],
        out_specs=pl.BlockSpec((1, ATT_WIDTH, T), lambda b: (b, 0, 0)),
        scratch_shapes=[pltpu.VMEM((ATT_KV_HEADS, ATT_KV_WIDTH, T), BF16),
                        pltpu.VMEM((T, cols), F32), pltpu.VMEM((T, cols), F32),
                        pltpu.VMEM((T, cols), BF16), pltpu.VMEM((T, cols), BF16),
                        pltpu.VMEM((1, cols), F32), pltpu.VMEM((1, cols), F32)],
        compiler_params=_cparams(("arbitrary",)),
        name="attn",
    )(aqt, ak, avt)


def _out_proj_kernel(x_ref, og_ref, oa_ref, wo_ref, ag_ref, g2_ref, wr_ref, br_ref, tri_ref,
                     h_ref, xn_ref, eid_ref, gate_ref, rank_ref, cnt_ref, run_scr):
    tm = x_ref.shape[0]

    @pl.when(pl.program_id(0) == 0)
    def _():
        run_scr[...] = jnp.zeros_like(run_scr)

    def branch_norm(ts):
        oat = oa_ref[0, :, ts].astype(F32)
        return (oat * lax.rsqrt(jnp.mean(oat * oat, axis=0, keepdims=True) + EPS) * ag_ref[...]).astype(BF16)

    def mix(ts, oan):
        return (jnp.dot(og_ref[ts, :], wo_ref[:GLA_WIDTH, :], preferred_element_type=F32)
                + lax.dot_general(oan, wo_ref[GLA_WIDTH:, :], (((0,), (0,)), ((), ())),
                                  preferred_element_type=F32))

    def residual_norm(ts, y):
        h = x_ref[ts, :] + y
        h_ref[ts, :] = h
        xn = _rms(h, g2_ref[...])
        xn_ref[ts, :] = _pack_pairs(xn)
        return xn.astype(BF16)

    def logits(xnb):
        return lax.dot_general(wr_ref[...], xnb, (((1,), (1,)), ((), ())), preferred_element_type=F32) + br_ref[...]

    whole = slice(0, tm)
    lt = logits(residual_norm(whole, mix(whole, branch_norm(whole))))
    iota8 = lax.broadcasted_iota(jnp.int32, (N_GROUPS, tm), 0)

    def first_argmax(v):
        top = jnp.max(v, axis=0, keepdims=True)
        idx = jnp.min(jnp.where(v == top, iota8, N_GROUPS), axis=0, keepdims=True)
        return top, idx

    gl = lt[0:N_GROUPS]
    gmax, gidx = first_argmax(gl)
    gw = 1.0 / jnp.sum(jnp.exp(gl - gmax), axis=0, keepdims=True)
    esel = jnp.zeros((EXPERTS_PER_GROUP, tm), F32)
    for g in range(N_GROUPS):
        lo = N_GROUPS + g * EXPERTS_PER_GROUP
        esel = jnp.where(gidx == g, lt[lo:lo + EXPERTS_PER_GROUP], esel)
    v1, i1 = first_argmax(esel)
    rest = jnp.where(iota8 == i1, -jnp.inf, esel)
    v2, i2 = first_argmax(rest)
    t = jnp.exp(v2 - v1)
    den = 1.0 + t
    e1 = gidx * EXPERTS_PER_GROUP + i1
    e2 = gidx * EXPERTS_PER_GROUP + i2
    eid_ref[0:1, :] = e1
    eid_ref[1:2, :] = e2
    gate_ref[0:1, :] = gw * (1.0 / den)
    gate_ref[1:2, :] = gw * (t / den)

    iota_e = lax.broadcasted_iota(jnp.int32, (N_EXPERTS, tm), 0)
    oh1 = iota_e == e1
    oh2 = iota_e == e2
    both = jnp.where(oh1, 1.0, jnp.where(oh2, 1.0, 0.0)).astype(BF16)
    prefix = jnp.dot(both, tri_ref[...], preferred_element_type=F32)
    run = run_scr[...]
    base = prefix - 1.0 + run[:, 0:1]
    rank_ref[0:1, :] = jnp.sum(jnp.where(oh1, base, 0.0), axis=0, keepdims=True).astype(jnp.int32)
    rank_ref[1:2, :] = jnp.sum(jnp.where(oh2, base, 0.0), axis=0, keepdims=True).astype(jnp.int32)
    run = run + prefix[:, tm - 1:tm]
    run_scr[...] = run
    cnt_ref[...] = run


def _out_proj(xt, o_gla, o_att_t, wo, ag, g2, wr, br, tm):
    N = xt.shape[0]
    tpb = o_att_t.shape[2] // tm
    const = lambda shape: pl.BlockSpec(shape, lambda i: (0,) * len(shape))
    rows = lambda width: pl.BlockSpec((tm, width), lambda i: (i, 0))
    cols = pl.BlockSpec((TOP_K, tm), lambda i: (0, i))
    tri = jnp.asarray(np.triu(np.ones((tm, tm), np.float32)), BF16)
    return pl.pallas_call(
        _out_proj_kernel,
        out_shape=(jax.ShapeDtypeStruct((N, D_MODEL), F32),
                   jax.ShapeDtypeStruct((N, D_PACKED), jnp.uint32),
                   jax.ShapeDtypeStruct((TOP_K, N), jnp.int32),
                   jax.ShapeDtypeStruct((TOP_K, N), F32),
                   jax.ShapeDtypeStruct((TOP_K, N), jnp.int32),
                   jax.ShapeDtypeStruct((N_EXPERTS, LANES), F32)),
        grid=(N // tm,),
        in_specs=[rows(D_MODEL), rows(GLA_WIDTH),
                  pl.BlockSpec((1, ATT_WIDTH, tm), lambda i: (i // tpb, 0, i % tpb)),
                  const((D_MODEL, D_MODEL)),
                  const((ATT_WIDTH, 1)), const((1, D_MODEL)), const((LANES, D_MODEL)),
                  const((LANES, 1)), const((tm, tm))],
        out_specs=(rows(D_MODEL), rows(D_PACKED), cols, cols, cols, const((N_EXPERTS, LANES))),
        scratch_shapes=[pltpu.VMEM((N_EXPERTS, LANES), F32)],
        compiler_params=_cparams(("arbitrary",)),
        name="out_proj",
    )(xt, o_gla, o_att_t, wo, ag, g2, wr, br, tri)


D_PACKED = D_MODEL // 2


def _pack_pairs(x):
    w = x.shape[1] // 2
    bits = pltpu.bitcast(x.astype(BF16).astype(F32), jnp.uint32)
    return bits[:, :w] | (bits[:, w:] >> 16)


def _unpack_pairs(words):
    hi = pltpu.bitcast(words & jnp.uint32(0xFFFF0000), F32)
    lo = pltpu.bitcast(words << 16, F32)
    return jnp.concatenate([hi, lo], axis=1)


def _dispatch_kernel(zs_ref, nu_ref, *refs):
    dest_refs = refs[:TOP_K]
    packed = refs[TOP_K]
    w_in = refs[TOP_K + 1:TOP_K + 4]
    xs_hbm = refs[TOP_K + 4]
    w_out = refs[TOP_K + 5:TOP_K + 8]
    zero_buf, zero_sem, row_sem = refs[TOP_K + 8:]
    tn = packed.shape[0]
    i = pl.program_id(0)

    def fill(row0):
        return pltpu.make_async_copy(
            zero_buf, xs_hbm.at[pl.ds(pl.multiple_of(row0, MOE_BLOCK), MOE_BLOCK)], zero_sem)

    @pl.when(i == 0)
    def _():
        zero_buf[...] = jnp.zeros_like(zero_buf)
        fills = [fill(zs_ref[e]) for e in range(N_EXPERTS)]
        for f in fills:
            f.start()
        for f in fills:
            f.wait()

        def fill_tail(blk, carry):
            f = fill(blk * MOE_BLOCK)
            f.start()
            f.wait()
            return carry

        lax.fori_loop(nu_ref[0], xs_hbm.shape[0] // MOE_BLOCK, fill_tail, 0)

    n_iter = tn // 16
    epw = w_in[0].shape[0]

    def issue(it, carry):
        chunks = []
        for src, dst in zip(w_in, w_out):
            rows = src.shape[1] // n_iter
            r0 = pl.multiple_of(it * rows, rows)
            for e in range(epw):
                chunks.append((dst, e, r0, rows, src[e, pl.ds(r0, rows), :]))
        for u in range(2):
            g = it * 2 + u
            tile = packed.at[pl.ds(pl.multiple_of(g * 8, 8), 8)]
            for s in range(8):
                for k in range(TOP_K):
                    pltpu.make_async_copy(tile.at[pl.ds(s, 1)], xs_hbm.at[pl.ds(dest_refs[k][g * 8 + s], 1)],
                                          row_sem).start(priority=k % 2)
        for dst, e, r0, rows, val in chunks:
            dst[e, pl.ds(r0, rows), :] = val.astype(BF16)
        return carry

    lax.fori_loop(0, n_iter, issue, 0)
    for k in range(TOP_K):
        pltpu.make_async_copy(packed, xs_hbm.at[pl.ds(0, tn)], row_sem).wait()


def _dest_specs(n_steps, tn, index):
    return [pl.BlockSpec((tn,), lambda i, *_, k=k: (k * n_steps + index(i),), memory_space=pltpu.SMEM)
            for k in range(TOP_K)]


def _dispatch(zero_starts, n_used, dest_flat, xn, weights, P, tn):
    N = xn.shape[0]
    n_steps = N // tn
    assert N_EXPERTS % n_steps == 0
    epw = N_EXPERTS // n_steps
    w_specs = [pl.BlockSpec((epw,) + w.shape[1:], lambda i, *_: (i, 0, 0)) for w in weights]
    for w in weights:
        assert w.shape[1] % (16 * (tn // 16)) == 0
    grid_spec = pltpu.PrefetchScalarGridSpec(
        num_scalar_prefetch=2,
        grid=(n_steps,),
        in_specs=_dest_specs(n_steps, tn, lambda i: i) + [pl.BlockSpec((tn, D_PACKED), lambda i, zs, nu: (i, 0))]
        + w_specs,
        out_specs=[pl.BlockSpec(memory_space=pl.ANY)] + w_specs,
        scratch_shapes=[pltpu.VMEM((MOE_BLOCK, D_PACKED), jnp.uint32),
                        pltpu.SemaphoreType.DMA, pltpu.SemaphoreType.DMA],
    )
    xs, *w_bf16 = pl.pallas_call(
        _dispatch_kernel,
        out_shape=[jax.ShapeDtypeStruct((P, D_PACKED), jnp.uint32)]
        + [jax.ShapeDtypeStruct(w.shape, BF16) for w in weights],
        grid_spec=grid_spec,
        compiler_params=_cparams(("arbitrary",)),
        name="dispatch",
    )(zero_starts, n_used, *([dest_flat] * TOP_K), xn, *weights)
    return xs, w_bf16


def _moe_ffn_kernel(be_ref, nu_ref, seg_ref, nxt_ref, xs_ref, wg_hbm, wu_hbm, wd_hbm, ys_ref,
                    wg_b, wu_b, wd_b, sems):
    i = pl.program_id(0)
    used = i < nu_ref[0]

    def weights(expert, slot):
        return [pltpu.make_async_copy(src.at[expert], dst.at[slot], sems.at[slot])
                for src, dst in ((wg_hbm, wg_b), (wu_hbm, wu_b), (wd_hbm, wd_b))]

    @pl.when(used)
    def _():
        new_expert = jnp.logical_or(i == 0, be_ref[i] != be_ref[jnp.maximum(i - 1, 0)])
        slot = seg_ref[i] % 2

        @pl.when(new_expert)
        def _():
            @pl.when(i == 0)
            def _():
                for cp in weights(be_ref[0], 0):
                    cp.start()

            for cp in weights(be_ref[i], slot):
                cp.wait()

            @pl.when(nxt_ref[i] >= 0)
            def _():
                for cp in weights(nxt_ref[i], 1 - slot):
                    cp.start()

        xb = _unpack_pairs(xs_ref[...]).astype(BF16)
        hg = jnp.dot(xb, wg_b[slot], preferred_element_type=F32)
        hu = jnp.dot(xb, wu_b[slot], preferred_element_type=F32)
        hm = (hg * _sigmoid(hg) * hu).astype(BF16)
        ys_ref[...] = _pack_pairs(jnp.dot(hm, wd_b[slot], preferred_element_type=F32))

    @pl.when(jnp.logical_not(used))
    def _():
        ys_ref[...] = jnp.zeros_like(ys_ref)


def _moe_ffn(blk_expert, n_used, xs, w_gate, w_up, w_down):
    P = xs.shape[0]
    nblk = P // MOE_BLOCK
    blk = jnp.arange(nblk, dtype=jnp.int32)
    starts = jnp.logical_and(jnp.concatenate([jnp.ones((1,), bool), blk_expert[1:] != blk_expert[:-1]]),
                             blk < n_used[0])
    seg = jnp.cumsum(starts.astype(jnp.int32)) - 1
    next_start = lax.cummin(jnp.where(starts, blk, nblk), reverse=True)
    next_start = jnp.concatenate([next_start[1:], jnp.full((1,), nblk, jnp.int32)])
    nxt = jnp.where(next_start < nblk, blk_expert[jnp.minimum(next_start, nblk - 1)], -1).astype(jnp.int32)
    row_block = pl.BlockSpec((MOE_BLOCK, D_PACKED), lambda i, *_: (i, 0))
    grid_spec = pltpu.PrefetchScalarGridSpec(
        num_scalar_prefetch=4,
        grid=(nblk,),
        in_specs=[row_block] + [pl.BlockSpec(memory_space=pl.ANY)] * 3,
        out_specs=row_block,
        scratch_shapes=[pltpu.VMEM((2, D_MODEL, D_EXPERT), BF16),
                        pltpu.VMEM((2, D_MODEL, D_EXPERT), BF16),
                        pltpu.VMEM((2, D_EXPERT, D_MODEL), BF16),
                        pltpu.SemaphoreType.DMA((2,))],
    )
    return pl.pallas_call(
        _moe_ffn_kernel,
        out_shape=jax.ShapeDtypeStruct((P, D_PACKED), jnp.uint32),
        grid_spec=grid_spec,
        compiler_params=_cparams(("arbitrary",)),
        name="moe_ffn",
    )(blk_expert, n_used, seg, nxt, xs, w_gate, w_up, w_down)


COMBINE_ROWS = 32

def _combine_kernel(*refs, final):
    dcur_refs, dnext_refs = refs[:TOP_K], refs[TOP_K:2 * TOP_K]
    h_ref, gate_ref, fg_ref, ys_hbm, o_ref, ybuf, sems = refs[2 * TOP_K:]
    tm = h_ref.shape[0]
    i = pl.program_id(0)
    n = pl.num_programs(0)

    slot = i % 2
    fg = fg_ref[...]

    def issue(drefs, to_slot, g):
        for k in range(TOP_K):
            tile = ybuf.at[to_slot, pl.ds(pl.multiple_of(k * tm + g * 8, 8), 8)]
            for s in range(8):
                pltpu.make_async_copy(ys_hbm.at[pl.ds(drefs[k][g * 8 + s], 1)],
                                      tile.at[pl.ds(s, 1)], sems.at[to_slot]).start(priority=k % 2)

    tiles = COMBINE_ROWS // 8

    def load_rows(j):
        r0 = pl.multiple_of(j * COMBINE_ROWS, COMBINE_ROWS)
        packed = [ybuf[slot, pl.ds(pl.multiple_of(k * tm + j * COMBINE_ROWS, COMBINE_ROWS), COMBINE_ROWS), :]
                  for k in range(TOP_K)]
        return r0, gate_ref[pl.ds(r0, COMBINE_ROWS), :], h_ref[pl.ds(r0, COMBINE_ROWS), :], packed

    def finish_rows(loaded):
        r0, gate, hh, packed = loaded
        for k in range(TOP_K):
            hh = hh + _unpack_pairs(packed[k]) * gate[:, k:k + 1]
        o_ref[pl.ds(r0, COMBINE_ROWS), :] = _rms(hh, fg) if final else hh

    @pl.when(i == 0)
    def _():
        lax.fori_loop(0, tm // 8, lambda g, c: (issue(dcur_refs, 0, g), c)[1], 0)

    pltpu.make_async_copy(ys_hbm.at[pl.ds(0, TOP_K * tm)], ybuf.at[slot], sems.at[slot]).wait()

    @pl.when(i + 1 < n)
    def _():
        def body(j, c):
            loaded = load_rows(j)
            for u in range(tiles):
                issue(dnext_refs, 1 - slot, j * tiles + u)
            finish_rows(loaded)
            return c

        lax.fori_loop(0, tm // COMBINE_ROWS, body, 0)

    @pl.when(i + 1 == n)
    def _():
        lax.fori_loop(0, tm // COMBINE_ROWS, lambda j, c: (finish_rows(load_rows(j)), c)[1], 0)


def _combine(h, ys, dest_flat, gate_t, fg, tm, final):
    N = h.shape[0]
    n = N // tm
    rows = pl.BlockSpec((tm, D_MODEL), lambda i: (i, 0))
    return pl.pallas_call(
        functools.partial(_combine_kernel, final=final),
        out_shape=jax.ShapeDtypeStruct((N, D_MODEL), F32),
        grid=(n,),
        in_specs=_dest_specs(n, tm, lambda i: i) + _dest_specs(n, tm, lambda i: jnp.minimum(i + 1, n - 1)) + [
                  rows, pl.BlockSpec((tm, TOP_K), lambda i: (i, 0)),
                  pl.BlockSpec((1, D_MODEL), lambda i: (0, 0)),
                  pl.BlockSpec(memory_space=pl.ANY)],
        out_specs=rows,
        scratch_shapes=[pltpu.VMEM((2, TOP_K * tm, D_PACKED), jnp.uint32), pltpu.SemaphoreType.DMA((2,))],
        compiler_params=_cparams(("arbitrary",)),
        name="combine",
    )(*([dest_flat] * (2 * TOP_K)), h, gate_t, fg, ys)


def _reorder_w_in(w_in):
    n_gla = 2 * GLA_KEY_WIDTH + 2 * GLA_WIDTH
    z0 = n_gla
    a0 = z0 + 2 * GLA_GATE_RANK
    pad = jnp.zeros((D_MODEL, Z_PAD - 2 * GLA_GATE_RANK), w_in.dtype)
    return jnp.concatenate([w_in[:, :n_gla], w_in[:, a0:], w_in[:, z0:a0], pad], axis=1)


def kernel(x, norm1_gain, w_in, gla_up_fwd, gla_up_fwd_bias, gla_up_bwd, gla_up_bwd_bias, gla_out_gain, q_norm_gain, k_norm_gain, att_out_gain, w_out, norm2_gain, w_group, b_group, w_expert, b_expert, w_gate, w_up, w_down, final_gain):
    B, T, D = x.shape
    N = B * T
    depth = norm1_gain.shape[0]
    tm = min(512, T)
    tq = 128
    key_block = min(256, T)
    h = x.reshape(N, D)
    cos, se, so = _rope_tables(T)
    head_mean = jnp.asarray(
        np.kron(np.eye(ATT_HEADS, dtype=np.float32),
                np.full((ATT_HEAD_DIM, ATT_HEAD_DIM), 1.0 / ATT_HEAD_DIM, np.float32)), BF16)
    for l in range(depth):
        w = _reorder_w_in(w_in[l]).astype(BF16)
        r = GLA_GATE_RANK
        up = jnp.zeros((Z_PAD, 2 * GLA_KEY_WIDTH), F32)
        up = up.at[:r, :GLA_KEY_WIDTH].set(gla_up_fwd[l]).at[r:2 * r, GLA_KEY_WIDTH:].set(gla_up_bwd[l])
        upb = jnp.concatenate([gla_up_fwd_bias[l], gla_up_bwd_bias[l]])[None, :]
        qg = jnp.tile(q_norm_gain[l], ATT_HEADS)[None, :]
        kg = jnp.tile(k_norm_gain[l], ATT_KV_HEADS)[None, :]
        gq, gk, gv, gg, laf, lab, aqt, ak, avt = _in_proj(
            h, B, T, tm, norm1_gain[l][None, :], w, up.astype(BF16), upb, qg, kg, head_mean, cos, se, so)
        o_gla = _gla(gq, gk, gv, gg, laf, lab, gla_out_gain[l][None, :], B, T)
        o_att_t = _attn(aqt, ak, avt, B, T, tq, key_block)

        wr = jnp.zeros((LANES, D), F32)
        wr = wr.at[:N_GROUPS].set(w_group[l].T).at[N_GROUPS:N_GROUPS + N_EXPERTS].set(w_expert[l].T)
        br = jnp.zeros((LANES, 1), F32)
        br = br.at[:N_GROUPS, 0].set(b_group[l]).at[N_GROUPS:N_GROUPS + N_EXPERTS, 0].set(b_expert[l])
        h, xn, eid, gate, rank, cnt = _out_proj(
            h, o_gla, o_att_t, w_out[l].astype(BF16), att_out_gain[l][:, None], norm2_gain[l][None, :],
            wr.astype(BF16), br, tm)

        counts = cnt[:, 0].astype(jnp.int32)
        padded = (counts + MOE_BLOCK - 1) // MOE_BLOCK * MOE_BLOCK
        pad_ends = jnp.cumsum(padded)
        pad_starts = pad_ends - padded
        seg_start = jnp.sum(jnp.where(eid[:, :, None] == jnp.arange(N_EXPERTS, dtype=jnp.int32),
                                      pad_starts.astype(jnp.int32), 0), axis=-1)
        dest = (seg_start + rank).reshape(-1)
        P = N * TOP_K + N_EXPERTS * MOE_BLOCK
        nblk = P // MOE_BLOCK
        blk_start = jnp.arange(nblk, dtype=jnp.int32) * MOE_BLOCK
        blk_expert = jnp.minimum(
            jnp.sum((pad_ends[None, :] <= blk_start[:, None]).astype(jnp.int32), axis=1), N_EXPERTS - 1)
        n_used = (pad_ends[-1:] // MOE_BLOCK).astype(jnp.int32)

        zero_starts = jnp.maximum(pad_ends - MOE_BLOCK, 0).astype(jnp.int32)
        xs, expert_w = _dispatch(zero_starts, n_used, dest, xn, (w_gate[l], w_up[l], w_down[l]), P, tm)
        ys = _moe_ffn(blk_expert, n_used, xs, *expert_w)
        h = _combine(h, ys, dest, gate.T, final_gain[None, :], tm, l == depth - 1)
    return h.reshape(B, T, D)
```

```python
import functools

import numpy as np
import jax
import jax.numpy as jnp
from jax import lax
from jax.experimental import pallas as pl
from jax.experimental.pallas import tpu as pltpu

F32 = jnp.float32
BF16 = jnp.bfloat16

D_MODEL = 1024
EPS = 1e-6
GRID_W = 64

GLA_HEADS = 4
GLA_DK = 64
GLA_DV = 128
GLA_KEY_WIDTH = GLA_HEADS * GLA_DK
GLA_WIDTH = GLA_HEADS * GLA_DV
GLA_GATE_RANK = 16
GLA_GATE_NORM = 16.0
GLA_CHUNK = 64

ATT_HEADS = 8
ATT_KV_HEADS = 2
ATT_HEAD_DIM = 64
ATT_GROUP = ATT_HEADS // ATT_KV_HEADS
ATT_WIDTH = ATT_HEADS * ATT_HEAD_DIM
ATT_KV_WIDTH = ATT_KV_HEADS * ATT_HEAD_DIM
ROPE_THETA = 10000.0
LOG2_E = 1.4426950408889634

N_GROUPS = 8
EXPERTS_PER_GROUP = 8
N_EXPERTS = N_GROUPS * EXPERTS_PER_GROUP
TOP_K = 2
D_EXPERT = 512
MOE_BLOCK = 256

LANES = 128
Z_PAD = LANES

_OFF_GQ = 0
_OFF_GK = _OFF_GQ + GLA_KEY_WIDTH
_OFF_GV = _OFF_GK + GLA_KEY_WIDTH
_OFF_GG = _OFF_GV + GLA_WIDTH
_OFF_AQ = _OFF_GG + GLA_WIDTH
_OFF_AK = _OFF_AQ + ATT_WIDTH
_OFF_AV = _OFF_AK + ATT_KV_WIDTH
_OFF_Z = _OFF_AV + ATT_KV_WIDTH
D_IN_PAD = _OFF_Z + Z_PAD

VMEM_LIMIT = 56 * 1024 * 1024


def _cparams(semantics):
    return pltpu.CompilerParams(dimension_semantics=semantics, vmem_limit_bytes=VMEM_LIMIT)


def _rms(x, gain):
    return x * lax.rsqrt(jnp.mean(x * x, axis=-1, keepdims=True) + EPS) * gain


def _sigmoid(x):
    return 1.0 / (1.0 + jnp.exp(-x))


def _head_norm_rope(x, head_mean, gain, cos, sin_even, sin_odd):
    w = x.shape[1]
    ms = jnp.dot((x * x).astype(BF16), head_mean, preferred_element_type=F32)
    xn = x * lax.rsqrt(ms + EPS) * gain
    reps = w // LANES
    tile = lambda t: t if reps == 1 else jnp.concatenate([t] * reps, axis=1)
    nxt = pltpu.roll(xn, w - 1, 1)
    prv = pltpu.roll(xn, 1, 1)
    return xn * tile(cos) + nxt * tile(sin_even) + prv * tile(sin_odd)


def _in_proj_kernel(x_ref, g1_ref, w_ref, up_ref, upb_ref, qg_ref, kg_ref, hm_ref,
                    cos_ref, se_ref, so_ref,
                    gq_ref, gk_ref, gv_ref, gg_ref, laf_ref, lab_ref, aqt_ref, ak_ref, avt_ref):
    u = _rms(x_ref[...], g1_ref[...]).astype(BF16)

    def proj(lo, hi):
        return jnp.dot(u, w_ref[:, lo:hi], preferred_element_type=F32)

    cos, se, so = cos_ref[...], se_ref[...], so_ref[...]
    hm = hm_ref[...]
    r_q = proj(_OFF_AQ, _OFF_AK)
    r_gqk = proj(_OFF_GQ, _OFF_GV)
    q = _head_norm_rope(r_q, hm, qg_ref[...], cos, se, so)
    aqt_ref[0] = (q * (ATT_HEAD_DIM ** -0.5 * LOG2_E)).T.astype(BF16)
    gq_ref[...] = r_gqk[:, :GLA_KEY_WIDTH].astype(BF16)
    gk_ref[...] = r_gqk[:, GLA_KEY_WIDTH:].astype(BF16)

    r_kvz = proj(_OFF_AK, D_IN_PAD)
    r_gv = proj(_OFF_GV, _OFF_GG)
    k = _head_norm_rope(r_kvz[:, :ATT_KV_WIDTH], hm[:ATT_KV_WIDTH, :ATT_KV_WIDTH], kg_ref[...], cos, se, so)
    ak_ref[...] = k.astype(BF16)
    avt_ref[0] = r_kvz[:, ATT_KV_WIDTH:2 * ATT_KV_WIDTH].T.astype(BF16)
    gv_ref[...] = r_gv.astype(BF16)

    z = r_kvz[:, 2 * ATT_KV_WIDTH:].astype(BF16)
    zl = jnp.dot(z, up_ref[...], preferred_element_type=F32) + upb_ref[...]
    r_gg = proj(_OFF_GG, _OFF_AQ)
    la = (jnp.minimum(zl, 0.0) - jnp.log(1.0 + jnp.exp(-jnp.abs(zl)))) * (1.0 / GLA_GATE_NORM)
    laf_ref[...] = la[:, :GLA_KEY_WIDTH]
    lab_ref[...] = la[:, GLA_KEY_WIDTH:]
    gg_ref[...] = r_gg.astype(BF16)


def _rope_tables(T):
    t = np.arange(T)
    row = (t // GRID_W).astype(np.float32)
    col = (t % GRID_W).astype(np.float32)
    axis_dim = ATT_HEAD_DIM // 2
    inv_freq = (ROPE_THETA ** (-np.arange(0, axis_dim, 2, dtype=np.float32) / axis_dim)).astype(np.float32)
    ang = np.concatenate([row[:, None] * inv_freq, col[:, None] * inv_freq], axis=-1)
    ang = np.repeat(ang, 2, axis=1)
    ang = np.tile(ang, (1, LANES // ATT_HEAD_DIM))
    even = (np.arange(LANES) % 2 == 0)[None, :]
    cos = np.cos(ang)
    sin = np.sin(ang)
    return (jnp.asarray(cos, F32), jnp.asarray(np.where(even, -sin, 0.0), F32),
            jnp.asarray(np.where(even, 0.0, sin), F32))


def _in_proj(xt, B, T, tm, g1, w, up, upb, qg, kg, hm, cos, se, so):
    N = xt.shape[0]
    tpb = T // tm
    const = lambda shape: pl.BlockSpec(shape, lambda i: (0,) * len(shape))
    rows = lambda width: pl.BlockSpec((tm, width), lambda i: (i, 0))
    pos = pl.BlockSpec((tm, LANES), lambda i: (i % tpb, 0))
    out_shape = (
        jax.ShapeDtypeStruct((N, GLA_KEY_WIDTH), BF16),
        jax.ShapeDtypeStruct((N, GLA_KEY_WIDTH), BF16),
        jax.ShapeDtypeStruct((N, GLA_WIDTH), BF16),
        jax.ShapeDtypeStruct((N, GLA_WIDTH), BF16),
        jax.ShapeDtypeStruct((N, GLA_KEY_WIDTH), F32),
        jax.ShapeDtypeStruct((N, GLA_KEY_WIDTH), F32),
        jax.ShapeDtypeStruct((B, ATT_WIDTH, T), BF16),
        jax.ShapeDtypeStruct((N, ATT_KV_WIDTH), BF16),
        jax.ShapeDtypeStruct((B, ATT_KV_WIDTH, T), BF16),
    )
    cols = lambda width: pl.BlockSpec((1, width, tm), lambda i: (i // tpb, 0, i % tpb))
    out_specs = (
        rows(GLA_KEY_WIDTH), rows(GLA_KEY_WIDTH), rows(GLA_WIDTH), rows(GLA_WIDTH),
        rows(GLA_KEY_WIDTH), rows(GLA_KEY_WIDTH),
        cols(ATT_WIDTH), rows(ATT_KV_WIDTH), cols(ATT_KV_WIDTH),
    )
    return pl.pallas_call(
        _in_proj_kernel,
        out_shape=out_shape,
        grid=(N // tm,),
        in_specs=[rows(D_MODEL), const((1, D_MODEL)), const((D_MODEL, D_IN_PAD)),
                  const((Z_PAD, 2 * GLA_KEY_WIDTH)), const((1, 2 * GLA_KEY_WIDTH)),
                  const((1, ATT_WIDTH)), const((1, ATT_KV_WIDTH)), const((ATT_WIDTH, ATT_WIDTH)),
                  pos, pos, pos],
        out_specs=out_specs,
        compiler_params=_cparams(("arbitrary",)),
        name="in_proj",
    )(xt, g1, w, up, upb, qg, kg, hm, cos, se, so)


GLA_GROUP = 4
GLA_FINISH_GROUP = 16


def _gla_kernel(q_ref, k_ref, v_ref, g_ref, laf_ref, lab_ref, gain_ref, o_ref,
                part_scr, qif_scr, qib_scr, kvf_scr, kvb_scr, decf_scr, decb_scr, lf_scr, lb_scr):
    C, G = GLA_CHUNK, GLA_GROUP
    R = C * G
    T = q_ref.shape[0]
    n_chunks = T // C
    gain = gain_ref[...]
    row = lax.broadcasted_iota(jnp.int32, (C, C), 0)
    col = lax.broadcasted_iota(jnp.int32, (C, C), 1)
    tril = row >= col
    triu = row <= col
    rr = lax.broadcasted_iota(jnp.int32, (R, R), 0)
    cc = lax.broadcasted_iota(jnp.int32, (R, R), 1)
    same_chunk = (rr // C) == (cc // C)
    lf_scr[...] = jnp.where(same_chunk & (rr >= cc), 1.0, 0.0).astype(BF16)
    lb_scr[...] = jnp.where(same_chunk & (rr <= cc), 1.0, 0.0).astype(BF16)

    directions = ((laf_ref, lf_scr, tril, C // 2 - 1, C - 1, qif_scr, kvf_scr, decf_scr),
                  (lab_ref, lb_scr, triu, C // 2, 0, qib_scr, kvb_scr, decb_scr))
    heads = tuple((slice(h * GLA_DK, (h + 1) * GLA_DK), slice(h * GLA_DV, (h + 1) * GLA_DV)) for h in range(2))
    nt = (((1,), (1,)), ((), ()))
    tn = (((0,), (0,)), ((), ()))

    def intra(sb, carry):
        r0 = pl.multiple_of(sb * R, R)
        q2 = q_ref[pl.ds(r0, R), :].astype(F32) * (GLA_DK ** -0.5)
        k2 = k_ref[pl.ds(r0, R), :].astype(F32)
        v2 = v_ref[pl.ds(r0, R), :]
        cums = []
        for la_ref, tri_scr, *_ in directions:
            la = la_ref[pl.ds(r0, R), :]
            la_hi = la.astype(BF16)
            la_lo = (la - la_hi.astype(F32)).astype(BF16)
            both = jnp.dot(tri_scr[...], jnp.concatenate([la_hi, la_lo], axis=1), preferred_element_type=F32)
            cums.append(both[:, :2 * GLA_DK] + both[:, 2 * GLA_DK:])
        scaled = []
        for b, (_, _, mask, i_ref, i_last, qi_scr, kv_scr, dec_scr) in zip(cums, directions):
            for c in range(G):
                rows = slice(c * C, (c + 1) * C)
                bc, qc, kc = b[rows], q2[rows], k2[rows]
                b_ref = bc[i_ref:i_ref + 1, :]
                b_last = bc[i_last:i_last + 1, :]
                qf = (qc * jnp.exp(bc - b_ref)).astype(BF16)
                kf = (kc * jnp.exp(b_ref - bc)).astype(BF16)
                qi_scr[pl.ds(r0 + c * C, C), :] = (qc * jnp.exp(bc)).astype(BF16)
                kl = (kc * jnp.exp(b_last - bc)).astype(BF16)
                decay = jnp.exp(b_last)
                for h, (ks, _) in enumerate(heads):
                    dec_scr[sb * G + c, h] = decay[:, ks]
                scaled.append((c, mask, kv_scr, qf, kf, kl))
        raw = [[lax.dot_general(qf[:, ks], kf[:, ks], nt, preferred_element_type=F32) for ks, _ in heads]
               for _, _, _, qf, kf, _ in scaled]
        o_intra = [[None, None] for _ in range(G)]
        for (c, mask, kv_scr, _, _, kl), sc2 in zip(scaled, raw):
            rows = slice(c * C, (c + 1) * C)
            for h, (ks, vs) in enumerate(heads):
                vh = v2[rows, vs]
                sc = jnp.where(mask, sc2[h], 0.0).astype(BF16)
                oi = jnp.dot(sc, vh, preferred_element_type=F32)
                o_intra[c][h] = oi if o_intra[c][h] is None else o_intra[c][h] + oi
                kv_scr[sb * G + c, h] = lax.dot_general(vh, kl[:, ks], tn, preferred_element_type=F32)
        for c in range(G):
            part_scr[pl.ds(r0 + c * C, C), :] = jnp.concatenate(o_intra[c], axis=1)
        return carry

    lax.fori_loop(0, n_chunks // G, intra, 0)

    def scan(n, carry):
        sf, sb = carry
        nb = n_chunks - 1 - n
        new_f, new_b = [], []
        for h in range(2):
            kv = kvf_scr[n, h]
            kvf_scr[n, h] = sf[h]
            new_f.append(decf_scr[n, h] * sf[h] + kv)
            kv = kvb_scr[nb, h]
            kvb_scr[nb, h] = sb[h]
            new_b.append(decb_scr[nb, h] * sb[h] + kv)
        return tuple(new_f), tuple(new_b)

    zero_state = tuple(jnp.zeros((GLA_DV, GLA_DK), F32) for _ in range(2))
    lax.fori_loop(0, n_chunks, scan, (zero_state, zero_state), unroll=4)

    GF = min(GLA_FINISH_GROUP, n_chunks)
    RF = C * GF

    def finish(sb, carry):
        r0 = pl.multiple_of(sb * RF, RF)
        g = g_ref[pl.ds(r0, RF), :].astype(F32)
        part = part_scr[pl.ds(r0, RF), :]
        qf = qif_scr[pl.ds(r0, RF), :]
        qb = qib_scr[pl.ds(r0, RF), :]
        inter = []
        for c in range(GF):
            rows = slice(c * C, (c + 1) * C)
            inter.append([
                lax.dot_general(qf[rows, ks], kvf_scr[sb * GF + c, h].astype(BF16), nt, preferred_element_type=F32)
                + lax.dot_general(qb[rows, ks], kvb_scr[sb * GF + c, h].astype(BF16), nt, preferred_element_type=F32)
                for h, (ks, _) in enumerate(heads)])
        gate = g * _sigmoid(g)
        for c in range(GF):
            rows = slice(c * C, (c + 1) * C)
            normed = jnp.concatenate([_rms(part[rows, vs] + inter[c][h], gain) for h, (_, vs) in enumerate(heads)],
                                     axis=1)
            o_ref[pl.ds(r0 + c * C, C), :] = (normed * gate[rows]).astype(BF16)
        return carry

    lax.fori_loop(0, n_chunks // GF, finish, 0)


def _gla(gq, gk, gv, gg, laf, lab, gain, B, T):
    N = gq.shape[0]
    pairs = GLA_HEADS // 2
    n_chunks = T // GLA_CHUNK
    assert n_chunks % GLA_GROUP == 0 and n_chunks % min(GLA_FINISH_GROUP, n_chunks) == 0
    group_rows = GLA_CHUNK * GLA_GROUP
    kspec = pl.BlockSpec((T, 2 * GLA_DK), lambda b, p: (b, p))
    vspec = pl.BlockSpec((T, 2 * GLA_DV), lambda b, p: (b, p))
    state = pltpu.VMEM((n_chunks, 2, GLA_DV, GLA_DK), F32)
    decay = pltpu.VMEM((n_chunks, 2, 1, GLA_DK), F32)
    return pl.pallas_call(
        _gla_kernel,
        out_shape=jax.ShapeDtypeStruct((N, GLA_WIDTH), BF16),
        grid=(B, pairs),
        in_specs=[kspec, kspec, vspec, vspec, kspec, kspec,
                  pl.BlockSpec((1, GLA_DV), lambda b, p: (0, 0))],
        out_specs=vspec,
        scratch_shapes=[pltpu.VMEM((T, 2 * GLA_DV), F32),
                        pltpu.VMEM((T, 2 * GLA_DK), BF16), pltpu.VMEM((T, 2 * GLA_DK), BF16),
                        state, state, decay, decay,
                        pltpu.VMEM((group_rows, group_rows), BF16), pltpu.VMEM((group_rows, group_rows), BF16)],
        compiler_params=_cparams(("arbitrary", "arbitrary")),
        name="gla",
    )(gq, gk, gv, gg, laf, lab, gain)


def _attn_kernel(qt_ref, k_ref, vt_ref, ot_ref, vext_scr, s0_scr, s1_scr, p0_scr, p1_scr, m0_scr, m1_scr,
                 *, tq, key_block, steps_per_iter):
    T = k_ref.shape[0]
    nq = T // tq
    n_blocks = ATT_KV_HEADS * nq
    dh = ATT_HEAD_DIM
    gw = ATT_GROUP * dh
    cols = ATT_GROUP * tq

    row = lax.broadcasted_iota(jnp.int32, (ATT_KV_WIDTH, T), 0)
    for hj in range(ATT_KV_HEADS):
        own = (row >= hj * dh) & (row < (hj + 1) * dh)
        vext_scr[hj] = jnp.where(own, vt_ref[0], jnp.ones((), BF16))

    s_scr, p_scr, m_scr = (s0_scr, s1_scr), (p0_scr, p1_scr), (m0_scr, m1_scr)

    def locate(blk):
        if isinstance(blk, int):
            hj = blk // nq
            return hj, hj == 0, (blk - hj * nq) * tq
        second = blk >= nq
        hj = jnp.where(second, 1, 0)
        return hj, jnp.logical_not(second), pl.multiple_of((blk - hj * nq) * tq, tq)

    def step(blk, sa, do_scores=True, do_probs=True, do_out=True):
        sb = 1 - sa
        if do_scores:
            hj, first, c0 = locate(blk)
            q4 = jnp.concatenate(
                [qt_ref[0, pl.ds(pl.multiple_of(hj * gw + h * dh, dh), dh), pl.ds(c0, tq)] for h in range(ATT_GROUP)],
                axis=1)
            zeros = jnp.zeros_like(q4)
            qe = jnp.where(first, jnp.concatenate([q4, zeros], axis=0), jnp.concatenate([zeros, q4], axis=0))
            m = jnp.full((8, cols), -jnp.inf, F32)
        if do_probs:
            mx = m_scr[sb][...]
        if do_out:
            hj_o, first_o, c0_o = locate(blk - 2)
            acc = jnp.zeros((ATT_KV_WIDTH, cols), F32)
        for kb in range(T // key_block):
            ks = slice(kb * key_block, (kb + 1) * key_block)
            if do_scores:
                s = jnp.dot(k_ref[ks, :], qe, preferred_element_type=F32)
                s_scr[sa][ks, :] = s
                m = jnp.maximum(m, jnp.max(s.reshape(key_block // 8, 8, cols), axis=0))
            if do_probs:
                p_scr[sb][ks, :] = jnp.exp2(s_scr[sb][ks, :] - mx).astype(BF16)
            if do_out:
                acc = acc + jnp.dot(vext_scr[hj_o, :, ks], p_scr[sa][ks, :], preferred_element_type=F32)
        if do_scores:
            m_scr[sa][...] = jnp.max(m, axis=0, keepdims=True)
        if do_out:
            num = jnp.where(first_o, acc[:dh], acc[dh:])
            den = jnp.where(first_o, acc[dh:dh + 1], acc[0:1])
            o = (num / den).astype(BF16)
            for h in range(ATT_GROUP):
                ot_ref[0, pl.ds(pl.multiple_of(hj_o * gw + h * dh, dh), dh), pl.ds(c0_o, tq)] = (
                    o[:, h * tq:(h + 1) * tq])

    step(0, 0, do_probs=False, do_out=False)
    step(1, 1, do_out=False)

    def steady(it, carry):
        for u in range(steps_per_iter):
            pl.when(it >= 0)(functools.partial(step, 2 + steps_per_iter * it + u, u % 2))
        return carry

    lax.fori_loop(0, (n_blocks - 2) // steps_per_iter, steady, 0)
    step(n_blocks, 0, do_scores=False)
    step(n_blocks + 1, 1, do_scores=False, do_probs=False)


def _attn(aqt, ak, avt, B, T, tq, key_block):
    assert T % (2 * tq) == 0 and tq % LANES == 0 and ATT_KV_HEADS == 2
    cols = ATT_GROUP * tq
    steps_per_iter = 2
    return pl.pallas_call(
        functools.partial(_attn_kernel, tq=tq, key_block=key_block, steps_per_iter=steps_per_iter),
        out_shape=jax.ShapeDtypeStruct((B, ATT_WIDTH, T), BF16),
        grid=(B,),
        in_specs=[pl.BlockSpec((1, ATT_WIDTH, T), lambda b: (b, 0, 0)),
                  pl.BlockSpec((T, ATT_KV_WIDTH), lambda b: (b, 0)),
                  pl.BlockSpec((1, ATT_KV_WIDTH, T), lambda b: (b, 0, 0))],
        out_specs=pl.BlockSpec((1, ATT_WIDTH, T), lambda b: (b, 0, 0)),
        scratch_shapes=[pltpu.VMEM((ATT_KV_HEADS, ATT_KV_WIDTH, T), BF16),
                        pltpu.VMEM((T, cols), F32), pltpu.VMEM((T, cols), F32),
                        pltpu.VMEM((T, cols), BF16), pltpu.VMEM((T, cols), BF16),
                        pltpu.VMEM((1, cols), F32), pltpu.VMEM((1, cols), F32)],
        compiler_params=_cparams(("arbitrary",)),
        name="attn",
    )(aqt, ak, avt)


def _out_proj_kernel(x_ref, og_ref, oa_ref, wo_ref, ag_ref, g2_ref, wr_ref, br_ref, tri_ref,
                     h_ref, xn_ref, eid_ref, gate_ref, rank_ref, cnt_ref, run_scr):
    tm = x_ref.shape[0]

    @pl.when(pl.program_id(0) == 0)
    def _():
        run_scr[...] = jnp.zeros_like(run_scr)

    def branch_norm(ts):
        oat = oa_ref[0, :, ts].astype(F32)
        return (oat * lax.rsqrt(jnp.mean(oat * oat, axis=0, keepdims=True) + EPS) * ag_ref[...]).astype(BF16)

    def mix(ts, oan):
        return (jnp.dot(og_ref[ts, :], wo_ref[:GLA_WIDTH, :], preferred_element_type=F32)
                + lax.dot_general(oan, wo_ref[GLA_WIDTH:, :], (((0,), (0,)), ((), ())),
                                  preferred_element_type=F32))

    def residual_norm(ts, y):
        h = x_ref[ts, :] + y
        h_ref[ts, :] = h
        xn = _rms(h, g2_ref[...])
        xn_ref[ts, :] = _pack_pairs(xn)
        return xn.astype(BF16)

    def logits(xnb):
        return lax.dot_general(wr_ref[...], xnb, (((1,), (1,)), ((), ())), preferred_element_type=F32) + br_ref[...]

    whole = slice(0, tm)
    lt = logits(residual_norm(whole, mix(whole, branch_norm(whole))))
    iota8 = lax.broadcasted_iota(jnp.int32, (N_GROUPS, tm), 0)

    def first_argmax(v):
        top = jnp.max(v, axis=0, keepdims=True)
        idx = jnp.min(jnp.where(v == top, iota8, N_GROUPS), axis=0, keepdims=True)
        return top, idx

    gl = lt[0:N_GROUPS]
    gmax, gidx = first_argmax(gl)
    gw = 1.0 / jnp.sum(jnp.exp(gl - gmax), axis=0, keepdims=True)
    esel = jnp.zeros((EXPERTS_PER_GROUP, tm), F32)
    for g in range(N_GROUPS):
        lo = N_GROUPS + g * EXPERTS_PER_GROUP
        esel = jnp.where(gidx == g, lt[lo:lo + EXPERTS_PER_GROUP], esel)
    v1, i1 = first_argmax(esel)
    rest = jnp.where(iota8 == i1, -jnp.inf, esel)
    v2, i2 = first_argmax(rest)
    t = jnp.exp(v2 - v1)
    den = 1.0 + t
    e1 = gidx * EXPERTS_PER_GROUP + i1
    e2 = gidx * EXPERTS_PER_GROUP + i2
    eid_ref[0:1, :] = e1
    eid_ref[1:2, :] = e2
    gate_ref[0:1, :] = gw * (1.0 / den)
    gate_ref[1:2, :] = gw * (t / den)

    iota_e = lax.broadcasted_iota(jnp.int32, (N_EXPERTS, tm), 0)
    oh1 = iota_e == e1
    oh2 = iota_e == e2
    both = jnp.where(oh1, 1.0, jnp.where(oh2, 1.0, 0.0)).astype(BF16)
    prefix = jnp.dot(both, tri_ref[...], preferred_element_type=F32)
    run = run_scr[...]
    base = prefix - 1.0 + run[:, 0:1]
    rank_ref[0:1, :] = jnp.sum(jnp.where(oh1, base, 0.0), axis=0, keepdims=True).astype(jnp.int32)
    rank_ref[1:2, :] = jnp.sum(jnp.where(oh2, base, 0.0), axis=0, keepdims=True).astype(jnp.int32)
    run = run + prefix[:, tm - 1:tm]
    run_scr[...] = run
    cnt_ref[...] = run


def _out_proj(xt, o_gla, o_att_t, wo, ag, g2, wr, br, tm):
    N = xt.shape[0]
    tpb = o_att_t.shape[2] // tm
    const = lambda shape: pl.BlockSpec(shape, lambda i: (0,) * len(shape))
    rows = lambda width: pl.BlockSpec((tm, width), lambda i: (i, 0))
    cols = pl.BlockSpec((TOP_K, tm), lambda i: (0, i))
    tri = jnp.asarray(np.triu(np.ones((tm, tm), np.float32)), BF16)
    return pl.pallas_call(
        _out_proj_kernel,
        out_shape=(jax.ShapeDtypeStruct((N, D_MODEL), F32),
                   jax.ShapeDtypeStruct((N, D_PACKED), jnp.uint32),
                   jax.ShapeDtypeStruct((TOP_K, N), jnp.int32),
                   jax.ShapeDtypeStruct((TOP_K, N), F32),
                   jax.ShapeDtypeStruct((TOP_K, N), jnp.int32),
                   jax.ShapeDtypeStruct((N_EXPERTS, LANES), F32)),
        grid=(N // tm,),
        in_specs=[rows(D_MODEL), rows(GLA_WIDTH),
                  pl.BlockSpec((1, ATT_WIDTH, tm), lambda i: (i // tpb, 0, i % tpb)),
                  const((D_MODEL, D_MODEL)),
                  const((ATT_WIDTH, 1)), const((1, D_MODEL)), const((LANES, D_MODEL)),
                  const((LANES, 1)), const((tm, tm))],
        out_specs=(rows(D_MODEL), rows(D_PACKED), cols, cols, cols, const((N_EXPERTS, LANES))),
        scratch_shapes=[pltpu.VMEM((N_EXPERTS, LANES), F32)],
        compiler_params=_cparams(("arbitrary",)),
        name="out_proj",
    )(xt, o_gla, o_att_t, wo, ag, g2, wr, br, tri)


D_PACKED = D_MODEL // 2


def _pack_pairs(x):
    w = x.shape[1] // 2
    bits = pltpu.bitcast(x.astype(BF16).astype(F32), jnp.uint32)
    return bits[:, :w] | (bits[:, w:] >> 16)


def _unpack_pairs(words):
    hi = pltpu.bitcast(words & jnp.uint32(0xFFFF0000), F32)
    lo = pltpu.bitcast(words << 16, F32)
    return jnp.concatenate([hi, lo], axis=1)


def _dispatch_kernel(zs_ref, nu_ref, *refs):
    dest_refs = refs[:TOP_K]
    packed = refs[TOP_K]
    w_in = refs[TOP_K + 1:TOP_K + 4]
    xs_hbm = refs[TOP_K + 4]
    w_out = refs[TOP_K + 5:TOP_K + 8]
    zero_buf, zero_sem, row_sem = refs[TOP_K + 8:]
    tn = packed.shape[0]
    i = pl.program_id(0)

    def fill(row0):
        return pltpu.make_async_copy(
            zero_buf, xs_hbm.at[pl.ds(pl.multiple_of(row0, MOE_BLOCK), MOE_BLOCK)], zero_sem)

    @pl.when(i == 0)
    def _():
        zero_buf[...] = jnp.zeros_like(zero_buf)
        fills = [fill(zs_ref[e]) for e in range(N_EXPERTS)]
        for f in fills:
            f.start()
        for f in fills:
            f.wait()

        n_blocks = xs_hbm.shape[0] // MOE_BLOCK
        lax.fori_loop(nu_ref[0], n_blocks, lambda blk, c: (fill(blk * MOE_BLOCK).start(), c)[1], 0)
        lax.fori_loop(nu_ref[0], n_blocks, lambda blk, c: (fill(blk * MOE_BLOCK).wait(), c)[1], 0)

    n_iter = tn // 16
    epw = w_in[0].shape[0]

    def issue(it, carry):
        chunks = []
        for src, dst in zip(w_in, w_out):
            rows = src.shape[1] // n_iter
            r0 = pl.multiple_of(it * rows, rows)
            for e in range(epw):
                chunks.append((dst, e, r0, rows, src[e, pl.ds(r0, rows), :]))
        for u in range(2):
            g = it * 2 + u
            tile = packed.at[pl.ds(pl.multiple_of(g * 8, 8), 8)]
            for s in range(8):
                for k in range(TOP_K):
                    pltpu.make_async_copy(tile.at[pl.ds(s, 1)], xs_hbm.at[pl.ds(dest_refs[k][g * 8 + s], 1)],
                                          row_sem).start(priority=k % 2)
        for dst, e, r0, rows, val in chunks:
            dst[e, pl.ds(r0, rows), :] = val.astype(BF16)
        return carry

    lax.fori_loop(0, n_iter, issue, 0)
    for k in range(TOP_K):
        pltpu.make_async_copy(packed, xs_hbm.at[pl.ds(0, tn)], row_sem).wait()


def _dest_specs(n_steps, tn, index):
    return [pl.BlockSpec((tn,), lambda i, *_, k=k: (k * n_steps + index(i),), memory_space=pltpu.SMEM)
            for k in range(TOP_K)]


def _dispatch(zero_starts, n_used, dest_flat, xn, weights, P, tn):
    N = xn.shape[0]
    n_steps = N // tn
    assert N_EXPERTS % n_steps == 0
    epw = N_EXPERTS // n_steps
    w_specs = [pl.BlockSpec((epw,) + w.shape[1:], lambda i, *_: (i, 0, 0)) for w in weights]
    for w in weights:
        assert w.shape[1] % (16 * (tn // 16)) == 0
    grid_spec = pltpu.PrefetchScalarGridSpec(
        num_scalar_prefetch=2,
        grid=(n_steps,),
        in_specs=_dest_specs(n_steps, tn, lambda i: i) + [pl.BlockSpec((tn, D_PACKED), lambda i, zs, nu: (i, 0))]
        + w_specs,
        out_specs=[pl.BlockSpec(memory_space=pl.ANY)] + w_specs,
        scratch_shapes=[pltpu.VMEM((MOE_BLOCK, D_PACKED), jnp.uint32),
                        pltpu.SemaphoreType.DMA, pltpu.SemaphoreType.DMA],
    )
    xs, *w_bf16 = pl.pallas_call(
        _dispatch_kernel,
        out_shape=[jax.ShapeDtypeStruct((P, D_PACKED), jnp.uint32)]
        + [jax.ShapeDtypeStruct(w.shape, BF16) for w in weights],
        grid_spec=grid_spec,
        compiler_params=_cparams(("arbitrary",)),
        name="dispatch",
    )(zero_starts, n_used, *([dest_flat] * TOP_K), xn, *weights)
    return xs, w_bf16


def _moe_ffn_kernel(be_ref, nu_ref, seg_ref, nxt_ref, xs_ref, wg_hbm, wu_hbm, wd_hbm, ys_ref,
                    wg_b, wu_b, wd_b, sems):
    i = pl.program_id(0)
    used = i < nu_ref[0]

    def weights(expert, slot):
        return [pltpu.make_async_copy(src.at[expert], dst.at[slot], sems.at[slot])
                for src, dst in ((wg_hbm, wg_b), (wu_hbm, wu_b), (wd_hbm, wd_b))]

    @pl.when(used)
    def _():
        new_expert = jnp.logical_or(i == 0, be_ref[i] != be_ref[jnp.maximum(i - 1, 0)])
        slot = seg_ref[i] % 2

        @pl.when(new_expert)
        def _():
            @pl.when(i == 0)
            def _():
                for cp in weights(be_ref[0], 0):
                    cp.start()

            for cp in weights(be_ref[i], slot):
                cp.wait()

            @pl.when(nxt_ref[i] >= 0)
            def _():
                for cp in weights(nxt_ref[i], 1 - slot):
                    cp.start()

        def gate_up(rows):
            xb = _unpack_pairs(xs_ref[rows, :]).astype(BF16)
            return (jnp.dot(xb, wg_b[slot], preferred_element_type=F32),
                    jnp.dot(xb, wu_b[slot], preferred_element_type=F32))

        def swish_mul(hg, hu):
            return (hg * _sigmoid(hg) * hu).astype(BF16)

        def down(rows, hm):
            ys_ref[rows, :] = _pack_pairs(jnp.dot(hm, wd_b[slot], preferred_element_type=F32))

        ra, rb = slice(0, MOE_BLOCK // 2), slice(MOE_BLOCK // 2, MOE_BLOCK)
        gu_a = gate_up(ra)
        hm_a = swish_mul(*gu_a)
        gu_b = gate_up(rb)
        down(ra, hm_a)
        hm_b = swish_mul(*gu_b)
        down(rb, hm_b)

    @pl.when(jnp.logical_not(used))
    def _():
        ys_ref[...] = jnp.zeros_like(ys_ref)


def _moe_ffn(blk_expert, n_used, xs, w_gate, w_up, w_down):
    P = xs.shape[0]
    nblk = P // MOE_BLOCK
    blk = jnp.arange(nblk, dtype=jnp.int32)
    starts = jnp.logical_and(jnp.concatenate([jnp.ones((1,), bool), blk_expert[1:] != blk_expert[:-1]]),
                             blk < n_used[0])
    seg = jnp.cumsum(starts.astype(jnp.int32)) - 1
    next_start = lax.cummin(jnp.where(starts, blk, nblk), reverse=True)
    next_start = jnp.concatenate([next_start[1:], jnp.full((1,), nblk, jnp.int32)])
    nxt = jnp.where(next_start < nblk, blk_expert[jnp.minimum(next_start, nblk - 1)], -1).astype(jnp.int32)
    row_block = pl.BlockSpec((MOE_BLOCK, D_PACKED), lambda i, *_: (i, 0))
    grid_spec = pltpu.PrefetchScalarGridSpec(
        num_scalar_prefetch=4,
        grid=(nblk,),
        in_specs=[row_block] + [pl.BlockSpec(memory_space=pl.ANY)] * 3,
        out_specs=row_block,
        scratch_shapes=[pltpu.VMEM((2, D_MODEL, D_EXPERT), BF16),
                        pltpu.VMEM((2, D_MODEL, D_EXPERT), BF16),
                        pltpu.VMEM((2, D_EXPERT, D_MODEL), BF16),
                        pltpu.SemaphoreType.DMA((2,))],
    )
    return pl.pallas_call(
        _moe_ffn_kernel,
        out_shape=jax.ShapeDtypeStruct((P, D_PACKED), jnp.uint32),
        grid_spec=grid_spec,
        compiler_params=_cparams(("arbitrary",)),
        name="moe_ffn",
    )(blk_expert, n_used, seg, nxt, xs, w_gate, w_up, w_down)


def _combine_kernel(*refs, final):
    dcur_refs, dnext_refs = refs[:TOP_K], refs[TOP_K:2 * TOP_K]
    h_ref, gate_ref, fg_ref, ys_hbm, o_ref, ybuf, sems = refs[2 * TOP_K:]
    tm = h_ref.shape[0]
    i = pl.program_id(0)
    n = pl.num_programs(0)

    def gather(drefs, slot):
        def issue(g, carry):
            for k in range(TOP_K):
                tile = ybuf.at[slot, pl.ds(pl.multiple_of(k * tm + g * 8, 8), 8)]
                for s in range(8):
                    pltpu.make_async_copy(ys_hbm.at[pl.ds(drefs[k][g * 8 + s], 1)],
                                          tile.at[pl.ds(s, 1)], sems.at[slot]).start(priority=k % 2)
            return carry

        lax.fori_loop(0, tm // 8, issue, 0)

    @pl.when(i == 0)
    def _():
        gather(dcur_refs, 0)

    @pl.when(i + 1 < n)
    def _():
        gather(dnext_refs, (i + 1) % 2)

    slot = i % 2
    pltpu.make_async_copy(ys_hbm.at[pl.ds(0, TOP_K * tm)], ybuf.at[slot], sems.at[slot]).wait()
    g = gate_ref[...]
    hh = h_ref[...]
    for k in range(TOP_K):
        hh = hh + _unpack_pairs(ybuf[slot, pl.ds(k * tm, tm), :]) * g[:, k:k + 1]
    o_ref[...] = _rms(hh, fg_ref[...]) if final else hh


def _combine(h, ys, dest_flat, gate_t, fg, tm, final):
    N = h.shape[0]
    n = N // tm
    rows = pl.BlockSpec((tm, D_MODEL), lambda i: (i, 0))
    return pl.pallas_call(
        functools.partial(_combine_kernel, final=final),
        out_shape=jax.ShapeDtypeStruct((N, D_MODEL), F32),
        grid=(n,),
        in_specs=_dest_specs(n, tm, lambda i: i) + _dest_specs(n, tm, lambda i: jnp.minimum(i + 1, n - 1)) + [
                  rows, pl.BlockSpec((tm, TOP_K), lambda i: (i, 0)),
                  pl.BlockSpec((1, D_MODEL), lambda i: (0, 0)),
                  pl.BlockSpec(memory_space=pl.ANY)],
        out_specs=rows,
        scratch_shapes=[pltpu.VMEM((2, TOP_K * tm, D_PACKED), jnp.uint32), pltpu.SemaphoreType.DMA((2,))],
        compiler_params=_cparams(("arbitrary",)),
        name="combine",
    )(*([dest_flat] * (2 * TOP_K)), h, gate_t, fg, ys)


def _reorder_w_in(w_in):
    n_gla = 2 * GLA_KEY_WIDTH + 2 * GLA_WIDTH
    z0 = n_gla
    a0 = z0 + 2 * GLA_GATE_RANK
    pad = jnp.zeros((D_MODEL, Z_PAD - 2 * GLA_GATE_RANK), w_in.dtype)
    return jnp.concatenate([w_in[:, :n_gla], w_in[:, a0:], w_in[:, z0:a0], pad], axis=1)


def kernel(x, norm1_gain, w_in, gla_up_fwd, gla_up_fwd_bias, gla_up_bwd, gla_up_bwd_bias, gla_out_gain, q_norm_gain, k_norm_gain, att_out_gain, w_out, norm2_gain, w_group, b_group, w_expert, b_expert, w_gate, w_up, w_down, final_gain):
    B, T, D = x.shape
    N = B * T
    depth = norm1_gain.shape[0]
    tm = min(512, T)
    tq = 128
    key_block = min(256, T)
    h = x.reshape(N, D)
    cos, se, so = _rope_tables(T)
    head_mean = jnp.asarray(
        np.kron(np.eye(ATT_HEADS, dtype=np.float32),
                np.full((ATT_HEAD_DIM, ATT_HEAD_DIM), 1.0 / ATT_HEAD_DIM, np.float32)), BF16)
    for l in range(depth):
        w = _reorder_w_in(w_in[l]).astype(BF16)
        r = GLA_GATE_RANK
        up = jnp.zeros((Z_PAD, 2 * GLA_KEY_WIDTH), F32)
        up = up.at[:r, :GLA_KEY_WIDTH].set(gla_up_fwd[l]).at[r:2 * r, GLA_KEY_WIDTH:].set(gla_up_bwd[l])
        upb = jnp.concatenate([gla_up_fwd_bias[l], gla_up_bwd_bias[l]])[None, :]
        qg = jnp.tile(q_norm_gain[l], ATT_HEADS)[None, :]
        kg = jnp.tile(k_norm_gain[l], ATT_KV_HEADS)[None, :]
        gq, gk, gv, gg, laf, lab, aqt, ak, avt = _in_proj(
            h, B, T, tm, norm1_gain[l][None, :], w, up.astype(BF16), upb, qg, kg, head_mean, cos, se, so)
        o_gla = _gla(gq, gk, gv, gg, laf, lab, gla_out_gain[l][None, :], B, T)
        o_att_t = _attn(aqt, ak, avt, B, T, tq, key_block)

        wr = jnp.zeros((LANES, D), F32)
        wr = wr.at[:N_GROUPS].set(w_group[l].T).at[N_GROUPS:N_GROUPS + N_EXPERTS].set(w_expert[l].T)
        br = jnp.zeros((LANES, 1), F32)
        br = br.at[:N_GROUPS, 0].set(b_group[l]).at[N_GROUPS:N_GROUPS + N_EXPERTS, 0].set(b_expert[l])
        h, xn, eid, gate, rank, cnt = _out_proj(
            h, o_gla, o_att_t, w_out[l].astype(BF16), att_out_gain[l][:, None], norm2_gain[l][None, :],
            wr.astype(BF16), br, tm)

        counts = cnt[:, 0].astype(jnp.int32)
        padded = (counts + MOE_BLOCK - 1) // MOE_BLOCK * MOE_BLOCK
        pad_ends = jnp.cumsum(padded)
        pad_starts = pad_ends - padded
        seg_start = jnp.sum(jnp.where(eid[:, :, None] == jnp.arange(N_EXPERTS, dtype=jnp.int32),
                                      pad_starts.astype(jnp.int32), 0), axis=-1)
        dest = (seg_start + rank).reshape(-1)
        P = N * TOP_K + N_EXPERTS * MOE_BLOCK
        nblk = P // MOE_BLOCK
        blk_start = jnp.arange(nblk, dtype=jnp.int32) * MOE_BLOCK
        blk_expert = jnp.minimum(
            jnp.sum((pad_ends[None, :] <= blk_start[:, None]).astype(jnp.int32), axis=1), N_EXPERTS - 1)
        n_used = (pad_ends[-1:] // MOE_BLOCK).astype(jnp.int32)

        zero_starts = jnp.maximum(pad_ends - MOE_BLOCK, 0).astype(jnp.int32)
        xs, expert_w = _dispatch(zero_starts, n_used, dest, xn, (w_gate[l], w_up[l], w_down[l]), P, tm)
        ys = _moe_ffn(blk_expert, n_used, xs, *expert_w)
        h = _combine(h, ys, dest, gate.T, final_gain[None, :], tm, l == depth - 1)
    return h.reshape(B, T, D)
```

```python
import functools

import numpy as np
import jax
import jax.numpy as jnp
from jax import lax
from jax.experimental import pallas as pl
from jax.experimental.pallas import tpu as pltpu

F32 = jnp.float32
BF16 = jnp.bfloat16

D_MODEL = 1024
EPS = 1e-6
GRID_W = 64

GLA_HEADS = 4
GLA_DK = 64
GLA_DV = 128
GLA_KEY_WIDTH = GLA_HEADS * GLA_DK
GLA_WIDTH = GLA_HEADS * GLA_DV
GLA_GATE_RANK = 16
GLA_GATE_NORM = 16.0
GLA_CHUNK = 64

ATT_HEADS = 8
ATT_KV_HEADS = 2
ATT_HEAD_DIM = 64
ATT_GROUP = ATT_HEADS // ATT_KV_HEADS
ATT_WIDTH = ATT_HEADS * ATT_HEAD_DIM
ATT_KV_WIDTH = ATT_KV_HEADS * ATT_HEAD_DIM
ROPE_THETA = 10000.0
LOG2_E = 1.4426950408889634

N_GROUPS = 8
EXPERTS_PER_GROUP = 8
N_EXPERTS = N_GROUPS * EXPERTS_PER_GROUP
TOP_K = 2
D_EXPERT = 512
MOE_BLOCK = 256

LANES = 128
Z_PAD = LANES

TOKEN_TILE = 512
ATT_Q_BLOCK = 128
ATT_KEY_BLOCK = 256

_OFF_GQ = 0
_OFF_GK = _OFF_GQ + GLA_KEY_WIDTH
_OFF_GV = _OFF_GK + GLA_KEY_WIDTH
_OFF_GG = _OFF_GV + GLA_WIDTH
_OFF_AQ = _OFF_GG + GLA_WIDTH
_OFF_AK = _OFF_AQ + ATT_WIDTH
_OFF_AV = _OFF_AK + ATT_KV_WIDTH
_OFF_Z = _OFF_AV + ATT_KV_WIDTH
D_IN_PAD = _OFF_Z + Z_PAD

VMEM_LIMIT = 56 * 1024 * 1024


def _cparams(semantics):
    return pltpu.CompilerParams(dimension_semantics=semantics, vmem_limit_bytes=VMEM_LIMIT)


def _rms(x, gain):
    return x * lax.rsqrt(jnp.mean(x * x, axis=-1, keepdims=True) + EPS) * gain


def _sigmoid(x):
    return 1.0 / (1.0 + jnp.exp(-x))


def _head_norm_rope(x, head_mean, gain, cos, sin_even, sin_odd):
    w = x.shape[1]
    ms = jnp.dot((x * x).astype(BF16), head_mean, preferred_element_type=F32)
    xn = x * lax.rsqrt(ms + EPS) * gain
    reps = w // LANES
    tile = lambda t: t if reps == 1 else jnp.concatenate([t] * reps, axis=1)
    nxt = pltpu.roll(xn, w - 1, 1)
    prv = pltpu.roll(xn, 1, 1)
    return xn * tile(cos) + nxt * tile(sin_even) + prv * tile(sin_odd)


def _in_proj_kernel(x_ref, g1_ref, w_ref, up_ref, upb_ref, qg_ref, kg_ref, hm_ref,
                    cos_ref, se_ref, so_ref,
                    gq_ref, gk_ref, gv_ref, gg_ref, laf_ref, lab_ref, aqt_ref, ak_ref, avt_ref):
    hm = hm_ref[...]
    tm = x_ref.shape[0]

    def rows(rs):
        u = _rms(x_ref[rs, :], g1_ref[...]).astype(BF16)

        def proj(lo, hi):
            return jnp.dot(u, w_ref[:, lo:hi], preferred_element_type=F32)

        cos, se, so = cos_ref[rs, :], se_ref[rs, :], so_ref[rs, :]
        r_q = proj(_OFF_AQ, _OFF_AK)
        r_gqk = proj(_OFF_GQ, _OFF_GV)
        q = _head_norm_rope(r_q, hm, qg_ref[...], cos, se, so)
        aqt_ref[0, :, rs] = (q * (ATT_HEAD_DIM ** -0.5 * LOG2_E)).T.astype(BF16)
        gq_ref[rs, :] = r_gqk[:, :GLA_KEY_WIDTH].astype(BF16)
        gk_ref[rs, :] = r_gqk[:, GLA_KEY_WIDTH:].astype(BF16)

        r_kvz = proj(_OFF_AK, D_IN_PAD)
        r_gv = proj(_OFF_GV, _OFF_GG)
        k = _head_norm_rope(r_kvz[:, :ATT_KV_WIDTH], hm[:ATT_KV_WIDTH, :ATT_KV_WIDTH], kg_ref[...], cos, se, so)
        ak_ref[rs, :] = k.astype(BF16)
        avt_ref[0, :, rs] = r_kvz[:, ATT_KV_WIDTH:2 * ATT_KV_WIDTH].T.astype(BF16)
        gv_ref[rs, :] = r_gv.astype(BF16)

        z = r_kvz[:, 2 * ATT_KV_WIDTH:].astype(BF16)
        zl = jnp.dot(z, up_ref[...], preferred_element_type=F32) + upb_ref[...]
        r_gg = proj(_OFF_GG, _OFF_AQ)
        la = (jnp.minimum(zl, 0.0) - jnp.log(1.0 + jnp.exp(-jnp.abs(zl)))) * (1.0 / GLA_GATE_NORM)
        laf_ref[rs, :] = la[:, :GLA_KEY_WIDTH]
        lab_ref[rs, :] = la[:, GLA_KEY_WIDTH:]
        gg_ref[rs, :] = r_gg.astype(BF16)

    rows(slice(0, tm // 2))
    rows(slice(tm // 2, tm))


def _rope_tables(T):
    t = np.arange(T)
    row = (t // GRID_W).astype(np.float32)
    col = (t % GRID_W).astype(np.float32)
    axis_dim = ATT_HEAD_DIM // 2
    inv_freq = (ROPE_THETA ** (-np.arange(0, axis_dim, 2, dtype=np.float32) / axis_dim)).astype(np.float32)
    ang = np.concatenate([row[:, None] * inv_freq, col[:, None] * inv_freq], axis=-1)
    ang = np.repeat(ang, 2, axis=1)
    ang = np.tile(ang, (1, LANES // ATT_HEAD_DIM))
    even = (np.arange(LANES) % 2 == 0)[None, :]
    cos = np.cos(ang)
    sin = np.sin(ang)
    return (jnp.asarray(cos, F32), jnp.asarray(np.where(even, -sin, 0.0), F32),
            jnp.asarray(np.where(even, 0.0, sin), F32))


def _in_proj(xt, B, T, tm, g1, w, up, upb, qg, kg, hm, cos, se, so):
    N = xt.shape[0]
    tpb = T // tm
    const = lambda shape: pl.BlockSpec(shape, lambda i: (0,) * len(shape))
    rows = lambda width: pl.BlockSpec((tm, width), lambda i: (i, 0))
    pos = pl.BlockSpec((tm, LANES), lambda i: (i % tpb, 0))
    out_shape = (
        jax.ShapeDtypeStruct((N, GLA_KEY_WIDTH), BF16),
        jax.ShapeDtypeStruct((N, GLA_KEY_WIDTH), BF16),
        jax.ShapeDtypeStruct((N, GLA_WIDTH), BF16),
        jax.ShapeDtypeStruct((N, GLA_WIDTH), BF16),
        jax.ShapeDtypeStruct((N, GLA_KEY_WIDTH), F32),
        jax.ShapeDtypeStruct((N, GLA_KEY_WIDTH), F32),
        jax.ShapeDtypeStruct((B, ATT_WIDTH, T), BF16),
        jax.ShapeDtypeStruct((N, ATT_KV_WIDTH), BF16),
        jax.ShapeDtypeStruct((B, ATT_KV_WIDTH, T), BF16),
    )
    cols = lambda width: pl.BlockSpec((1, width, tm), lambda i: (i // tpb, 0, i % tpb))
    out_specs = (
        rows(GLA_KEY_WIDTH), rows(GLA_KEY_WIDTH), rows(GLA_WIDTH), rows(GLA_WIDTH),
        rows(GLA_KEY_WIDTH), rows(GLA_KEY_WIDTH),
        cols(ATT_WIDTH), rows(ATT_KV_WIDTH), cols(ATT_KV_WIDTH),
    )
    return pl.pallas_call(
        _in_proj_kernel,
        out_shape=out_shape,
        grid=(N // tm,),
        in_specs=[rows(D_MODEL), const((1, D_MODEL)), const((D_MODEL, D_IN_PAD)),
                  const((Z_PAD, 2 * GLA_KEY_WIDTH)), const((1, 2 * GLA_KEY_WIDTH)),
                  const((1, ATT_WIDTH)), const((1, ATT_KV_WIDTH)), const((ATT_WIDTH, ATT_WIDTH)),
                  pos, pos, pos],
        out_specs=out_specs,
        compiler_params=_cparams(("arbitrary",)),
        name="in_proj",
    )(xt, g1, w, up, upb, qg, kg, hm, cos, se, so)


GLA_GROUP = 4
GLA_FINISH_GROUP = 16


def _gla_kernel(q_ref, k_ref, v_ref, g_ref, laf_ref, lab_ref, gain_ref, o_ref,
                part_scr, qif_scr, qib_scr, kvf_scr, kvb_scr, decf_scr, decb_scr, lf_scr, lb_scr):
    C, G = GLA_CHUNK, GLA_GROUP
    R = C * G
    T = q_ref.shape[0]
    n_chunks = T // C
    gain = gain_ref[...]
    row = lax.broadcasted_iota(jnp.int32, (C, C), 0)
    col = lax.broadcasted_iota(jnp.int32, (C, C), 1)
    tril = row >= col
    triu = row <= col
    rr = lax.broadcasted_iota(jnp.int32, (R, R), 0)
    cc = lax.broadcasted_iota(jnp.int32, (R, R), 1)
    same_chunk = (rr // C) == (cc // C)
    lf_scr[...] = jnp.where(same_chunk & (rr >= cc), 1.0, 0.0).astype(BF16)
    lb_scr[...] = jnp.where(same_chunk & (rr <= cc), 1.0, 0.0).astype(BF16)

    directions = ((laf_ref, lf_scr, tril, C // 2 - 1, C - 1, qif_scr, kvf_scr, decf_scr),
                  (lab_ref, lb_scr, triu, C // 2, 0, qib_scr, kvb_scr, decb_scr))
    heads = tuple((slice(h * GLA_DK, (h + 1) * GLA_DK), slice(h * GLA_DV, (h + 1) * GLA_DV)) for h in range(2))
    nt = (((1,), (1,)), ((), ()))
    tn = (((0,), (0,)), ((), ()))

    def intra(sb, carry):
        r0 = pl.multiple_of(sb * R, R)
        q2 = q_ref[pl.ds(r0, R), :].astype(F32) * (GLA_DK ** -0.5)
        k2 = k_ref[pl.ds(r0, R), :].astype(F32)
        v2 = v_ref[pl.ds(r0, R), :]
        def cumulate(d):
            la_ref, tri_scr = d[0], d[1]
            la = la_ref[pl.ds(r0, R), :]
            la_hi = la.astype(BF16)
            la_lo = (la - la_hi.astype(F32)).astype(BF16)
            both = jnp.dot(tri_scr[...], jnp.concatenate([la_hi, la_lo], axis=1), preferred_element_type=F32)
            return both[:, :2 * GLA_DK] + both[:, 2 * GLA_DK:]

        def scale(d, b):
            _, _, _, i_ref, i_last, qi_scr, _, dec_scr = d
            out = []
            for c in range(G):
                rows = slice(c * C, (c + 1) * C)
                bc, qc, kc = b[rows], q2[rows], k2[rows]
                b_ref = bc[i_ref:i_ref + 1, :]
                b_last = bc[i_last:i_last + 1, :]
                qf = (qc * jnp.exp(bc - b_ref)).astype(BF16)
                kf = (kc * jnp.exp(b_ref - bc)).astype(BF16)
                qi_scr[pl.ds(r0 + c * C, C), :] = (qc * jnp.exp(bc)).astype(BF16)
                kl = (kc * jnp.exp(b_last - bc)).astype(BF16)
                decay = jnp.exp(b_last)
                for h, (ks, _) in enumerate(heads):
                    dec_scr[sb * G + c, h] = decay[:, ks]
                out.append((qf, kf, kl))
            return out

        def scores(scaled):
            return [[lax.dot_general(qf[:, ks], kf[:, ks], nt, preferred_element_type=F32) for ks, _ in heads]
                    for qf, kf, _ in scaled]

        def apply(d, scaled, raw):
            mask, kv_scr = d[2], d[6]
            outs = []
            for c in range(G):
                rows = slice(c * C, (c + 1) * C)
                per_head = []
                for h, (ks, vs) in enumerate(heads):
                    vh = v2[rows, vs]
                    sc = jnp.where(mask, raw[c][h], 0.0).astype(BF16)
                    per_head.append(jnp.dot(sc, vh, preferred_element_type=F32))
                    kv_scr[sb * G + c, h] = lax.dot_general(vh, scaled[c][2][:, ks], tn, preferred_element_type=F32)
                outs.append(per_head)
            return outs

        fwd, bwd = directions
        b_f = cumulate(fwd)
        b_b = cumulate(bwd)
        sc_f = scale(fwd, b_f)
        sc_b = scale(bwd, b_b)
        raw_f = scores(sc_f)
        raw_b = scores(sc_b)
        o_f = apply(fwd, sc_f, raw_f)
        o_b = apply(bwd, sc_b, raw_b)
        for c in range(G):
            part_scr[pl.ds(r0 + c * C, C), :] = jnp.concatenate(
                [o_f[c][h] + o_b[c][h] for h in range(2)], axis=1)
        return carry

    lax.fori_loop(0, n_chunks // G, intra, 0)

    def scan(n, carry):
        sf, sb = carry
        nb = n_chunks - 1 - n
        new_f, new_b = [], []
        for h in range(2):
            kv = kvf_scr[n, h]
            kvf_scr[n, h] = sf[h]
            new_f.append(decf_scr[n, h] * sf[h] + kv)
            kv = kvb_scr[nb, h]
            kvb_scr[nb, h] = sb[h]
            new_b.append(decb_scr[nb, h] * sb[h] + kv)
        return tuple(new_f), tuple(new_b)

    zero_state = tuple(jnp.zeros((GLA_DV, GLA_DK), F32) for _ in range(2))
    lax.fori_loop(0, n_chunks, scan, (zero_state, zero_state), unroll=4)

    GF = min(GLA_FINISH_GROUP, n_chunks)
    RF = C * GF

    def finish(sb, carry):
        r0 = pl.multiple_of(sb * RF, RF)
        g = g_ref[pl.ds(r0, RF), :].astype(F32)
        part = part_scr[pl.ds(r0, RF), :]
        qf = qif_scr[pl.ds(r0, RF), :]
        qb = qib_scr[pl.ds(r0, RF), :]
        inter = []
        for c in range(GF):
            rows = slice(c * C, (c + 1) * C)
            inter.append([
                lax.dot_general(qf[rows, ks], kvf_scr[sb * GF + c, h].astype(BF16), nt, preferred_element_type=F32)
                + lax.dot_general(qb[rows, ks], kvb_scr[sb * GF + c, h].astype(BF16), nt, preferred_element_type=F32)
                for h, (ks, _) in enumerate(heads)])
        gate = g * _sigmoid(g)
        for c in range(GF):
            rows = slice(c * C, (c + 1) * C)
            normed = jnp.concatenate([_rms(part[rows, vs] + inter[c][h], gain) for h, (_, vs) in enumerate(heads)],
                                     axis=1)
            o_ref[pl.ds(r0 + c * C, C), :] = (normed * gate[rows]).astype(BF16)
        return carry

    lax.fori_loop(0, n_chunks // GF, finish, 0)


def _gla(gq, gk, gv, gg, laf, lab, gain, B, T):
    N = gq.shape[0]
    pairs = GLA_HEADS // 2
    n_chunks = T // GLA_CHUNK
    assert n_chunks % GLA_GROUP == 0 and n_chunks % min(GLA_FINISH_GROUP, n_chunks) == 0
    group_rows = GLA_CHUNK * GLA_GROUP
    kspec = pl.BlockSpec((T, 2 * GLA_DK), lambda b, p: (b, p))
    vspec = pl.BlockSpec((T, 2 * GLA_DV), lambda b, p: (b, p))
    state = pltpu.VMEM((n_chunks, 2, GLA_DV, GLA_DK), F32)
    decay = pltpu.VMEM((n_chunks, 2, 1, GLA_DK), F32)
    return pl.pallas_call(
        _gla_kernel,
        out_shape=jax.ShapeDtypeStruct((N, GLA_WIDTH), BF16),
        grid=(B, pairs),
        in_specs=[kspec, kspec, vspec, vspec, kspec, kspec,
                  pl.BlockSpec((1, GLA_DV), lambda b, p: (0, 0))],
        out_specs=vspec,
        scratch_shapes=[pltpu.VMEM((T, 2 * GLA_DV), F32),
                        pltpu.VMEM((T, 2 * GLA_DK), BF16), pltpu.VMEM((T, 2 * GLA_DK), BF16),
                        state, state, decay, decay,
                        pltpu.VMEM((group_rows, group_rows), BF16), pltpu.VMEM((group_rows, group_rows), BF16)],
        compiler_params=_cparams(("arbitrary", "arbitrary")),
        name="gla",
    )(gq, gk, gv, gg, laf, lab, gain)


def _attn_kernel(qt_ref, k_ref, vt_ref, ot_ref, vext_scr, s0_scr, s1_scr, p0_scr, p1_scr, m0_scr, m1_scr,
                 *, tq, key_block, steps_per_iter):
    T = k_ref.shape[0]
    nq = T // tq
    n_blocks = ATT_KV_HEADS * nq
    dh = ATT_HEAD_DIM
    gw = ATT_GROUP * dh
    cols = ATT_GROUP * tq

    row = lax.broadcasted_iota(jnp.int32, (ATT_KV_WIDTH, T), 0)
    for hj in range(ATT_KV_HEADS):
        own = (row >= hj * dh) & (row < (hj + 1) * dh)
        vext_scr[hj] = jnp.where(own, vt_ref[0], jnp.ones((), BF16))

    s_scr, p_scr, m_scr = (s0_scr, s1_scr), (p0_scr, p1_scr), (m0_scr, m1_scr)

    def locate(blk):
        if isinstance(blk, int):
            hj = blk // nq
            return hj, hj == 0, (blk - hj * nq) * tq
        second = blk >= nq
        hj = jnp.where(second, 1, 0)
        return hj, jnp.logical_not(second), pl.multiple_of((blk - hj * nq) * tq, tq)

    def step(blk, sa, do_scores=True, do_probs=True, do_out=True):
        sb = 1 - sa
        if do_scores:
            hj, first, c0 = locate(blk)
            q4 = jnp.concatenate(
                [qt_ref[0, pl.ds(pl.multiple_of(hj * gw + h * dh, dh), dh), pl.ds(c0, tq)] for h in range(ATT_GROUP)],
                axis=1)
            zeros = jnp.zeros_like(q4)
            qe = jnp.where(first, jnp.concatenate([q4, zeros], axis=0), jnp.concatenate([zeros, q4], axis=0))
            m = jnp.full((8, cols), -jnp.inf, F32)
        if do_probs:
            mx = m_scr[sb][...]
        if do_out:
            hj_o, first_o, c0_o = locate(blk - 2)
            acc = jnp.zeros((ATT_KV_WIDTH, cols), F32)
        for kb in range(T // key_block):
            ks = slice(kb * key_block, (kb + 1) * key_block)
            if do_scores:
                s = jnp.dot(k_ref[ks, :], qe, preferred_element_type=F32)
                s_scr[sa][ks, :] = s
                m = jnp.maximum(m, jnp.max(s.reshape(key_block // 8, 8, cols), axis=0))
            if do_probs:
                p_scr[sb][ks, :] = jnp.exp2(s_scr[sb][ks, :] - mx).astype(BF16)
            if do_out:
                acc = acc + jnp.dot(vext_scr[hj_o, :, ks], p_scr[sa][ks, :], preferred_element_type=F32)
        if do_scores:
            m_scr[sa][...] = jnp.max(m, axis=0, keepdims=True)
        if do_out:
            num = jnp.where(first_o, acc[:dh], acc[dh:])
            den = jnp.where(first_o, acc[dh:dh + 1], acc[0:1])
            o = (num / den).astype(BF16)
            for h in range(ATT_GROUP):
                ot_ref[0, pl.ds(pl.multiple_of(hj_o * gw + h * dh, dh), dh), pl.ds(c0_o, tq)] = (
                    o[:, h * tq:(h + 1) * tq])

    step(0, 0, do_probs=False, do_out=False)
    step(1, 1, do_out=False)

    def steady(it, carry):
        for u in range(steps_per_iter):
            pl.when(it >= 0)(functools.partial(step, 2 + steps_per_iter * it + u, u % 2))
        return carry

    lax.fori_loop(0, (n_blocks - 2) // steps_per_iter, steady, 0)
    step(n_blocks, 0, do_scores=False)
    step(n_blocks + 1, 1, do_scores=False, do_probs=False)


def _attn(aqt, ak, avt, B, T, tq, key_block):
    assert T % (2 * tq) == 0 and tq % LANES == 0 and ATT_KV_HEADS == 2
    cols = ATT_GROUP * tq
    steps_per_iter = 2
    return pl.pallas_call(
        functools.partial(_attn_kernel, tq=tq, key_block=key_block, steps_per_iter=steps_per_iter),
        out_shape=jax.ShapeDtypeStruct((B, ATT_WIDTH, T), BF16),
        grid=(B,),
        in_specs=[pl.BlockSpec((1, ATT_WIDTH, T), lambda b: (b, 0, 0)),
                  pl.BlockSpec((T, ATT_KV_WIDTH), lambda b: (b, 0)),
                  pl.BlockSpec((1, ATT_KV_WIDTH, T), lambda b: (b, 0, 0))],
        out_specs=pl.BlockSpec((1, ATT_WIDTH, T), lambda b: (b, 0, 0)),
        scratch_shapes=[pltpu.VMEM((ATT_KV_HEADS, ATT_KV_WIDTH, T), BF16),
                        pltpu.VMEM((T, cols), F32), pltpu.VMEM((T, cols), F32),
                        pltpu.VMEM((T, cols), BF16), pltpu.VMEM((T, cols), BF16),
                        pltpu.VMEM((1, cols), F32), pltpu.VMEM((1, cols), F32)],
        compiler_params=_cparams(("arbitrary",)),
        name="attn",
    )(aqt, ak, avt)


def _out_proj_kernel(x_ref, og_ref, oa_ref, wo_ref, ag_ref, g2_ref, wr_ref, br_ref, tri_ref,
                     h_ref, xn_ref, eid_ref, gate_ref, rank_ref, cnt_ref, run_scr):
    tm = x_ref.shape[0]

    @pl.when(pl.program_id(0) == 0)
    def _():
        run_scr[...] = jnp.zeros_like(run_scr)

    def branch_norm(ts):
        oat = oa_ref[0, :, ts].astype(F32)
        return (oat * lax.rsqrt(jnp.mean(oat * oat, axis=0, keepdims=True) + EPS) * ag_ref[...]).astype(BF16)

    def mix(ts, oan):
        return (jnp.dot(og_ref[ts, :], wo_ref[:GLA_WIDTH, :], preferred_element_type=F32)
                + lax.dot_general(oan, wo_ref[GLA_WIDTH:, :], (((0,), (0,)), ((), ())),
                                  preferred_element_type=F32))

    def residual_norm(ts, y):
        h = x_ref[ts, :] + y
        h_ref[ts, :] = h
        xn = _rms(h, g2_ref[...])
        xn_ref[ts, :] = _pack_pairs(xn)
        return xn.astype(BF16)

    def logits(xnb):
        return lax.dot_general(wr_ref[...], xnb, (((1,), (1,)), ((), ())), preferred_element_type=F32) + br_ref[...]

    whole = slice(0, tm)
    lt = logits(residual_norm(whole, mix(whole, branch_norm(whole))))
    iota8 = lax.broadcasted_iota(jnp.int32, (N_GROUPS, tm), 0)

    def first_argmax(v):
        top = jnp.max(v, axis=0, keepdims=True)
        idx = jnp.min(jnp.where(v == top, iota8, N_GROUPS), axis=0, keepdims=True)
        return top, idx

    gl = lt[0:N_GROUPS]
    gmax, gidx = first_argmax(gl)
    gw = 1.0 / jnp.sum(jnp.exp(gl - gmax), axis=0, keepdims=True)
    esel = jnp.zeros((EXPERTS_PER_GROUP, tm), F32)
    for g in range(N_GROUPS):
        lo = N_GROUPS + g * EXPERTS_PER_GROUP
        esel = jnp.where(gidx == g, lt[lo:lo + EXPERTS_PER_GROUP], esel)
    v1, i1 = first_argmax(esel)
    rest = jnp.where(iota8 == i1, -jnp.inf, esel)
    v2, i2 = first_argmax(rest)
    t = jnp.exp(v2 - v1)
    den = 1.0 + t
    e1 = gidx * EXPERTS_PER_GROUP + i1
    e2 = gidx * EXPERTS_PER_GROUP + i2
    eid_ref[0:1, :] = e1
    eid_ref[1:2, :] = e2
    gate_ref[0:1, :] = gw * (1.0 / den)
    gate_ref[1:2, :] = gw * (t / den)

    iota_e = lax.broadcasted_iota(jnp.int32, (N_EXPERTS, tm), 0)
    oh1 = iota_e == e1
    oh2 = iota_e == e2
    both = jnp.where(oh1, 1.0, jnp.where(oh2, 1.0, 0.0)).astype(BF16)
    prefix = jnp.dot(both, tri_ref[...], preferred_element_type=F32)
    run = run_scr[...]
    base = prefix - 1.0 + run[:, 0:1]
    rank_ref[0:1, :] = jnp.sum(jnp.where(oh1, base, 0.0), axis=0, keepdims=True).astype(jnp.int32)
    rank_ref[1:2, :] = jnp.sum(jnp.where(oh2, base, 0.0), axis=0, keepdims=True).astype(jnp.int32)
    run = run + prefix[:, tm - 1:tm]
    run_scr[...] = run
    cnt_ref[...] = run


def _out_proj(xt, o_gla, o_att_t, wo, ag, g2, wr, br, tm):
    N = xt.shape[0]
    tpb = o_att_t.shape[2] // tm
    const = lambda shape: pl.BlockSpec(shape, lambda i: (0,) * len(shape))
    rows = lambda width: pl.BlockSpec((tm, width), lambda i: (i, 0))
    cols = pl.BlockSpec((TOP_K, tm), lambda i: (0, i))
    tri = jnp.asarray(np.triu(np.ones((tm, tm), np.float32)), BF16)
    return pl.pallas_call(
        _out_proj_kernel,
        out_shape=(jax.ShapeDtypeStruct((N, D_MODEL), F32),
                   jax.ShapeDtypeStruct((N, D_PACKED), jnp.uint32),
                   jax.ShapeDtypeStruct((TOP_K, N), jnp.int32),
                   jax.ShapeDtypeStruct((TOP_K, N), F32),
                   jax.ShapeDtypeStruct((TOP_K, N), jnp.int32),
                   jax.ShapeDtypeStruct((N_EXPERTS, LANES), F32)),
        grid=(N // tm,),
        in_specs=[rows(D_MODEL), rows(GLA_WIDTH),
                  pl.BlockSpec((1, ATT_WIDTH, tm), lambda i: (i // tpb, 0, i % tpb)),
                  const((D_MODEL, D_MODEL)),
                  const((ATT_WIDTH, 1)), const((1, D_MODEL)), const((LANES, D_MODEL)),
                  const((LANES, 1)), const((tm, tm))],
        out_specs=(rows(D_MODEL), rows(D_PACKED), cols, cols, cols, const((N_EXPERTS, LANES))),
        scratch_shapes=[pltpu.VMEM((N_EXPERTS, LANES), F32)],
        compiler_params=_cparams(("arbitrary",)),
        name="out_proj",
    )(xt, o_gla, o_att_t, wo, ag, g2, wr, br, tri)


D_PACKED = D_MODEL // 2


def _pack_pairs(x):
    w = x.shape[1] // 2
    bits = pltpu.bitcast(x.astype(BF16).astype(F32), jnp.uint32)
    return bits[:, :w] | (bits[:, w:] >> 16)


def _unpack_pairs(words):
    hi = pltpu.bitcast(words & jnp.uint32(0xFFFF0000), F32)
    lo = pltpu.bitcast(words << 16, F32)
    return jnp.concatenate([hi, lo], axis=1)


def _dispatch_kernel(zs_ref, nu_ref, *refs):
    dest_refs = refs[:TOP_K]
    packed = refs[TOP_K]
    w_in = refs[TOP_K + 1:TOP_K + 4]
    xs_hbm = refs[TOP_K + 4]
    w_out = refs[TOP_K + 5:TOP_K + 8]
    zero_buf, zero_sem, row_sem = refs[TOP_K + 8:]
    tn = packed.shape[0]
    i = pl.program_id(0)

    def fill(row0):
        return pltpu.make_async_copy(
            zero_buf, xs_hbm.at[pl.ds(pl.multiple_of(row0, MOE_BLOCK), MOE_BLOCK)], zero_sem)

    @pl.when(i == 0)
    def _():
        zero_buf[...] = jnp.zeros_like(zero_buf)
        fills = [fill(zs_ref[e]) for e in range(N_EXPERTS)]
        for f in fills:
            f.start()
        for f in fills:
            f.wait()

        n_blocks = xs_hbm.shape[0] // MOE_BLOCK
        lax.fori_loop(nu_ref[0], n_blocks, lambda blk, c: (fill(blk * MOE_BLOCK).start(), c)[1], 0)
        lax.fori_loop(nu_ref[0], n_blocks, lambda blk, c: (fill(blk * MOE_BLOCK).wait(), c)[1], 0)

    n_iter = tn // 16
    epw = w_in[0].shape[0]

    def issue(it, carry):
        chunks = []
        for src, dst in zip(w_in, w_out):
            rows = src.shape[1] // n_iter
            r0 = pl.multiple_of(it * rows, rows)
            for e in range(epw):
                chunks.append((dst, e, r0, rows, src[e, pl.ds(r0, rows), :]))
        for u in range(2):
            g = it * 2 + u
            tile = packed.at[pl.ds(pl.multiple_of(g * 8, 8), 8)]
            for s in range(8):
                for k in range(TOP_K):
                    pltpu.make_async_copy(tile.at[pl.ds(s, 1)], xs_hbm.at[pl.ds(dest_refs[k][g * 8 + s], 1)],
                                          row_sem).start(priority=k % 2)
        for dst, e, r0, rows, val in chunks:
            dst[e, pl.ds(r0, rows), :] = val.astype(BF16)
        return carry

    lax.fori_loop(0, n_iter, issue, 0)
    for k in range(TOP_K):
        pltpu.make_async_copy(packed, xs_hbm.at[pl.ds(0, tn)], row_sem).wait()


def _dest_specs(n_steps, tn, index):
    return [pl.BlockSpec((tn,), lambda i, *_, k=k: (k * n_steps + index(i),), memory_space=pltpu.SMEM)
            for k in range(TOP_K)]


def _dispatch(zero_starts, n_used, dest_flat, xn, weights, P, tn):
    N = xn.shape[0]
    n_steps = N // tn
    assert N_EXPERTS % n_steps == 0
    epw = N_EXPERTS // n_steps
    w_specs = [pl.BlockSpec((epw,) + w.shape[1:], lambda i, *_: (i, 0, 0)) for w in weights]
    for w in weights:
        assert w.shape[1] % (16 * (tn // 16)) == 0
    grid_spec = pltpu.PrefetchScalarGridSpec(
        num_scalar_prefetch=2,
        grid=(n_steps,),
        in_specs=_dest_specs(n_steps, tn, lambda i: i) + [pl.BlockSpec((tn, D_PACKED), lambda i, zs, nu: (i, 0))]
        + w_specs,
        out_specs=[pl.BlockSpec(memory_space=pl.ANY)] + w_specs,
        scratch_shapes=[pltpu.VMEM((MOE_BLOCK, D_PACKED), jnp.uint32),
                        pltpu.SemaphoreType.DMA, pltpu.SemaphoreType.DMA],
    )
    xs, *w_bf16 = pl.pallas_call(
        _dispatch_kernel,
        out_shape=[jax.ShapeDtypeStruct((P, D_PACKED), jnp.uint32)]
        + [jax.ShapeDtypeStruct(w.shape, BF16) for w in weights],
        grid_spec=grid_spec,
        compiler_params=_cparams(("arbitrary",)),
        name="dispatch",
    )(zero_starts, n_used, *([dest_flat] * TOP_K), xn, *weights)
    return xs, w_bf16


def _moe_ffn_kernel(be_ref, nu_ref, seg_ref, nxt_ref, xs_ref, wg_hbm, wu_hbm, wd_hbm, ys_ref,
                    wg_b, wu_b, wd_b, sems):
    i = pl.program_id(0)
    used = i < nu_ref[0]

    def weights(expert, slot):
        return [pltpu.make_async_copy(src.at[expert], dst.at[slot], sems.at[slot])
                for src, dst in ((wg_hbm, wg_b), (wu_hbm, wu_b), (wd_hbm, wd_b))]

    @pl.when(used)
    def _():
        new_expert = jnp.logical_or(i == 0, be_ref[i] != be_ref[jnp.maximum(i - 1, 0)])
        slot = seg_ref[i] % 2

        @pl.when(new_expert)
        def _():
            @pl.when(i == 0)
            def _():
                for cp in weights(be_ref[0], 0):
                    cp.start()

            for cp in weights(be_ref[i], slot):
                cp.wait()

            @pl.when(nxt_ref[i] >= 0)
            def _():
                for cp in weights(nxt_ref[i], 1 - slot):
                    cp.start()

        def gate_up(rows):
            xb = _unpack_pairs(xs_ref[rows, :]).astype(BF16)
            return (jnp.dot(xb, wg_b[slot], preferred_element_type=F32),
                    jnp.dot(xb, wu_b[slot], preferred_element_type=F32))

        def swish_mul(hg, hu):
            return (hg * _sigmoid(hg) * hu).astype(BF16)

        def down(rows, hm):
            ys_ref[rows, :] = _pack_pairs(jnp.dot(hm, wd_b[slot], preferred_element_type=F32))

        ra, rb = slice(0, MOE_BLOCK // 2), slice(MOE_BLOCK // 2, MOE_BLOCK)
        gu_a = gate_up(ra)
        hm_a = swish_mul(*gu_a)
        gu_b = gate_up(rb)
        down(ra, hm_a)
        hm_b = swish_mul(*gu_b)
        down(rb, hm_b)

    @pl.when(jnp.logical_not(used))
    def _():
        ys_ref[...] = jnp.zeros_like(ys_ref)


def _moe_ffn(blk_expert, n_used, xs, w_gate, w_up, w_down):
    P = xs.shape[0]
    nblk = P // MOE_BLOCK
    blk = jnp.arange(nblk, dtype=jnp.int32)
    starts = jnp.logical_and(jnp.concatenate([jnp.ones((1,), bool), blk_expert[1:] != blk_expert[:-1]]),
                             blk < n_used[0])
    seg = jnp.cumsum(starts.astype(jnp.int32)) - 1
    next_start = lax.cummin(jnp.where(starts, blk, nblk), reverse=True)
    next_start = jnp.concatenate([next_start[1:], jnp.full((1,), nblk, jnp.int32)])
    nxt = jnp.where(next_start < nblk, blk_expert[jnp.minimum(next_start, nblk - 1)], -1).astype(jnp.int32)
    row_block = pl.BlockSpec((MOE_BLOCK, D_PACKED), lambda i, *_: (i, 0))
    grid_spec = pltpu.PrefetchScalarGridSpec(
        num_scalar_prefetch=4,
        grid=(nblk,),
        in_specs=[row_block] + [pl.BlockSpec(memory_space=pl.ANY)] * 3,
        out_specs=row_block,
        scratch_shapes=[pltpu.VMEM((2, D_MODEL, D_EXPERT), BF16),
                        pltpu.VMEM((2, D_MODEL, D_EXPERT), BF16),
                        pltpu.VMEM((2, D_EXPERT, D_MODEL), BF16),
                        pltpu.SemaphoreType.DMA((2,))],
    )
    return pl.pallas_call(
        _moe_ffn_kernel,
        out_shape=jax.ShapeDtypeStruct((P, D_PACKED), jnp.uint32),
        grid_spec=grid_spec,
        compiler_params=_cparams(("arbitrary",)),
        name="moe_ffn",
    )(blk_expert, n_used, seg, nxt, xs, w_gate, w_up, w_down)


def _combine_kernel(*refs, final):
    dcur_refs, dnext_refs = refs[:TOP_K], refs[TOP_K:2 * TOP_K]
    h_ref, gate_ref, fg_ref, ys_hbm, o_ref, ybuf, sems = refs[2 * TOP_K:]
    tm = h_ref.shape[0]
    i = pl.program_id(0)
    n = pl.num_programs(0)

    def gather(drefs, slot):
        def issue(g, carry):
            for k in range(TOP_K):
                tile = ybuf.at[slot, pl.ds(pl.multiple_of(k * tm + g * 8, 8), 8)]
                for s in range(8):
                    pltpu.make_async_copy(ys_hbm.at[pl.ds(drefs[k][g * 8 + s], 1)],
                                          tile.at[pl.ds(s, 1)], sems.at[slot]).start(priority=k % 2)
            return carry

        lax.fori_loop(0, tm // 8, issue, 0)

    @pl.when(i == 0)
    def _():
        gather(dcur_refs, 0)

    @pl.when(i + 1 < n)
    def _():
        gather(dnext_refs, (i + 1) % 2)

    slot = i % 2
    pltpu.make_async_copy(ys_hbm.at[pl.ds(0, TOP_K * tm)], ybuf.at[slot], sems.at[slot]).wait()
    g = gate_ref[...]
    hh = h_ref[...]
    for k in range(TOP_K):
        hh = hh + _unpack_pairs(ybuf[slot, pl.ds(k * tm, tm), :]) * g[:, k:k + 1]
    o_ref[...] = _rms(hh, fg_ref[...]) if final else hh


def _combine(h, ys, dest_flat, gate_t, fg, tm, final):
    N = h.shape[0]
    n = N // tm
    rows = pl.BlockSpec((tm, D_MODEL), lambda i: (i, 0))
    return pl.pallas_call(
        functools.partial(_combine_kernel, final=final),
        out_shape=jax.ShapeDtypeStruct((N, D_MODEL), F32),
        grid=(n,),
        in_specs=_dest_specs(n, tm, lambda i: i) + _dest_specs(n, tm, lambda i: jnp.minimum(i + 1, n - 1)) + [
                  rows, pl.BlockSpec((tm, TOP_K), lambda i: (i, 0)),
                  pl.BlockSpec((1, D_MODEL), lambda i: (0, 0)),
                  pl.BlockSpec(memory_space=pl.ANY)],
        out_specs=rows,
        scratch_shapes=[pltpu.VMEM((2, TOP_K * tm, D_PACKED), jnp.uint32), pltpu.SemaphoreType.DMA((2,))],
        compiler_params=_cparams(("arbitrary",)),
        name="combine",
    )(*([dest_flat] * (2 * TOP_K)), h, gate_t, fg, ys)


def _reorder_w_in(w_in):
    n_gla = 2 * GLA_KEY_WIDTH + 2 * GLA_WIDTH
    z0 = n_gla
    a0 = z0 + 2 * GLA_GATE_RANK
    pad = jnp.zeros((D_MODEL, Z_PAD - 2 * GLA_GATE_RANK), w_in.dtype)
    return jnp.concatenate([w_in[:, :n_gla], w_in[:, a0:], w_in[:, z0:a0], pad], axis=1)


def kernel(x, norm1_gain, w_in, gla_up_fwd, gla_up_fwd_bias, gla_up_bwd, gla_up_bwd_bias, gla_out_gain, q_norm_gain, k_norm_gain, att_out_gain, w_out, norm2_gain, w_group, b_group, w_expert, b_expert, w_gate, w_up, w_down, final_gain):
    B, T, D = x.shape
    N = B * T
    depth = norm1_gain.shape[0]
    assert D == D_MODEL and T % TOKEN_TILE == 0 and T % GRID_W == 0
    tm, tq, key_block = TOKEN_TILE, ATT_Q_BLOCK, ATT_KEY_BLOCK
    h = x.reshape(N, D)
    cos, se, so = _rope_tables(T)
    head_mean = jnp.asarray(
        np.kron(np.eye(ATT_HEADS, dtype=np.float32),
                np.full((ATT_HEAD_DIM, ATT_HEAD_DIM), 1.0 / ATT_HEAD_DIM, np.float32)), BF16)
    for l in range(depth):
        w = _reorder_w_in(w_in[l]).astype(BF16)
        r = GLA_GATE_RANK
        up = jnp.zeros((Z_PAD, 2 * GLA_KEY_WIDTH), F32)
        up = up.at[:r, :GLA_KEY_WIDTH].set(gla_up_fwd[l]).at[r:2 * r, GLA_KEY_WIDTH:].set(gla_up_bwd[l])
        upb = jnp.concatenate([gla_up_fwd_bias[l], gla_up_bwd_bias[l]])[None, :]
        qg = jnp.tile(q_norm_gain[l], ATT_HEADS)[None, :]
        kg = jnp.tile(k_norm_gain[l], ATT_KV_HEADS)[None, :]
        gq, gk, gv, gg, laf, lab, aqt, ak, avt = _in_proj(
            h, B, T, tm, norm1_gain[l][None, :], w, up.astype(BF16), upb, qg, kg, head_mean, cos, se, so)
        o_gla = _gla(gq, gk, gv, gg, laf, lab, gla_out_gain[l][None, :], B, T)
        o_att_t = _attn(aqt, ak, avt, B, T, tq, key_block)

        wr = jnp.zeros((LANES, D), F32)
        wr = wr.at[:N_GROUPS].set(w_group[l].T).at[N_GROUPS:N_GROUPS + N_EXPERTS].set(w_expert[l].T)
        br = jnp.zeros((LANES, 1), F32)
        br = br.at[:N_GROUPS, 0].set(b_group[l]).at[N_GROUPS:N_GROUPS + N_EXPERTS, 0].set(b_expert[l])
        h, xn, eid, gate, rank, cnt = _out_proj(
            h, o_gla, o_att_t, w_out[l].astype(BF16), att_out_gain[l][:, None], norm2_gain[l][None, :],
            wr.astype(BF16), br, tm)

        counts = cnt[:, 0].astype(jnp.int32)
        padded = (counts + MOE_BLOCK - 1) // MOE_BLOCK * MOE_BLOCK
        pad_ends = jnp.cumsum(padded)
        pad_starts = pad_ends - padded
        seg_start = jnp.sum(jnp.where(eid[:, :, None] == jnp.arange(N_EXPERTS, dtype=jnp.int32),
                                      pad_starts.astype(jnp.int32), 0), axis=-1)
        dest = (seg_start + rank).reshape(-1)
        P = N * TOP_K + N_EXPERTS * MOE_BLOCK
        nblk = P // MOE_BLOCK
        blk_start = jnp.arange(nblk, dtype=jnp.int32) * MOE_BLOCK
        blk_expert = jnp.minimum(
            jnp.sum((pad_ends[None, :] <= blk_start[:, None]).astype(jnp.int32), axis=1), N_EXPERTS - 1)
        n_used = (pad_ends[-1:] // MOE_BLOCK).astype(jnp.int32)

        zero_starts = jnp.maximum(pad_ends - MOE_BLOCK, 0).astype(jnp.int32)
        xs, expert_w = _dispatch(zero_starts, n_used, dest, xn, (w_gate[l], w_up[l], w_down[l]), P, tm)
        ys = _moe_ffn(blk_expert, n_used, xs, *expert_w)
        h = _combine(h, ys, dest, gate.T, final_gain[None, :], tm, l == depth - 1)
    return h.reshape(B, T, D)
```

```python
import functools

import numpy as np
import jax
import jax.numpy as jnp
from jax import lax
from jax.experimental import pallas as pl
from jax.experimental.pallas import tpu as pltpu

F32 = jnp.float32
BF16 = jnp.bfloat16

D_MODEL = 1024
EPS = 1e-6
GRID_W = 64

GLA_HEADS = 4
GLA_DK = 64
GLA_DV = 128
GLA_KEY_WIDTH = GLA_HEADS * GLA_DK
GLA_WIDTH = GLA_HEADS * GLA_DV
GLA_GATE_RANK = 16
GLA_GATE_NORM = 16.0
GLA_CHUNK = 64

ATT_HEADS = 8
ATT_KV_HEADS = 2
ATT_HEAD_DIM = 64
ATT_GROUP = ATT_HEADS // ATT_KV_HEADS
ATT_WIDTH = ATT_HEADS * ATT_HEAD_DIM
ATT_KV_WIDTH = ATT_KV_HEADS * ATT_HEAD_DIM
ROPE_THETA = 10000.0
LOG2_E = 1.4426950408889634

N_GROUPS = 8
EXPERTS_PER_GROUP = 8
N_EXPERTS = N_GROUPS * EXPERTS_PER_GROUP
TOP_K = 2
D_EXPERT = 512
MOE_BLOCK = 256

LANES = 128
Z_PAD = LANES

TOKEN_TILE = 512
ATT_Q_BLOCK = 128
ATT_KEY_BLOCK = 256
IN_PROJ_PARTS = 2

_OFF_GQ = 0
_OFF_GK = _OFF_GQ + GLA_KEY_WIDTH
_OFF_GV = _OFF_GK + GLA_KEY_WIDTH
_OFF_GG = _OFF_GV + GLA_WIDTH
_OFF_AQ = _OFF_GG + GLA_WIDTH
_OFF_AK = _OFF_AQ + ATT_WIDTH
_OFF_AV = _OFF_AK + ATT_KV_WIDTH
_OFF_Z = _OFF_AV + ATT_KV_WIDTH
D_IN_PAD = _OFF_Z + Z_PAD

VMEM_LIMIT = 56 * 1024 * 1024


def _cparams(semantics):
    return pltpu.CompilerParams(dimension_semantics=semantics, vmem_limit_bytes=VMEM_LIMIT)


def _rms(x, gain):
    return x * lax.rsqrt(jnp.mean(x * x, axis=-1, keepdims=True) + EPS) * gain


def _sigmoid(x):
    return 1.0 / (1.0 + jnp.exp(-x))


def _head_norm_rope(x, head_mean, gain, cos, sin_even, sin_odd):
    w = x.shape[1]
    ms = jnp.dot((x * x).astype(BF16), head_mean, preferred_element_type=F32)
    xn = x * lax.rsqrt(ms + EPS) * gain
    reps = w // LANES
    tile = lambda t: t if reps == 1 else jnp.concatenate([t] * reps, axis=1)
    nxt = pltpu.roll(xn, w - 1, 1)
    prv = pltpu.roll(xn, 1, 1)
    return xn * tile(cos) + nxt * tile(sin_even) + prv * tile(sin_odd)


def _in_proj_kernel(x_ref, g1_ref, w_ref, up_ref, upb_ref, qg_ref, kg_ref, hm_ref,
                    cos_ref, se_ref, so_ref,
                    gq_ref, gk_ref, gv_ref, gg_ref, laf_ref, lab_ref, aqt_ref, ak_ref, avt_ref):
    hm = hm_ref[...]
    tm = x_ref.shape[0]

    def rows(rs):
        u = _rms(x_ref[rs, :], g1_ref[...]).astype(BF16)

        def proj(lo, hi):
            return jnp.dot(u, w_ref[:, lo:hi], preferred_element_type=F32)

        cos, se, so = cos_ref[rs, :], se_ref[rs, :], so_ref[rs, :]
        r_q = proj(_OFF_AQ, _OFF_AK)
        r_gqk = proj(_OFF_GQ, _OFF_GV)
        q = _head_norm_rope(r_q, hm, qg_ref[...], cos, se, so)
        aqt_ref[0, :, rs] = (q * (ATT_HEAD_DIM ** -0.5 * LOG2_E)).T.astype(BF16)
        gq_ref[rs, :] = r_gqk[:, :GLA_KEY_WIDTH].astype(BF16)
        gk_ref[rs, :] = r_gqk[:, GLA_KEY_WIDTH:].astype(BF16)

        r_kvz = proj(_OFF_AK, D_IN_PAD)
        r_gv = proj(_OFF_GV, _OFF_GG)
        k = _head_norm_rope(r_kvz[:, :ATT_KV_WIDTH], hm[:ATT_KV_WIDTH, :ATT_KV_WIDTH], kg_ref[...], cos, se, so)
        ak_ref[rs, :] = k.astype(BF16)
        avt_ref[0, :, rs] = r_kvz[:, ATT_KV_WIDTH:2 * ATT_KV_WIDTH].T.astype(BF16)
        gv_ref[rs, :] = r_gv.astype(BF16)

        z = r_kvz[:, 2 * ATT_KV_WIDTH:].astype(BF16)
        zl = jnp.dot(z, up_ref[...], preferred_element_type=F32) + upb_ref[...]
        r_gg = proj(_OFF_GG, _OFF_AQ)
        la = (jnp.minimum(zl, 0.0) - jnp.log(1.0 + jnp.exp(-jnp.abs(zl)))) * (1.0 / GLA_GATE_NORM)
        laf_ref[rs, :] = la[:, :GLA_KEY_WIDTH]
        lab_ref[rs, :] = la[:, GLA_KEY_WIDTH:]
        gg_ref[rs, :] = r_gg.astype(BF16)

    for p in range(IN_PROJ_PARTS):
        rows(slice(p * tm // IN_PROJ_PARTS, (p + 1) * tm // IN_PROJ_PARTS))


def _rope_tables(T):
    t = np.arange(T)
    row = (t // GRID_W).astype(np.float32)
    col = (t % GRID_W).astype(np.float32)
    axis_dim = ATT_HEAD_DIM // 2
    inv_freq = (ROPE_THETA ** (-np.arange(0, axis_dim, 2, dtype=np.float32) / axis_dim)).astype(np.float32)
    ang = np.concatenate([row[:, None] * inv_freq, col[:, None] * inv_freq], axis=-1)
    ang = np.repeat(ang, 2, axis=1)
    ang = np.tile(ang, (1, LANES // ATT_HEAD_DIM))
    even = (np.arange(LANES) % 2 == 0)[None, :]
    cos = np.cos(ang)
    sin = np.sin(ang)
    return (jnp.asarray(cos, F32), jnp.asarray(np.where(even, -sin, 0.0), F32),
            jnp.asarray(np.where(even, 0.0, sin), F32))


def _in_proj(xt, B, T, tm, g1, w, up, upb, qg, kg, hm, cos, se, so):
    N = xt.shape[0]
    tpb = T // tm
    const = lambda shape: pl.BlockSpec(shape, lambda i: (0,) * len(shape))
    rows = lambda width: pl.BlockSpec((tm, width), lambda i: (i, 0))
    pos = pl.BlockSpec((tm, LANES), lambda i: (i % tpb, 0))
    out_shape = (
        jax.ShapeDtypeStruct((N, GLA_KEY_WIDTH), BF16),
        jax.ShapeDtypeStruct((N, GLA_KEY_WIDTH), BF16),
        jax.ShapeDtypeStruct((N, GLA_WIDTH), BF16),
        jax.ShapeDtypeStruct((N, GLA_WIDTH), BF16),
        jax.ShapeDtypeStruct((N, GLA_KEY_WIDTH), F32),
        jax.ShapeDtypeStruct((N, GLA_KEY_WIDTH), F32),
        jax.ShapeDtypeStruct((B, ATT_WIDTH, T), BF16),
        jax.ShapeDtypeStruct((N, ATT_KV_WIDTH), BF16),
        jax.ShapeDtypeStruct((B, ATT_KV_WIDTH, T), BF16),
    )
    cols = lambda width: pl.BlockSpec((1, width, tm), lambda i: (i // tpb, 0, i % tpb))
    out_specs = (
        rows(GLA_KEY_WIDTH), rows(GLA_KEY_WIDTH), rows(GLA_WIDTH), rows(GLA_WIDTH),
        rows(GLA_KEY_WIDTH), rows(GLA_KEY_WIDTH),
        cols(ATT_WIDTH), rows(ATT_KV_WIDTH), cols(ATT_KV_WIDTH),
    )
    return pl.pallas_call(
        _in_proj_kernel,
        out_shape=out_shape,
        grid=(N // tm,),
        in_specs=[rows(D_MODEL), const((1, D_MODEL)), const((D_MODEL, D_IN_PAD)),
                  const((Z_PAD, 2 * GLA_KEY_WIDTH)), const((1, 2 * GLA_KEY_WIDTH)),
                  const((1, ATT_WIDTH)), const((1, ATT_KV_WIDTH)), const((ATT_WIDTH, ATT_WIDTH)),
                  pos, pos, pos],
        out_specs=out_specs,
        compiler_params=_cparams(("arbitrary",)),
        name="in_proj",
    )(xt, g1, w, up, upb, qg, kg, hm, cos, se, so)


GLA_GROUP = 4
GLA_FINISH_GROUP = 32


def _gla_kernel(q_ref, k_ref, v_ref, g_ref, laf_ref, lab_ref, gain_ref, o_ref,
                part_scr, qif_scr, qib_scr, kvf_scr, kvb_scr, decf_scr, decb_scr, lf_scr, lb_scr):
    C, G = GLA_CHUNK, GLA_GROUP
    R = C * G
    T = q_ref.shape[0]
    n_chunks = T // C
    gain = gain_ref[...]
    row = lax.broadcasted_iota(jnp.int32, (C, C), 0)
    col = lax.broadcasted_iota(jnp.int32, (C, C), 1)
    tril = row >= col
    triu = row <= col
    rr = lax.broadcasted_iota(jnp.int32, (R, R), 0)
    cc = lax.broadcasted_iota(jnp.int32, (R, R), 1)
    same_chunk = (rr // C) == (cc // C)
    lf_scr[...] = jnp.where(same_chunk & (rr >= cc), 1.0, 0.0).astype(BF16)
    lb_scr[...] = jnp.where(same_chunk & (rr <= cc), 1.0, 0.0).astype(BF16)

    directions = ((laf_ref, lf_scr, tril, C // 2 - 1, C - 1, qif_scr, kvf_scr, decf_scr),
                  (lab_ref, lb_scr, triu, C // 2, 0, qib_scr, kvb_scr, decb_scr))
    heads = tuple((slice(h * GLA_DK, (h + 1) * GLA_DK), slice(h * GLA_DV, (h + 1) * GLA_DV)) for h in range(2))
    nt = (((1,), (1,)), ((), ()))
    tn = (((0,), (0,)), ((), ()))

    def intra(sb, carry):
        r0 = pl.multiple_of(sb * R, R)
        q2 = q_ref[pl.ds(r0, R), :].astype(F32) * (GLA_DK ** -0.5)
        k2 = k_ref[pl.ds(r0, R), :].astype(F32)
        v2 = v_ref[pl.ds(r0, R), :]
        def cumulate(d):
            la_ref, tri_scr = d[0], d[1]
            la = la_ref[pl.ds(r0, R), :]
            la_hi = la.astype(BF16)
            la_lo = (la - la_hi.astype(F32)).astype(BF16)
            both = jnp.dot(tri_scr[...], jnp.concatenate([la_hi, la_lo], axis=1), preferred_element_type=F32)
            return both[:, :2 * GLA_DK] + both[:, 2 * GLA_DK:]

        def scale(d, b):
            _, _, _, i_ref, i_last, qi_scr, _, dec_scr = d
            out = []
            for c in range(G):
                rows = slice(c * C, (c + 1) * C)
                bc, qc, kc = b[rows], q2[rows], k2[rows]
                b_ref = bc[i_ref:i_ref + 1, :]
                b_last = bc[i_last:i_last + 1, :]
                qf = (qc * jnp.exp(bc - b_ref)).astype(BF16)
                kf = (kc * jnp.exp(b_ref - bc)).astype(BF16)
                qi_scr[pl.ds(r0 + c * C, C), :] = (qc * jnp.exp(bc)).astype(BF16)
                kl = (kc * jnp.exp(b_last - bc)).astype(BF16)
                decay = jnp.exp(b_last)
                for h, (ks, _) in enumerate(heads):
                    dec_scr[sb * G + c, h] = decay[:, ks]
                out.append((qf, kf, kl))
            return out

        def scores(scaled):
            return [[lax.dot_general(qf[:, ks], kf[:, ks], nt, preferred_element_type=F32) for ks, _ in heads]
                    for qf, kf, _ in scaled]

        def apply(d, scaled, raw):
            mask, kv_scr = d[2], d[6]
            outs = []
            for c in range(G):
                rows = slice(c * C, (c + 1) * C)
                per_head = []
                for h, (ks, vs) in enumerate(heads):
                    vh = v2[rows, vs]
                    sc = jnp.where(mask, raw[c][h], 0.0).astype(BF16)
                    per_head.append(jnp.dot(sc, vh, preferred_element_type=F32))
                    kv_scr[sb * G + c, h] = lax.dot_general(vh, scaled[c][2][:, ks], tn, preferred_element_type=F32)
                outs.append(per_head)
            return outs

        fwd, bwd = directions
        b_f = cumulate(fwd)
        b_b = cumulate(bwd)
        sc_f = scale(fwd, b_f)
        sc_b = scale(bwd, b_b)
        raw_f = scores(sc_f)
        raw_b = scores(sc_b)
        o_f = apply(fwd, sc_f, raw_f)
        o_b = apply(bwd, sc_b, raw_b)
        for c in range(G):
            part_scr[pl.ds(r0 + c * C, C), :] = jnp.concatenate(
                [o_f[c][h] + o_b[c][h] for h in range(2)], axis=1)
        return carry

    lax.fori_loop(0, n_chunks // G, intra, 0)

    def scan(n, carry):
        sf, sb = carry
        nb = n_chunks - 1 - n
        new_f, new_b = [], []
        for h in range(2):
            kv = kvf_scr[n, h]
            kvf_scr[n, h] = sf[h]
            new_f.append(decf_scr[n, h] * sf[h] + kv)
            kv = kvb_scr[nb, h]
            kvb_scr[nb, h] = sb[h]
            new_b.append(decb_scr[nb, h] * sb[h] + kv)
        return tuple(new_f), tuple(new_b)

    zero_state = tuple(jnp.zeros((GLA_DV, GLA_DK), F32) for _ in range(2))
    lax.fori_loop(0, n_chunks, scan, (zero_state, zero_state), unroll=4)

    GF = min(GLA_FINISH_GROUP, n_chunks)
    RF = C * GF

    def finish(sb, carry):
        r0 = pl.multiple_of(sb * RF, RF)
        g = g_ref[pl.ds(r0, RF), :].astype(F32)
        part = part_scr[pl.ds(r0, RF), :]
        qf = qif_scr[pl.ds(r0, RF), :]
        qb = qib_scr[pl.ds(r0, RF), :]
        inter = []
        for c in range(GF):
            rows = slice(c * C, (c + 1) * C)
            inter.append([
                lax.dot_general(qf[rows, ks], kvf_scr[sb * GF + c, h].astype(BF16), nt, preferred_element_type=F32)
                + lax.dot_general(qb[rows, ks], kvb_scr[sb * GF + c, h].astype(BF16), nt, preferred_element_type=F32)
                for h, (ks, _) in enumerate(heads)])
        gate = g * _sigmoid(g)
        for c in range(GF):
            rows = slice(c * C, (c + 1) * C)
            normed = jnp.concatenate([_rms(part[rows, vs] + inter[c][h], gain) for h, (_, vs) in enumerate(heads)],
                                     axis=1)
            o_ref[pl.ds(r0 + c * C, C), :] = (normed * gate[rows]).astype(BF16)
        return carry

    lax.fori_loop(0, n_chunks // GF, finish, 0)


def _gla(gq, gk, gv, gg, laf, lab, gain, B, T):
    N = gq.shape[0]
    pairs = GLA_HEADS // 2
    n_chunks = T // GLA_CHUNK
    assert n_chunks % GLA_GROUP == 0 and n_chunks % min(GLA_FINISH_GROUP, n_chunks) == 0
    group_rows = GLA_CHUNK * GLA_GROUP
    kspec = pl.BlockSpec((T, 2 * GLA_DK), lambda b, p: (b, p))
    vspec = pl.BlockSpec((T, 2 * GLA_DV), lambda b, p: (b, p))
    state = pltpu.VMEM((n_chunks, 2, GLA_DV, GLA_DK), F32)
    decay = pltpu.VMEM((n_chunks, 2, 1, GLA_DK), F32)
    return pl.pallas_call(
        _gla_kernel,
        out_shape=jax.ShapeDtypeStruct((N, GLA_WIDTH), BF16),
        grid=(B, pairs),
        in_specs=[kspec, kspec, vspec, vspec, kspec, kspec,
                  pl.BlockSpec((1, GLA_DV), lambda b, p: (0, 0))],
        out_specs=vspec,
        scratch_shapes=[pltpu.VMEM((T, 2 * GLA_DV), F32),
                        pltpu.VMEM((T, 2 * GLA_DK), BF16), pltpu.VMEM((T, 2 * GLA_DK), BF16),
                        state, state, decay, decay,
                        pltpu.VMEM((group_rows, group_rows), BF16), pltpu.VMEM((group_rows, group_rows), BF16)],
        compiler_params=_cparams(("arbitrary", "arbitrary")),
        name="gla",
    )(gq, gk, gv, gg, laf, lab, gain)


def _attn_kernel(qt_ref, k_ref, vt_ref, ot_ref, vext_scr, s0_scr, s1_scr, p0_scr, p1_scr, m0_scr, m1_scr,
                 *, tq, key_block, steps_per_iter):
    T = k_ref.shape[0]
    nq = T // tq
    n_blocks = ATT_KV_HEADS * nq
    dh = ATT_HEAD_DIM
    gw = ATT_GROUP * dh
    cols = ATT_GROUP * tq

    row = lax.broadcasted_iota(jnp.int32, (ATT_KV_WIDTH, T), 0)
    for hj in range(ATT_KV_HEADS):
        own = (row >= hj * dh) & (row < (hj + 1) * dh)
        vext_scr[hj] = jnp.where(own, vt_ref[0], jnp.ones((), BF16))

    s_scr, p_scr, m_scr = (s0_scr, s1_scr), (p0_scr, p1_scr), (m0_scr, m1_scr)

    def locate(blk):
        if isinstance(blk, int):
            hj = blk // nq
            return hj, hj == 0, (blk - hj * nq) * tq
        second = blk >= nq
        hj = jnp.where(second, 1, 0)
        return hj, jnp.logical_not(second), pl.multiple_of((blk - hj * nq) * tq, tq)

    def step(blk, sa, do_scores=True, do_probs=True, do_out=True):
        sb = 1 - sa
        if do_scores:
            hj, first, c0 = locate(blk)
            q4 = jnp.concatenate(
                [qt_ref[0, pl.ds(pl.multiple_of(hj * gw + h * dh, dh), dh), pl.ds(c0, tq)] for h in range(ATT_GROUP)],
                axis=1)
            zeros = jnp.zeros_like(q4)
            qe = jnp.where(first, jnp.concatenate([q4, zeros], axis=0), jnp.concatenate([zeros, q4], axis=0))
            m = jnp.full((8, cols), -jnp.inf, F32)
        if do_probs:
            mx = m_scr[sb][...]
        if do_out:
            hj_o, first_o, c0_o = locate(blk - 2)
            acc = jnp.zeros((ATT_KV_WIDTH, cols), F32)
        for kb in range(T // key_block):
            ks = slice(kb * key_block, (kb + 1) * key_block)
            if do_scores:
                s = jnp.dot(k_ref[ks, :], qe, preferred_element_type=F32)
                s_scr[sa][ks, :] = s
                m = jnp.maximum(m, jnp.max(s.reshape(key_block // 8, 8, cols), axis=0))
            if do_probs:
                p_scr[sb][ks, :] = jnp.exp2(s_scr[sb][ks, :] - mx).astype(BF16)
            if do_out:
                acc = acc + jnp.dot(vext_scr[hj_o, :, ks], p_scr[sa][ks, :], preferred_element_type=F32)
        if do_scores:
            m_scr[sa][...] = jnp.max(m, axis=0, keepdims=True)
        if do_out:
            num = jnp.where(first_o, acc[:dh], acc[dh:])
            den = jnp.where(first_o, acc[dh:dh + 1], acc[0:1])
            o = (num / den).astype(BF16)
            for h in range(ATT_GROUP):
                ot_ref[0, pl.ds(pl.multiple_of(hj_o * gw + h * dh, dh), dh), pl.ds(c0_o, tq)] = (
                    o[:, h * tq:(h + 1) * tq])

    step(0, 0, do_probs=False, do_out=False)
    step(1, 1, do_out=False)

    def steady(it, carry):
        for u in range(steps_per_iter):
            pl.when(it >= 0)(functools.partial(step, 2 + steps_per_iter * it + u, u % 2))
        return carry

    lax.fori_loop(0, (n_blocks - 2) // steps_per_iter, steady, 0)
    step(n_blocks, 0, do_scores=False)
    step(n_blocks + 1, 1, do_scores=False, do_probs=False)


def _attn(aqt, ak, avt, B, T, tq, key_block):
    assert T % (2 * tq) == 0 and tq % LANES == 0 and ATT_KV_HEADS == 2
    cols = ATT_GROUP * tq
    steps_per_iter = 2
    return pl.pallas_call(
        functools.partial(_attn_kernel, tq=tq, key_block=key_block, steps_per_iter=steps_per_iter),
        out_shape=jax.ShapeDtypeStruct((B, ATT_WIDTH, T), BF16),
        grid=(B,),
        in_specs=[pl.BlockSpec((1, ATT_WIDTH, T), lambda b: (b, 0, 0)),
                  pl.BlockSpec((T, ATT_KV_WIDTH), lambda b: (b, 0)),
                  pl.BlockSpec((1, ATT_KV_WIDTH, T), lambda b: (b, 0, 0))],
        out_specs=pl.BlockSpec((1, ATT_WIDTH, T), lambda b: (b, 0, 0)),
        scratch_shapes=[pltpu.VMEM((ATT_KV_HEADS, ATT_KV_WIDTH, T), BF16),
                        pltpu.VMEM((T, cols), F32), pltpu.VMEM((T, cols), F32),
                        pltpu.VMEM((T, cols), BF16), pltpu.VMEM((T, cols), BF16),
                        pltpu.VMEM((1, cols), F32), pltpu.VMEM((1, cols), F32)],
        compiler_params=_cparams(("arbitrary",)),
        name="attn",
    )(aqt, ak, avt)


def _out_proj_kernel(x_ref, og_ref, oa_ref, wo_ref, ag_ref, g2_ref, wr_ref, br_ref, tri_ref,
                     h_ref, xn_ref, eid_ref, gate_ref, rank_ref, cnt_ref, run_scr):
    tm = x_ref.shape[0]

    @pl.when(pl.program_id(0) == 0)
    def _():
        run_scr[...] = jnp.zeros_like(run_scr)

    def branch_norm(ts):
        oat = oa_ref[0, :, ts].astype(F32)
        return (oat * lax.rsqrt(jnp.mean(oat * oat, axis=0, keepdims=True) + EPS) * ag_ref[...]).astype(BF16)

    def mix(ts, oan):
        return (jnp.dot(og_ref[ts, :], wo_ref[:GLA_WIDTH, :], preferred_element_type=F32)
                + lax.dot_general(oan, wo_ref[GLA_WIDTH:, :], (((0,), (0,)), ((), ())),
                                  preferred_element_type=F32))

    def residual_norm(ts, y):
        h = x_ref[ts, :] + y
        h_ref[ts, :] = h
        xn = _rms(h, g2_ref[...])
        xn_ref[ts, :] = _pack_pairs(xn)
        return xn.astype(BF16)

    def logits(xnb):
        return lax.dot_general(wr_ref[...], xnb, (((1,), (1,)), ((), ())), preferred_element_type=F32) + br_ref[...]

    whole = slice(0, tm)
    lt = logits(residual_norm(whole, mix(whole, branch_norm(whole))))
    iota8 = lax.broadcasted_iota(jnp.int32, (N_GROUPS, tm), 0)

    def first_argmax(v):
        top = jnp.max(v, axis=0, keepdims=True)
        idx = jnp.min(jnp.where(v == top, iota8, N_GROUPS), axis=0, keepdims=True)
        return top, idx

    gl = lt[0:N_GROUPS]
    gmax, gidx = first_argmax(gl)
    gw = 1.0 / jnp.sum(jnp.exp(gl - gmax), axis=0, keepdims=True)
    esel = jnp.zeros((EXPERTS_PER_GROUP, tm), F32)
    for g in range(N_GROUPS):
        lo = N_GROUPS + g * EXPERTS_PER_GROUP
        esel = jnp.where(gidx == g, lt[lo:lo + EXPERTS_PER_GROUP], esel)
    v1, i1 = first_argmax(esel)
    rest = jnp.where(iota8 == i1, -jnp.inf, esel)
    v2, i2 = first_argmax(rest)
    t = jnp.exp(v2 - v1)
    den = 1.0 + t
    e1 = gidx * EXPERTS_PER_GROUP + i1
    e2 = gidx * EXPERTS_PER_GROUP + i2
    eid_ref[0:1, :] = e1
    eid_ref[1:2, :] = e2
    gate_ref[0:1, :] = gw * (1.0 / den)
    gate_ref[1:2, :] = gw * (t / den)

    iota_e = lax.broadcasted_iota(jnp.int32, (N_EXPERTS, tm), 0)
    oh1 = iota_e == e1
    oh2 = iota_e == e2
    both = jnp.where(oh1, 1.0, jnp.where(oh2, 1.0, 0.0)).astype(BF16)
    prefix = jnp.dot(both, tri_ref[...], preferred_element_type=F32)
    run = run_scr[...]
    base = prefix - 1.0 + run[:, 0:1]
    rank_ref[0:1, :] = jnp.sum(jnp.where(oh1, base, 0.0), axis=0, keepdims=True).astype(jnp.int32)
    rank_ref[1:2, :] = jnp.sum(jnp.where(oh2, base, 0.0), axis=0, keepdims=True).astype(jnp.int32)
    run = run + prefix[:, tm - 1:tm]
    run_scr[...] = run
    cnt_ref[...] = run


def _out_proj(xt, o_gla, o_att_t, wo, ag, g2, wr, br, tm):
    N = xt.shape[0]
    tpb = o_att_t.shape[2] // tm
    const = lambda shape: pl.BlockSpec(shape, lambda i: (0,) * len(shape))
    rows = lambda width: pl.BlockSpec((tm, width), lambda i: (i, 0))
    cols = pl.BlockSpec((TOP_K, tm), lambda i: (0, i))
    tri = jnp.asarray(np.triu(np.ones((tm, tm), np.float32)), BF16)
    return pl.pallas_call(
        _out_proj_kernel,
        out_shape=(jax.ShapeDtypeStruct((N, D_MODEL), F32),
                   jax.ShapeDtypeStruct((N, D_PACKED), jnp.uint32),
                   jax.ShapeDtypeStruct((TOP_K, N), jnp.int32),
                   jax.ShapeDtypeStruct((TOP_K, N), F32),
                   jax.ShapeDtypeStruct((TOP_K, N), jnp.int32),
                   jax.ShapeDtypeStruct((N_EXPERTS, LANES), F32)),
        grid=(N // tm,),
        in_specs=[rows(D_MODEL), rows(GLA_WIDTH),
                  pl.BlockSpec((1, ATT_WIDTH, tm), lambda i: (i // tpb, 0, i % tpb)),
                  const((D_MODEL, D_MODEL)),
                  const((ATT_WIDTH, 1)), const((1, D_MODEL)), const((LANES, D_MODEL)),
                  const((LANES, 1)), const((tm, tm))],
        out_specs=(rows(D_MODEL), rows(D_PACKED), cols, cols, cols, const((N_EXPERTS, LANES))),
        scratch_shapes=[pltpu.VMEM((N_EXPERTS, LANES), F32)],
        compiler_params=_cparams(("arbitrary",)),
        name="out_proj",
    )(xt, o_gla, o_att_t, wo, ag, g2, wr, br, tri)


D_PACKED = D_MODEL // 2


def _pack_pairs(x):
    w = x.shape[1] // 2
    bits = pltpu.bitcast(x.astype(BF16).astype(F32), jnp.uint32)
    return bits[:, :w] | (bits[:, w:] >> 16)


def _unpack_pairs(words):
    hi = pltpu.bitcast(words & jnp.uint32(0xFFFF0000), F32)
    lo = pltpu.bitcast(words << 16, F32)
    return jnp.concatenate([hi, lo], axis=1)


def _dispatch_kernel(zs_ref, nu_ref, *refs):
    dest_refs = refs[:TOP_K]
    packed = refs[TOP_K]
    w_in = refs[TOP_K + 1:TOP_K + 4]
    xs_hbm = refs[TOP_K + 4]
    w_out = refs[TOP_K + 5:TOP_K + 8]
    zero_buf, zero_sem, row_sem = refs[TOP_K + 8:]
    tn = packed.shape[0]
    i = pl.program_id(0)

    def fill(row0):
        return pltpu.make_async_copy(
            zero_buf, xs_hbm.at[pl.ds(pl.multiple_of(row0, MOE_BLOCK), MOE_BLOCK)], zero_sem)

    @pl.when(i == 0)
    def _():
        zero_buf[...] = jnp.zeros_like(zero_buf)
        fills = [fill(zs_ref[e]) for e in range(N_EXPERTS)]
        for f in fills:
            f.start()
        for f in fills:
            f.wait()

        n_blocks = xs_hbm.shape[0] // MOE_BLOCK
        lax.fori_loop(nu_ref[0], n_blocks, lambda blk, c: (fill(blk * MOE_BLOCK).start(), c)[1], 0)
        lax.fori_loop(nu_ref[0], n_blocks, lambda blk, c: (fill(blk * MOE_BLOCK).wait(), c)[1], 0)

    n_iter = tn // 16
    epw = w_in[0].shape[0]

    def issue(it, carry):
        chunks = []
        for src, dst in zip(w_in, w_out):
            rows = src.shape[1] // n_iter
            r0 = pl.multiple_of(it * rows, rows)
            for e in range(epw):
                chunks.append((dst, e, r0, rows, src[e, pl.ds(r0, rows), :]))
        for u in range(2):
            g = it * 2 + u
            tile = packed.at[pl.ds(pl.multiple_of(g * 8, 8), 8)]
            for s in range(8):
                for k in range(TOP_K):
                    pltpu.make_async_copy(tile.at[pl.ds(s, 1)], xs_hbm.at[pl.ds(dest_refs[k][g * 8 + s], 1)],
                                          row_sem).start(priority=k % 2)
        for dst, e, r0, rows, val in chunks:
            dst[e, pl.ds(r0, rows), :] = val.astype(BF16)
        return carry

    lax.fori_loop(0, n_iter, issue, 0)
    for k in range(TOP_K):
        pltpu.make_async_copy(packed, xs_hbm.at[pl.ds(0, tn)], row_sem).wait()


def _dest_specs(n_steps, tn, index):
    return [pl.BlockSpec((tn,), lambda i, *_, k=k: (k * n_steps + index(i),), memory_space=pltpu.SMEM)
            for k in range(TOP_K)]


def _dispatch(zero_starts, n_used, dest_flat, xn, weights, P, tn):
    N = xn.shape[0]
    n_steps = N // tn
    assert N_EXPERTS % n_steps == 0
    epw = N_EXPERTS // n_steps
    w_specs = [pl.BlockSpec((epw,) + w.shape[1:], lambda i, *_: (i, 0, 0)) for w in weights]
    for w in weights:
        assert w.shape[1] % (16 * (tn // 16)) == 0
    grid_spec = pltpu.PrefetchScalarGridSpec(
        num_scalar_prefetch=2,
        grid=(n_steps,),
        in_specs=_dest_specs(n_steps, tn, lambda i: i) + [pl.BlockSpec((tn, D_PACKED), lambda i, zs, nu: (i, 0))]
        + w_specs,
        out_specs=[pl.BlockSpec(memory_space=pl.ANY)] + w_specs,
        scratch_shapes=[pltpu.VMEM((MOE_BLOCK, D_PACKED), jnp.uint32),
                        pltpu.SemaphoreType.DMA, pltpu.SemaphoreType.DMA],
    )
    xs, *w_bf16 = pl.pallas_call(
        _dispatch_kernel,
        out_shape=[jax.ShapeDtypeStruct((P, D_PACKED), jnp.uint32)]
        + [jax.ShapeDtypeStruct(w.shape, BF16) for w in weights],
        grid_spec=grid_spec,
        compiler_params=_cparams(("arbitrary",)),
        name="dispatch",
    )(zero_starts, n_used, *([dest_flat] * TOP_K), xn, *weights)
    return xs, w_bf16


def _moe_ffn_kernel(be_ref, nu_ref, seg_ref, nxt_ref, xs_ref, wg_hbm, wu_hbm, wd_hbm, ys_ref,
                    wg_b, wu_b, wd_b, sems):
    i = pl.program_id(0)
    used = i < nu_ref[0]

    def weights(expert, slot):
        return [pltpu.make_async_copy(src.at[expert], dst.at[slot], sems.at[slot])
                for src, dst in ((wg_hbm, wg_b), (wu_hbm, wu_b), (wd_hbm, wd_b))]

    @pl.when(used)
    def _():
        new_expert = jnp.logical_or(i == 0, be_ref[i] != be_ref[jnp.maximum(i - 1, 0)])
        slot = seg_ref[i] % 2

        @pl.when(new_expert)
        def _():
            @pl.when(i == 0)
            def _():
                for cp in weights(be_ref[0], 0):
                    cp.start()

            for cp in weights(be_ref[i], slot):
                cp.wait()

            @pl.when(nxt_ref[i] >= 0)
            def _():
                for cp in weights(nxt_ref[i], 1 - slot):
                    cp.start()

        def gate_up(rows):
            xb = _unpack_pairs(xs_ref[rows, :]).astype(BF16)
            return (jnp.dot(xb, wg_b[slot], preferred_element_type=F32),
                    jnp.dot(xb, wu_b[slot], preferred_element_type=F32))

        def swish_mul(hg, hu):
            return (hg * _sigmoid(hg) * hu).astype(BF16)

        def down(rows, hm):
            ys_ref[rows, :] = _pack_pairs(jnp.dot(hm, wd_b[slot], preferred_element_type=F32))

        ra, rb = slice(0, MOE_BLOCK // 2), slice(MOE_BLOCK // 2, MOE_BLOCK)
        gu_a = gate_up(ra)
        hm_a = swish_mul(*gu_a)
        gu_b = gate_up(rb)
        down(ra, hm_a)
        hm_b = swish_mul(*gu_b)
        down(rb, hm_b)

    @pl.when(jnp.logical_not(used))
    def _():
        ys_ref[...] = jnp.zeros_like(ys_ref)


def _moe_ffn(blk_expert, n_used, xs, w_gate, w_up, w_down):
    P = xs.shape[0]
    nblk = P // MOE_BLOCK
    blk = jnp.arange(nblk, dtype=jnp.int32)
    starts = jnp.logical_and(jnp.concatenate([jnp.ones((1,), bool), blk_expert[1:] != blk_expert[:-1]]),
                             blk < n_used[0])
    seg = jnp.cumsum(starts.astype(jnp.int32)) - 1
    next_start = lax.cummin(jnp.where(starts, blk, nblk), reverse=True)
    next_start = jnp.concatenate([next_start[1:], jnp.full((1,), nblk, jnp.int32)])
    nxt = jnp.where(next_start < nblk, blk_expert[jnp.minimum(next_start, nblk - 1)], -1).astype(jnp.int32)
    row_block = pl.BlockSpec((MOE_BLOCK, D_PACKED), lambda i, *_: (i, 0))
    grid_spec = pltpu.PrefetchScalarGridSpec(
        num_scalar_prefetch=4,
        grid=(nblk,),
        in_specs=[row_block] + [pl.BlockSpec(memory_space=pl.ANY)] * 3,
        out_specs=row_block,
        scratch_shapes=[pltpu.VMEM((2, D_MODEL, D_EXPERT), BF16),
                        pltpu.VMEM((2, D_MODEL, D_EXPERT), BF16),
                        pltpu.VMEM((2, D_EXPERT, D_MODEL), BF16),
                        pltpu.SemaphoreType.DMA((2,))],
    )
    return pl.pallas_call(
        _moe_ffn_kernel,
        out_shape=jax.ShapeDtypeStruct((P, D_PACKED), jnp.uint32),
        grid_spec=grid_spec,
        compiler_params=_cparams(("arbitrary",)),
        name="moe_ffn",
    )(blk_expert, n_used, seg, nxt, xs, w_gate, w_up, w_down)


def _combine_kernel(*refs, final):
    dcur_refs, dnext_refs = refs[:TOP_K], refs[TOP_K:2 * TOP_K]
    h_ref, gate_ref, fg_ref, ys_hbm, o_ref, ybuf, sems = refs[2 * TOP_K:]
    tm = h_ref.shape[0]
    i = pl.program_id(0)
    n = pl.num_programs(0)

    def gather(drefs, slot):
        def issue(g, carry):
            for k in range(TOP_K):
                tile = ybuf.at[slot, pl.ds(pl.multiple_of(k * tm + g * 8, 8), 8)]
                for s in range(8):
                    pltpu.make_async_copy(ys_hbm.at[pl.ds(drefs[k][g * 8 + s], 1)],
                                          tile.at[pl.ds(s, 1)], sems.at[slot]).start(priority=k % 2)
            return carry

        lax.fori_loop(0, tm // 8, issue, 0)

    @pl.when(i == 0)
    def _():
        gather(dcur_refs, 0)

    @pl.when(i + 1 < n)
    def _():
        gather(dnext_refs, (i + 1) % 2)

    slot = i % 2
    pltpu.make_async_copy(ys_hbm.at[pl.ds(0, TOP_K * tm)], ybuf.at[slot], sems.at[slot]).wait()
    g = gate_ref[...]
    hh = h_ref[...]
    for k in range(TOP_K):
        hh = hh + _unpack_pairs(ybuf[slot, pl.ds(k * tm, tm), :]) * g[:, k:k + 1]
    o_ref[...] = _rms(hh, fg_ref[...]) if final else hh


def _combine(h, ys, dest_flat, gate_t, fg, tm, final):
    N = h.shape[0]
    n = N // tm
    rows = pl.BlockSpec((tm, D_MODEL), lambda i: (i, 0))
    return pl.pallas_call(
        functools.partial(_combine_kernel, final=final),
        out_shape=jax.ShapeDtypeStruct((N, D_MODEL), F32),
        grid=(n,),
        in_specs=_dest_specs(n, tm, lambda i: i) + _dest_specs(n, tm, lambda i: jnp.minimum(i + 1, n - 1)) + [
                  rows, pl.BlockSpec((tm, TOP_K), lambda i: (i, 0)),
                  pl.BlockSpec((1, D_MODEL), lambda i: (0, 0)),
                  pl.BlockSpec(memory_space=pl.ANY)],
        out_specs=rows,
        scratch_shapes=[pltpu.VMEM((2, TOP_K * tm, D_PACKED), jnp.uint32), pltpu.SemaphoreType.DMA((2,))],
        compiler_params=_cparams(("arbitrary",)),
        name="combine",
    )(*([dest_flat] * (2 * TOP_K)), h, gate_t, fg, ys)


def _reorder_w_in(w_in):
    n_gla = 2 * GLA_KEY_WIDTH + 2 * GLA_WIDTH
    z0 = n_gla
    a0 = z0 + 2 * GLA_GATE_RANK
    pad = jnp.zeros((D_MODEL, Z_PAD - 2 * GLA_GATE_RANK), w_in.dtype)
    return jnp.concatenate([w_in[:, :n_gla], w_in[:, a0:], w_in[:, z0:a0], pad], axis=1)


def kernel(x, norm1_gain, w_in, gla_up_fwd, gla_up_fwd_bias, gla_up_bwd, gla_up_bwd_bias, gla_out_gain, q_norm_gain, k_norm_gain, att_out_gain, w_out, norm2_gain, w_group, b_group, w_expert, b_expert, w_gate, w_up, w_down, final_gain):
    B, T, D = x.shape
    N = B * T
    depth = norm1_gain.shape[0]
    assert D == D_MODEL and T % TOKEN_TILE == 0 and T % GRID_W == 0
    tm, tq, key_block = TOKEN_TILE, ATT_Q_BLOCK, ATT_KEY_BLOCK
    h = x.reshape(N, D)
    cos, se, so = _rope_tables(T)
    head_mean = jnp.asarray(
        np.kron(np.eye(ATT_HEADS, dtype=np.float32),
                np.full((ATT_HEAD_DIM, ATT_HEAD_DIM), 1.0 / ATT_HEAD_DIM, np.float32)), BF16)
    for l in range(depth):
        w = _reorder_w_in(w_in[l]).astype(BF16)
        r = GLA_GATE_RANK
        up = jnp.zeros((Z_PAD, 2 * GLA_KEY_WIDTH), F32)
        up = up.at[:r, :GLA_KEY_WIDTH].set(gla_up_fwd[l]).at[r:2 * r, GLA_KEY_WIDTH:].set(gla_up_bwd[l])
        upb = jnp.concatenate([gla_up_fwd_bias[l], gla_up_bwd_bias[l]])[None, :]
        qg = jnp.tile(q_norm_gain[l], ATT_HEADS)[None, :]
        kg = jnp.tile(k_norm_gain[l], ATT_KV_HEADS)[None, :]
        gq, gk, gv, gg, laf, lab, aqt, ak, avt = _in_proj(
            h, B, T, tm, norm1_gain[l][None, :], w, up.astype(BF16), upb, qg, kg, head_mean, cos, se, so)
        o_gla = _gla(gq, gk, gv, gg, laf, lab, gla_out_gain[l][None, :], B, T)
        o_att_t = _attn(aqt, ak, avt, B, T, tq, key_block)

        wr = jnp.zeros((LANES, D), F32)
        wr = wr.at[:N_GROUPS].set(w_group[l].T).at[N_GROUPS:N_GROUPS + N_EXPERTS].set(w_expert[l].T)
        br = jnp.zeros((LANES, 1), F32)
        br = br.at[:N_GROUPS, 0].set(b_group[l]).at[N_GROUPS:N_GROUPS + N_EXPERTS, 0].set(b_expert[l])
        h, xn, eid, gate, rank, cnt = _out_proj(
            h, o_gla, o_att_t, w_out[l].astype(BF16), att_out_gain[l][:, None], norm2_gain[l][None, :],
            wr.astype(BF16), br, tm)

        counts = cnt[:, 0].astype(jnp.int32)
        padded = (counts + MOE_BLOCK - 1) // MOE_BLOCK * MOE_BLOCK
        pad_ends = jnp.cumsum(padded)
        pad_starts = pad_ends - padded
        seg_start = jnp.sum(jnp.where(eid[:, :, None] == jnp.arange(N_EXPERTS, dtype=jnp.int32),
                                      pad_starts.astype(jnp.int32), 0), axis=-1)
        dest = (seg_start + rank).reshape(-1)
        P = N * TOP_K + N_EXPERTS * MOE_BLOCK
        nblk = P // MOE_BLOCK
        blk_start = jnp.arange(nblk, dtype=jnp.int32) * MOE_BLOCK
        blk_expert = jnp.minimum(
            jnp.sum((pad_ends[None, :] <= blk_start[:, None]).astype(jnp.int32), axis=1), N_EXPERTS - 1)
        n_used = (pad_ends[-1:] // MOE_BLOCK).astype(jnp.int32)

        zero_starts = jnp.maximum(pad_ends - MOE_BLOCK, 0).astype(jnp.int32)
        xs, expert_w = _dispatch(zero_starts, n_used, dest, xn, (w_gate[l], w_up[l], w_down[l]), P, tm)
        ys = _moe_ffn(blk_expert, n_used, xs, *expert_w)
        h = _combine(h, ys, dest, gate.T, final_gain[None, :], tm, l == depth - 1)
    return h.reshape(B, T, D)
```

```python
import functools

import numpy as np
import jax
import jax.numpy as jnp
from jax import lax
from jax.experimental import pallas as pl
from jax.experimental.pallas import tpu as pltpu

F32 = jnp.float32
BF16 = jnp.bfloat16

D_MODEL = 1024
EPS = 1e-6
GRID_W = 64

GLA_HEADS = 4
GLA_DK = 64
GLA_DV = 128
GLA_KEY_WIDTH = GLA_HEADS * GLA_DK
GLA_WIDTH = GLA_HEADS * GLA_DV
GLA_GATE_RANK = 16
GLA_GATE_NORM = 16.0
GLA_CHUNK = 64

ATT_HEADS = 8
ATT_KV_HEADS = 2
ATT_HEAD_DIM = 64
ATT_GROUP = ATT_HEADS // ATT_KV_HEADS
ATT_WIDTH = ATT_HEADS * ATT_HEAD_DIM
ATT_KV_WIDTH = ATT_KV_HEADS * ATT_HEAD_DIM
ROPE_THETA = 10000.0
LOG2_E = 1.4426950408889634

N_GROUPS = 8
EXPERTS_PER_GROUP = 8
N_EXPERTS = N_GROUPS * EXPERTS_PER_GROUP
TOP_K = 2
D_EXPERT = 512
MOE_BLOCK = 256

LANES = 128
Z_PAD = LANES

TOKEN_TILE = 512
ATT_Q_BLOCK = 128
ATT_KEY_BLOCK = 256
IN_PROJ_PARTS = 2

_OFF_GQ = 0
_OFF_GK = _OFF_GQ + GLA_KEY_WIDTH
_OFF_GV = _OFF_GK + GLA_KEY_WIDTH
_OFF_GG = _OFF_GV + GLA_WIDTH
_OFF_AQ = _OFF_GG + GLA_WIDTH
_OFF_AK = _OFF_AQ + ATT_WIDTH
_OFF_AV = _OFF_AK + ATT_KV_WIDTH
_OFF_Z = _OFF_AV + ATT_KV_WIDTH
D_IN_PAD = _OFF_Z + Z_PAD

VMEM_LIMIT = 56 * 1024 * 1024


def _cparams(semantics):
    return pltpu.CompilerParams(dimension_semantics=semantics, vmem_limit_bytes=VMEM_LIMIT)


def _rms(x, gain):
    return x * lax.rsqrt(jnp.mean(x * x, axis=-1, keepdims=True) + EPS) * gain


def _sigmoid(x):
    return 1.0 / (1.0 + jnp.exp(-x))


def _head_norm_rope(x, head_mean, gain, cos, sin_even, sin_odd):
    w = x.shape[1]
    ms = jnp.dot((x * x).astype(BF16), head_mean, preferred_element_type=F32)
    xn = x * lax.rsqrt(ms + EPS) * gain
    reps = w // LANES
    tile = lambda t: t if reps == 1 else jnp.concatenate([t] * reps, axis=1)
    nxt = pltpu.roll(xn, w - 1, 1)
    prv = pltpu.roll(xn, 1, 1)
    return xn * tile(cos) + nxt * tile(sin_even) + prv * tile(sin_odd)


def _in_proj_kernel(x_ref, g1_ref, w_ref, up_ref, upb_ref, qg_ref, kg_ref, hm_ref,
                    cos_ref, se_ref, so_ref,
                    gq_ref, gk_ref, gv_ref, gg_ref, laf_ref, lab_ref, aqt_ref, ak_ref, avt_ref):
    hm = hm_ref[...]
    tm = x_ref.shape[0]

    def rows(rs):
        u = _rms(x_ref[rs, :], g1_ref[...]).astype(BF16)

        def proj(lo, hi):
            return jnp.dot(u, w_ref[:, lo:hi], preferred_element_type=F32)

        cos, se, so = cos_ref[rs, :], se_ref[rs, :], so_ref[rs, :]
        r_q = proj(_OFF_AQ, _OFF_AK)
        r_gqk = proj(_OFF_GQ, _OFF_GV)
        q = _head_norm_rope(r_q, hm, qg_ref[...], cos, se, so)
        aqt_ref[0, :, rs] = (q * (ATT_HEAD_DIM ** -0.5 * LOG2_E)).T.astype(BF16)
        gq_ref[rs, :] = r_gqk[:, :GLA_KEY_WIDTH].astype(BF16)
        gk_ref[rs, :] = r_gqk[:, GLA_KEY_WIDTH:].astype(BF16)

        r_kvz = proj(_OFF_AK, D_IN_PAD)
        r_gv = proj(_OFF_GV, _OFF_GG)
        k = _head_norm_rope(r_kvz[:, :ATT_KV_WIDTH], hm[:ATT_KV_WIDTH, :ATT_KV_WIDTH], kg_ref[...], cos, se, so)
        ak_ref[rs, :] = k.astype(BF16)
        avt_ref[0, :, rs] = r_kvz[:, ATT_KV_WIDTH:2 * ATT_KV_WIDTH].T.astype(BF16)
        gv_ref[rs, :] = r_gv.astype(BF16)

        z = r_kvz[:, 2 * ATT_KV_WIDTH:].astype(BF16)
        zl = jnp.dot(z, up_ref[...], preferred_element_type=F32) + upb_ref[...]
        r_gg = proj(_OFF_GG, _OFF_AQ)
        la = (jnp.minimum(zl, 0.0) - jnp.log(1.0 + jnp.exp(-jnp.abs(zl)))) * (1.0 / GLA_GATE_NORM)
        laf_ref[rs, :] = la[:, :GLA_KEY_WIDTH]
        lab_ref[rs, :] = la[:, GLA_KEY_WIDTH:]
        gg_ref[rs, :] = r_gg.astype(BF16)

    for p in range(IN_PROJ_PARTS):
        rows(slice(p * tm // IN_PROJ_PARTS, (p + 1) * tm // IN_PROJ_PARTS))


def _rope_tables(T):
    t = np.arange(T)
    row = (t // GRID_W).astype(np.float32)
    col = (t % GRID_W).astype(np.float32)
    axis_dim = ATT_HEAD_DIM // 2
    inv_freq = (ROPE_THETA ** (-np.arange(0, axis_dim, 2, dtype=np.float32) / axis_dim)).astype(np.float32)
    ang = np.concatenate([row[:, None] * inv_freq, col[:, None] * inv_freq], axis=-1)
    ang = np.repeat(ang, 2, axis=1)
    ang = np.tile(ang, (1, LANES // ATT_HEAD_DIM))
    even = (np.arange(LANES) % 2 == 0)[None, :]
    cos = np.cos(ang)
    sin = np.sin(ang)
    return (jnp.asarray(cos, F32), jnp.asarray(np.where(even, -sin, 0.0), F32),
            jnp.asarray(np.where(even, 0.0, sin), F32))


def _in_proj(xt, B, T, tm, g1, w, up, upb, qg, kg, hm, cos, se, so):
    N = xt.shape[0]
    tpb = T // tm
    const = lambda shape: pl.BlockSpec(shape, lambda i: (0,) * len(shape))
    rows = lambda width: pl.BlockSpec((tm, width), lambda i: (i, 0))
    pos = pl.BlockSpec((tm, LANES), lambda i: (i % tpb, 0))
    out_shape = (
        jax.ShapeDtypeStruct((N, GLA_KEY_WIDTH), BF16),
        jax.ShapeDtypeStruct((N, GLA_KEY_WIDTH), BF16),
        jax.ShapeDtypeStruct((N, GLA_WIDTH), BF16),
        jax.ShapeDtypeStruct((N, GLA_WIDTH), BF16),
        jax.ShapeDtypeStruct((N, GLA_KEY_WIDTH), F32),
        jax.ShapeDtypeStruct((N, GLA_KEY_WIDTH), F32),
        jax.ShapeDtypeStruct((B, ATT_WIDTH, T), BF16),
        jax.ShapeDtypeStruct((N, ATT_KV_WIDTH), BF16),
        jax.ShapeDtypeStruct((B, ATT_KV_WIDTH, T), BF16),
    )
    cols = lambda width: pl.BlockSpec((1, width, tm), lambda i: (i // tpb, 0, i % tpb))
    out_specs = (
        rows(GLA_KEY_WIDTH), rows(GLA_KEY_WIDTH), rows(GLA_WIDTH), rows(GLA_WIDTH),
        rows(GLA_KEY_WIDTH), rows(GLA_KEY_WIDTH),
        cols(ATT_WIDTH), rows(ATT_KV_WIDTH), cols(ATT_KV_WIDTH),
    )
    return pl.pallas_call(
        _in_proj_kernel,
        out_shape=out_shape,
        grid=(N // tm,),
        in_specs=[rows(D_MODEL), const((1, D_MODEL)), const((D_MODEL, D_IN_PAD)),
                  const((Z_PAD, 2 * GLA_KEY_WIDTH)), const((1, 2 * GLA_KEY_WIDTH)),
                  const((1, ATT_WIDTH)), const((1, ATT_KV_WIDTH)), const((ATT_WIDTH, ATT_WIDTH)),
                  pos, pos, pos],
        out_specs=out_specs,
        compiler_params=_cparams(("arbitrary",)),
        name="in_proj",
    )(xt, g1, w, up, upb, qg, kg, hm, cos, se, so)


GLA_GROUP = 4
GLA_FINISH_GROUP = 32


def _gla_kernel(q_ref, k_ref, v_ref, g_ref, laf_ref, lab_ref, gain_ref, o_ref,
                part_scr, qif_scr, qib_scr, kvf_scr, kvb_scr, decf_scr, decb_scr, lf_scr, lb_scr):
    C, G = GLA_CHUNK, GLA_GROUP
    R = C * G
    T = q_ref.shape[0]
    n_chunks = T // C
    gain = gain_ref[...]
    row = lax.broadcasted_iota(jnp.int32, (C, C), 0)
    col = lax.broadcasted_iota(jnp.int32, (C, C), 1)
    tril = row >= col
    triu = row <= col
    rr = lax.broadcasted_iota(jnp.int32, (R, R), 0)
    cc = lax.broadcasted_iota(jnp.int32, (R, R), 1)
    same_chunk = (rr // C) == (cc // C)
    lf_scr[...] = jnp.where(same_chunk & (rr >= cc), 1.0, 0.0).astype(BF16)
    lb_scr[...] = jnp.where(same_chunk & (rr <= cc), 1.0, 0.0).astype(BF16)

    directions = ((laf_ref, lf_scr, tril, C // 2 - 1, C - 1, qif_scr, kvf_scr, decf_scr),
                  (lab_ref, lb_scr, triu, C // 2, 0, qib_scr, kvb_scr, decb_scr))
    heads = tuple((slice(h * GLA_DK, (h + 1) * GLA_DK), slice(h * GLA_DV, (h + 1) * GLA_DV)) for h in range(2))
    nt = (((1,), (1,)), ((), ()))
    tn = (((0,), (0,)), ((), ()))

    def intra(sb, carry):
        r0 = pl.multiple_of(sb * R, R)
        q2 = q_ref[pl.ds(r0, R), :].astype(F32) * (GLA_DK ** -0.5)
        k2 = k_ref[pl.ds(r0, R), :].astype(F32)
        v2 = v_ref[pl.ds(r0, R), :]
        def cumulate(d):
            la_ref, tri_scr = d[0], d[1]
            la = la_ref[pl.ds(r0, R), :]
            la_hi = la.astype(BF16)
            la_lo = (la - la_hi.astype(F32)).astype(BF16)
            both = jnp.dot(tri_scr[...], jnp.concatenate([la_hi, la_lo], axis=1), preferred_element_type=F32)
            return both[:, :2 * GLA_DK] + both[:, 2 * GLA_DK:]

        def scale(d, b):
            _, _, _, i_ref, i_last, qi_scr, _, dec_scr = d
            out = []
            for c in range(G):
                rows = slice(c * C, (c + 1) * C)
                bc, qc, kc = b[rows], q2[rows], k2[rows]
                b_ref = bc[i_ref:i_ref + 1, :]
                b_last = bc[i_last:i_last + 1, :]
                qf = (qc * jnp.exp(bc - b_ref)).astype(BF16)
                kf = (kc * jnp.exp(b_ref - bc)).astype(BF16)
                qi_scr[pl.ds(r0 + c * C, C), :] = (qc * jnp.exp(bc)).astype(BF16)
                kl = (kc * jnp.exp(b_last - bc)).astype(BF16)
                decay = jnp.exp(b_last)
                for h, (ks, _) in enumerate(heads):
                    dec_scr[sb * G + c, h] = decay[:, ks]
                out.append((qf, kf, kl))
            return out

        def scores(scaled):
            return [[lax.dot_general(qf[:, ks], kf[:, ks], nt, preferred_element_type=F32) for ks, _ in heads]
                    for qf, kf, _ in scaled]

        def apply(d, scaled, raw):
            mask, kv_scr = d[2], d[6]
            outs = []
            for c in range(G):
                rows = slice(c * C, (c + 1) * C)
                per_head = []
                for h, (ks, vs) in enumerate(heads):
                    vh = v2[rows, vs]
                    sc = jnp.where(mask, raw[c][h], 0.0).astype(BF16)
                    per_head.append(jnp.dot(sc, vh, preferred_element_type=F32))
                    kv_scr[sb * G + c, h] = lax.dot_general(vh, scaled[c][2][:, ks], tn, preferred_element_type=F32)
                outs.append(per_head)
            return outs

        fwd, bwd = directions
        b_f = cumulate(fwd)
        b_b = cumulate(bwd)
        sc_f = scale(fwd, b_f)
        sc_b = scale(bwd, b_b)
        raw_f = scores(sc_f)
        raw_b = scores(sc_b)
        o_f = apply(fwd, sc_f, raw_f)
        o_b = apply(bwd, sc_b, raw_b)
        for c in range(G):
            part_scr[pl.ds(r0 + c * C, C), :] = jnp.concatenate(
                [o_f[c][h] + o_b[c][h] for h in range(2)], axis=1)
        return carry

    lax.fori_loop(0, n_chunks // G, intra, 0)

    def scan(n, carry):
        sf, sb = carry
        nb = n_chunks - 1 - n
        new_f, new_b = [], []
        for h in range(2):
            kv = kvf_scr[n, h]
            kvf_scr[n, h] = sf[h]
            new_f.append(decf_scr[n, h] * sf[h] + kv)
            kv = kvb_scr[nb, h]
            kvb_scr[nb, h] = sb[h]
            new_b.append(decb_scr[nb, h] * sb[h] + kv)
        return tuple(new_f), tuple(new_b)

    zero_state = tuple(jnp.zeros((GLA_DV, GLA_DK), F32) for _ in range(2))
    lax.fori_loop(0, n_chunks, scan, (zero_state, zero_state), unroll=4)

    GF = min(GLA_FINISH_GROUP, n_chunks)
    RF = C * GF

    def finish(sb, carry):
        r0 = pl.multiple_of(sb * RF, RF)
        g = g_ref[pl.ds(r0, RF), :].astype(F32)
        part = part_scr[pl.ds(r0, RF), :]
        qf = qif_scr[pl.ds(r0, RF), :]
        qb = qib_scr[pl.ds(r0, RF), :]
        inter = []
        for c in range(GF):
            rows = slice(c * C, (c + 1) * C)
            inter.append([
                lax.dot_general(qf[rows, ks], kvf_scr[sb * GF + c, h].astype(BF16), nt, preferred_element_type=F32)
                + lax.dot_general(qb[rows, ks], kvb_scr[sb * GF + c, h].astype(BF16), nt, preferred_element_type=F32)
                for h, (ks, _) in enumerate(heads)])
        gate = g * _sigmoid(g)
        for c in range(GF):
            rows = slice(c * C, (c + 1) * C)
            normed = jnp.concatenate([_rms(part[rows, vs] + inter[c][h], gain) for h, (_, vs) in enumerate(heads)],
                                     axis=1)
            o_ref[pl.ds(r0 + c * C, C), :] = (normed * gate[rows]).astype(BF16)
        return carry

    lax.fori_loop(0, n_chunks // GF, finish, 0)


def _gla(gq, gk, gv, gg, laf, lab, gain, B, T):
    N = gq.shape[0]
    pairs = GLA_HEADS // 2
    n_chunks = T // GLA_CHUNK
    assert n_chunks % GLA_GROUP == 0 and n_chunks % min(GLA_FINISH_GROUP, n_chunks) == 0
    group_rows = GLA_CHUNK * GLA_GROUP
    kspec = pl.BlockSpec((T, 2 * GLA_DK), lambda b, p: (b, p))
    vspec = pl.BlockSpec((T, 2 * GLA_DV), lambda b, p: (b, p))
    state = pltpu.VMEM((n_chunks, 2, GLA_DV, GLA_DK), F32)
    decay = pltpu.VMEM((n_chunks, 2, 1, GLA_DK), F32)
    return pl.pallas_call(
        _gla_kernel,
        out_shape=jax.ShapeDtypeStruct((N, GLA_WIDTH), BF16),
        grid=(B, pairs),
        in_specs=[kspec, kspec, vspec, vspec, kspec, kspec,
                  pl.BlockSpec((1, GLA_DV), lambda b, p: (0, 0))],
        out_specs=vspec,
        scratch_shapes=[pltpu.VMEM((T, 2 * GLA_DV), F32),
                        pltpu.VMEM((T, 2 * GLA_DK), BF16), pltpu.VMEM((T, 2 * GLA_DK), BF16),
                        state, state, decay, decay,
                        pltpu.VMEM((group_rows, group_rows), BF16), pltpu.VMEM((group_rows, group_rows), BF16)],
        compiler_params=_cparams(("arbitrary", "arbitrary")),
        name="gla",
    )(gq, gk, gv, gg, laf, lab, gain)


def _attn_kernel(qt_ref, k_ref, vt_ref, ot_ref, vext_scr, s0_scr, s1_scr, p0_scr, p1_scr, m0_scr, m1_scr,
                 *, tq, key_block, steps_per_iter):
    T = k_ref.shape[0]
    nq = T // tq
    n_blocks = ATT_KV_HEADS * nq
    dh = ATT_HEAD_DIM
    gw = ATT_GROUP * dh
    cols = ATT_GROUP * tq

    row = lax.broadcasted_iota(jnp.int32, (ATT_KV_WIDTH, T), 0)
    for hj in range(ATT_KV_HEADS):
        own = (row >= hj * dh) & (row < (hj + 1) * dh)
        vext_scr[hj] = jnp.where(own, vt_ref[0], jnp.ones((), BF16))

    s_scr, p_scr, m_scr = (s0_scr, s1_scr), (p0_scr, p1_scr), (m0_scr, m1_scr)

    def locate(blk):
        if isinstance(blk, int):
            hj = blk // nq
            return hj, hj == 0, (blk - hj * nq) * tq
        second = blk >= nq
        hj = jnp.where(second, 1, 0)
        return hj, jnp.logical_not(second), pl.multiple_of((blk - hj * nq) * tq, tq)

    def step(blk, sa, do_scores=True, do_probs=True, do_out=True):
        sb = 1 - sa
        if do_scores:
            hj, first, c0 = locate(blk)
            q4 = jnp.concatenate(
                [qt_ref[0, pl.ds(pl.multiple_of(hj * gw + h * dh, dh), dh), pl.ds(c0, tq)] for h in range(ATT_GROUP)],
                axis=1)
            zeros = jnp.zeros_like(q4)
            qe = jnp.where(first, jnp.concatenate([q4, zeros], axis=0), jnp.concatenate([zeros, q4], axis=0))
            m = jnp.full((8, cols), -jnp.inf, F32)
        if do_probs:
            mx = m_scr[sb][...]
        if do_out:
            hj_o, first_o, c0_o = locate(blk - 2)
            acc = jnp.zeros((ATT_KV_WIDTH, cols), F32)
        for kb in range(T // key_block):
            ks = slice(kb * key_block, (kb + 1) * key_block)
            if do_scores:
                s = jnp.dot(k_ref[ks, :], qe, preferred_element_type=F32)
                s_scr[sa][ks, :] = s
                m = jnp.maximum(m, jnp.max(s.reshape(key_block // 8, 8, cols), axis=0))
            if do_probs:
                p_scr[sb][ks, :] = jnp.exp2(s_scr[sb][ks, :] - mx).astype(BF16)
            if do_out:
                acc = acc + jnp.dot(vext_scr[hj_o, :, ks], p_scr[sa][ks, :], preferred_element_type=F32)
        if do_scores:
            m_scr[sa][...] = jnp.max(m, axis=0, keepdims=True)
        if do_out:
            num = jnp.where(first_o, acc[:dh], acc[dh:])
            den = jnp.where(first_o, acc[dh:dh + 1], acc[0:1])
            o = (num / den).astype(BF16)
            for h in range(ATT_GROUP):
                ot_ref[0, pl.ds(pl.multiple_of(hj_o * gw + h * dh, dh), dh), pl.ds(c0_o, tq)] = (
                    o[:, h * tq:(h + 1) * tq])

    step(0, 0, do_probs=False, do_out=False)
    step(1, 1, do_out=False)

    def steady(it, carry):
        for u in range(steps_per_iter):
            pl.when(it >= 0)(functools.partial(step, 2 + steps_per_iter * it + u, u % 2))
        return carry

    lax.fori_loop(0, (n_blocks - 2) // steps_per_iter, steady, 0)
    step(n_blocks, 0, do_scores=False)
    step(n_blocks + 1, 1, do_scores=False, do_probs=False)


def _attn(aqt, ak, avt, B, T, tq, key_block):
    assert T % (2 * tq) == 0 and tq % LANES == 0 and ATT_KV_HEADS == 2
    cols = ATT_GROUP * tq
    steps_per_iter = 2
    return pl.pallas_call(
        functools.partial(_attn_kernel, tq=tq, key_block=key_block, steps_per_iter=steps_per_iter),
        out_shape=jax.ShapeDtypeStruct((B, ATT_WIDTH, T), BF16),
        grid=(B,),
        in_specs=[pl.BlockSpec((1, ATT_WIDTH, T), lambda b: (b, 0, 0)),
                  pl.BlockSpec((T, ATT_KV_WIDTH), lambda b: (b, 0)),
                  pl.BlockSpec((1, ATT_KV_WIDTH, T), lambda b: (b, 0, 0))],
        out_specs=pl.BlockSpec((1, ATT_WIDTH, T), lambda b: (b, 0, 0)),
        scratch_shapes=[pltpu.VMEM((ATT_KV_HEADS, ATT_KV_WIDTH, T), BF16),
                        pltpu.VMEM((T, cols), F32), pltpu.VMEM((T, cols), F32),
                        pltpu.VMEM((T, cols), BF16), pltpu.VMEM((T, cols), BF16),
                        pltpu.VMEM((1, cols), F32), pltpu.VMEM((1, cols), F32)],
        compiler_params=_cparams(("arbitrary",)),
        name="attn",
    )(aqt, ak, avt)


def _out_proj_kernel(x_ref, og_ref, oa_ref, wo_ref, ag_ref, g2_ref, wr_ref, br_ref, tri_ref,
                     h_ref, xn_ref, eid_ref, gate_ref, rank_ref, cnt_ref, run_scr):
    tm = x_ref.shape[0]

    @pl.when(pl.program_id(0) == 0)
    def _():
        run_scr[...] = jnp.zeros_like(run_scr)

    def branch_norm(ts):
        oat = oa_ref[0, :, ts].astype(F32)
        return (oat * lax.rsqrt(jnp.mean(oat * oat, axis=0, keepdims=True) + EPS) * ag_ref[...]).astype(BF16)

    def mix(ts, oan):
        return (jnp.dot(og_ref[ts, :], wo_ref[:GLA_WIDTH, :], preferred_element_type=F32)
                + lax.dot_general(oan, wo_ref[GLA_WIDTH:, :], (((0,), (0,)), ((), ())),
                                  preferred_element_type=F32))

    def residual_norm(ts, y):
        h = x_ref[ts, :] + y
        h_ref[ts, :] = h
        xn = _rms(h, g2_ref[...])
        xn_ref[ts, :] = _pack_pairs(xn)
        return xn.astype(BF16)

    def logits(xnb):
        return lax.dot_general(wr_ref[...], xnb, (((1,), (1,)), ((), ())), preferred_element_type=F32) + br_ref[...]

    whole = slice(0, tm)
    lt = logits(residual_norm(whole, mix(whole, branch_norm(whole))))
    iota8 = lax.broadcasted_iota(jnp.int32, (N_GROUPS, tm), 0)

    def first_argmax(v):
        top = jnp.max(v, axis=0, keepdims=True)
        idx = jnp.min(jnp.where(v == top, iota8, N_GROUPS), axis=0, keepdims=True)
        return top, idx

    gl = lt[0:N_GROUPS]
    gmax, gidx = first_argmax(gl)
    gw = 1.0 / jnp.sum(jnp.exp(gl - gmax), axis=0, keepdims=True)
    esel = jnp.zeros((EXPERTS_PER_GROUP, tm), F32)
    for g in range(N_GROUPS):
        lo = N_GROUPS + g * EXPERTS_PER_GROUP
        esel = jnp.where(gidx == g, lt[lo:lo + EXPERTS_PER_GROUP], esel)
    v1, i1 = first_argmax(esel)
    rest = jnp.where(iota8 == i1, -jnp.inf, esel)
    v2, i2 = first_argmax(rest)
    t = jnp.exp(v2 - v1)
    den = 1.0 + t
    e1 = gidx * EXPERTS_PER_GROUP + i1
    e2 = gidx * EXPERTS_PER_GROUP + i2
    eid_ref[0:1, :] = e1
    eid_ref[1:2, :] = e2
    gate_ref[0:1, :] = gw * (1.0 / den)
    gate_ref[1:2, :] = gw * (t / den)

    iota_e = lax.broadcasted_iota(jnp.int32, (N_EXPERTS, tm), 0)
    oh1 = iota_e == e1
    oh2 = iota_e == e2
    both = jnp.where(oh1, 1.0, jnp.where(oh2, 1.0, 0.0)).astype(BF16)
    prefix = jnp.dot(both, tri_ref[...], preferred_element_type=F32)
    run = run_scr[...]
    base = prefix - 1.0 + run[:, 0:1]
    rank_ref[0:1, :] = jnp.sum(jnp.where(oh1, base, 0.0), axis=0, keepdims=True).astype(jnp.int32)
    rank_ref[1:2, :] = jnp.sum(jnp.where(oh2, base, 0.0), axis=0, keepdims=True).astype(jnp.int32)
    run = run + prefix[:, tm - 1:tm]
    run_scr[...] = run
    cnt_ref[...] = run


def _out_proj(xt, o_gla, o_att_t, wo, ag, g2, wr, br, tm):
    N = xt.shape[0]
    tpb = o_att_t.shape[2] // tm
    const = lambda shape: pl.BlockSpec(shape, lambda i: (0,) * len(shape))
    rows = lambda width: pl.BlockSpec((tm, width), lambda i: (i, 0))
    cols = pl.BlockSpec((TOP_K, tm), lambda i: (0, i))
    tri = jnp.asarray(np.triu(np.ones((tm, tm), np.float32)), BF16)
    return pl.pallas_call(
        _out_proj_kernel,
        out_shape=(jax.ShapeDtypeStruct((N, D_MODEL), F32),
                   jax.ShapeDtypeStruct((N, D_PACKED), jnp.uint32),
                   jax.ShapeDtypeStruct((TOP_K, N), jnp.int32),
                   jax.ShapeDtypeStruct((TOP_K, N), F32),
                   jax.ShapeDtypeStruct((TOP_K, N), jnp.int32),
                   jax.ShapeDtypeStruct((N_EXPERTS, LANES), F32)),
        grid=(N // tm,),
        in_specs=[rows(D_MODEL), rows(GLA_WIDTH),
                  pl.BlockSpec((1, ATT_WIDTH, tm), lambda i: (i // tpb, 0, i % tpb)),
                  const((D_MODEL, D_MODEL)),
                  const((ATT_WIDTH, 1)), const((1, D_MODEL)), const((LANES, D_MODEL)),
                  const((LANES, 1)), const((tm, tm))],
        out_specs=(rows(D_MODEL), rows(D_PACKED), cols, cols, cols, const((N_EXPERTS, LANES))),
        scratch_shapes=[pltpu.VMEM((N_EXPERTS, LANES), F32)],
        compiler_params=_cparams(("arbitrary",)),
        name="out_proj",
    )(xt, o_gla, o_att_t, wo, ag, g2, wr, br, tri)


D_PACKED = D_MODEL // 2


def _pack_pairs(x):
    w = x.shape[1] // 2
    bits = pltpu.bitcast(x.astype(BF16).astype(F32), jnp.uint32)
    return bits[:, :w] | (bits[:, w:] >> 16)


def _unpack_pairs(words):
    hi = pltpu.bitcast(words & jnp.uint32(0xFFFF0000), F32)
    lo = pltpu.bitcast(words << 16, F32)
    return jnp.concatenate([hi, lo], axis=1)


def _dispatch_kernel(zs_ref, nu_ref, *refs):
    dest_refs = refs[:TOP_K]
    packed = refs[TOP_K]
    w_in = refs[TOP_K + 1:TOP_K + 4]
    xs_hbm = refs[TOP_K + 4]
    w_out = refs[TOP_K + 5:TOP_K + 8]
    zero_buf, stage, zero_sem, row_sems = refs[TOP_K + 8:]
    tn = packed.shape[0]
    i = pl.program_id(0)
    n = pl.num_programs(0)
    slot = i % 2

    def wait_rows(s):
        for k in range(TOP_K):
            pltpu.make_async_copy(stage.at[s], xs_hbm.at[pl.ds(0, tn)], row_sems.at[s]).wait()

    def fill(row0):
        return pltpu.make_async_copy(
            zero_buf, xs_hbm.at[pl.ds(pl.multiple_of(row0, MOE_BLOCK), MOE_BLOCK)], zero_sem)

    @pl.when(i == 0)
    def _():
        zero_buf[...] = jnp.zeros_like(zero_buf)
        fills = [fill(zs_ref[e]) for e in range(N_EXPERTS)]
        for f in fills:
            f.start()
        for f in fills:
            f.wait()

        n_blocks = xs_hbm.shape[0] // MOE_BLOCK
        lax.fori_loop(nu_ref[0], n_blocks, lambda blk, c: (fill(blk * MOE_BLOCK).start(), c)[1], 0)
        lax.fori_loop(nu_ref[0], n_blocks, lambda blk, c: (fill(blk * MOE_BLOCK).wait(), c)[1], 0)

    @pl.when(i >= 2)
    def _():
        wait_rows(slot)

    stage[slot] = packed[...]
    n_iter = tn // 16
    epw = w_in[0].shape[0]

    def issue(it, carry):
        chunks = []
        for src, dst in zip(w_in, w_out):
            rows = src.shape[1] // n_iter
            r0 = pl.multiple_of(it * rows, rows)
            for e in range(epw):
                chunks.append((dst, e, r0, rows, src[e, pl.ds(r0, rows), :]))
        for u in range(2):
            g = it * 2 + u
            tile = stage.at[slot, pl.ds(pl.multiple_of(g * 8, 8), 8)]
            for s in range(8):
                for k in range(TOP_K):
                    pltpu.make_async_copy(tile.at[pl.ds(s, 1)], xs_hbm.at[pl.ds(dest_refs[k][g * 8 + s], 1)],
                                          row_sems.at[slot]).start(priority=k % 2)
        for dst, e, r0, rows, val in chunks:
            dst[e, pl.ds(r0, rows), :] = val.astype(BF16)
        return carry

    lax.fori_loop(0, n_iter, issue, 0)

    @pl.when(i == n - 1)
    def _():
        @pl.when(i >= 1)
        def _():
            wait_rows(1 - slot)

        wait_rows(slot)


def _dest_specs(n_steps, tn, index):
    return [pl.BlockSpec((tn,), lambda i, *_, k=k: (k * n_steps + index(i),), memory_space=pltpu.SMEM)
            for k in range(TOP_K)]


def _dispatch(zero_starts, n_used, dest_flat, xn, weights, P, tn):
    N = xn.shape[0]
    n_steps = N // tn
    assert N_EXPERTS % n_steps == 0
    epw = N_EXPERTS // n_steps
    w_specs = [pl.BlockSpec((epw,) + w.shape[1:], lambda i, *_: (i, 0, 0)) for w in weights]
    for w in weights:
        assert w.shape[1] % (16 * (tn // 16)) == 0
    grid_spec = pltpu.PrefetchScalarGridSpec(
        num_scalar_prefetch=2,
        grid=(n_steps,),
        in_specs=_dest_specs(n_steps, tn, lambda i: i) + [pl.BlockSpec((tn, D_PACKED), lambda i, zs, nu: (i, 0))]
        + w_specs,
        out_specs=[pl.BlockSpec(memory_space=pl.ANY)] + w_specs,
        scratch_shapes=[pltpu.VMEM((MOE_BLOCK, D_PACKED), jnp.uint32),
                        pltpu.VMEM((2, tn, D_PACKED), jnp.uint32),
                        pltpu.SemaphoreType.DMA, pltpu.SemaphoreType.DMA((2,))],
    )
    xs, *w_bf16 = pl.pallas_call(
        _dispatch_kernel,
        out_shape=[jax.ShapeDtypeStruct((P, D_PACKED), jnp.uint32)]
        + [jax.ShapeDtypeStruct(w.shape, BF16) for w in weights],
        grid_spec=grid_spec,
        compiler_params=_cparams(("arbitrary",)),
        name="dispatch",
    )(zero_starts, n_used, *([dest_flat] * TOP_K), xn, *weights)
    return xs, w_bf16


def _moe_ffn_kernel(be_ref, nu_ref, seg_ref, nxt_ref, xs_ref, wg_hbm, wu_hbm, wd_hbm, ys_ref,
                    wg_b, wu_b, wd_b, sems):
    i = pl.program_id(0)
    used = i < nu_ref[0]

    def weights(expert, slot):
        return [pltpu.make_async_copy(src.at[expert], dst.at[slot], sems.at[slot])
                for src, dst in ((wg_hbm, wg_b), (wu_hbm, wu_b), (wd_hbm, wd_b))]

    @pl.when(used)
    def _():
        new_expert = jnp.logical_or(i == 0, be_ref[i] != be_ref[jnp.maximum(i - 1, 0)])
        slot = seg_ref[i] % 2

        @pl.when(new_expert)
        def _():
            @pl.when(i == 0)
            def _():
                for cp in weights(be_ref[0], 0):
                    cp.start()

            for cp in weights(be_ref[i], slot):
                cp.wait()

            @pl.when(nxt_ref[i] >= 0)
            def _():
                for cp in weights(nxt_ref[i], 1 - slot):
                    cp.start()

        def gate_up(rows):
            xb = _unpack_pairs(xs_ref[rows, :]).astype(BF16)
            return (jnp.dot(xb, wg_b[slot], preferred_element_type=F32),
                    jnp.dot(xb, wu_b[slot], preferred_element_type=F32))

        def swish_mul(hg, hu):
            return (hg * _sigmoid(hg) * hu).astype(BF16)

        def down(rows, hm):
            ys_ref[rows, :] = _pack_pairs(jnp.dot(hm, wd_b[slot], preferred_element_type=F32))

        ra, rb = slice(0, MOE_BLOCK // 2), slice(MOE_BLOCK // 2, MOE_BLOCK)
        gu_a = gate_up(ra)
        hm_a = swish_mul(*gu_a)
        gu_b = gate_up(rb)
        down(ra, hm_a)
        hm_b = swish_mul(*gu_b)
        down(rb, hm_b)

    @pl.when(jnp.logical_not(used))
    def _():
        ys_ref[...] = jnp.zeros_like(ys_ref)


def _moe_ffn(blk_expert, n_used, xs, w_gate, w_up, w_down):
    P = xs.shape[0]
    nblk = P // MOE_BLOCK
    blk = jnp.arange(nblk, dtype=jnp.int32)
    starts = jnp.logical_and(jnp.concatenate([jnp.ones((1,), bool), blk_expert[1:] != blk_expert[:-1]]),
                             blk < n_used[0])
    seg = jnp.cumsum(starts.astype(jnp.int32)) - 1
    next_start = lax.cummin(jnp.where(starts, blk, nblk), reverse=True)
    next_start = jnp.concatenate([next_start[1:], jnp.full((1,), nblk, jnp.int32)])
    nxt = jnp.where(next_start < nblk, blk_expert[jnp.minimum(next_start, nblk - 1)], -1).astype(jnp.int32)
    row_block = pl.BlockSpec((MOE_BLOCK, D_PACKED), lambda i, *_: (i, 0))
    grid_spec = pltpu.PrefetchScalarGridSpec(
        num_scalar_prefetch=4,
        grid=(nblk,),
        in_specs=[row_block] + [pl.BlockSpec(memory_space=pl.ANY)] * 3,
        out_specs=row_block,
        scratch_shapes=[pltpu.VMEM((2, D_MODEL, D_EXPERT), BF16),
                        pltpu.VMEM((2, D_MODEL, D_EXPERT), BF16),
                        pltpu.VMEM((2, D_EXPERT, D_MODEL), BF16),
                        pltpu.SemaphoreType.DMA((2,))],
    )
    return pl.pallas_call(
        _moe_ffn_kernel,
        out_shape=jax.ShapeDtypeStruct((P, D_PACKED), jnp.uint32),
        grid_spec=grid_spec,
        compiler_params=_cparams(("arbitrary",)),
        name="moe_ffn",
    )(blk_expert, n_used, seg, nxt, xs, w_gate, w_up, w_down)


def _combine_kernel(*refs, final):
    dcur_refs, dnext_refs = refs[:TOP_K], refs[TOP_K:2 * TOP_K]
    h_ref, gate_ref, fg_ref, ys_hbm, o_ref, ybuf, sems = refs[2 * TOP_K:]
    tm = h_ref.shape[0]
    i = pl.program_id(0)
    n = pl.num_programs(0)

    def gather(drefs, slot):
        def issue(g, carry):
            for k in range(TOP_K):
                tile = ybuf.at[slot, pl.ds(pl.multiple_of(k * tm + g * 8, 8), 8)]
                for s in range(8):
                    pltpu.make_async_copy(ys_hbm.at[pl.ds(drefs[k][g * 8 + s], 1)],
                                          tile.at[pl.ds(s, 1)], sems.at[slot]).start(priority=k % 2)
            return carry

        lax.fori_loop(0, tm // 8, issue, 0)

    @pl.when(i == 0)
    def _():
        gather(dcur_refs, 0)

    @pl.when(i + 1 < n)
    def _():
        gather(dnext_refs, (i + 1) % 2)

    slot = i % 2
    pltpu.make_async_copy(ys_hbm.at[pl.ds(0, TOP_K * tm)], ybuf.at[slot], sems.at[slot]).wait()
    g = gate_ref[...]
    hh = h_ref[...]
    for k in range(TOP_K):
        hh = hh + _unpack_pairs(ybuf[slot, pl.ds(k * tm, tm), :]) * g[:, k:k + 1]
    o_ref[...] = _rms(hh, fg_ref[...]) if final else hh


def _combine(h, ys, dest_flat, gate_t, fg, tm, final):
    N = h.shape[0]
    n = N // tm
    rows = pl.BlockSpec((tm, D_MODEL), lambda i: (i, 0))
    return pl.pallas_call(
        functools.partial(_combine_kernel, final=final),
        out_shape=jax.ShapeDtypeStruct((N, D_MODEL), F32),
        grid=(n,),
        in_specs=_dest_specs(n, tm, lambda i: i) + _dest_specs(n, tm, lambda i: jnp.minimum(i + 1, n - 1)) + [
                  rows, pl.BlockSpec((tm, TOP_K), lambda i: (i, 0)),
                  pl.BlockSpec((1, D_MODEL), lambda i: (0, 0)),
                  pl.BlockSpec(memory_space=pl.ANY)],
        out_specs=rows,
        scratch_shapes=[pltpu.VMEM((2, TOP_K * tm, D_PACKED), jnp.uint32), pltpu.SemaphoreType.DMA((2,))],
        compiler_params=_cparams(("arbitrary",)),
        name="combine",
    )(*([dest_flat] * (2 * TOP_K)), h, gate_t, fg, ys)


def _reorder_w_in(w_in):
    n_gla = 2 * GLA_KEY_WIDTH + 2 * GLA_WIDTH
    z0 = n_gla
    a0 = z0 + 2 * GLA_GATE_RANK
    pad = jnp.zeros((D_MODEL, Z_PAD - 2 * GLA_GATE_RANK), w_in.dtype)
    return jnp.concatenate([w_in[:, :n_gla], w_in[:, a0:], w_in[:, z0:a0], pad], axis=1)


def kernel(x, norm1_gain, w_in, gla_up_fwd, gla_up_fwd_bias, gla_up_bwd, gla_up_bwd_bias, gla_out_gain, q_norm_gain, k_norm_gain, att_out_gain, w_out, norm2_gain, w_group, b_group, w_expert, b_expert, w_gate, w_up, w_down, final_gain):
    B, T, D = x.shape
    N = B * T
    depth = norm1_gain.shape[0]
    assert D == D_MODEL and T % TOKEN_TILE == 0 and T % GRID_W == 0
    tm, tq, key_block = TOKEN_TILE, ATT_Q_BLOCK, ATT_KEY_BLOCK
    h = x.reshape(N, D)
    cos, se, so = _rope_tables(T)
    head_mean = jnp.asarray(
        np.kron(np.eye(ATT_HEADS, dtype=np.float32),
                np.full((ATT_HEAD_DIM, ATT_HEAD_DIM), 1.0 / ATT_HEAD_DIM, np.float32)), BF16)
    for l in range(depth):
        w = _reorder_w_in(w_in[l]).astype(BF16)
        r = GLA_GATE_RANK
        up = jnp.zeros((Z_PAD, 2 * GLA_KEY_WIDTH), F32)
        up = up.at[:r, :GLA_KEY_WIDTH].set(gla_up_fwd[l]).at[r:2 * r, GLA_KEY_WIDTH:].set(gla_up_bwd[l])
        upb = jnp.concatenate([gla_up_fwd_bias[l], gla_up_bwd_bias[l]])[None, :]
        qg = jnp.tile(q_norm_gain[l], ATT_HEADS)[None, :]
        kg = jnp.tile(k_norm_gain[l], ATT_KV_HEADS)[None, :]
        gq, gk, gv, gg, laf, lab, aqt, ak, avt = _in_proj(
            h, B, T, tm, norm1_gain[l][None, :], w, up.astype(BF16), upb, qg, kg, head_mean, cos, se, so)
        o_gla = _gla(gq, gk, gv, gg, laf, lab, gla_out_gain[l][None, :], B, T)
        o_att_t = _attn(aqt, ak, avt, B, T, tq, key_block)

        wr = jnp.zeros((LANES, D), F32)
        wr = wr.at[:N_GROUPS].set(w_group[l].T).at[N_GROUPS:N_GROUPS + N_EXPERTS].set(w_expert[l].T)
        br = jnp.zeros((LANES, 1), F32)
        br = br.at[:N_GROUPS, 0].set(b_group[l]).at[N_GROUPS:N_GROUPS + N_EXPERTS, 0].set(b_expert[l])
        h, xn, eid, gate, rank, cnt = _out_proj(
            h, o_gla, o_att_t, w_out[l].astype(BF16), att_out_gain[l][:, None], norm2_gain[l][None, :],
            wr.astype(BF16), br, tm)

        counts = cnt[:, 0].astype(jnp.int32)
        padded = (counts + MOE_BLOCK - 1) // MOE_BLOCK * MOE_BLOCK
        pad_ends = jnp.cumsum(padded)
        pad_starts = pad_ends - padded
        seg_start = jnp.sum(jnp.where(eid[:, :, None] == jnp.arange(N_EXPERTS, dtype=jnp.int32),
                                      pad_starts.astype(jnp.int32), 0), axis=-1)
        dest = (seg_start + rank).reshape(-1)
        P = N * TOP_K + N_EXPERTS * MOE_BLOCK
        nblk = P // MOE_BLOCK
        blk_start = jnp.arange(nblk, dtype=jnp.int32) * MOE_BLOCK
        blk_expert = jnp.minimum(
            jnp.sum((pad_ends[None, :] <= blk_start[:, None]).astype(jnp.int32), axis=1), N_EXPERTS - 1)
        n_used = (pad_ends[-1:] // MOE_BLOCK).astype(jnp.int32)

        zero_starts = jnp.maximum(pad_ends - MOE_BLOCK, 0).astype(jnp.int32)
        xs, expert_w = _dispatch(zero_starts, n_used, dest, xn, (w_gate[l], w_up[l], w_down[l]), P, tm)
        ys = _moe_ffn(blk_expert, n_used, xs, *expert_w)
        h = _combine(h, ys, dest, gate.T, final_gain[None, :], tm, l == depth - 1)
    return h.reshape(B, T, D)
```

```python
import functools

import numpy as np
import jax
import jax.numpy as jnp
from jax import lax
from jax.experimental import pallas as pl
from jax.experimental.pallas import tpu as pltpu

F32 = jnp.float32
BF16 = jnp.bfloat16

D_MODEL = 1024
EPS = 1e-6
GRID_W = 64

GLA_HEADS = 4
GLA_DK = 64
GLA_DV = 128
GLA_KEY_WIDTH = GLA_HEADS * GLA_DK
GLA_WIDTH = GLA_HEADS * GLA_DV
GLA_GATE_RANK = 16
GLA_GATE_NORM = 16.0
GLA_CHUNK = 64

ATT_HEADS = 8
ATT_KV_HEADS = 2
ATT_HEAD_DIM = 64
ATT_GROUP = ATT_HEADS // ATT_KV_HEADS
ATT_WIDTH = ATT_HEADS * ATT_HEAD_DIM
ATT_KV_WIDTH = ATT_KV_HEADS * ATT_HEAD_DIM
ROPE_THETA = 10000.0
LOG2_E = 1.4426950408889634

N_GROUPS = 8
EXPERTS_PER_GROUP = 8
N_EXPERTS = N_GROUPS * EXPERTS_PER_GROUP
TOP_K = 2
D_EXPERT = 512
MOE_BLOCK = 256

LANES = 128
Z_PAD = LANES

TOKEN_TILE = 512
ATT_Q_BLOCK = 128
ATT_KEY_BLOCK = 256
IN_PROJ_PARTS = 4

_OFF_GQ = 0
_OFF_GK = _OFF_GQ + GLA_KEY_WIDTH
_OFF_GV = _OFF_GK + GLA_KEY_WIDTH
_OFF_GG = _OFF_GV + GLA_WIDTH
_OFF_AQ = _OFF_GG + GLA_WIDTH
_OFF_AK = _OFF_AQ + ATT_WIDTH
_OFF_AV = _OFF_AK + ATT_KV_WIDTH
_OFF_Z = _OFF_AV + ATT_KV_WIDTH
D_IN_PAD = _OFF_Z + Z_PAD

VMEM_LIMIT = 56 * 1024 * 1024


def _cparams(semantics):
    return pltpu.CompilerParams(dimension_semantics=semantics, vmem_limit_bytes=VMEM_LIMIT)


def _rms(x, gain):
    return x * lax.rsqrt(jnp.mean(x * x, axis=-1, keepdims=True) + EPS) * gain


def _sigmoid(x):
    return 1.0 / (1.0 + jnp.exp(-x))


def _head_norm_rope(x, head_mean, gain, cos, sin_even, sin_odd):
    w = x.shape[1]
    ms = jnp.dot((x * x).astype(BF16), head_mean, preferred_element_type=F32)
    xn = x * lax.rsqrt(ms + EPS) * gain
    reps = w // LANES
    tile = lambda t: t if reps == 1 else jnp.concatenate([t] * reps, axis=1)
    nxt = pltpu.roll(xn, w - 1, 1)
    prv = pltpu.roll(xn, 1, 1)
    return xn * tile(cos) + nxt * tile(sin_even) + prv * tile(sin_odd)


def _in_proj_kernel(x_ref, g1_ref, w_ref, up_ref, upb_ref, qg_ref, kg_ref, hm_ref,
                    cos_ref, se_ref, so_ref,
                    gq_ref, gk_ref, gv_ref, gg_ref, laf_ref, lab_ref, aqt_ref, ak_ref, avt_ref):
    hm = hm_ref[...]
    tm = x_ref.shape[0]

    def rows(rs):
        u = _rms(x_ref[rs, :], g1_ref[...]).astype(BF16)

        def proj(lo, hi):
            return jnp.dot(u, w_ref[:, lo:hi], preferred_element_type=F32)

        cos, se, so = cos_ref[rs, :], se_ref[rs, :], so_ref[rs, :]
        r_q = proj(_OFF_AQ, _OFF_AK)
        r_gqk = proj(_OFF_GQ, _OFF_GV)
        q = _head_norm_rope(r_q, hm, qg_ref[...], cos, se, so)
        aqt_ref[0, :, rs] = (q * (ATT_HEAD_DIM ** -0.5 * LOG2_E)).T.astype(BF16)
        gq_ref[rs, :] = r_gqk[:, :GLA_KEY_WIDTH].astype(BF16)
        gk_ref[rs, :] = r_gqk[:, GLA_KEY_WIDTH:].astype(BF16)

        r_kvz = proj(_OFF_AK, D_IN_PAD)
        r_gv = proj(_OFF_GV, _OFF_GG)
        k = _head_norm_rope(r_kvz[:, :ATT_KV_WIDTH], hm[:ATT_KV_WIDTH, :ATT_KV_WIDTH], kg_ref[...], cos, se, so)
        ak_ref[rs, :] = k.astype(BF16)
        avt_ref[0, :, rs] = r_kvz[:, ATT_KV_WIDTH:2 * ATT_KV_WIDTH].T.astype(BF16)
        gv_ref[rs, :] = r_gv.astype(BF16)

        z = r_kvz[:, 2 * ATT_KV_WIDTH:].astype(BF16)
        zl = jnp.dot(z, up_ref[...], preferred_element_type=F32) + upb_ref[...]
        r_gg = proj(_OFF_GG, _OFF_AQ)
        la = (jnp.minimum(zl, 0.0) - jnp.log(1.0 + jnp.exp(-jnp.abs(zl)))) * (1.0 / GLA_GATE_NORM)
        laf_ref[rs, :] = la[:, :GLA_KEY_WIDTH]
        lab_ref[rs, :] = la[:, GLA_KEY_WIDTH:]
        gg_ref[rs, :] = r_gg.astype(BF16)

    for p in range(IN_PROJ_PARTS):
        rows(slice(p * tm // IN_PROJ_PARTS, (p + 1) * tm // IN_PROJ_PARTS))


def _rope_tables(T):
    t = np.arange(T)
    row = (t // GRID_W).astype(np.float32)
    col = (t % GRID_W).astype(np.float32)
    axis_dim = ATT_HEAD_DIM // 2
    inv_freq = (ROPE_THETA ** (-np.arange(0, axis_dim, 2, dtype=np.float32) / axis_dim)).astype(np.float32)
    ang = np.concatenate([row[:, None] * inv_freq, col[:, None] * inv_freq], axis=-1)
    ang = np.repeat(ang, 2, axis=1)
    ang = np.tile(ang, (1, LANES // ATT_HEAD_DIM))
    even = (np.arange(LANES) % 2 == 0)[None, :]
    cos = np.cos(ang)
    sin = np.sin(ang)
    return (jnp.asarray(cos, F32), jnp.asarray(np.where(even, -sin, 0.0), F32),
            jnp.asarray(np.where(even, 0.0, sin), F32))


def _in_proj(xt, B, T, tm, g1, w, up, upb, qg, kg, hm, cos, se, so):
    N = xt.shape[0]
    tpb = T // tm
    const = lambda shape: pl.BlockSpec(shape, lambda i: (0,) * len(shape))
    rows = lambda width: pl.BlockSpec((tm, width), lambda i: (i, 0))
    pos = pl.BlockSpec((tm, LANES), lambda i: (i % tpb, 0))
    out_shape = (
        jax.ShapeDtypeStruct((N, GLA_KEY_WIDTH), BF16),
        jax.ShapeDtypeStruct((N, GLA_KEY_WIDTH), BF16),
        jax.ShapeDtypeStruct((N, GLA_WIDTH), BF16),
        jax.ShapeDtypeStruct((N, GLA_WIDTH), BF16),
        jax.ShapeDtypeStruct((N, GLA_KEY_WIDTH), F32),
        jax.ShapeDtypeStruct((N, GLA_KEY_WIDTH), F32),
        jax.ShapeDtypeStruct((B, ATT_WIDTH, T), BF16),
        jax.ShapeDtypeStruct((N, ATT_KV_WIDTH), BF16),
        jax.ShapeDtypeStruct((B, ATT_KV_WIDTH, T), BF16),
    )
    cols = lambda width: pl.BlockSpec((1, width, tm), lambda i: (i // tpb, 0, i % tpb))
    out_specs = (
        rows(GLA_KEY_WIDTH), rows(GLA_KEY_WIDTH), rows(GLA_WIDTH), rows(GLA_WIDTH),
        rows(GLA_KEY_WIDTH), rows(GLA_KEY_WIDTH),
        cols(ATT_WIDTH), rows(ATT_KV_WIDTH), cols(ATT_KV_WIDTH),
    )
    return pl.pallas_call(
        _in_proj_kernel,
        out_shape=out_shape,
        grid=(N // tm,),
        in_specs=[rows(D_MODEL), const((1, D_MODEL)), const((D_MODEL, D_IN_PAD)),
                  const((Z_PAD, 2 * GLA_KEY_WIDTH)), const((1, 2 * GLA_KEY_WIDTH)),
                  const((1, ATT_WIDTH)), const((1, ATT_KV_WIDTH)), const((ATT_WIDTH, ATT_WIDTH)),
                  pos, pos, pos],
        out_specs=out_specs,
        compiler_params=_cparams(("arbitrary",)),
        name="in_proj",
    )(xt, g1, w, up, upb, qg, kg, hm, cos, se, so)


GLA_GROUP = 4
GLA_FINISH_GROUP = 32


def _gla_kernel(q_ref, k_ref, v_ref, g_ref, laf_ref, lab_ref, gain_ref, o_ref,
                part_scr, qif_scr, qib_scr, kvf_scr, kvb_scr, decf_scr, decb_scr, lf_scr, lb_scr):
    C, G = GLA_CHUNK, GLA_GROUP
    R = C * G
    T = q_ref.shape[0]
    n_chunks = T // C
    gain = gain_ref[...]
    row = lax.broadcasted_iota(jnp.int32, (C, C), 0)
    col = lax.broadcasted_iota(jnp.int32, (C, C), 1)
    tril = row >= col
    triu = row <= col
    rr = lax.broadcasted_iota(jnp.int32, (R, R), 0)
    cc = lax.broadcasted_iota(jnp.int32, (R, R), 1)
    same_chunk = (rr // C) == (cc // C)
    lf_scr[...] = jnp.where(same_chunk & (rr >= cc), 1.0, 0.0).astype(BF16)
    lb_scr[...] = jnp.where(same_chunk & (rr <= cc), 1.0, 0.0).astype(BF16)

    directions = ((laf_ref, lf_scr, tril, C // 2 - 1, C - 1, qif_scr, kvf_scr, decf_scr),
                  (lab_ref, lb_scr, triu, C // 2, 0, qib_scr, kvb_scr, decb_scr))
    heads = tuple((slice(h * GLA_DK, (h + 1) * GLA_DK), slice(h * GLA_DV, (h + 1) * GLA_DV)) for h in range(2))
    nt = (((1,), (1,)), ((), ()))
    tn = (((0,), (0,)), ((), ()))

    def intra(sb, carry):
        r0 = pl.multiple_of(sb * R, R)
        q2 = q_ref[pl.ds(r0, R), :].astype(F32) * (GLA_DK ** -0.5)
        k2 = k_ref[pl.ds(r0, R), :].astype(F32)
        v2 = v_ref[pl.ds(r0, R), :]
        def cumulate(d):
            la_ref, tri_scr = d[0], d[1]
            la = la_ref[pl.ds(r0, R), :]
            la_hi = la.astype(BF16)
            la_lo = (la - la_hi.astype(F32)).astype(BF16)
            both = jnp.dot(tri_scr[...], jnp.concatenate([la_hi, la_lo], axis=1), preferred_element_type=F32)
            return both[:, :2 * GLA_DK] + both[:, 2 * GLA_DK:]

        def scale(d, b):
            _, _, _, i_ref, i_last, qi_scr, _, dec_scr = d
            out = []
            for c in range(G):
                rows = slice(c * C, (c + 1) * C)
                bc, qc, kc = b[rows], q2[rows], k2[rows]
                b_ref = bc[i_ref:i_ref + 1, :]
                b_last = bc[i_last:i_last + 1, :]
                qf = (qc * jnp.exp(bc - b_ref)).astype(BF16)
                kf = (kc * jnp.exp(b_ref - bc)).astype(BF16)
                qi_scr[pl.ds(r0 + c * C, C), :] = (qc * jnp.exp(bc)).astype(BF16)
                kl = (kc * jnp.exp(b_last - bc)).astype(BF16)
                decay = jnp.exp(b_last)
                for h, (ks, _) in enumerate(heads):
                    dec_scr[sb * G + c, h] = decay[:, ks]
                out.append((qf, kf, kl))
            return out

        def scores(scaled):
            return [[lax.dot_general(qf[:, ks], kf[:, ks], nt, preferred_element_type=F32) for ks, _ in heads]
                    for qf, kf, _ in scaled]

        def apply(d, scaled, raw):
            mask, kv_scr = d[2], d[6]
            outs = []
            for c in range(G):
                rows = slice(c * C, (c + 1) * C)
                per_head = []
                for h, (ks, vs) in enumerate(heads):
                    vh = v2[rows, vs]
                    sc = jnp.where(mask, raw[c][h], 0.0).astype(BF16)
                    per_head.append(jnp.dot(sc, vh, preferred_element_type=F32))
                    kv_scr[sb * G + c, h] = lax.dot_general(vh, scaled[c][2][:, ks], tn, preferred_element_type=F32)
                outs.append(per_head)
            return outs

        fwd, bwd = directions
        b_f = cumulate(fwd)
        b_b = cumulate(bwd)
        sc_f = scale(fwd, b_f)
        sc_b = scale(bwd, b_b)
        raw_f = scores(sc_f)
        raw_b = scores(sc_b)
        o_f = apply(fwd, sc_f, raw_f)
        o_b = apply(bwd, sc_b, raw_b)
        for c in range(G):
            part_scr[pl.ds(r0 + c * C, C), :] = jnp.concatenate(
                [o_f[c][h] + o_b[c][h] for h in range(2)], axis=1)
        return carry

    lax.fori_loop(0, n_chunks // G, intra, 0)

    def scan(n, carry):
        sf, sb = carry
        nb = n_chunks - 1 - n
        new_f, new_b = [], []
        for h in range(2):
            kv = kvf_scr[n, h]
            kvf_scr[n, h] = sf[h]
            new_f.append(decf_scr[n, h] * sf[h] + kv)
            kv = kvb_scr[nb, h]
            kvb_scr[nb, h] = sb[h]
            new_b.append(decb_scr[nb, h] * sb[h] + kv)
        return tuple(new_f), tuple(new_b)

    zero_state = tuple(jnp.zeros((GLA_DV, GLA_DK), F32) for _ in range(2))
    lax.fori_loop(0, n_chunks, scan, (zero_state, zero_state), unroll=4)

    GF = min(GLA_FINISH_GROUP, n_chunks)
    RF = C * GF

    def finish(sb, carry):
        r0 = pl.multiple_of(sb * RF, RF)
        g = g_ref[pl.ds(r0, RF), :].astype(F32)
        part = part_scr[pl.ds(r0, RF), :]
        qf = qif_scr[pl.ds(r0, RF), :]
        qb = qib_scr[pl.ds(r0, RF), :]
        inter = []
        for c in range(GF):
            rows = slice(c * C, (c + 1) * C)
            inter.append([
                lax.dot_general(qf[rows, ks], kvf_scr[sb * GF + c, h].astype(BF16), nt, preferred_element_type=F32)
                + lax.dot_general(qb[rows, ks], kvb_scr[sb * GF + c, h].astype(BF16), nt, preferred_element_type=F32)
                for h, (ks, _) in enumerate(heads)])
        gate = g * _sigmoid(g)
        for c in range(GF):
            rows = slice(c * C, (c + 1) * C)
            normed = jnp.concatenate([_rms(part[rows, vs] + inter[c][h], gain) for h, (_, vs) in enumerate(heads)],
                                     axis=1)
            o_ref[pl.ds(r0 + c * C, C), :] = (normed * gate[rows]).astype(BF16)
        return carry

    lax.fori_loop(0, n_chunks // GF, finish, 0)


def _gla(gq, gk, gv, gg, laf, lab, gain, B, T):
    N = gq.shape[0]
    pairs = GLA_HEADS // 2
    n_chunks = T // GLA_CHUNK
    assert n_chunks % GLA_GROUP == 0 and n_chunks % min(GLA_FINISH_GROUP, n_chunks) == 0
    group_rows = GLA_CHUNK * GLA_GROUP
    kspec = pl.BlockSpec((T, 2 * GLA_DK), lambda b, p: (b, p))
    vspec = pl.BlockSpec((T, 2 * GLA_DV), lambda b, p: (b, p))
    state = pltpu.VMEM((n_chunks, 2, GLA_DV, GLA_DK), F32)
    decay = pltpu.VMEM((n_chunks, 2, 1, GLA_DK), F32)
    return pl.pallas_call(
        _gla_kernel,
        out_shape=jax.ShapeDtypeStruct((N, GLA_WIDTH), BF16),
        grid=(B, pairs),
        in_specs=[kspec, kspec, vspec, vspec, kspec, kspec,
                  pl.BlockSpec((1, GLA_DV), lambda b, p: (0, 0))],
        out_specs=vspec,
        scratch_shapes=[pltpu.VMEM((T, 2 * GLA_DV), F32),
                        pltpu.VMEM((T, 2 * GLA_DK), BF16), pltpu.VMEM((T, 2 * GLA_DK), BF16),
                        state, state, decay, decay,
                        pltpu.VMEM((group_rows, group_rows), BF16), pltpu.VMEM((group_rows, group_rows), BF16)],
        compiler_params=_cparams(("arbitrary", "arbitrary")),
        name="gla",
    )(gq, gk, gv, gg, laf, lab, gain)


def _attn_kernel(qt_ref, k_ref, vt_ref, ot_ref, vext_scr, s0_scr, s1_scr, p0_scr, p1_scr, m0_scr, m1_scr,
                 *, tq, key_block, steps_per_iter):
    T = k_ref.shape[0]
    nq = T // tq
    n_blocks = ATT_KV_HEADS * nq
    dh = ATT_HEAD_DIM
    gw = ATT_GROUP * dh
    cols = ATT_GROUP * tq

    row = lax.broadcasted_iota(jnp.int32, (ATT_KV_WIDTH, T), 0)
    for hj in range(ATT_KV_HEADS):
        own = (row >= hj * dh) & (row < (hj + 1) * dh)
        vext_scr[hj] = jnp.where(own, vt_ref[0], jnp.ones((), BF16))

    s_scr, p_scr, m_scr = (s0_scr, s1_scr), (p0_scr, p1_scr), (m0_scr, m1_scr)

    def locate(blk):
        if isinstance(blk, int):
            hj = blk // nq
            return hj, hj == 0, (blk - hj * nq) * tq
        second = blk >= nq
        hj = jnp.where(second, 1, 0)
        return hj, jnp.logical_not(second), pl.multiple_of((blk - hj * nq) * tq, tq)

    def step(blk, sa, do_scores=True, do_probs=True, do_out=True):
        sb = 1 - sa
        if do_scores:
            hj, first, c0 = locate(blk)
            q4 = jnp.concatenate(
                [qt_ref[0, pl.ds(pl.multiple_of(hj * gw + h * dh, dh), dh), pl.ds(c0, tq)] for h in range(ATT_GROUP)],
                axis=1)
            zeros = jnp.zeros_like(q4)
            qe = jnp.where(first, jnp.concatenate([q4, zeros], axis=0), jnp.concatenate([zeros, q4], axis=0))
            m = jnp.full((8, cols), -jnp.inf, F32)
        if do_probs:
            mx = m_scr[sb][...]
        if do_out:
            hj_o, first_o, c0_o = locate(blk - 2)
            acc = jnp.zeros((ATT_KV_WIDTH, cols), F32)
        for kb in range(T // key_block):
            ks = slice(kb * key_block, (kb + 1) * key_block)
            if do_scores:
                s = jnp.dot(k_ref[ks, :], qe, preferred_element_type=F32)
                s_scr[sa][ks, :] = s
                m = jnp.maximum(m, jnp.max(s.reshape(key_block // 8, 8, cols), axis=0))
            if do_probs:
                p_scr[sb][ks, :] = jnp.exp2(s_scr[sb][ks, :] - mx).astype(BF16)
            if do_out:
                acc = acc + jnp.dot(vext_scr[hj_o, :, ks], p_scr[sa][ks, :], preferred_element_type=F32)
        if do_scores:
            m_scr[sa][...] = jnp.max(m, axis=0, keepdims=True)
        if do_out:
            num = jnp.where(first_o, acc[:dh], acc[dh:])
            den = jnp.where(first_o, acc[dh:dh + 1], acc[0:1])
            o = (num / den).astype(BF16)
            for h in range(ATT_GROUP):
                ot_ref[0, pl.ds(pl.multiple_of(hj_o * gw + h * dh, dh), dh), pl.ds(c0_o, tq)] = (
                    o[:, h * tq:(h + 1) * tq])

    step(0, 0, do_probs=False, do_out=False)
    step(1, 1, do_out=False)

    def steady(it, carry):
        for u in range(steps_per_iter):
            pl.when(it >= 0)(functools.partial(step, 2 + steps_per_iter * it + u, u % 2))
        return carry

    lax.fori_loop(0, (n_blocks - 2) // steps_per_iter, steady, 0)
    step(n_blocks, 0, do_scores=False)
    step(n_blocks + 1, 1, do_scores=False, do_probs=False)


def _attn(aqt, ak, avt, B, T, tq, key_block):
    assert T % (2 * tq) == 0 and tq % LANES == 0 and ATT_KV_HEADS == 2
    cols = ATT_GROUP * tq
    steps_per_iter = 2
    return pl.pallas_call(
        functools.partial(_attn_kernel, tq=tq, key_block=key_block, steps_per_iter=steps_per_iter),
        out_shape=jax.ShapeDtypeStruct((B, ATT_WIDTH, T), BF16),
        grid=(B,),
        in_specs=[pl.BlockSpec((1, ATT_WIDTH, T), lambda b: (b, 0, 0)),
                  pl.BlockSpec((T, ATT_KV_WIDTH), lambda b: (b, 0)),
                  pl.BlockSpec((1, ATT_KV_WIDTH, T), lambda b: (b, 0, 0))],
        out_specs=pl.BlockSpec((1, ATT_WIDTH, T), lambda b: (b, 0, 0)),
        scratch_shapes=[pltpu.VMEM((ATT_KV_HEADS, ATT_KV_WIDTH, T), BF16),
                        pltpu.VMEM((T, cols), F32), pltpu.VMEM((T, cols), F32),
                        pltpu.VMEM((T, cols), BF16), pltpu.VMEM((T, cols), BF16),
                        pltpu.VMEM((1, cols), F32), pltpu.VMEM((1, cols), F32)],
        compiler_params=_cparams(("arbitrary",)),
        name="attn",
    )(aqt, ak, avt)


def _out_proj_kernel(x_ref, og_ref, oa_ref, wo_ref, ag_ref, g2_ref, wr_ref, br_ref, tri_ref,
                     h_ref, xn_ref, eid_ref, gate_ref, rank_ref, cnt_ref, run_scr):
    tm = x_ref.shape[0]

    @pl.when(pl.program_id(0) == 0)
    def _():
        run_scr[...] = jnp.zeros_like(run_scr)

    def branch_norm(ts):
        oat = oa_ref[0, :, ts].astype(F32)
        return (oat * lax.rsqrt(jnp.mean(oat * oat, axis=0, keepdims=True) + EPS) * ag_ref[...]).astype(BF16)

    def mix(ts, oan):
        return (jnp.dot(og_ref[ts, :], wo_ref[:GLA_WIDTH, :], preferred_element_type=F32)
                + lax.dot_general(oan, wo_ref[GLA_WIDTH:, :], (((0,), (0,)), ((), ())),
                                  preferred_element_type=F32))

    def residual_norm(ts, y):
        h = x_ref[ts, :] + y
        h_ref[ts, :] = h
        xn = _rms(h, g2_ref[...])
        xn_ref[ts, :] = _pack_pairs(xn)
        return xn.astype(BF16)

    def logits(xnb):
        return lax.dot_general(wr_ref[...], xnb, (((1,), (1,)), ((), ())), preferred_element_type=F32) + br_ref[...]

    whole = slice(0, tm)
    lt = logits(residual_norm(whole, mix(whole, branch_norm(whole))))
    iota8 = lax.broadcasted_iota(jnp.int32, (N_GROUPS, tm), 0)

    def first_argmax(v):
        top = jnp.max(v, axis=0, keepdims=True)
        idx = jnp.min(jnp.where(v == top, iota8, N_GROUPS), axis=0, keepdims=True)
        return top, idx

    gl = lt[0:N_GROUPS]
    gmax, gidx = first_argmax(gl)
    gw = 1.0 / jnp.sum(jnp.exp(gl - gmax), axis=0, keepdims=True)
    esel = jnp.zeros((EXPERTS_PER_GROUP, tm), F32)
    for g in range(N_GROUPS):
        lo = N_GROUPS + g * EXPERTS_PER_GROUP
        esel = jnp.where(gidx == g, lt[lo:lo + EXPERTS_PER_GROUP], esel)
    v1, i1 = first_argmax(esel)
    rest = jnp.where(iota8 == i1, -jnp.inf, esel)
    v2, i2 = first_argmax(rest)
    t = jnp.exp(v2 - v1)
    den = 1.0 + t
    e1 = gidx * EXPERTS_PER_GROUP + i1
    e2 = gidx * EXPERTS_PER_GROUP + i2
    eid_ref[0:1, :] = e1
    eid_ref[1:2, :] = e2
    gate_ref[0:1, :] = gw * (1.0 / den)
    gate_ref[1:2, :] = gw * (t / den)

    iota_e = lax.broadcasted_iota(jnp.int32, (N_EXPERTS, tm), 0)
    oh1 = iota_e == e1
    oh2 = iota_e == e2
    both = jnp.where(oh1, 1.0, jnp.where(oh2, 1.0, 0.0)).astype(BF16)
    prefix = jnp.dot(both, tri_ref[...], preferred_element_type=F32)
    run = run_scr[...]
    base = prefix - 1.0 + run[:, 0:1]
    rank_ref[0:1, :] = jnp.sum(jnp.where(oh1, base, 0.0), axis=0, keepdims=True).astype(jnp.int32)
    rank_ref[1:2, :] = jnp.sum(jnp.where(oh2, base, 0.0), axis=0, keepdims=True).astype(jnp.int32)
    run = run + prefix[:, tm - 1:tm]
    run_scr[...] = run
    cnt_ref[...] = run


def _out_proj(xt, o_gla, o_att_t, wo, ag, g2, wr, br, tm):
    N = xt.shape[0]
    tpb = o_att_t.shape[2] // tm
    const = lambda shape: pl.BlockSpec(shape, lambda i: (0,) * len(shape))
    rows = lambda width: pl.BlockSpec((tm, width), lambda i: (i, 0))
    cols = pl.BlockSpec((TOP_K, tm), lambda i: (0, i))
    tri = jnp.asarray(np.triu(np.ones((tm, tm), np.float32)), BF16)
    return pl.pallas_call(
        _out_proj_kernel,
        out_shape=(jax.ShapeDtypeStruct((N, D_MODEL), F32),
                   jax.ShapeDtypeStruct((N, D_PACKED), jnp.uint32),
                   jax.ShapeDtypeStruct((TOP_K, N), jnp.int32),
                   jax.ShapeDtypeStruct((TOP_K, N), F32),
                   jax.ShapeDtypeStruct((TOP_K, N), jnp.int32),
                   jax.ShapeDtypeStruct((N_EXPERTS, LANES), F32)),
        grid=(N // tm,),
        in_specs=[rows(D_MODEL), rows(GLA_WIDTH),
                  pl.BlockSpec((1, ATT_WIDTH, tm), lambda i: (i // tpb, 0, i % tpb)),
                  const((D_MODEL, D_MODEL)),
                  const((ATT_WIDTH, 1)), const((1, D_MODEL)), const((LANES, D_MODEL)),
                  const((LANES, 1)), const((tm, tm))],
        out_specs=(rows(D_MODEL), rows(D_PACKED), cols, cols, cols, const((N_EXPERTS, LANES))),
        scratch_shapes=[pltpu.VMEM((N_EXPERTS, LANES), F32)],
        compiler_params=_cparams(("arbitrary",)),
        name="out_proj",
    )(xt, o_gla, o_att_t, wo, ag, g2, wr, br, tri)


D_PACKED = D_MODEL // 2


def _pack_pairs(x):
    w = x.shape[1] // 2
    bits = pltpu.bitcast(x.astype(BF16).astype(F32), jnp.uint32)
    return bits[:, :w] | (bits[:, w:] >> 16)


def _unpack_pairs(words):
    hi = pltpu.bitcast(words & jnp.uint32(0xFFFF0000), F32)
    lo = pltpu.bitcast(words << 16, F32)
    return jnp.concatenate([hi, lo], axis=1)


def _dispatch_kernel(zs_ref, nu_ref, *refs):
    dest_refs = refs[:TOP_K]
    packed = refs[TOP_K]
    w_in = refs[TOP_K + 1:TOP_K + 4]
    xs_hbm = refs[TOP_K + 4]
    w_out = refs[TOP_K + 5:TOP_K + 8]
    zero_buf, stage, zero_sem, row_sems = refs[TOP_K + 8:]
    tn = packed.shape[0]
    i = pl.program_id(0)
    n = pl.num_programs(0)
    slot = i % 2

    def wait_rows(s):
        for k in range(TOP_K):
            pltpu.make_async_copy(stage.at[s], xs_hbm.at[pl.ds(0, tn)], row_sems.at[s]).wait()

    def fill(row0):
        return pltpu.make_async_copy(
            zero_buf, xs_hbm.at[pl.ds(pl.multiple_of(row0, MOE_BLOCK), MOE_BLOCK)], zero_sem)

    @pl.when(i == 0)
    def _():
        zero_buf[...] = jnp.zeros_like(zero_buf)
        fills = [fill(zs_ref[e]) for e in range(N_EXPERTS)]
        for f in fills:
            f.start()
        for f in fills:
            f.wait()

        n_blocks = xs_hbm.shape[0] // MOE_BLOCK
        lax.fori_loop(nu_ref[0], n_blocks, lambda blk, c: (fill(blk * MOE_BLOCK).start(), c)[1], 0)
        lax.fori_loop(nu_ref[0], n_blocks, lambda blk, c: (fill(blk * MOE_BLOCK).wait(), c)[1], 0)

    @pl.when(i >= 2)
    def _():
        wait_rows(slot)

    stage[slot] = packed[...]
    n_iter = tn // 16
    epw = w_in[0].shape[0]

    def issue(it, carry):
        chunks = []
        for src, dst in zip(w_in, w_out):
            rows = src.shape[1] // n_iter
            r0 = pl.multiple_of(it * rows, rows)
            for e in range(epw):
                chunks.append((dst, e, r0, rows, src[e, pl.ds(r0, rows), :]))
        for u in range(2):
            g = it * 2 + u
            tile = stage.at[slot, pl.ds(pl.multiple_of(g * 8, 8), 8)]
            for s in range(8):
                for k in range(TOP_K):
                    pltpu.make_async_copy(tile.at[pl.ds(s, 1)], xs_hbm.at[pl.ds(dest_refs[k][g * 8 + s], 1)],
                                          row_sems.at[slot]).start(priority=k % 2)
        for dst, e, r0, rows, val in chunks:
            dst[e, pl.ds(r0, rows), :] = val.astype(BF16)
        return carry

    lax.fori_loop(0, n_iter, issue, 0)

    @pl.when(i == n - 1)
    def _():
        @pl.when(i >= 1)
        def _():
            wait_rows(1 - slot)

        wait_rows(slot)


def _dest_specs(n_steps, tn, index):
    return [pl.BlockSpec((tn,), lambda i, *_, k=k: (k * n_steps + index(i),), memory_space=pltpu.SMEM)
            for k in range(TOP_K)]


def _dispatch(zero_starts, n_used, dest_flat, xn, weights, P, tn):
    N = xn.shape[0]
    n_steps = N // tn
    assert N_EXPERTS % n_steps == 0
    epw = N_EXPERTS // n_steps
    w_specs = [pl.BlockSpec((epw,) + w.shape[1:], lambda i, *_: (i, 0, 0)) for w in weights]
    for w in weights:
        assert w.shape[1] % (16 * (tn // 16)) == 0
    grid_spec = pltpu.PrefetchScalarGridSpec(
        num_scalar_prefetch=2,
        grid=(n_steps,),
        in_specs=_dest_specs(n_steps, tn, lambda i: i) + [pl.BlockSpec((tn, D_PACKED), lambda i, zs, nu: (i, 0))]
        + w_specs,
        out_specs=[pl.BlockSpec(memory_space=pl.ANY)] + w_specs,
        scratch_shapes=[pltpu.VMEM((MOE_BLOCK, D_PACKED), jnp.uint32),
                        pltpu.VMEM((2, tn, D_PACKED), jnp.uint32),
                        pltpu.SemaphoreType.DMA, pltpu.SemaphoreType.DMA((2,))],
    )
    xs, *w_bf16 = pl.pallas_call(
        _dispatch_kernel,
        out_shape=[jax.ShapeDtypeStruct((P, D_PACKED), jnp.uint32)]
        + [jax.ShapeDtypeStruct(w.shape, BF16) for w in weights],
        grid_spec=grid_spec,
        compiler_params=_cparams(("arbitrary",)),
        name="dispatch",
    )(zero_starts, n_used, *([dest_flat] * TOP_K), xn, *weights)
    return xs, w_bf16


def _moe_ffn_kernel(be_ref, nu_ref, seg_ref, nxt_ref, xs_ref, wg_hbm, wu_hbm, wd_hbm, ys_ref,
                    wg_b, wu_b, wd_b, sems):
    i = pl.program_id(0)
    used = i < nu_ref[0]

    def weights(expert, slot):
        return [pltpu.make_async_copy(src.at[expert], dst.at[slot], sems.at[slot])
                for src, dst in ((wg_hbm, wg_b), (wu_hbm, wu_b), (wd_hbm, wd_b))]

    @pl.when(used)
    def _():
        new_expert = jnp.logical_or(i == 0, be_ref[i] != be_ref[jnp.maximum(i - 1, 0)])
        slot = seg_ref[i] % 2

        @pl.when(new_expert)
        def _():
            @pl.when(i == 0)
            def _():
                for cp in weights(be_ref[0], 0):
                    cp.start()

            for cp in weights(be_ref[i], slot):
                cp.wait()

            @pl.when(nxt_ref[i] >= 0)
            def _():
                for cp in weights(nxt_ref[i], 1 - slot):
                    cp.start()

        def gate_up(rows):
            xb = _unpack_pairs(xs_ref[rows, :]).astype(BF16)
            return (jnp.dot(xb, wg_b[slot], preferred_element_type=F32),
                    jnp.dot(xb, wu_b[slot], preferred_element_type=F32))

        def swish_mul(hg, hu):
            return (hg * _sigmoid(hg) * hu).astype(BF16)

        def down(rows, hm):
            ys_ref[rows, :] = _pack_pairs(jnp.dot(hm, wd_b[slot], preferred_element_type=F32))

        ra, rb = slice(0, MOE_BLOCK // 2), slice(MOE_BLOCK // 2, MOE_BLOCK)
        gu_a = gate_up(ra)
        hm_a = swish_mul(*gu_a)
        gu_b = gate_up(rb)
        down(ra, hm_a)
        hm_b = swish_mul(*gu_b)
        down(rb, hm_b)

    @pl.when(jnp.logical_not(used))
    def _():
        ys_ref[...] = jnp.zeros_like(ys_ref)


def _moe_ffn(blk_expert, n_used, xs, w_gate, w_up, w_down):
    P = xs.shape[0]
    nblk = P // MOE_BLOCK
    blk = jnp.arange(nblk, dtype=jnp.int32)
    starts = jnp.logical_and(jnp.concatenate([jnp.ones((1,), bool), blk_expert[1:] != blk_expert[:-1]]),
                             blk < n_used[0])
    seg = jnp.cumsum(starts.astype(jnp.int32)) - 1
    next_start = lax.cummin(jnp.where(starts, blk, nblk), reverse=True)
    next_start = jnp.concatenate([next_start[1:], jnp.full((1,), nblk, jnp.int32)])
    nxt = jnp.where(next_start < nblk, blk_expert[jnp.minimum(next_start, nblk - 1)], -1).astype(jnp.int32)
    row_block = pl.BlockSpec((MOE_BLOCK, D_PACKED), lambda i, *_: (i, 0))
    grid_spec = pltpu.PrefetchScalarGridSpec(
        num_scalar_prefetch=4,
        grid=(nblk,),
        in_specs=[row_block] + [pl.BlockSpec(memory_space=pl.ANY)] * 3,
        out_specs=row_block,
        scratch_shapes=[pltpu.VMEM((2, D_MODEL, D_EXPERT), BF16),
                        pltpu.VMEM((2, D_MODEL, D_EXPERT), BF16),
                        pltpu.VMEM((2, D_EXPERT, D_MODEL), BF16),
                        pltpu.SemaphoreType.DMA((2,))],
    )
    return pl.pallas_call(
        _moe_ffn_kernel,
        out_shape=jax.ShapeDtypeStruct((P, D_PACKED), jnp.uint32),
        grid_spec=grid_spec,
        compiler_params=_cparams(("arbitrary",)),
        name="moe_ffn",
    )(blk_expert, n_used, seg, nxt, xs, w_gate, w_up, w_down)


def _combine_kernel(*refs, final):
    dcur_refs, dnext_refs = refs[:TOP_K], refs[TOP_K:2 * TOP_K]
    h_ref, gate_ref, fg_ref, ys_hbm, o_ref, ybuf, sems = refs[2 * TOP_K:]
    tm = h_ref.shape[0]
    i = pl.program_id(0)
    n = pl.num_programs(0)

    def gather(drefs, slot):
        def issue(g, carry):
            for k in range(TOP_K):
                tile = ybuf.at[slot, pl.ds(pl.multiple_of(k * tm + g * 8, 8), 8)]
                for s in range(8):
                    pltpu.make_async_copy(ys_hbm.at[pl.ds(drefs[k][g * 8 + s], 1)],
                                          tile.at[pl.ds(s, 1)], sems.at[slot]).start(priority=k % 2)
            return carry

        lax.fori_loop(0, tm // 8, issue, 0)

    @pl.when(i == 0)
    def _():
        gather(dcur_refs, 0)

    @pl.when(i + 1 < n)
    def _():
        gather(dnext_refs, (i + 1) % 2)

    slot = i % 2
    pltpu.make_async_copy(ys_hbm.at[pl.ds(0, TOP_K * tm)], ybuf.at[slot], sems.at[slot]).wait()
    g = gate_ref[...]
    hh = h_ref[...]
    for k in range(TOP_K):
        hh = hh + _unpack_pairs(ybuf[slot, pl.ds(k * tm, tm), :]) * g[:, k:k + 1]
    o_ref[...] = _rms(hh, fg_ref[...]) if final else hh


def _combine(h, ys, dest_flat, gate_t, fg, tm, final):
    N = h.shape[0]
    n = N // tm
    rows = pl.BlockSpec((tm, D_MODEL), lambda i: (i, 0))
    return pl.pallas_call(
        functools.partial(_combine_kernel, final=final),
        out_shape=jax.ShapeDtypeStruct((N, D_MODEL), F32),
        grid=(n,),
        in_specs=_dest_specs(n, tm, lambda i: i) + _dest_specs(n, tm, lambda i: jnp.minimum(i + 1, n - 1)) + [
                  rows, pl.BlockSpec((tm, TOP_K), lambda i: (i, 0)),
                  pl.BlockSpec((1, D_MODEL), lambda i: (0, 0)),
                  pl.BlockSpec(memory_space=pl.ANY)],
        out_specs=rows,
        scratch_shapes=[pltpu.VMEM((2, TOP_K * tm, D_PACKED), jnp.uint32), pltpu.SemaphoreType.DMA((2,))],
        compiler_params=_cparams(("arbitrary",)),
        name="combine",
    )(*([dest_flat] * (2 * TOP_K)), h, gate_t, fg, ys)


def _reorder_w_in(w_in):
    n_gla = 2 * GLA_KEY_WIDTH + 2 * GLA_WIDTH
    z0 = n_gla
    a0 = z0 + 2 * GLA_GATE_RANK
    pad = jnp.zeros((D_MODEL, Z_PAD - 2 * GLA_GATE_RANK), w_in.dtype)
    return jnp.concatenate([w_in[:, :n_gla], w_in[:, a0:], w_in[:, z0:a0], pad], axis=1)


def kernel(x, norm1_gain, w_in, gla_up_fwd, gla_up_fwd_bias, gla_up_bwd, gla_up_bwd_bias, gla_out_gain, q_norm_gain, k_norm_gain, att_out_gain, w_out, norm2_gain, w_group, b_group, w_expert, b_expert, w_gate, w_up, w_down, final_gain):
    B, T, D = x.shape
    N = B * T
    depth = norm1_gain.shape[0]
    assert D == D_MODEL and T % TOKEN_TILE == 0 and T % GRID_W == 0
    tm, tq, key_block = TOKEN_TILE, ATT_Q_BLOCK, ATT_KEY_BLOCK
    h = x.reshape(N, D)
    cos, se, so = _rope_tables(T)
    head_mean = jnp.asarray(
        np.kron(np.eye(ATT_HEADS, dtype=np.float32),
                np.full((ATT_HEAD_DIM, ATT_HEAD_DIM), 1.0 / ATT_HEAD_DIM, np.float32)), BF16)
    for l in range(depth):
        w = _reorder_w_in(w_in[l]).astype(BF16)
        r = GLA_GATE_RANK
        up = jnp.zeros((Z_PAD, 2 * GLA_KEY_WIDTH), F32)
        up = up.at[:r, :GLA_KEY_WIDTH].set(gla_up_fwd[l]).at[r:2 * r, GLA_KEY_WIDTH:].set(gla_up_bwd[l])
        upb = jnp.concatenate([gla_up_fwd_bias[l], gla_up_bwd_bias[l]])[None, :]
        qg = jnp.tile(q_norm_gain[l], ATT_HEADS)[None, :]
        kg = jnp.tile(k_norm_gain[l], ATT_KV_HEADS)[None, :]
        gq, gk, gv, gg, laf, lab, aqt, ak, avt = _in_proj(
            h, B, T, tm, norm1_gain[l][None, :], w, up.astype(BF16), upb, qg, kg, head_mean, cos, se, so)
        o_gla = _gla(gq, gk, gv, gg, laf, lab, gla_out_gain[l][None, :], B, T)
        o_att_t = _attn(aqt, ak, avt, B, T, tq, key_block)

        wr = jnp.zeros((LANES, D), F32)
        wr = wr.at[:N_GROUPS].set(w_group[l].T).at[N_GROUPS:N_GROUPS + N_EXPERTS].set(w_expert[l].T)
        br = jnp.zeros((LANES, 1), F32)
        br = br.at[:N_GROUPS, 0].set(b_group[l]).at[N_GROUPS:N_GROUPS + N_EXPERTS, 0].set(b_expert[l])
        h, xn, eid, gate, rank, cnt = _out_proj(
            h, o_gla, o_att_t, w_out[l].astype(BF16), att_out_gain[l][:, None], norm2_gain[l][None, :],
            wr.astype(BF16), br, tm)

        counts = cnt[:, 0].astype(jnp.int32)
        padded = (counts + MOE_BLOCK - 1) // MOE_BLOCK * MOE_BLOCK
        pad_ends = jnp.cumsum(padded)
        pad_starts = pad_ends - padded
        seg_start = jnp.sum(jnp.where(eid[:, :, None] == jnp.arange(N_EXPERTS, dtype=jnp.int32),
                                      pad_starts.astype(jnp.int32), 0), axis=-1)
        dest = (seg_start + rank).reshape(-1)
        P = N * TOP_K + N_EXPERTS * MOE_BLOCK
        nblk = P // MOE_BLOCK
        blk_start = jnp.arange(nblk, dtype=jnp.int32) * MOE_BLOCK
        blk_expert = jnp.minimum(
            jnp.sum((pad_ends[None, :] <= blk_start[:, None]).astype(jnp.int32), axis=1), N_EXPERTS - 1)
        n_used = (pad_ends[-1:] // MOE_BLOCK).astype(jnp.int32)

        zero_starts = jnp.maximum(pad_ends - MOE_BLOCK, 0).astype(jnp.int32)
        xs, expert_w = _dispatch(zero_starts, n_used, dest, xn, (w_gate[l], w_up[l], w_down[l]), P, tm)
        ys = _moe_ffn(blk_expert, n_used, xs, *expert_w)
        h = _combine(h, ys, dest, gate.T, final_gain[None, :], tm, l == depth - 1)
    return h.reshape(B, T, D)
```
